```python
import math
import jax, jax.numpy as jnp
from jax import lax
import numpy as np

D_MODEL = 1024
BATCH = 1
SEQ = 16384
DEPTH = 1
DEC_BATCH = 32
DEC_SEQ = 64
PAST_LEN = 4096

CHUNK = 64
QBLOCK = 128
MIX_WIDTH = D_MODEL
FOX_HEADS = 8
FOX_HD = MIX_WIDTH // 2 // FOX_HEADS
FOX_W = FOX_HEADS * FOX_HD
DIFF_HEADS = 4
DIFF_HD = MIX_WIDTH // 2 // (2 * DIFF_HEADS)
DIFF_W = DIFF_HEADS * 2 * DIFF_HD
NUM_BUCKETS = 32
MAX_DISTANCE = 128
EPS = 1e-6
NEG_INF = -1e30
IN_SPLITS = (FOX_W, FOX_W, FOX_W, FOX_HEADS, FOX_W, DIFF_W, DIFF_W, DIFF_W, DIFF_W)
IN_COLS = sum(IN_SPLITS)
SPLIT_POINTS = [int(c) for c in np.cumsum(IN_SPLITS)[:-1]]

kernel_name = "hybrid_fox_diffattn_streaming_step"


def rmsnorm(x, g):
    xf = x.astype(jnp.float32)
    y = xf * lax.rsqrt(jnp.mean(xf * xf, axis=-1, keepdims=True) + EPS)
    return (y * g.astype(jnp.float32)).astype(x.dtype)


def t5_bucket(rel):
    half = NUM_BUCKETS // 2
    max_exact = half // 2
    ret = jnp.where(rel > 0, half, 0)
    n = jnp.abs(rel)
    nf = jnp.maximum(n, 1).astype(jnp.float32)
    large = max_exact + (jnp.log(nf / max_exact) / math.log(MAX_DISTANCE / max_exact)
                         * (half - max_exact)).astype(jnp.int32)
    large = jnp.minimum(large, half - 1)
    return ret + jnp.where(n < max_exact, n, large)


def _to_blocks(a, qb):
    b, sq = a.shape[:2]
    a = a.reshape((b, sq // qb, qb) + a.shape[2:])
    return jnp.moveaxis(a, 1, 0)


def _from_blocks(a):
    a = jnp.moveaxis(a, 0, 1)
    return a.reshape((a.shape[0], a.shape[1] * a.shape[2]) + a.shape[3:])


def fox_attention(q, k, v, cum_q, cum_k, q_pos, k_pos):
    qb = min(QBLOCK, q.shape[1])
    scale = FOX_HD ** -0.5
    cum_kT = jnp.swapaxes(cum_k, 1, 2)

    def block(args):
        qi, ci, pi = args
        s = jnp.einsum('bqhd,bkhd->bhqk', qi, k).astype(jnp.float32) * scale
        s = s + jnp.swapaxes(ci, 1, 2)[..., :, None] - cum_kT[..., None, :]
        s = jnp.where(k_pos[None, :] <= pi[:, None], s, NEG_INF)
        p = jax.nn.softmax(s, axis=-1).astype(v.dtype)
        return jnp.einsum('bhqk,bkhd->bqhd', p, v)

    out = lax.map(block, (_to_blocks(q, qb), _to_blocks(cum_q, qb), q_pos.reshape(-1, qb)))
    return _from_blocks(out)


def diff_attention(q1, q2, k1, k2, v, lam, rel_bias, q_pos, k_pos):
    qb = min(QBLOCK, q1.shape[1])
    scale = DIFF_HD ** -0.5
    k_chunk = k_pos // CHUNK

    def block(args):
        q1i, q2i, pi = args
        bias = jnp.moveaxis(rel_bias[t5_bucket(k_pos[None, :] - pi[:, None])], -1, 0).astype(jnp.float32)
        mask = k_chunk[None, :] <= (pi // CHUNK)[:, None]

        def probs(qi, ki):
            s = jnp.einsum('bqhd,bkhd->bhqk', qi, ki).astype(jnp.float32) * scale + bias
            return jax.nn.softmax(jnp.where(mask, s, NEG_INF), axis=-1)

        p = probs(q1i, k1) - lam * probs(q2i, k2)
        return jnp.einsum('bhqk,bkhd->bqhd', p.astype(v.dtype), v)

    out = lax.map(block, (_to_blocks(q1, qb), _to_blocks(q2, qb), q_pos.reshape(-1, qb)))
    return _from_blocks(out)


def mixer_layer(x, past, g_pre, w_in, b_f, lq1, lk1, lq2, lk2, subln_g, w_out, g_post, rel_bias, layer_idx):
    b, s, _ = x.shape
    h = rmsnorm(x, g_pre)
    proj = jnp.einsum('bsd,dc->bsc', h, w_in)
    fq, fk, fv, ff, fg, dq, dk, dv, dg = jnp.split(proj, SPLIT_POINTS, axis=-1)
    fq = fq.reshape(b, s, FOX_HEADS, FOX_HD)
    fk = fk.reshape(b, s, FOX_HEADS, FOX_HD)
    fv = fv.reshape(b, s, FOX_HEADS, FOX_HD)
    logf = jax.nn.log_sigmoid(ff.astype(jnp.float32) + b_f.astype(jnp.float32))
    dq = dq.reshape(b, s, DIFF_HEADS, 2 * DIFF_HD)
    dk = dk.reshape(b, s, DIFF_HEADS, 2 * DIFF_HD)
    dv = dv.reshape(b, s, DIFF_HEADS, 2 * DIFF_HD)
    new_rows = (fk, fv, logf.astype(x.dtype), dk, dv)

    if past is None:
        p_len = 0
        fk_all, fv_all, logf_all, dk_all, dv_all = fk, fv, logf, dk, dv
    else:
        pk, pv, plogf, pdk, pdv = past
        p_len = pk.shape[1]
        fk_all = jnp.concatenate([pk.astype(fk.dtype), fk], axis=1)
        fv_all = jnp.concatenate([pv.astype(fv.dtype), fv], axis=1)
        logf_all = jnp.concatenate([plogf.astype(jnp.float32), logf], axis=1)
        dk_all = jnp.concatenate([pdk.astype(dk.dtype), dk], axis=1)
        dv_all = jnp.concatenate([pdv.astype(dv.dtype), dv], axis=1)
    k_pos = jnp.arange(p_len + s, dtype=jnp.int32)
    q_pos = k_pos[p_len:]

    cum = jnp.cumsum(logf_all, axis=1)
    fox_o = fox_attention(fq, fk_all, fv_all, cum[:, p_len:], cum, q_pos, k_pos).reshape(b, s, FOX_W)

    lam_init = 0.8 - 0.6 * math.exp(-0.3 * layer_idx)
    lam = (jnp.exp(jnp.sum(lq1.astype(jnp.float32) * lk1.astype(jnp.float32)))
           - jnp.exp(jnp.sum(lq2.astype(jnp.float32) * lk2.astype(jnp.float32))) + lam_init)
    diff_o = diff_attention(dq[..., :DIFF_HD], dq[..., DIFF_HD:], dk_all[..., :DIFF_HD], dk_all[..., DIFF_HD:],
                            dv_all, lam, rel_bias, q_pos, k_pos)
    diff_o = (rmsnorm(diff_o, subln_g) * (1.0 - lam_init)).reshape(b, s, DIFF_W)

    mixed = jnp.concatenate([fox_o * jax.nn.silu(fg), diff_o * jax.nn.silu(dg)], axis=-1)
    out = jnp.einsum('bsc,cd->bsd', mixed, w_out)
    return x + rmsnorm(out, g_post), new_rows


def setup_inputs(seed: int = 0) -> dict:
    key = jax.random.key(seed)
    ks = jax.random.split(key, 20)
    f32 = jnp.float32
    cache_shape_f = (DEPTH, DEC_BATCH, PAST_LEN, FOX_HEADS, FOX_HD)
    cache_shape_d = (DEPTH, DEC_BATCH, PAST_LEN, DIFF_HEADS, 2 * DIFF_HD)
    return {
        "x_prompt": jax.random.normal(ks[0], (BATCH, SEQ, D_MODEL), f32),
        "x_sample": jax.random.normal(ks[1], (DEC_BATCH, DEC_SEQ, D_MODEL), f32),
        "cache_fox_k": jax.random.normal(ks[2], cache_shape_f, f32),
        "cache_fox_v": jax.random.normal(ks[3], cache_shape_f, f32),
        "cache_fox_logf": jax.nn.log_sigmoid(3.0 + jax.random.normal(ks[4], (DEPTH, DEC_BATCH, PAST_LEN, FOX_HEADS), f32)),
        "cache_diff_k": jax.random.normal(ks[5], cache_shape_d, f32),
        "cache_diff_v": jax.random.normal(ks[6], cache_shape_d, f32),
        "norm_pre_g": 1.0 + 0.02 * jax.random.normal(ks[7], (DEPTH, D_MODEL), f32),
        "w_in": jax.random.normal(ks[8], (DEPTH, D_MODEL, IN_COLS), f32) * D_MODEL ** -0.5,
        "forget_bias": 3.0 + 0.1 * jax.random.normal(ks[9], (DEPTH, FOX_HEADS), f32),
        "lambda_q1": 0.1 * jax.random.normal(ks[10], (DEPTH, DIFF_HD), f32),
        "lambda_k1": 0.1 * jax.random.normal(ks[11], (DEPTH, DIFF_HD), f32),
        "lambda_q2": 0.1 * jax.random.normal(ks[12], (DEPTH, DIFF_HD), f32),
        "lambda_k2": 0.1 * jax.random.normal(ks[13], (DEPTH, DIFF_HD), f32),
        "subln_g": 1.0 + 0.02 * jax.random.normal(ks[14], (DEPTH, 2 * DIFF_HD), f32),
        "w_out": jax.random.normal(ks[15], (DEPTH, MIX_WIDTH, D_MODEL), f32) * MIX_WIDTH ** -0.5,
        "norm_post_g": 1.0 + 0.02 * jax.random.normal(ks[16], (DEPTH, D_MODEL), f32),
        "rel_bias": 0.5 * jax.random.normal(ks[17], (NUM_BUCKETS, DIFF_HEADS), f32),
    }


def reference(x_prompt, x_sample, cache_fox_k, cache_fox_v, cache_fox_logf, cache_diff_k, cache_diff_v,
              norm_pre_g, w_in, forget_bias, lambda_q1, lambda_k1, lambda_q2, lambda_k2, subln_g, w_out,
              norm_post_g, rel_bias):
    y_p = x_prompt
    y_s = x_sample
    rows_p = []
    rows_s = []
    for l in range(DEPTH):
        params = (norm_pre_g[l], w_in[l], forget_bias[l], lambda_q1[l], lambda_k1[l], lambda_q2[l],
                  lambda_k2[l], subln_g[l], w_out[l], norm_post_g[l], rel_bias)
        y_p, new_p = mixer_layer(y_p, None, *params, l)
        past = (cache_fox_k[l], cache_fox_v[l], cache_fox_logf[l], cache_diff_k[l], cache_diff_v[l])
        y_s, new_s = mixer_layer(y_s, past, *params, l)
        rows_p.append(new_p)
        rows_s.append(new_s)
    fox_k_p = jnp.stack([r[0] for r in rows_p], axis=0)
    fox_v_p = jnp.stack([r[1] for r in rows_p], axis=0)
    fox_logf_p = jnp.stack([r[2] for r in rows_p], axis=0)
    diff_k_p = jnp.stack([r[3] for r in rows_p], axis=0)
    diff_v_p = jnp.stack([r[4] for r in rows_p], axis=0)
    fox_k_s = jnp.stack([r[0] for r in rows_s], axis=0)
    fox_v_s = jnp.stack([r[1] for r in rows_s], axis=0)
    fox_logf_s = jnp.stack([r[2] for r in rows_s], axis=0)
    diff_k_s = jnp.stack([r[3] for r in rows_s], axis=0)
    diff_v_s = jnp.stack([r[4] for r in rows_s], axis=0)
    return (y_p, y_s, fox_k_p, fox_v_p, fox_logf_p, diff_k_p, diff_v_p,
            fox_k_s, fox_v_s, fox_logf_s, diff_k_s, diff_v_s)
```

```python
import functools
import math

import jax
import jax.numpy as jnp
import numpy as np
from jax import lax
from jax.experimental import pallas as pl
from jax.experimental.pallas import tpu as pltpu

F32 = jnp.float32
BF16 = jnp.bfloat16

FOX_HEADS = 8
FOX_HD = 64
DIFF_HEADS = 4
DIFF_HD = 64
FOX_W = FOX_HEADS * FOX_HD
DIFF_W = DIFF_HEADS * 2 * DIFF_HD
CHUNK = 64
NUM_BUCKETS = 32
MAX_DISTANCE = 128
EPS = 1e-6
NEG_INF = -1e30
LOG2E = 1.4426950408889634

LANES = 128
HEAD_PAIR_W = 2 * FOX_HD
VMEM_LIMIT_BYTES = 56 * 1024 * 1024

PIECES = 3
ONES_AT = PIECES * FOX_HEADS
QEXT_ROWS = 32


def _cparams(n_axes):
    return pltpu.CompilerParams(
        dimension_semantics=("arbitrary",) * n_axes,
        vmem_limit_bytes=VMEM_LIMIT_BYTES,
    )


def _split3(a):
    hi = a.astype(BF16)
    r1 = a - hi.astype(F32)
    mid = r1.astype(BF16)
    lo = (r1 - mid.astype(F32)).astype(BF16)
    return hi, mid, lo


def _nt_dot(a, b):
    return lax.dot_general(a, b, (((1,), (1,)), ((), ())), preferred_element_type=F32)


def _dot(a, b):
    return jnp.dot(a, b, preferred_element_type=F32)


def _inproj_body(x_ref, g_ref, wkv_ref, wg_ref, wff_ref, bf_ref, wq_ref, wvt_ref,
                 fk_ref, fv_ref, dk_ref, dv_ref, logf_ref, gate_ref, q_ref, *rest,
                 transposed, tk, qscale):
    x = x_ref[...]
    ms = jnp.mean(x * x, axis=-1, keepdims=True)
    h = (x * lax.rsqrt(ms + EPS)) * g_ref[...]
    hb = h.astype(BF16)

    fk = _dot(hb, wkv_ref[:, 0 * FOX_W:1 * FOX_W])
    fk_ref[...] = fk
    fv_ref[...] = _dot(hb, wkv_ref[:, 1 * FOX_W:2 * FOX_W])
    dk = _dot(hb, wkv_ref[:, 2 * FOX_W:3 * FOX_W])
    dk_ref[...] = dk
    dv_ref[...] = _dot(hb, wkv_ref[:, 3 * FOX_W:4 * FOX_W])

    ff = _dot(hb, wff_ref[...])
    logf = jax.nn.log_sigmoid(ff + bf_ref[...])
    logf_ref[...] = logf[:, :FOX_HEADS]

    gates = _dot(hb, wg_ref[...])
    gate_ref[...] = (gates * jax.nn.sigmoid(gates)).astype(BF16)

    if transposed:
        k16_ref, dk16_ref, vt_ref = rest
        k16_ref[...] = fk.astype(BF16)
        dk16_ref[...] = dk.astype(BF16)
        q_ref[...] = (_nt_dot(wq_ref[...], hb) * qscale).astype(BF16)
        vt = _nt_dot(wvt_ref[...], hb).astype(BF16)
        for c in range(vt.shape[1] // tk):
            vt_ref[c] = vt[:, c * tk:(c + 1) * tk]
    else:
        q_ref[...] = (_dot(hb, wq_ref[...]) * qscale).astype(BF16)


def _in_projection(x2d, g_pre, w_in, b_f, *, transposed, tm, tk):
    n, d = x2d.shape
    tm = min(tm, n)
    assert n % tm == 0 and (not transposed or tm % tk == 0)
    qscale = FOX_HD ** -0.5 * LOG2E
    o = np.cumsum([0, FOX_W, FOX_W, FOX_W, FOX_HEADS, FOX_W, DIFF_W, DIFF_W, DIFF_W, DIFF_W])
    w_fq, w_fk, w_fv, w_ff, w_fg, w_dq, w_dk, w_dv, w_dg = [w_in[:, o[i]:o[i + 1]] for i in range(9)]
    wkv = jnp.concatenate([w_fk, w_fv, w_dk, w_dv], axis=1).astype(BF16)
    wg = jnp.concatenate([w_fg, w_dg], axis=1).astype(BF16)
    wff = jnp.pad(w_ff, ((0, 0), (0, LANES - FOX_HEADS))).astype(BF16)
    bfp = jnp.pad(b_f.reshape(1, FOX_HEADS), ((0, 0), (0, LANES - FOX_HEADS)))
    wq = jnp.concatenate([w_fq, w_dq], axis=1)
    wv = jnp.concatenate([w_fv, w_dv], axis=1)
    if transposed:
        wq = wq.T.astype(BF16)
        wvt = wv.T.astype(BF16)
    else:
        wq = wq.astype(BF16)
        wvt = jnp.zeros((8, LANES), BF16)

    const = lambda shape: pl.BlockSpec(shape, lambda i: (0,) * len(shape))
    rows = lambda w: pl.BlockSpec((tm, w), lambda i: (i, 0))
    in_specs = [rows(d), const((1, d)), const(wkv.shape), const(wg.shape), const(wff.shape),
                const(bfp.shape), const(wq.shape), const(wvt.shape)]
    out_shape = [jax.ShapeDtypeStruct((n, FOX_W), F32)] * 2 + [jax.ShapeDtypeStruct((n, DIFF_W), F32)] * 2
    out_shape += [jax.ShapeDtypeStruct((n, FOX_HEADS), F32), jax.ShapeDtypeStruct((n, FOX_W + DIFF_W), BF16)]
    out_specs = [rows(FOX_W), rows(FOX_W), rows(DIFF_W), rows(DIFF_W), rows(FOX_HEADS), rows(FOX_W + DIFF_W)]
    if transposed:
        out_shape += [jax.ShapeDtypeStruct((FOX_W + DIFF_W, n), BF16),
                      jax.ShapeDtypeStruct((n, FOX_W), BF16), jax.ShapeDtypeStruct((n, DIFF_W), BF16),
                      jax.ShapeDtypeStruct((n // tk, FOX_W + DIFF_W, tk), BF16)]
        out_specs += [pl.BlockSpec((FOX_W + DIFF_W, tm), lambda i: (0, i)), rows(FOX_W), rows(DIFF_W),
                      pl.BlockSpec((tm // tk, FOX_W + DIFF_W, tk), lambda i: (i, 0, 0))]
    else:
        out_shape += [jax.ShapeDtypeStruct((n, FOX_W + DIFF_W), BF16)]
        out_specs += [rows(FOX_W + DIFF_W)]
    return pl.pallas_call(
        functools.partial(_inproj_body, transposed=transposed, tk=tk, qscale=qscale),
        grid=(n // tm,), in_specs=in_specs, out_specs=out_specs, out_shape=out_shape,
        compiler_params=_cparams(1), name="in_projection_t" if transposed else "in_projection_r",
    )(x2d, g_pre.reshape(1, d), wkv, wg, wff, bfp, wq, wvt)


def _cumsum_body(x_ref, cum_ref, hi_ref, mid_ref, lo_ref, *, seg, group, scale):
    x = x_ref[...]
    r = x.shape[0]
    a = lax.broadcasted_iota(jnp.int32, (LANES, LANES), 0)
    b = lax.broadcasted_iota(jnp.int32, (LANES, LANES), 1)
    sh = int(math.log2(seg))
    upper = ((a <= b) & ((a >> sh) == (b >> sh))).astype(BF16)
    pieces = _split3(x)
    w = sum(_dot(p, upper) for p in pieces)
    if group > 1:
        ones = jnp.ones((LANES, LANES), BF16)
        tot = sum(_dot(p, ones) for p in pieces)
        ri = lax.broadcasted_iota(jnp.int32, (r, r), 0)
        ci = lax.broadcasted_iota(jnp.int32, (r, r), 1)
        gsh = int(math.log2(group))
        lower = ((ci < ri) & ((ri >> gsh) == (ci >> gsh))).astype(BF16)
        w = w + sum(_dot(lower, p) for p in _split3(tot))
    cum_ref[...] = w
    hi, mid, lo = _split3(w * scale)
    hi_ref[...] = hi
    mid_ref[...] = mid
    lo_ref[...] = lo


def _prefix_sums(x2d, *, seg, group, rows_per_step, scale):
    r = x2d.shape[0]
    rb = min(rows_per_step, r)
    assert r % rb == 0 and rb % group == 0 and (seg == LANES or group == 1)
    spec = pl.BlockSpec((rb, LANES), lambda i: (i, 0))
    return pl.pallas_call(
        functools.partial(_cumsum_body, seg=seg, group=group, scale=scale),
        grid=(r // rb,), in_specs=[spec], out_specs=[spec] * 4,
        out_shape=[jax.ShapeDtypeStruct((r, LANES), F32)] + [jax.ShapeDtypeStruct((r, LANES), BF16)] * 3,
        compiler_params=_cparams(1), name="prefix_sums",
    )(x2d)


def _online_update(s, m, l, acc, vt):
    mn = jnp.maximum(m, jnp.max(s, axis=0, keepdims=True))
    alpha = jnp.exp2(m - mn)
    p = jnp.exp2(s - mn)
    l = alpha * l + jnp.sum(p, axis=0, keepdims=True)
    acc = alpha * acc + _dot(vt, p.astype(BF16))
    return mn, l, acc


def _fox_prompt_body(q_ref, qx_ref, k_ref, kx_ref, vt_ref, gate_ref, o_ref, *, t):
    i = pl.program_id(1)
    z64 = jnp.zeros((FOX_HD, t), BF16)
    zpad = jnp.zeros((LANES - QEXT_ROWS, t), BF16)
    q = q_ref[...]
    qa = (jnp.concatenate([q[:FOX_HD], z64, qx_ref[0], zpad], axis=0),
          jnp.concatenate([z64, q[FOX_HD:], qx_ref[1], zpad], axis=0))

    def step(j, carry, masked):
        off = pl.multiple_of(j * t, t)
        ka = jnp.concatenate([k_ref[pl.ds(off, t), :], kx_ref[pl.ds(off, t), :]], axis=1)
        out = []
        for e in range(2):
            m, l, acc = carry[e]
            s = _dot(ka, qa[e])
            if masked:
                kk = lax.broadcasted_iota(jnp.int32, (t, t), 0)
                qq = lax.broadcasted_iota(jnp.int32, (t, t), 1)
                s = jnp.where(kk <= qq, s, NEG_INF)
            out.append(_online_update(s, m, l, acc, vt_ref[j, e * FOX_HD:(e + 1) * FOX_HD, :]))
        return tuple(out)

    init = tuple((jnp.full((1, t), NEG_INF, F32), jnp.zeros((1, t), F32), jnp.zeros((FOX_HD, t), F32))
                 for _ in range(2))
    carry = lax.fori_loop(0, i, functools.partial(step, masked=False), init)
    carry = step(i, carry, True)
    ot = jnp.concatenate([carry[0][2] / carry[0][1], carry[1][2] / carry[1][1]], axis=0)
    o_ref[...] = (ot.T * gate_ref[...].astype(F32)).astype(BF16)


def _fox_prompt_attention(qt, qext, k16, kext, vt, gate, *, t):
    n = k16.shape[0]
    nq = n // t
    pairs = FOX_HEADS // 2
    return pl.pallas_call(
        functools.partial(_fox_prompt_body, t=t),
        grid=(pairs, nq),
        in_specs=[pl.BlockSpec((HEAD_PAIR_W, t), lambda p, i: (p, i)),
                  pl.BlockSpec((2, QEXT_ROWS, t), lambda p, i: (p, 0, i)),
                  pl.BlockSpec((n, HEAD_PAIR_W), lambda p, i: (0, p)),
                  pl.BlockSpec((n, LANES), lambda p, i: (0, 0)),
                  pl.BlockSpec((nq, HEAD_PAIR_W, t), lambda p, i: (0, p, 0)),
                  pl.BlockSpec((t, HEAD_PAIR_W), lambda p, i: (i, p))],
        out_specs=pl.BlockSpec((t, HEAD_PAIR_W), lambda p, i: (i, p)),
        out_shape=jax.ShapeDtypeStruct((n, FOX_W), BF16),
        compiler_params=_cparams(2), name="fox_prompt_attention",
    )(qt, qext, k16, kext, vt, gate)


def _lambda_value(lq1, lk1, lq2, lk2, lam_init):
    a = jnp.sum(lq1 * lk1, axis=-1, keepdims=True)
    b = jnp.sum(lq2 * lk2, axis=-1, keepdims=True)
    return jnp.exp(a) - jnp.exp(b) + lam_init


def _subln_gate(y, g, gate, lam_init):
    ms = jnp.mean(y * y, axis=-1, keepdims=True)
    yn = (y * lax.rsqrt(ms + EPS)) * g
    return ((yn * (1.0 - lam_init)) * gate.astype(F32)).astype(BF16)


def _diff_prompt_body(q_ref, k_ref, vt_ref, bd_ref, bs_ref, gate_ref, sg_ref, lq1_ref, lk1_ref, lq2_ref,
                      lk2_ref, o_ref, *, t, lam_init):
    i = pl.program_id(1)
    z64 = jnp.zeros((DIFF_HD, t), BF16)
    q = q_ref[...]
    qa = (jnp.concatenate([q[:DIFF_HD], z64], axis=0), jnp.concatenate([z64, q[DIFF_HD:]], axis=0))

    def step(j, carry, bias):
        off = pl.multiple_of(j * t, t)
        kk = k_ref[pl.ds(off, t), :]
        v = vt_ref[j]
        out = []
        for e in range(2):
            m, l, acc = carry[e]
            s = _dot(kk, qa[e])
            if bias is not None:
                s = s + bias
            out.append(_online_update(s, m, l, acc, v))
        return tuple(out)

    init = tuple((jnp.full((1, t), NEG_INF, F32), jnp.zeros((1, t), F32), jnp.zeros((2 * DIFF_HD, t), F32))
                 for _ in range(2))
    carry = lax.fori_loop(0, jnp.maximum(i - 1, 0), functools.partial(step, bias=None), init)
    carry = lax.cond(i >= 1, lambda c: step(i - 1, c, bs_ref[0]), lambda c: c, carry)
    carry = step(i, carry, bd_ref[0])
    lam = _lambda_value(lq1_ref[...], lk1_ref[...], lq2_ref[...], lk2_ref[...], lam_init)
    ot = carry[0][2] / carry[0][1] - lam * (carry[1][2] / carry[1][1])
    o_ref[...] = _subln_gate(ot.T, sg_ref[...], gate_ref[...], lam_init)


def _diff_prompt_attention(qt, dk16, vt, bias_diag, bias_sub, gate, subln_g, lams, *, t, lam_init):
    n = dk16.shape[0]
    nq = n // t
    w = 2 * DIFF_HD
    fox_blocks = FOX_W // w
    vec = lambda width: pl.BlockSpec((1, width), lambda h, i: (0, 0))
    return pl.pallas_call(
        functools.partial(_diff_prompt_body, t=t, lam_init=lam_init),
        grid=(DIFF_HEADS, nq),
        in_specs=[pl.BlockSpec((w, t), lambda h, i: (fox_blocks + h, i)),
                  pl.BlockSpec((n, w), lambda h, i: (0, h)),
                  pl.BlockSpec((nq, w, t), lambda h, i: (0, fox_blocks + h, 0)),
                  pl.BlockSpec((1, t, t), lambda h, i: (h, 0, 0)),
                  pl.BlockSpec((1, t, t), lambda h, i: (h, 0, 0)),
                  pl.BlockSpec((t, w), lambda h, i: (i, fox_blocks + h)),
                  vec(w), vec(DIFF_HD), vec(DIFF_HD), vec(DIFF_HD), vec(DIFF_HD)],
        out_specs=pl.BlockSpec((t, w), lambda h, i: (i, h)),
        out_shape=jax.ShapeDtypeStruct((n, DIFF_W), BF16),
        compiler_params=_cparams(2), name="diff_prompt_attention",
    )(qt, dk16, vt, bias_diag, bias_sub, gate, subln_g, *lams)


def _block_diag_queries(qb):
    lane = lax.broadcasted_iota(jnp.int32, qb.shape, 1)
    zero = jnp.zeros_like(qb)
    return jnp.concatenate([jnp.where(lane < FOX_HD, qb, zero), jnp.where(lane >= FOX_HD, qb, zero)], axis=0)


def _fox_sample_body(q_ref, kc_ref, vc_ref, kn_ref, vn_ref, bc_ref, bn_ref, gate_ref, o_ref, *, s):
    qbd = _block_diag_queries(q_ref[...])
    kc = kc_ref[0].astype(BF16)
    zrows = jnp.zeros((LANES - s, HEAD_PAIR_W), BF16)
    kn = jnp.concatenate([kn_ref[...].astype(BF16), zrows], axis=0)
    vn = jnp.concatenate([vn_ref[...].astype(BF16), zrows], axis=0)
    bc = bc_ref[0, 0]
    bn = bn_ref[0, 0]
    p_len = kc.shape[0]

    def rows_bias(b, width):
        return jnp.concatenate([jnp.broadcast_to(b[0:1], (s, width)), jnp.broadcast_to(b[1:2], (s, width))], axis=0)

    sc = _nt_dot(qbd, kc) + rows_bias(bc, p_len)
    sn = _nt_dot(qbd, kn) + rows_bias(bn, LANES)
    qi = lax.broadcasted_iota(jnp.int32, (2 * s, LANES), 0)
    ki = lax.broadcasted_iota(jnp.int32, (2 * s, LANES), 1)
    qi = jnp.where(qi >= s, qi - s, qi)
    sn = jnp.where(ki <= qi, sn, NEG_INF)
    m = jnp.maximum(jnp.max(sc, axis=1, keepdims=True), jnp.max(sn, axis=1, keepdims=True))
    pc = jnp.exp2(sc - m)
    pn = jnp.exp2(sn - m)
    l = jnp.sum(pc, axis=1, keepdims=True) + jnp.sum(pn, axis=1, keepdims=True)
    o = (_dot(pc.astype(BF16), vc_ref[0].astype(BF16)) + _dot(pn.astype(BF16), vn)) / l
    lane = lax.broadcasted_iota(jnp.int32, (s, HEAD_PAIR_W), 1)
    out = jnp.where(lane < FOX_HD, o[:s], o[s:])
    o_ref[...] = (out * gate_ref[...].astype(F32)).astype(BF16)


def _fox_sample_attention(q16, cache_k, cache_v, fk32, fv32, bias_c, bias_n, gate, *, s):
    b, p_len, _ = cache_k.shape
    pairs = FOX_HEADS // 2
    blk = lambda: pl.BlockSpec((s, HEAD_PAIR_W), lambda bi, p: (bi, p))
    cache = lambda: pl.BlockSpec((1, p_len, HEAD_PAIR_W), lambda bi, p: (bi, 0, p))
    return pl.pallas_call(
        functools.partial(_fox_sample_body, s=s),
        grid=(b, pairs),
        in_specs=[blk(), cache(), cache(), blk(), blk(),
                  pl.BlockSpec((1, 1, 2, p_len), lambda bi, p: (bi, p, 0, 0)),
                  pl.BlockSpec((1, 1, 2, LANES), lambda bi, p: (bi, p, 0, 0)),
                  blk()],
        out_specs=blk(),
        out_shape=jax.ShapeDtypeStruct((b * s, FOX_W), BF16),
        compiler_params=_cparams(2), name="fox_sample_attention",
    )(q16, cache_k, cache_v, fk32, fv32, bias_c, bias_n, gate)


def _diff_sample_body(q_ref, kc_ref, vc_ref, kn_ref, vn_ref, bc_ref, bn_ref, gate_ref, sg_ref, lq1_ref,
                      lk1_ref, lq2_ref, lk2_ref, o_ref, *, s, lam_init):
    qbd = _block_diag_queries(q_ref[...])
    kc = kc_ref[0].astype(BF16)
    zrows = jnp.zeros((LANES - s, 2 * DIFF_HD), BF16)
    kn = jnp.concatenate([kn_ref[...].astype(BF16), zrows], axis=0)
    vn = jnp.concatenate([vn_ref[...].astype(BF16), zrows], axis=0)
    bc = bc_ref[0]
    bn = bn_ref[0]
    sc = _nt_dot(qbd, kc) + jnp.concatenate([bc, bc], axis=0)
    sn = _nt_dot(qbd, kn) + jnp.concatenate([bn, bn], axis=0)
    m = jnp.maximum(jnp.max(sc, axis=1, keepdims=True), jnp.max(sn, axis=1, keepdims=True))
    pc = jnp.exp2(sc - m)
    pn = jnp.exp2(sn - m)
    l = jnp.sum(pc, axis=1, keepdims=True) + jnp.sum(pn, axis=1, keepdims=True)
    lam = _lambda_value(lq1_ref[...], lk1_ref[...], lq2_ref[...], lk2_ref[...], lam_init)
    pc = pc / l
    pn = pn / l
    dc = (pc[:s] - lam * pc[s:]).astype(BF16)
    dn = (pn[:s] - lam * pn[s:]).astype(BF16)
    o = _dot(dc, vc_ref[0].astype(BF16)) + _dot(dn, vn)
    o_ref[...] = _subln_gate(o, sg_ref[...], gate_ref[...], lam_init)


def _diff_sample_attention(q16, cache_k, cache_v, dk32, dv32, bias_c, bias_n, gate, subln_g, lams, *, s,
                           lam_init):
    b, p_len, _ = cache_k.shape
    w = 2 * DIFF_HD
    fox_blocks = FOX_W // w
    blk = lambda: pl.BlockSpec((s, w), lambda bi, h: (bi, h))
    qblk = lambda: pl.BlockSpec((s, w), lambda bi, h: (bi, fox_blocks + h))
    cache = lambda: pl.BlockSpec((1, p_len, w), lambda bi, h: (bi, 0, h))
    vec = lambda width: pl.BlockSpec((1, width), lambda bi, h: (0, 0))
    return pl.pallas_call(
        functools.partial(_diff_sample_body, s=s, lam_init=lam_init),
        grid=(b, DIFF_HEADS),
        in_specs=[qblk(), cache(), cache(), blk(), blk(),
                  pl.BlockSpec((1, s, p_len), lambda bi, h: (h, 0, 0)),
                  pl.BlockSpec((1, s, LANES), lambda bi, h: (h, 0, 0)),
                  qblk(), vec(w), vec(DIFF_HD), vec(DIFF_HD), vec(DIFF_HD), vec(DIFF_HD)],
        out_specs=blk(),
        out_shape=jax.ShapeDtypeStruct((b * s, DIFF_W), BF16),
        compiler_params=_cparams(2), name="diff_sample_attention",
    )(q16, cache_k, cache_v, dk32, dv32, bias_c, bias_n, gate, subln_g, *lams)


def _outproj_body(mf_ref, md_ref, w_ref, x_ref, g_ref, y_ref):
    out = _dot(mf_ref[...], w_ref[:FOX_W, :]) + _dot(md_ref[...], w_ref[FOX_W:, :])
    ms = jnp.mean(out * out, axis=-1, keepdims=True)
    y_ref[...] = x_ref[...] + (out * lax.rsqrt(ms + EPS)) * g_ref[...]


def _out_projection(mf, md, w_out, x2d, g_post, *, tm):
    n, d = x2d.shape
    tm = min(tm, n)
    assert n % tm == 0
    rows = lambda w: pl.BlockSpec((tm, w), lambda i: (i, 0))
    return pl.pallas_call(
        _outproj_body, grid=(n // tm,),
        in_specs=[rows(FOX_W), rows(DIFF_W), pl.BlockSpec(w_out.shape, lambda i: (0, 0)), rows(d),
                  pl.BlockSpec((1, d), lambda i: (0, 0))],
        out_specs=rows(d), out_shape=jax.ShapeDtypeStruct((n, d), F32),
        compiler_params=_cparams(1), name="out_projection",
    )(mf, md, w_out.astype(BF16), x2d, g_post.reshape(1, d))


def _t5_bucket(rel):
    half = NUM_BUCKETS // 2
    max_exact = half // 2
    ret = jnp.where(rel > 0, half, 0)
    n = jnp.abs(rel)
    nf = jnp.maximum(n, 1).astype(jnp.float32)
    large = max_exact + (jnp.log(nf / max_exact) / math.log(MAX_DISTANCE / max_exact)
                         * (half - max_exact)).astype(jnp.int32)
    large = jnp.minimum(large, half - 1)
    return ret + jnp.where(n < max_exact, n, large)


def _diff_bias(rel_bias, q_pos, k_pos, far_pos):
    rel = k_pos[:, None] - q_pos[None, :]
    bias = jnp.moveaxis(rel_bias[_t5_bucket(rel)], -1, 0).astype(F32)
    if far_pos is not None:
        bias = bias - rel_bias[_t5_bucket(jnp.int32(far_pos))].astype(F32)[:, None, None]
    mask = (k_pos // CHUNK)[:, None] <= (q_pos // CHUNK)[None, :]
    return jnp.where(mask[None], bias * LOG2E, NEG_INF)


def _prompt_layer(x, g_pre, w_in, b_f, lams, subln_g, w_out, g_post, rel_bias, layer_idx, *, t=256, tm=512):
    b, s, d = x.shape
    assert b == 1 and s % t == 0 and s // LANES >= 1
    x2d = x.reshape(s, d)
    (fk, fv, dk, dv, logf, gate, qt, k16, dk16, vt) = _in_projection(
        x2d, g_pre, w_in, b_f, transposed=True, tm=tm, tk=t)

    rows = s // LANES
    _, hi, mid, lo = _prefix_sums(logf.T.reshape(FOX_HEADS * rows, LANES), seg=LANES, group=rows,
                                  rows_per_step=FOX_HEADS * rows, scale=LOG2E)
    pieces = jnp.stack([hi, mid, lo], axis=0).reshape(PIECES, FOX_HEADS, s)
    kext = jnp.concatenate([
        (-pieces).transpose(2, 1, 0).reshape(s, ONES_AT), jnp.ones((s, PIECES), BF16),
        jnp.zeros((s, LANES - ONES_AT - PIECES), BF16)], axis=1)
    sel = (jnp.arange(ONES_AT)[None, :] // PIECES == jnp.arange(FOX_HEADS)[:, None]).astype(BF16)
    qext = jnp.concatenate([
        jnp.broadcast_to(sel[:, :, None], (FOX_HEADS, ONES_AT, s)), pieces.transpose(1, 0, 2),
        jnp.zeros((FOX_HEADS, QEXT_ROWS - ONES_AT - PIECES, s), BF16)], axis=1)

    mf = _fox_prompt_attention(qt, qext, k16, kext, vt, gate, t=t)

    lam_init = 0.8 - 0.6 * math.exp(-0.3 * layer_idx)
    pos = jnp.arange(t, dtype=jnp.int32)
    bias_diag = _diff_bias(rel_bias, pos, pos, -2 * t)
    bias_sub = _diff_bias(rel_bias, pos + t, pos, -2 * t)
    md = _diff_prompt_attention(qt, dk16, vt, bias_diag, bias_sub, gate, subln_g.reshape(1, -1), lams,
                                t=t, lam_init=lam_init)

    y = _out_projection(mf, md, w_out, x2d, g_post, tm=tm)
    return y.reshape(b, s, d), (fk, fv, logf, dk, dv)


def _sample_layer(x, past, g_pre, w_in, b_f, lams, subln_g, w_out, g_post, rel_bias, layer_idx, *, tm=512):
    b, s, d = x.shape
    pk, pv, plogf, pdk, pdv = past
    p_len = pk.shape[1]
    assert s <= LANES and LANES % s == 0 and p_len % LANES == 0
    x2d = x.reshape(b * s, d)
    fk, fv, dk, dv, logf, gate, q16 = _in_projection(x2d, g_pre, w_in, b_f, transposed=False, tm=tm, tk=tm)

    rows = p_len // LANES
    cum_c, _, _, _ = _prefix_sums(plogf.transpose(0, 2, 1).reshape(b * FOX_HEADS * rows, LANES), seg=LANES,
                                  group=rows, rows_per_step=32 * rows, scale=1.0)
    cum_c = cum_c.reshape(b, FOX_HEADS, p_len)
    bias_c = ((cum_c[:, :, -1:] - cum_c) * LOG2E).reshape(b, FOX_HEADS // 2, 2, p_len)
    cum_n, _, _, _ = _prefix_sums(
        logf.reshape(b, s, FOX_HEADS).transpose(0, 2, 1).reshape(b * FOX_HEADS * s // LANES, LANES),
        seg=s, group=1, rows_per_step=1024, scale=1.0)
    bias_n = jnp.concatenate([-cum_n.reshape(b, FOX_HEADS, s) * LOG2E,
                              jnp.full((b, FOX_HEADS, LANES - s), NEG_INF, F32)], axis=-1)
    bias_n = bias_n.reshape(b, FOX_HEADS // 2, 2, LANES)
    mf = _fox_sample_attention(q16, pk.reshape(b, p_len, FOX_W), pv.reshape(b, p_len, FOX_W), fk, fv,
                               bias_c, bias_n, gate, s=s)

    lam_init = 0.8 - 0.6 * math.exp(-0.3 * layer_idx)
    q_pos = p_len + jnp.arange(s, dtype=jnp.int32)
    bd_c = _diff_bias(rel_bias, q_pos, jnp.arange(p_len, dtype=jnp.int32), None).transpose(0, 2, 1)
    bd_n = _diff_bias(rel_bias, q_pos, q_pos, None).transpose(0, 2, 1)
    bd_n = jnp.concatenate([bd_n, jnp.full((DIFF_HEADS, s, LANES - s), NEG_INF, F32)], axis=-1)
    md = _diff_sample_attention(q16, pdk.reshape(b, p_len, DIFF_W), pdv.reshape(b, p_len, DIFF_W), dk, dv,
                                bd_c, bd_n, gate, subln_g.reshape(1, -1), lams, s=s, lam_init=lam_init)

    y = _out_projection(mf, md, w_out, x2d, g_post, tm=tm)
    return y.reshape(b, s, d), (fk, fv, logf, dk, dv)


def kernel(x_prompt, x_sample, cache_fox_k, cache_fox_v, cache_fox_logf, cache_diff_k, cache_diff_v,
           norm_pre_g, w_in, forget_bias, lambda_q1, lambda_k1, lambda_q2, lambda_k2, subln_g, w_out,
           norm_post_g, rel_bias):
    depth = w_in.shape[0]
    y_p, y_s = x_prompt, x_sample
    rows_p, rows_s = [], []
    for l in range(depth):
        lams = tuple(a[l].reshape(1, -1) for a in (lambda_q1, lambda_k1, lambda_q2, lambda_k2))
        params = (norm_pre_g[l], w_in[l], forget_bias[l], lams, subln_g[l], w_out[l], norm_post_g[l], rel_bias, l)
        y_p, new_p = _prompt_layer(y_p, *params)
        past = (cache_fox_k[l], cache_fox_v[l], cache_fox_logf[l], cache_diff_k[l], cache_diff_v[l])
        y_s, new_s = _sample_layer(y_s, past, *params)
        rows_p.append(new_p)
        rows_s.append(new_s)

    bp, sp = x_prompt.shape[:2]
    bs, ss = x_sample.shape[:2]

    def stack(rows, idx, shape):
        return jnp.stack([r[idx].reshape(shape) for r in rows], axis=0)

    outs = [y_p, y_s]
    for rows, (b, s) in ((rows_p, (bp, sp)), (rows_s, (bs, ss))):
        outs += [stack(rows, 0, (b, s, FOX_HEADS, FOX_HD)), stack(rows, 1, (b, s, FOX_HEADS, FOX_HD)),
                 stack(rows, 2, (b, s, FOX_HEADS)), stack(rows, 3, (b, s, DIFF_HEADS, 2 * DIFF_HD)),
                 stack(rows, 4, (b, s, DIFF_HEADS, 2 * DIFF_HD))]
    return tuple(outs)
```

```python
import functools
import math

import jax
import jax.numpy as jnp
import numpy as np
from jax import lax
from jax.experimental import pallas as pl
from jax.experimental.pallas import tpu as pltpu

F32 = jnp.float32
BF16 = jnp.bfloat16

FOX_HEADS = 8
FOX_HD = 64
DIFF_HEADS = 4
DIFF_HD = 64
FOX_W = FOX_HEADS * FOX_HD
DIFF_W = DIFF_HEADS * 2 * DIFF_HD
CHUNK = 64
NUM_BUCKETS = 32
MAX_DISTANCE = 128
EPS = 1e-6
NEG_INF = -1e30
LOG2E = 1.4426950408889634

LANES = 128
HEAD_PAIR_W = 2 * FOX_HD
VMEM_LIMIT_BYTES = 56 * 1024 * 1024

PIECES = 3
ONES_AT = PIECES * FOX_HEADS
QEXT_ROWS = 32


def _cparams(n_axes):
    return pltpu.CompilerParams(
        dimension_semantics=("arbitrary",) * n_axes,
        vmem_limit_bytes=VMEM_LIMIT_BYTES,
    )


def _split3(a):
    hi = a.astype(BF16)
    r1 = a - hi.astype(F32)
    mid = r1.astype(BF16)
    lo = (r1 - mid.astype(F32)).astype(BF16)
    return hi, mid, lo


def _nt_dot(a, b):
    return lax.dot_general(a, b, (((1,), (1,)), ((), ())), preferred_element_type=F32)


def _dot(a, b):
    return jnp.dot(a, b, preferred_element_type=F32)


def _inproj_body(x_ref, g_ref, wkv_ref, wg_ref, wff_ref, bf_ref, wq_ref, wvt_ref,
                 fk_ref, fv_ref, dk_ref, dv_ref, logf_ref, gate_ref, q_ref, *rest,
                 transposed, tk, qscale):
    x = x_ref[...]
    ms = jnp.mean(x * x, axis=-1, keepdims=True)
    h = (x * lax.rsqrt(ms + EPS)) * g_ref[...]
    hb = h.astype(BF16)

    fk = _dot(hb, wkv_ref[:, 0 * FOX_W:1 * FOX_W])
    fk_ref[...] = fk
    fv_ref[...] = _dot(hb, wkv_ref[:, 1 * FOX_W:2 * FOX_W])
    dk = _dot(hb, wkv_ref[:, 2 * FOX_W:3 * FOX_W])
    dk_ref[...] = dk
    dv_ref[...] = _dot(hb, wkv_ref[:, 3 * FOX_W:4 * FOX_W])

    ff = _dot(hb, wff_ref[...])
    logf = jax.nn.log_sigmoid(ff + bf_ref[...])
    logf_ref[...] = logf[:, :FOX_HEADS]

    gates = _dot(hb, wg_ref[...])
    gate_ref[...] = (gates * jax.nn.sigmoid(gates)).astype(BF16)

    if transposed:
        k16_ref, dk16_ref, vt_ref = rest
        k16_ref[...] = fk.astype(BF16)
        dk16_ref[...] = dk.astype(BF16)
        q_ref[...] = (_nt_dot(wq_ref[...], hb) * qscale).astype(BF16)
        vt = _nt_dot(wvt_ref[...], hb).astype(BF16)
        for c in range(vt.shape[1] // tk):
            vt_ref[c] = vt[:, c * tk:(c + 1) * tk]
    else:
        q_ref[...] = (_dot(hb, wq_ref[...]) * qscale).astype(BF16)


def _in_projection(x2d, g_pre, w_in, b_f, *, transposed, tm, tk):
    n, d = x2d.shape
    tm = min(tm, n)
    assert n % tm == 0 and (not transposed or tm % tk == 0)
    qscale = FOX_HD ** -0.5 * LOG2E
    o = np.cumsum([0, FOX_W, FOX_W, FOX_W, FOX_HEADS, FOX_W, DIFF_W, DIFF_W, DIFF_W, DIFF_W])
    w_fq, w_fk, w_fv, w_ff, w_fg, w_dq, w_dk, w_dv, w_dg = [w_in[:, o[i]:o[i + 1]] for i in range(9)]
    wkv = jnp.concatenate([w_fk, w_fv, w_dk, w_dv], axis=1).astype(BF16)
    wg = jnp.concatenate([w_fg, w_dg], axis=1).astype(BF16)
    wff = jnp.pad(w_ff, ((0, 0), (0, LANES - FOX_HEADS))).astype(BF16)
    bfp = jnp.pad(b_f.reshape(1, FOX_HEADS), ((0, 0), (0, LANES - FOX_HEADS)))
    wq = jnp.concatenate([w_fq, w_dq], axis=1)
    wv = jnp.concatenate([w_fv, w_dv], axis=1)
    if transposed:
        wq = wq.T.astype(BF16)
        wvt = wv.T.astype(BF16)
    else:
        wq = wq.astype(BF16)
        wvt = jnp.zeros((8, LANES), BF16)

    const = lambda shape: pl.BlockSpec(shape, lambda i: (0,) * len(shape))
    rows = lambda w: pl.BlockSpec((tm, w), lambda i: (i, 0))
    in_specs = [rows(d), const((1, d)), const(wkv.shape), const(wg.shape), const(wff.shape),
                const(bfp.shape), const(wq.shape), const(wvt.shape)]
    out_shape = [jax.ShapeDtypeStruct((n, FOX_W), F32)] * 2 + [jax.ShapeDtypeStruct((n, DIFF_W), F32)] * 2
    out_shape += [jax.ShapeDtypeStruct((n, FOX_HEADS), F32), jax.ShapeDtypeStruct((n, FOX_W + DIFF_W), BF16)]
    out_specs = [rows(FOX_W), rows(FOX_W), rows(DIFF_W), rows(DIFF_W), rows(FOX_HEADS), rows(FOX_W + DIFF_W)]
    if transposed:
        out_shape += [jax.ShapeDtypeStruct((FOX_W + DIFF_W, n), BF16),
                      jax.ShapeDtypeStruct((n, FOX_W), BF16), jax.ShapeDtypeStruct((n, DIFF_W), BF16),
                      jax.ShapeDtypeStruct((n // tk, FOX_W + DIFF_W, tk), BF16)]
        out_specs += [pl.BlockSpec((FOX_W + DIFF_W, tm), lambda i: (0, i)), rows(FOX_W), rows(DIFF_W),
                      pl.BlockSpec((tm // tk, FOX_W + DIFF_W, tk), lambda i: (i, 0, 0))]
    else:
        out_shape += [jax.ShapeDtypeStruct((n, FOX_W + DIFF_W), BF16)]
        out_specs += [rows(FOX_W + DIFF_W)]
    return pl.pallas_call(
        functools.partial(_inproj_body, transposed=transposed, tk=tk, qscale=qscale),
        grid=(n // tm,), in_specs=in_specs, out_specs=out_specs, out_shape=out_shape,
        compiler_params=_cparams(1), name="in_projection_t" if transposed else "in_projection_r",
    )(x2d, g_pre.reshape(1, d), wkv, wg, wff, bfp, wq, wvt)


def _cumsum_body(x_ref, cum_ref, hi_ref, mid_ref, lo_ref, *, seg, group, scale):
    x = x_ref[...]
    r = x.shape[0]
    a = lax.broadcasted_iota(jnp.int32, (LANES, LANES), 0)
    b = lax.broadcasted_iota(jnp.int32, (LANES, LANES), 1)
    sh = int(math.log2(seg))
    upper = ((a <= b) & ((a >> sh) == (b >> sh))).astype(BF16)
    pieces = _split3(x)
    w = sum(_dot(p, upper) for p in pieces)
    if group > 1:
        ones = jnp.ones((LANES, LANES), BF16)
        tot = sum(_dot(p, ones) for p in pieces)
        ri = lax.broadcasted_iota(jnp.int32, (r, r), 0)
        ci = lax.broadcasted_iota(jnp.int32, (r, r), 1)
        gsh = int(math.log2(group))
        lower = ((ci < ri) & ((ri >> gsh) == (ci >> gsh))).astype(BF16)
        w = w + sum(_dot(lower, p) for p in _split3(tot))
    cum_ref[...] = w
    hi, mid, lo = _split3(w * scale)
    hi_ref[...] = hi
    mid_ref[...] = mid
    lo_ref[...] = lo


def _prefix_sums(x2d, *, seg, group, rows_per_step, scale):
    r = x2d.shape[0]
    rb = min(rows_per_step, r)
    assert r % rb == 0 and rb % group == 0 and (seg == LANES or group == 1)
    spec = pl.BlockSpec((rb, LANES), lambda i: (i, 0))
    return pl.pallas_call(
        functools.partial(_cumsum_body, seg=seg, group=group, scale=scale),
        grid=(r // rb,), in_specs=[spec], out_specs=[spec] * 4,
        out_shape=[jax.ShapeDtypeStruct((r, LANES), F32)] + [jax.ShapeDtypeStruct((r, LANES), BF16)] * 3,
        compiler_params=_cparams(1), name="prefix_sums",
    )(x2d)


def _online_update(s, h, m_ref, l_ref, acc_ref, vt):
    m = m_ref[h]
    mn = jnp.maximum(m, jnp.max(s, axis=0, keepdims=True))
    alpha = jnp.exp2(m - mn)
    p = jnp.exp2(s - mn)
    l_ref[h] = alpha * l_ref[h] + jnp.sum(p, axis=0, keepdims=True)
    acc_ref[h] = alpha * acc_ref[h] + _dot(vt, p.astype(BF16))
    m_ref[h] = mn


def _init_state(m_ref, l_ref, acc_ref):
    m_ref[...] = jnp.full(m_ref.shape, NEG_INF, F32)
    l_ref[...] = jnp.zeros(l_ref.shape, F32)
    acc_ref[...] = jnp.zeros(acc_ref.shape, F32)


def _resident(shape, index_map):
    return pl.BlockSpec(shape, index_map, pipeline_mode=pl.Buffered(1))


def _fox_prompt_body(q_ref, qx_ref, k_ref, kx_ref, vt_ref, gate_ref, o_ref, m_ref, l_ref, acc_ref, *, t, pairs):
    i = pl.program_id(1)
    _init_state(m_ref, l_ref, acc_ref)
    z64 = jnp.zeros((FOX_HD, t), BF16)
    zpad = jnp.zeros((LANES - QEXT_ROWS, t), BF16)
    qa = []
    for p in range(pairs):
        q = q_ref[p * HEAD_PAIR_W:(p + 1) * HEAD_PAIR_W, :]
        qa.append(jnp.concatenate([q[:FOX_HD], z64, qx_ref[2 * p], zpad], axis=0))
        qa.append(jnp.concatenate([z64, q[FOX_HD:], qx_ref[2 * p + 1], zpad], axis=0))

    def step(j, masked):
        off = pl.multiple_of(j * t, t)
        kx = kx_ref[pl.ds(off, t), :]
        scores = []
        for p in range(pairs):
            ka = jnp.concatenate([k_ref[pl.ds(off, t), p * HEAD_PAIR_W:(p + 1) * HEAD_PAIR_W], kx], axis=1)
            scores += [_dot(ka, qa[2 * p]), _dot(ka, qa[2 * p + 1])]
        for h, s in enumerate(scores):
            if masked:
                kk = lax.broadcasted_iota(jnp.int32, (t, t), 0)
                qq = lax.broadcasted_iota(jnp.int32, (t, t), 1)
                s = jnp.where(kk <= qq, s, NEG_INF)
            _online_update(s, h, m_ref, l_ref, acc_ref, vt_ref[j, h * FOX_HD:(h + 1) * FOX_HD, :])

    def far_step(j, c):
        step(j, False)
        return c

    lax.fori_loop(0, i, far_step, 0)
    step(i, True)
    for p in range(pairs):
        ot = jnp.concatenate([acc_ref[2 * p] / l_ref[2 * p], acc_ref[2 * p + 1] / l_ref[2 * p + 1]], axis=0)
        cols = slice(p * HEAD_PAIR_W, (p + 1) * HEAD_PAIR_W)
        o_ref[:, cols] = (ot.T * gate_ref[:, cols].astype(F32)).astype(BF16)


def _fox_prompt_attention(qt, qext, k16, kext, vt, gate, *, t, pairs):
    n = k16.shape[0]
    nq = n // t
    w = pairs * HEAD_PAIR_W
    heads = 2 * pairs
    assert (FOX_HEADS // 2) % pairs == 0
    return pl.pallas_call(
        functools.partial(_fox_prompt_body, t=t, pairs=pairs),
        grid=(FOX_HEADS // heads, nq),
        in_specs=[pl.BlockSpec((w, t), lambda g, i: (g, i)),
                  pl.BlockSpec((heads, QEXT_ROWS, t), lambda g, i: (g, 0, i)),
                  _resident((n, w), lambda g, i: (0, g)),
                  _resident((n, LANES), lambda g, i: (0, 0)),
                  _resident((nq, w, t), lambda g, i: (0, g, 0)),
                  pl.BlockSpec((t, w), lambda g, i: (i, g))],
        out_specs=pl.BlockSpec((t, w), lambda g, i: (i, g)),
        out_shape=jax.ShapeDtypeStruct((n, FOX_W), BF16),
        scratch_shapes=[pltpu.VMEM((heads, 1, t), F32), pltpu.VMEM((heads, 1, t), F32),
                        pltpu.VMEM((heads, FOX_HD, t), F32)],
        compiler_params=_cparams(2), name="fox_prompt_attention",
    )(qt, qext, k16, kext, vt, gate)


def _lambda_value(lq1, lk1, lq2, lk2, lam_init):
    a = jnp.sum(lq1 * lk1, axis=-1, keepdims=True)
    b = jnp.sum(lq2 * lk2, axis=-1, keepdims=True)
    return jnp.exp(a) - jnp.exp(b) + lam_init


def _subln_gate(y, g, gate, lam_init):
    ms = jnp.mean(y * y, axis=-1, keepdims=True)
    yn = (y * lax.rsqrt(ms + EPS)) * g
    return ((yn * (1.0 - lam_init)) * gate.astype(F32)).astype(BF16)


def _diff_prompt_body(q_ref, k_ref, vt_ref, bd_ref, bs_ref, gate_ref, sg_ref, lq1_ref, lk1_ref, lq2_ref,
                      lk2_ref, o_ref, m_ref, l_ref, acc_ref, *, t, heads, lam_init):
    i = pl.program_id(1)
    w = 2 * DIFF_HD
    _init_state(m_ref, l_ref, acc_ref)
    z64 = jnp.zeros((DIFF_HD, t), BF16)
    qa = []
    for h in range(heads):
        q = q_ref[h * w:(h + 1) * w, :]
        qa.append(jnp.concatenate([q[:DIFF_HD], z64], axis=0))
        qa.append(jnp.concatenate([z64, q[DIFF_HD:]], axis=0))

    def step(j, bias_ref):
        off = pl.multiple_of(j * t, t)
        scores = []
        for h in range(heads):
            kk = k_ref[pl.ds(off, t), h * w:(h + 1) * w]
            scores += [_dot(kk, qa[2 * h]), _dot(kk, qa[2 * h + 1])]
        for c, s in enumerate(scores):
            h = c // 2
            if bias_ref is not None:
                s = s + bias_ref[h]
            _online_update(s, c, m_ref, l_ref, acc_ref, vt_ref[j, h * w:(h + 1) * w, :])

    def far_step(j, c):
        step(j, None)
        return c

    lax.fori_loop(0, jnp.maximum(i - 1, 0), far_step, 0)

    @pl.when(i >= 1)
    def _():
        step(i - 1, bs_ref)

    step(i, bd_ref)
    lam = _lambda_value(lq1_ref[...], lk1_ref[...], lq2_ref[...], lk2_ref[...], lam_init)
    for h in range(heads):
        ot = acc_ref[2 * h] / l_ref[2 * h] - lam * (acc_ref[2 * h + 1] / l_ref[2 * h + 1])
        cols = slice(h * w, (h + 1) * w)
        o_ref[:, cols] = _subln_gate(ot.T, sg_ref[...], gate_ref[:, cols], lam_init)


def _diff_prompt_attention(qt, dk16, vt, bias_diag, bias_sub, gate, subln_g, lams, *, t, heads, lam_init):
    n = dk16.shape[0]
    nq = n // t
    w = heads * 2 * DIFF_HD
    assert DIFF_HEADS % heads == 0 and FOX_W % w == 0
    fox_blocks = FOX_W // w
    vec = lambda width: pl.BlockSpec((1, width), lambda g, i: (0, 0))
    return pl.pallas_call(
        functools.partial(_diff_prompt_body, t=t, heads=heads, lam_init=lam_init),
        grid=(DIFF_HEADS // heads, nq),
        in_specs=[pl.BlockSpec((w, t), lambda g, i: (fox_blocks + g, i)),
                  _resident((n, w), lambda g, i: (0, g)),
                  _resident((nq, w, t), lambda g, i: (0, fox_blocks + g, 0)),
                  pl.BlockSpec((heads, t, t), lambda g, i: (g, 0, 0)),
                  pl.BlockSpec((heads, t, t), lambda g, i: (g, 0, 0)),
                  pl.BlockSpec((t, w), lambda g, i: (i, fox_blocks + g)),
                  vec(2 * DIFF_HD), vec(DIFF_HD), vec(DIFF_HD), vec(DIFF_HD), vec(DIFF_HD)],
        out_specs=pl.BlockSpec((t, w), lambda g, i: (i, g)),
        out_shape=jax.ShapeDtypeStruct((n, DIFF_W), BF16),
        scratch_shapes=[pltpu.VMEM((2 * heads, 1, t), F32), pltpu.VMEM((2 * heads, 1, t), F32),
                        pltpu.VMEM((2 * heads, 2 * DIFF_HD, t), F32)],
        compiler_params=_cparams(2), name="diff_prompt_attention",
    )(qt, dk16, vt, bias_diag, bias_sub, gate, subln_g, *lams)


def _block_diag_queries(qb):
    lane = lax.broadcasted_iota(jnp.int32, qb.shape, 1)
    zero = jnp.zeros_like(qb)
    return jnp.concatenate([jnp.where(lane < FOX_HD, qb, zero), jnp.where(lane >= FOX_HD, qb, zero)], axis=0)


def _fox_sample_body(q_ref, kc_ref, vc_ref, kn_ref, vn_ref, bc_ref, bn_ref, gate_ref, o_ref, *, s):
    qbd = _block_diag_queries(q_ref[...])
    kc = kc_ref[0].astype(BF16)
    zrows = jnp.zeros((LANES - s, HEAD_PAIR_W), BF16)
    kn = jnp.concatenate([kn_ref[...].astype(BF16), zrows], axis=0)
    vn = jnp.concatenate([vn_ref[...].astype(BF16), zrows], axis=0)
    bc = bc_ref[0, 0]
    bn = bn_ref[0, 0]
    p_len = kc.shape[0]

    def rows_bias(b, width):
        return jnp.concatenate([jnp.broadcast_to(b[0:1], (s, width)), jnp.broadcast_to(b[1:2], (s, width))], axis=0)

    sc = _nt_dot(qbd, kc) + rows_bias(bc, p_len)
    sn = _nt_dot(qbd, kn) + rows_bias(bn, LANES)
    qi = lax.broadcasted_iota(jnp.int32, (2 * s, LANES), 0)
    ki = lax.broadcasted_iota(jnp.int32, (2 * s, LANES), 1)
    qi = jnp.where(qi >= s, qi - s, qi)
    sn = jnp.where(ki <= qi, sn, NEG_INF)
    m = jnp.maximum(jnp.max(sc, axis=1, keepdims=True), jnp.max(sn, axis=1, keepdims=True))
    pc = jnp.exp2(sc - m)
    pn = jnp.exp2(sn - m)
    l = jnp.sum(pc, axis=1, keepdims=True) + jnp.sum(pn, axis=1, keepdims=True)
    o = (_dot(pc.astype(BF16), vc_ref[0].astype(BF16)) + _dot(pn.astype(BF16), vn)) / l
    lane = lax.broadcasted_iota(jnp.int32, (s, HEAD_PAIR_W), 1)
    out = jnp.where(lane < FOX_HD, o[:s], o[s:])
    o_ref[...] = (out * gate_ref[...].astype(F32)).astype(BF16)


def _fox_sample_attention(q16, cache_k, cache_v, fk32, fv32, bias_c, bias_n, gate, *, s):
    b, p_len, _ = cache_k.shape
    pairs = FOX_HEADS // 2
    blk = lambda: pl.BlockSpec((s, HEAD_PAIR_W), lambda bi, p: (bi, p))
    cache = lambda: pl.BlockSpec((1, p_len, HEAD_PAIR_W), lambda bi, p: (bi, 0, p))
    return pl.pallas_call(
        functools.partial(_fox_sample_body, s=s),
        grid=(b, pairs),
        in_specs=[blk(), cache(), cache(), blk(), blk(),
                  pl.BlockSpec((1, 1, 2, p_len), lambda bi, p: (bi, p, 0, 0)),
                  pl.BlockSpec((1, 1, 2, LANES), lambda bi, p: (bi, p, 0, 0)),
                  blk()],
        out_specs=blk(),
        out_shape=jax.ShapeDtypeStruct((b * s, FOX_W), BF16),
        compiler_params=_cparams(2), name="fox_sample_attention",
    )(q16, cache_k, cache_v, fk32, fv32, bias_c, bias_n, gate)


def _diff_sample_body(q_ref, kc_ref, vc_ref, kn_ref, vn_ref, bc_ref, bn_ref, gate_ref, sg_ref, lq1_ref,
                      lk1_ref, lq2_ref, lk2_ref, o_ref, *, s, lam_init):
    qbd = _block_diag_queries(q_ref[...])
    kc = kc_ref[0].astype(BF16)
    zrows = jnp.zeros((LANES - s, 2 * DIFF_HD), BF16)
    kn = jnp.concatenate([kn_ref[...].astype(BF16), zrows], axis=0)
    vn = jnp.concatenate([vn_ref[...].astype(BF16), zrows], axis=0)
    bc = bc_ref[0]
    bn = bn_ref[0]
    sc = _nt_dot(qbd, kc) + jnp.concatenate([bc, bc], axis=0)
    sn = _nt_dot(qbd, kn) + jnp.concatenate([bn, bn], axis=0)
    m = jnp.maximum(jnp.max(sc, axis=1, keepdims=True), jnp.max(sn, axis=1, keepdims=True))
    pc = jnp.exp2(sc - m)
    pn = jnp.exp2(sn - m)
    l = jnp.sum(pc, axis=1, keepdims=True) + jnp.sum(pn, axis=1, keepdims=True)
    lam = _lambda_value(lq1_ref[...], lk1_ref[...], lq2_ref[...], lk2_ref[...], lam_init)
    pc = pc / l
    pn = pn / l
    dc = (pc[:s] - lam * pc[s:]).astype(BF16)
    dn = (pn[:s] - lam * pn[s:]).astype(BF16)
    o = _dot(dc, vc_ref[0].astype(BF16)) + _dot(dn, vn)
    o_ref[...] = _subln_gate(o, sg_ref[...], gate_ref[...], lam_init)


def _diff_sample_attention(q16, cache_k, cache_v, dk32, dv32, bias_c, bias_n, gate, subln_g, lams, *, s,
                           lam_init):
    b, p_len, _ = cache_k.shape
    w = 2 * DIFF_HD
    fox_blocks = FOX_W // w
    blk = lambda: pl.BlockSpec((s, w), lambda bi, h: (bi, h))
    qblk = lambda: pl.BlockSpec((s, w), lambda bi, h: (bi, fox_blocks + h))
    cache = lambda: pl.BlockSpec((1, p_len, w), lambda bi, h: (bi, 0, h))
    vec = lambda width: pl.BlockSpec((1, width), lambda bi, h: (0, 0))
    return pl.pallas_call(
        functools.partial(_diff_sample_body, s=s, lam_init=lam_init),
        grid=(b, DIFF_HEADS),
        in_specs=[qblk(), cache(), cache(), blk(), blk(),
                  pl.BlockSpec((1, s, p_len), lambda bi, h: (h, 0, 0)),
                  pl.BlockSpec((1, s, LANES), lambda bi, h: (h, 0, 0)),
                  qblk(), vec(w), vec(DIFF_HD), vec(DIFF_HD), vec(DIFF_HD), vec(DIFF_HD)],
        out_specs=blk(),
        out_shape=jax.ShapeDtypeStruct((b * s, DIFF_W), BF16),
        compiler_params=_cparams(2), name="diff_sample_attention",
    )(q16, cache_k, cache_v, dk32, dv32, bias_c, bias_n, gate, subln_g, *lams)


def _outproj_body(mf_ref, md_ref, w_ref, x_ref, g_ref, y_ref):
    out = _dot(mf_ref[...], w_ref[:FOX_W, :]) + _dot(md_ref[...], w_ref[FOX_W:, :])
    ms = jnp.mean(out * out, axis=-1, keepdims=True)
    y_ref[...] = x_ref[...] + (out * lax.rsqrt(ms + EPS)) * g_ref[...]


def _out_projection(mf, md, w_out, x2d, g_post, *, tm):
    n, d = x2d.shape
    tm = min(tm, n)
    assert n % tm == 0
    rows = lambda w: pl.BlockSpec((tm, w), lambda i: (i, 0))
    return pl.pallas_call(
        _outproj_body, grid=(n // tm,),
        in_specs=[rows(FOX_W), rows(DIFF_W), pl.BlockSpec(w_out.shape, lambda i: (0, 0)), rows(d),
                  pl.BlockSpec((1, d), lambda i: (0, 0))],
        out_specs=rows(d), out_shape=jax.ShapeDtypeStruct((n, d), F32),
        compiler_params=_cparams(1), name="out_projection",
    )(mf, md, w_out.astype(BF16), x2d, g_post.reshape(1, d))


def _t5_bucket(rel):
    half = NUM_BUCKETS // 2
    max_exact = half // 2
    ret = jnp.where(rel > 0, half, 0)
    n = jnp.abs(rel)
    nf = jnp.maximum(n, 1).astype(jnp.float32)
    large = max_exact + (jnp.log(nf / max_exact) / math.log(MAX_DISTANCE / max_exact)
                         * (half - max_exact)).astype(jnp.int32)
    large = jnp.minimum(large, half - 1)
    return ret + jnp.where(n < max_exact, n, large)


def _diff_bias(rel_bias, q_pos, k_pos, far_pos):
    rel = k_pos[:, None] - q_pos[None, :]
    bias = jnp.moveaxis(rel_bias[_t5_bucket(rel)], -1, 0).astype(F32)
    if far_pos is not None:
        bias = bias - rel_bias[_t5_bucket(jnp.int32(far_pos))].astype(F32)[:, None, None]
    mask = (k_pos // CHUNK)[:, None] <= (q_pos // CHUNK)[None, :]
    return jnp.where(mask[None], bias * LOG2E, NEG_INF)


def _prompt_layer(x, g_pre, w_in, b_f, lams, subln_g, w_out, g_post, rel_bias, layer_idx, *, t=256, tm=512,
                  fox_pairs=4, diff_heads=4):
    b, s, d = x.shape
    assert b == 1 and s % t == 0 and s // LANES >= 1
    x2d = x.reshape(s, d)
    (fk, fv, dk, dv, logf, gate, qt, k16, dk16, vt) = _in_projection(
        x2d, g_pre, w_in, b_f, transposed=True, tm=tm, tk=t)

    rows = s // LANES
    _, hi, mid, lo = _prefix_sums(logf.T.reshape(FOX_HEADS * rows, LANES), seg=LANES, group=rows,
                                  rows_per_step=FOX_HEADS * rows, scale=LOG2E)
    pieces = jnp.stack([hi, mid, lo], axis=0).reshape(PIECES, FOX_HEADS, s)
    kext = jnp.concatenate([
        (-pieces).transpose(2, 1, 0).reshape(s, ONES_AT), jnp.ones((s, PIECES), BF16),
        jnp.zeros((s, LANES - ONES_AT - PIECES), BF16)], axis=1)
    sel = (jnp.arange(ONES_AT)[None, :] // PIECES == jnp.arange(FOX_HEADS)[:, None]).astype(BF16)
    qext = jnp.concatenate([
        jnp.broadcast_to(sel[:, :, None], (FOX_HEADS, ONES_AT, s)), pieces.transpose(1, 0, 2),
        jnp.zeros((FOX_HEADS, QEXT_ROWS - ONES_AT - PIECES, s), BF16)], axis=1)

    mf = _fox_prompt_attention(qt, qext, k16, kext, vt, gate, t=t, pairs=fox_pairs)

    lam_init = 0.8 - 0.6 * math.exp(-0.3 * layer_idx)
    pos = jnp.arange(t, dtype=jnp.int32)
    bias_diag = _diff_bias(rel_bias, pos, pos, -2 * t)
    bias_sub = _diff_bias(rel_bias, pos + t, pos, -2 * t)
    md = _diff_prompt_attention(qt, dk16, vt, bias_diag, bias_sub, gate, subln_g.reshape(1, -1), lams,
                                t=t, heads=diff_heads, lam_init=lam_init)

    y = _out_projection(mf, md, w_out, x2d, g_post, tm=tm)
    return y.reshape(b, s, d), (fk, fv, logf, dk, dv)


def _sample_layer(x, past, g_pre, w_in, b_f, lams, subln_g, w_out, g_post, rel_bias, layer_idx, *, tm=512):
    b, s, d = x.shape
    pk, pv, plogf, pdk, pdv = past
    p_len = pk.shape[1]
    assert s <= LANES and LANES % s == 0 and p_len % LANES == 0
    x2d = x.reshape(b * s, d)
    fk, fv, dk, dv, logf, gate, q16 = _in_projection(x2d, g_pre, w_in, b_f, transposed=False, tm=tm, tk=tm)

    rows = p_len // LANES
    cum_c, _, _, _ = _prefix_sums(plogf.transpose(0, 2, 1).reshape(b * FOX_HEADS * rows, LANES), seg=LANES,
                                  group=rows, rows_per_step=32 * rows, scale=1.0)
    cum_c = cum_c.reshape(b, FOX_HEADS, p_len)
    bias_c = ((cum_c[:, :, -1:] - cum_c) * LOG2E).reshape(b, FOX_HEADS // 2, 2, p_len)
    cum_n, _, _, _ = _prefix_sums(
        logf.reshape(b, s, FOX_HEADS).transpose(0, 2, 1).reshape(b * FOX_HEADS * s // LANES, LANES),
        seg=s, group=1, rows_per_step=1024, scale=1.0)
    bias_n = jnp.concatenate([-cum_n.reshape(b, FOX_HEADS, s) * LOG2E,
                              jnp.full((b, FOX_HEADS, LANES - s), NEG_INF, F32)], axis=-1)
    bias_n = bias_n.reshape(b, FOX_HEADS // 2, 2, LANES)
    mf = _fox_sample_attention(q16, pk.reshape(b, p_len, FOX_W), pv.reshape(b, p_len, FOX_W), fk, fv,
                               bias_c, bias_n, gate, s=s)

    lam_init = 0.8 - 0.6 * math.exp(-0.3 * layer_idx)
    q_pos = p_len + jnp.arange(s, dtype=jnp.int32)
    bd_c = _diff_bias(rel_bias, q_pos, jnp.arange(p_len, dtype=jnp.int32), None).transpose(0, 2, 1)
    bd_n = _diff_bias(rel_bias, q_pos, q_pos, None).transpose(0, 2, 1)
    bd_n = jnp.concatenate([bd_n, jnp.full((DIFF_HEADS, s, LANES - s), NEG_INF, F32)], axis=-1)
    md = _diff_sample_attention(q16, pdk.reshape(b, p_len, DIFF_W), pdv.reshape(b, p_len, DIFF_W), dk, dv,
                                bd_c, bd_n, gate, subln_g.reshape(1, -1), lams, s=s, lam_init=lam_init)

    y = _out_projection(mf, md, w_out, x2d, g_post, tm=tm)
    return y.reshape(b, s, d), (fk, fv, logf, dk, dv)


def kernel(x_prompt, x_sample, cache_fox_k, cache_fox_v, cache_fox_logf, cache_diff_k, cache_diff_v,
           norm_pre_g, w_in, forget_bias, lambda_q1, lambda_k1, lambda_q2, lambda_k2, subln_g, w_out,
           norm_post_g, rel_bias):
    depth = w_in.shape[0]
    y_p, y_s = x_prompt, x_sample
    rows_p, rows_s = [], []
    for l in range(depth):
        lams = tuple(a[l].reshape(1, -1) for a in (lambda_q1, lambda_k1, lambda_q2, lambda_k2))
        params = (norm_pre_g[l], w_in[l], forget_bias[l], lams, subln_g[l], w_out[l], norm_post_g[l], rel_bias, l)
        y_p, new_p = _prompt_layer(y_p, *params)
        past = (cache_fox_k[l], cache_fox_v[l], cache_fox_logf[l], cache_diff_k[l], cache_diff_v[l])
        y_s, new_s = _sample_layer(y_s, past, *params)
        rows_p.append(new_p)
        rows_s.append(new_s)

    bp, sp = x_prompt.shape[:2]
    bs, ss = x_sample.shape[:2]

    def stack(rows, idx, shape):
        return jnp.stack([r[idx].reshape(shape) for r in rows], axis=0)

    outs = [y_p, y_s]
    for rows, (b, s) in ((rows_p, (bp, sp)), (rows_s, (bs, ss))):
        outs += [stack(rows, 0, (b, s, FOX_HEADS, FOX_HD)), stack(rows, 1, (b, s, FOX_HEADS, FOX_HD)),
                 stack(rows, 2, (b, s, FOX_HEADS)), stack(rows, 3, (b, s, DIFF_HEADS, 2 * DIFF_HD)),
                 stack(rows, 4, (b, s, DIFF_HEADS, 2 * DIFF_HD))]
    return tuple(outs)
```

```python
import functools
import math

import jax
import jax.numpy as jnp
import numpy as np
from jax import lax
from jax.experimental import pallas as pl
from jax.experimental.pallas import tpu as pltpu

F32 = jnp.float32
BF16 = jnp.bfloat16

FOX_HEADS = 8
FOX_HD = 64
DIFF_HEADS = 4
DIFF_HD = 64
FOX_W = FOX_HEADS * FOX_HD
DIFF_W = DIFF_HEADS * 2 * DIFF_HD
CHUNK = 64
NUM_BUCKETS = 32
MAX_DISTANCE = 128
EPS = 1e-6
NEG_INF = -1e30
LOG2E = 1.4426950408889634

LANES = 128
HEAD_PAIR_W = 2 * FOX_HD
VMEM_LIMIT_BYTES = 56 * 1024 * 1024

PIECES = 3
ONES_AT = PIECES * FOX_HEADS
QEXT_ROWS = 32


def _cparams(n_axes):
    return pltpu.CompilerParams(
        dimension_semantics=("arbitrary",) * n_axes,
        vmem_limit_bytes=VMEM_LIMIT_BYTES,
    )


def _split3(a):
    hi = a.astype(BF16)
    r1 = a - hi.astype(F32)
    mid = r1.astype(BF16)
    lo = (r1 - mid.astype(F32)).astype(BF16)
    return hi, mid, lo


def _nt_dot(a, b):
    return lax.dot_general(a, b, (((1,), (1,)), ((), ())), preferred_element_type=F32)


def _dot(a, b):
    return jnp.dot(a, b, preferred_element_type=F32)


def _inproj_body(x_ref, g_ref, wkv_ref, wg_ref, wff_ref, bf_ref, wq_ref, wvt_ref,
                 fk_ref, fv_ref, dk_ref, dv_ref, logf_ref, gate_ref, q_ref, *rest,
                 transposed, tk, qscale):
    x = x_ref[...]
    ms = jnp.mean(x * x, axis=-1, keepdims=True)
    h = (x * lax.rsqrt(ms + EPS)) * g_ref[...]
    hb = h.astype(BF16)

    fk = _dot(hb, wkv_ref[:, 0 * FOX_W:1 * FOX_W])
    fk_ref[...] = fk
    fv_ref[...] = _dot(hb, wkv_ref[:, 1 * FOX_W:2 * FOX_W])
    dk = _dot(hb, wkv_ref[:, 2 * FOX_W:3 * FOX_W])
    dk_ref[...] = dk
    dv_ref[...] = _dot(hb, wkv_ref[:, 3 * FOX_W:4 * FOX_W])

    ff = _dot(hb, wff_ref[...])
    logf = jax.nn.log_sigmoid(ff + bf_ref[...])
    logf_ref[...] = logf[:, :FOX_HEADS]

    gates = _dot(hb, wg_ref[...])
    gate_ref[...] = (gates * jax.nn.sigmoid(gates)).astype(BF16)

    if transposed:
        k16_ref, dk16_ref, vt_ref = rest
        k16_ref[...] = fk.astype(BF16)
        dk16_ref[...] = dk.astype(BF16)
        q_ref[...] = (_nt_dot(wq_ref[...], hb) * qscale).astype(BF16)
        vt = _nt_dot(wvt_ref[...], hb).astype(BF16)
        for c in range(vt.shape[1] // tk):
            vt_ref[c] = vt[:, c * tk:(c + 1) * tk]
    else:
        q_ref[...] = (_dot(hb, wq_ref[...]) * qscale).astype(BF16)


def _in_projection(x2d, g_pre, w_in, b_f, *, transposed, tm, tk):
    n, d = x2d.shape
    tm = min(tm, n)
    assert n % tm == 0 and (not transposed or tm % tk == 0)
    qscale = FOX_HD ** -0.5 * LOG2E
    o = np.cumsum([0, FOX_W, FOX_W, FOX_W, FOX_HEADS, FOX_W, DIFF_W, DIFF_W, DIFF_W, DIFF_W])
    w_fq, w_fk, w_fv, w_ff, w_fg, w_dq, w_dk, w_dv, w_dg = [w_in[:, o[i]:o[i + 1]] for i in range(9)]
    wkv = jnp.concatenate([w_fk, w_fv, w_dk, w_dv], axis=1).astype(BF16)
    wg = jnp.concatenate([w_fg, w_dg], axis=1).astype(BF16)
    wff = jnp.pad(w_ff, ((0, 0), (0, LANES - FOX_HEADS))).astype(BF16)
    bfp = jnp.pad(b_f.reshape(1, FOX_HEADS), ((0, 0), (0, LANES - FOX_HEADS)))
    wq = jnp.concatenate([w_fq, w_dq], axis=1)
    wv = jnp.concatenate([w_fv, w_dv], axis=1)
    if transposed:
        wq = wq.T.astype(BF16)
        wvt = wv.T.astype(BF16)
    else:
        wq = wq.astype(BF16)
        wvt = jnp.zeros((8, LANES), BF16)

    const = lambda shape: pl.BlockSpec(shape, lambda i: (0,) * len(shape))
    rows = lambda w: pl.BlockSpec((tm, w), lambda i: (i, 0))
    in_specs = [rows(d), const((1, d)), const(wkv.shape), const(wg.shape), const(wff.shape),
                const(bfp.shape), const(wq.shape), const(wvt.shape)]
    out_shape = [jax.ShapeDtypeStruct((n, FOX_W), F32)] * 2 + [jax.ShapeDtypeStruct((n, DIFF_W), F32)] * 2
    out_shape += [jax.ShapeDtypeStruct((n, FOX_HEADS), F32), jax.ShapeDtypeStruct((n, FOX_W + DIFF_W), BF16)]
    out_specs = [rows(FOX_W), rows(FOX_W), rows(DIFF_W), rows(DIFF_W), rows(FOX_HEADS), rows(FOX_W + DIFF_W)]
    if transposed:
        out_shape += [jax.ShapeDtypeStruct((FOX_W + DIFF_W, n), BF16),
                      jax.ShapeDtypeStruct((n, FOX_W), BF16), jax.ShapeDtypeStruct((n, DIFF_W), BF16),
                      jax.ShapeDtypeStruct((n // tk, FOX_W + DIFF_W, tk), BF16)]
        out_specs += [pl.BlockSpec((FOX_W + DIFF_W, tm), lambda i: (0, i)), rows(FOX_W), rows(DIFF_W),
                      pl.BlockSpec((tm // tk, FOX_W + DIFF_W, tk), lambda i: (i, 0, 0))]
    else:
        out_shape += [jax.ShapeDtypeStruct((n, FOX_W + DIFF_W), BF16)]
        out_specs += [rows(FOX_W + DIFF_W)]
    return pl.pallas_call(
        functools.partial(_inproj_body, transposed=transposed, tk=tk, qscale=qscale),
        grid=(n // tm,), in_specs=in_specs, out_specs=out_specs, out_shape=out_shape,
        compiler_params=_cparams(1), name="in_projection_t" if transposed else "in_projection_r",
    )(x2d, g_pre.reshape(1, d), wkv, wg, wff, bfp, wq, wvt)


def _cumsum_body(x_ref, cum_ref, hi_ref, mid_ref, lo_ref, *, seg, group, scale):
    x = x_ref[...]
    r = x.shape[0]
    a = lax.broadcasted_iota(jnp.int32, (LANES, LANES), 0)
    b = lax.broadcasted_iota(jnp.int32, (LANES, LANES), 1)
    sh = int(math.log2(seg))
    upper = ((a <= b) & ((a >> sh) == (b >> sh))).astype(BF16)
    pieces = _split3(x)
    w = sum(_dot(p, upper) for p in pieces)
    if group > 1:
        ones = jnp.ones((LANES, LANES), BF16)
        tot = sum(_dot(p, ones) for p in pieces)
        ri = lax.broadcasted_iota(jnp.int32, (r, r), 0)
        ci = lax.broadcasted_iota(jnp.int32, (r, r), 1)
        gsh = int(math.log2(group))
        lower = ((ci < ri) & ((ri >> gsh) == (ci >> gsh))).astype(BF16)
        w = w + sum(_dot(lower, p) for p in _split3(tot))
    cum_ref[...] = w
    hi, mid, lo = _split3(w * scale)
    hi_ref[...] = hi
    mid_ref[...] = mid
    lo_ref[...] = lo


def _prefix_sums(x2d, *, seg, group, rows_per_step, scale):
    r = x2d.shape[0]
    rb = min(rows_per_step, r)
    assert r % rb == 0 and rb % group == 0 and (seg == LANES or group == 1)
    spec = pl.BlockSpec((rb, LANES), lambda i: (i, 0))
    return pl.pallas_call(
        functools.partial(_cumsum_body, seg=seg, group=group, scale=scale),
        grid=(r // rb,), in_specs=[spec], out_specs=[spec] * 4,
        out_shape=[jax.ShapeDtypeStruct((r, LANES), F32)] + [jax.ShapeDtypeStruct((r, LANES), BF16)] * 3,
        compiler_params=_cparams(1), name="prefix_sums",
    )(x2d)


def _online_update(s, h, m_ref, l_ref, acc_ref, vt):
    m = m_ref[h]
    mn = jnp.maximum(m, jnp.max(s, axis=0, keepdims=True))
    alpha = jnp.exp2(m - mn)
    p = jnp.exp2(s - mn)
    l_ref[h] = alpha * l_ref[h] + jnp.sum(p, axis=0, keepdims=True)
    acc_ref[h] = alpha * acc_ref[h] + _dot(vt, p.astype(BF16))
    m_ref[h] = mn


def _init_state(m_ref, l_ref, acc_ref):
    m_ref[...] = jnp.full(m_ref.shape, NEG_INF, F32)
    l_ref[...] = jnp.zeros(l_ref.shape, F32)
    acc_ref[...] = jnp.zeros(acc_ref.shape, F32)


def _resident(shape, index_map):
    return pl.BlockSpec(shape, index_map, pipeline_mode=pl.Buffered(1))


def _fox_prompt_body(q_ref, qx_ref, k_ref, kx_ref, vt_ref, gate_ref, o_ref, m_ref, l_ref, acc_ref, *, t, pairs):
    i = pl.program_id(1)
    _init_state(m_ref, l_ref, acc_ref)
    z64 = jnp.zeros((FOX_HD, t), BF16)
    zpad = jnp.zeros((LANES - QEXT_ROWS, t), BF16)
    qa = []
    for p in range(pairs):
        q = q_ref[p * HEAD_PAIR_W:(p + 1) * HEAD_PAIR_W, :]
        qa.append(jnp.concatenate([q[:FOX_HD], z64, qx_ref[2 * p], zpad], axis=0))
        qa.append(jnp.concatenate([z64, q[FOX_HD:], qx_ref[2 * p + 1], zpad], axis=0))

    def step(j, masked):
        off = pl.multiple_of(j * t, t)
        kx = kx_ref[pl.ds(off, t), :]
        scores = []
        for p in range(pairs):
            ka = jnp.concatenate([k_ref[pl.ds(off, t), p * HEAD_PAIR_W:(p + 1) * HEAD_PAIR_W], kx], axis=1)
            scores += [_dot(ka, qa[2 * p]), _dot(ka, qa[2 * p + 1])]
        for h, s in enumerate(scores):
            if masked:
                kk = lax.broadcasted_iota(jnp.int32, (t, t), 0)
                qq = lax.broadcasted_iota(jnp.int32, (t, t), 1)
                s = jnp.where(kk <= qq, s, NEG_INF)
            _online_update(s, h, m_ref, l_ref, acc_ref, vt_ref[j, h * FOX_HD:(h + 1) * FOX_HD, :])

    def far_step(j, c):
        step(j, False)
        return c

    lax.fori_loop(0, i, far_step, 0)
    step(i, True)
    for p in range(pairs):
        ot = jnp.concatenate([acc_ref[2 * p] / l_ref[2 * p], acc_ref[2 * p + 1] / l_ref[2 * p + 1]], axis=0)
        cols = slice(p * HEAD_PAIR_W, (p + 1) * HEAD_PAIR_W)
        o_ref[:, cols] = (ot.T * gate_ref[:, cols].astype(F32)).astype(BF16)


def _fox_prompt_attention(qt, qext, k16, kext, vt, gate, *, t, pairs):
    n = k16.shape[0]
    nq = n // t
    w = pairs * HEAD_PAIR_W
    heads = 2 * pairs
    assert (FOX_HEADS // 2) % pairs == 0
    return pl.pallas_call(
        functools.partial(_fox_prompt_body, t=t, pairs=pairs),
        grid=(FOX_HEADS // heads, nq),
        in_specs=[pl.BlockSpec((w, t), lambda g, i: (g, i)),
                  pl.BlockSpec((heads, QEXT_ROWS, t), lambda g, i: (g, 0, i)),
                  _resident((n, w), lambda g, i: (0, g)),
                  _resident((n, LANES), lambda g, i: (0, 0)),
                  _resident((nq, w, t), lambda g, i: (0, g, 0)),
                  pl.BlockSpec((t, w), lambda g, i: (i, g))],
        out_specs=pl.BlockSpec((t, w), lambda g, i: (i, g)),
        out_shape=jax.ShapeDtypeStruct((n, FOX_W), BF16),
        scratch_shapes=[pltpu.VMEM((heads, 1, t), F32), pltpu.VMEM((heads, 1, t), F32),
                        pltpu.VMEM((heads, FOX_HD, t), F32)],
        compiler_params=_cparams(2), name="fox_prompt_attention",
    )(qt, qext, k16, kext, vt, gate)


def _lambda_value(lq1, lk1, lq2, lk2, lam_init):
    a = jnp.sum(lq1 * lk1, axis=-1, keepdims=True)
    b = jnp.sum(lq2 * lk2, axis=-1, keepdims=True)
    return jnp.exp(a) - jnp.exp(b) + lam_init


def _subln_gate(y, g, gate, lam_init):
    ms = jnp.mean(y * y, axis=-1, keepdims=True)
    yn = (y * lax.rsqrt(ms + EPS)) * g
    return ((yn * (1.0 - lam_init)) * gate.astype(F32)).astype(BF16)


def _diff_prompt_body(q_ref, k_ref, vt_ref, bd_ref, bs_ref, gate_ref, sg_ref, lq1_ref, lk1_ref, lq2_ref,
                      lk2_ref, o_ref, m_ref, l_ref, acc_ref, *, t, heads, lam_init):
    i = pl.program_id(1)
    w = 2 * DIFF_HD
    _init_state(m_ref, l_ref, acc_ref)
    z64 = jnp.zeros((DIFF_HD, t), BF16)
    qa = []
    for h in range(heads):
        q = q_ref[h * w:(h + 1) * w, :]
        qa.append(jnp.concatenate([q[:DIFF_HD], z64], axis=0))
        qa.append(jnp.concatenate([z64, q[DIFF_HD:]], axis=0))

    def step(j, bias_ref):
        off = pl.multiple_of(j * t, t)
        scores = []
        for h in range(heads):
            kk = k_ref[pl.ds(off, t), h * w:(h + 1) * w]
            scores += [_dot(kk, qa[2 * h]), _dot(kk, qa[2 * h + 1])]
        for c, s in enumerate(scores):
            h = c // 2
            if bias_ref is not None:
                s = s + bias_ref[h]
            _online_update(s, c, m_ref, l_ref, acc_ref, vt_ref[j, h * w:(h + 1) * w, :])

    def far_step(j, c):
        step(j, None)
        return c

    lax.fori_loop(0, jnp.maximum(i - 1, 0), far_step, 0)

    @pl.when(i >= 1)
    def _():
        step(i - 1, bs_ref)

    step(i, bd_ref)
    lam = _lambda_value(lq1_ref[...], lk1_ref[...], lq2_ref[...], lk2_ref[...], lam_init)
    for h in range(heads):
        ot = acc_ref[2 * h] / l_ref[2 * h] - lam * (acc_ref[2 * h + 1] / l_ref[2 * h + 1])
        cols = slice(h * w, (h + 1) * w)
        o_ref[:, cols] = _subln_gate(ot.T, sg_ref[...], gate_ref[:, cols], lam_init)


def _diff_prompt_attention(qt, dk16, vt, bias_diag, bias_sub, gate, subln_g, lams, *, t, heads, lam_init):
    n = dk16.shape[0]
    nq = n // t
    w = heads * 2 * DIFF_HD
    assert DIFF_HEADS % heads == 0 and FOX_W % w == 0
    fox_blocks = FOX_W // w
    vec = lambda width: pl.BlockSpec((1, width), lambda g, i: (0, 0))
    return pl.pallas_call(
        functools.partial(_diff_prompt_body, t=t, heads=heads, lam_init=lam_init),
        grid=(DIFF_HEADS // heads, nq),
        in_specs=[pl.BlockSpec((w, t), lambda g, i: (fox_blocks + g, i)),
                  _resident((n, w), lambda g, i: (0, g)),
                  _resident((nq, w, t), lambda g, i: (0, fox_blocks + g, 0)),
                  pl.BlockSpec((heads, t, t), lambda g, i: (g, 0, 0)),
                  pl.BlockSpec((heads, t, t), lambda g, i: (g, 0, 0)),
                  pl.BlockSpec((t, w), lambda g, i: (i, fox_blocks + g)),
                  vec(2 * DIFF_HD), vec(DIFF_HD), vec(DIFF_HD), vec(DIFF_HD), vec(DIFF_HD)],
        out_specs=pl.BlockSpec((t, w), lambda g, i: (i, g)),
        out_shape=jax.ShapeDtypeStruct((n, DIFF_W), BF16),
        scratch_shapes=[pltpu.VMEM((2 * heads, 1, t), F32), pltpu.VMEM((2 * heads, 1, t), F32),
                        pltpu.VMEM((2 * heads, 2 * DIFF_HD, t), F32)],
        compiler_params=_cparams(2), name="diff_prompt_attention",
    )(qt, dk16, vt, bias_diag, bias_sub, gate, subln_g, *lams)


def _block_diag_queries(qb):
    lane = lax.broadcasted_iota(jnp.int32, qb.shape, 1)
    zero = jnp.zeros_like(qb)
    return jnp.concatenate([jnp.where(lane < FOX_HD, qb, zero), jnp.where(lane >= FOX_HD, qb, zero)], axis=0)


def _fox_sample_body(q_ref, kc_ref, vc_ref, kn_ref, vn_ref, bc_ref, bn_ref, gate_ref, o_ref, *, s):
    qbd = _block_diag_queries(q_ref[...])
    p_len = kc_ref.shape[-1]
    kt = kc_ref[0].reshape(HEAD_PAIR_W, p_len).astype(BF16)
    vt = vc_ref[0].reshape(HEAD_PAIR_W, p_len).astype(BF16)
    zrows = jnp.zeros((LANES - s, HEAD_PAIR_W), BF16)
    kn = jnp.concatenate([kn_ref[...].astype(BF16), zrows], axis=0)
    vn = jnp.concatenate([vn_ref[...].astype(BF16), zrows], axis=0)
    bc = bc_ref[0, 0]
    bn = bn_ref[0, 0]

    def rows_bias(b, width):
        return jnp.concatenate([jnp.broadcast_to(b[0:1], (s, width)), jnp.broadcast_to(b[1:2], (s, width))], axis=0)

    sc = _dot(qbd, kt) + rows_bias(bc, p_len)
    sn = _nt_dot(qbd, kn) + rows_bias(bn, LANES)
    qi = lax.broadcasted_iota(jnp.int32, (2 * s, LANES), 0)
    ki = lax.broadcasted_iota(jnp.int32, (2 * s, LANES), 1)
    qi = jnp.where(qi >= s, qi - s, qi)
    sn = jnp.where(ki <= qi, sn, NEG_INF)
    m = jnp.maximum(jnp.max(sc, axis=1, keepdims=True), jnp.max(sn, axis=1, keepdims=True))
    pc = jnp.exp2(sc - m)
    pn = jnp.exp2(sn - m)
    l = jnp.sum(pc, axis=1, keepdims=True) + jnp.sum(pn, axis=1, keepdims=True)
    o = (_nt_dot(pc.astype(BF16), vt) + _dot(pn.astype(BF16), vn)) / l
    lane = lax.broadcasted_iota(jnp.int32, (s, HEAD_PAIR_W), 1)
    out = jnp.where(lane < FOX_HD, o[:s], o[s:])
    o_ref[...] = (out * gate_ref[...].astype(F32)).astype(BF16)


def _fox_sample_attention(q16, cache_kt, cache_vt, fk32, fv32, bias_c, bias_n, gate, *, s):
    b, _, _, p_len = cache_kt.shape
    pairs = FOX_HEADS // 2
    blk = lambda: pl.BlockSpec((s, HEAD_PAIR_W), lambda bi, p: (bi, p))
    cache = lambda: pl.BlockSpec((1, 2, FOX_HD, p_len), lambda bi, p: (bi, p, 0, 0))
    return pl.pallas_call(
        functools.partial(_fox_sample_body, s=s),
        grid=(b, pairs),
        in_specs=[blk(), cache(), cache(), blk(), blk(),
                  pl.BlockSpec((1, 1, 2, p_len), lambda bi, p: (bi, p, 0, 0)),
                  pl.BlockSpec((1, 1, 2, LANES), lambda bi, p: (bi, p, 0, 0)),
                  blk()],
        out_specs=blk(),
        out_shape=jax.ShapeDtypeStruct((b * s, FOX_W), BF16),
        compiler_params=_cparams(2), name="fox_sample_attention",
    )(q16, cache_kt, cache_vt, fk32, fv32, bias_c, bias_n, gate)


def _diff_sample_body(q_ref, kc_ref, vc_ref, kn_ref, vn_ref, bc_ref, bn_ref, gate_ref, sg_ref, lq1_ref,
                      lk1_ref, lq2_ref, lk2_ref, o_ref, m_ref, l_ref, acc_ref, *, s, pc, lam_init):
    c = pl.program_id(1)
    last = pl.num_programs(1) - 1
    w = 2 * DIFF_HD

    @pl.when(c == 0)
    def _():
        _init_state(m_ref, l_ref, acc_ref)

    qbd = [_block_diag_queries(q_ref[:, h * w:(h + 1) * w]) for h in range(DIFF_HEADS)]

    def update(h, sc, v16):
        m = m_ref[h]
        mn = jnp.maximum(m, jnp.max(sc, axis=1, keepdims=True))
        alpha = jnp.exp2(m - mn)
        p = jnp.exp2(sc - mn)
        l_ref[h] = alpha * l_ref[h] + jnp.sum(p, axis=1, keepdims=True)
        acc_ref[h] = alpha * acc_ref[h] + _dot(p.astype(BF16), v16)
        m_ref[h] = mn

    def chunk(with_bias):
        scores = [_nt_dot(qbd[h], kc_ref[0, pl.ds(h, pc, stride=DIFF_HEADS), :].astype(BF16))
                  for h in range(DIFF_HEADS)]
        for h, sc in enumerate(scores):
            if with_bias:
                sc = sc + jnp.concatenate([bc_ref[h], bc_ref[h]], axis=0)
            update(h, sc, vc_ref[0, pl.ds(h, pc, stride=DIFF_HEADS), :].astype(BF16))

    @pl.when(c != last)
    def _():
        chunk(False)

    @pl.when(c == last)
    def _():
        chunk(True)
        lam = _lambda_value(lq1_ref[...], lk1_ref[...], lq2_ref[...], lk2_ref[...], lam_init)
        zrows = jnp.zeros((LANES - s, w), BF16)
        for h in range(DIFF_HEADS):
            cols = slice(h * w, (h + 1) * w)
            kn = jnp.concatenate([kn_ref[:, cols].astype(BF16), zrows], axis=0)
            vn = jnp.concatenate([vn_ref[:, cols].astype(BF16), zrows], axis=0)
            sn = _nt_dot(qbd[h], kn) + jnp.concatenate([bn_ref[h], bn_ref[h]], axis=0)
            update(h, sn, vn)
            o = acc_ref[h] / l_ref[h]
            o = o[:s] - lam * o[s:]
            o_ref[:, cols] = _subln_gate(o, sg_ref[...], gate_ref[:, cols], lam_init)


def _diff_sample_attention(q16, cache_k, cache_v, dk32, dv32, bias_c, bias_n, gate, subln_g, lams, *, s, pc,
                           lam_init):
    b, rows, w = cache_k.shape
    p_len = rows // DIFF_HEADS
    assert p_len % pc == 0 and bias_c.shape == (DIFF_HEADS, s, pc)
    cache = lambda: pl.BlockSpec((1, DIFF_HEADS * pc, w), lambda bi, c: (bi, c, 0))
    new = lambda: pl.BlockSpec((s, DIFF_W), lambda bi, c: (bi, 0))
    half = lambda: pl.BlockSpec((s, DIFF_W), lambda bi, c: (bi, 1))
    const = lambda shape: pl.BlockSpec(shape, lambda bi, c: (0,) * len(shape))
    return pl.pallas_call(
        functools.partial(_diff_sample_body, s=s, pc=pc, lam_init=lam_init),
        grid=(b, p_len // pc),
        in_specs=[half(), cache(), cache(), new(), new(), const(bias_c.shape), const(bias_n.shape), half(),
                  const((1, w)), const((1, DIFF_HD)), const((1, DIFF_HD)), const((1, DIFF_HD)), const((1, DIFF_HD))],
        out_specs=new(),
        out_shape=jax.ShapeDtypeStruct((b * s, DIFF_W), BF16),
        scratch_shapes=[pltpu.VMEM((DIFF_HEADS, 2 * s, 1), F32), pltpu.VMEM((DIFF_HEADS, 2 * s, 1), F32),
                        pltpu.VMEM((DIFF_HEADS, 2 * s, w), F32)],
        compiler_params=_cparams(2), name="diff_sample_attention",
    )(q16, cache_k, cache_v, dk32, dv32, bias_c, bias_n, gate, subln_g, *lams)


def _outproj_body(mf_ref, md_ref, w_ref, x_ref, g_ref, y_ref):
    out = _dot(mf_ref[...], w_ref[:FOX_W, :]) + _dot(md_ref[...], w_ref[FOX_W:, :])
    ms = jnp.mean(out * out, axis=-1, keepdims=True)
    y_ref[...] = x_ref[...] + (out * lax.rsqrt(ms + EPS)) * g_ref[...]


def _out_projection(mf, md, w_out, x2d, g_post, *, tm):
    n, d = x2d.shape
    tm = min(tm, n)
    assert n % tm == 0
    rows = lambda w: pl.BlockSpec((tm, w), lambda i: (i, 0))
    return pl.pallas_call(
        _outproj_body, grid=(n // tm,),
        in_specs=[rows(FOX_W), rows(DIFF_W), pl.BlockSpec(w_out.shape, lambda i: (0, 0)), rows(d),
                  pl.BlockSpec((1, d), lambda i: (0, 0))],
        out_specs=rows(d), out_shape=jax.ShapeDtypeStruct((n, d), F32),
        compiler_params=_cparams(1), name="out_projection",
    )(mf, md, w_out.astype(BF16), x2d, g_post.reshape(1, d))


def _t5_bucket(rel):
    half = NUM_BUCKETS // 2
    max_exact = half // 2
    ret = jnp.where(rel > 0, half, 0)
    n = jnp.abs(rel)
    nf = jnp.maximum(n, 1).astype(jnp.float32)
    large = max_exact + (jnp.log(nf / max_exact) / math.log(MAX_DISTANCE / max_exact)
                         * (half - max_exact)).astype(jnp.int32)
    large = jnp.minimum(large, half - 1)
    return ret + jnp.where(n < max_exact, n, large)


def _rel_bias_of(rel_bias, rel):
    return rel_bias[_t5_bucket(rel)].astype(F32).T


def _toeplitz(g, rows, cols):
    h = g.shape[0]
    period = rows + cols
    v = jnp.concatenate([g[:, rows - 1:], jnp.zeros((h, 1), g.dtype), g[:, :rows - 1]], axis=1)
    flat = jnp.tile(v, (1, rows))[:, :rows * (period - 1)]
    return flat.reshape(h, rows, period - 1)[:, :, :cols]


def _chunk_mask(bias, row_pos, col_pos, rows_are_keys):
    r = (row_pos // CHUNK)[:, None]
    c = (col_pos // CHUNK)[None, :]
    mask = (r <= c) if rows_are_keys else (c <= r)
    return jnp.where(mask[None], bias * LOG2E, NEG_INF)


def _prompt_layer(x, g_pre, w_in, b_f, lams, subln_g, w_out, g_post, rel_bias, layer_idx, *, t=256, tm=512,
                  fox_pairs=4, diff_heads=4):
    b, s, d = x.shape
    assert b == 1 and s % t == 0 and s // LANES >= 1
    x2d = x.reshape(s, d)
    (fk, fv, dk, dv, logf, gate, qt, k16, dk16, vt) = _in_projection(
        x2d, g_pre, w_in, b_f, transposed=True, tm=tm, tk=t)

    rows = s // LANES
    _, hi, mid, lo = _prefix_sums(logf.T.reshape(FOX_HEADS * rows, LANES), seg=LANES, group=rows,
                                  rows_per_step=FOX_HEADS * rows, scale=LOG2E)
    pieces = jnp.stack([hi, mid, lo], axis=0).reshape(PIECES, FOX_HEADS, s)
    kext = jnp.concatenate([
        (-pieces).transpose(2, 1, 0).reshape(s, ONES_AT), jnp.ones((s, PIECES), BF16),
        jnp.zeros((s, LANES - ONES_AT - PIECES), BF16)], axis=1)
    sel = (jnp.arange(ONES_AT)[None, :] // PIECES == jnp.arange(FOX_HEADS)[:, None]).astype(BF16)
    qext = jnp.concatenate([
        jnp.broadcast_to(sel[:, :, None], (FOX_HEADS, ONES_AT, s)), pieces.transpose(1, 0, 2),
        jnp.zeros((FOX_HEADS, QEXT_ROWS - ONES_AT - PIECES, s), BF16)], axis=1)

    mf = _fox_prompt_attention(qt, qext, k16, kext, vt, gate, t=t, pairs=fox_pairs)

    lam_init = 0.8 - 0.6 * math.exp(-0.3 * layer_idx)
    assert t >= MAX_DISTANCE
    pos = jnp.arange(t, dtype=jnp.int32)
    off = jnp.arange(-(t - 1), t, dtype=jnp.int32)
    far = _rel_bias_of(rel_bias, jnp.full((1,), -2 * t, jnp.int32))
    bias_diag = _chunk_mask(_toeplitz(_rel_bias_of(rel_bias, -off) - far, t, t), pos, pos, True)
    bias_sub = _chunk_mask(_toeplitz(_rel_bias_of(rel_bias, -off - t) - far, t, t), pos, pos + t, True)
    md = _diff_prompt_attention(qt, dk16, vt, bias_diag, bias_sub, gate, subln_g.reshape(1, -1), lams,
                                t=t, heads=diff_heads, lam_init=lam_init)

    y = _out_projection(mf, md, w_out, x2d, g_post, tm=tm)
    return y.reshape(b, s, d), (fk, fv, logf, dk, dv)


def _sample_layer(x, past, g_pre, w_in, b_f, lams, subln_g, w_out, g_post, rel_bias, layer_idx, *, tm=512,
                  diff_chunk=1024):
    b, s, d = x.shape
    pk, pv, plogf, pdk, pdv = past
    p_len = pk.shape[1]
    assert s <= LANES and LANES % s == 0 and p_len % LANES == 0
    x2d = x.reshape(b * s, d)
    fk, fv, dk, dv, logf, gate, q16 = _in_projection(x2d, g_pre, w_in, b_f, transposed=False, tm=tm, tk=tm)

    rows = p_len // LANES
    cum_c, _, _, _ = _prefix_sums(plogf.transpose(0, 2, 1).reshape(b * FOX_HEADS * rows, LANES), seg=LANES,
                                  group=rows, rows_per_step=32 * rows, scale=1.0)
    cum_c = cum_c.reshape(b, FOX_HEADS, p_len)
    bias_c = ((cum_c[:, :, -1:] - cum_c) * LOG2E).reshape(b, FOX_HEADS // 2, 2, p_len)
    cum_n, _, _, _ = _prefix_sums(
        logf.reshape(b, s, FOX_HEADS).transpose(0, 2, 1).reshape(b * FOX_HEADS * s // LANES, LANES),
        seg=s, group=1, rows_per_step=1024, scale=1.0)
    bias_n = jnp.concatenate([-cum_n.reshape(b, FOX_HEADS, s) * LOG2E,
                              jnp.full((b, FOX_HEADS, LANES - s), NEG_INF, F32)], axis=-1)
    bias_n = bias_n.reshape(b, FOX_HEADS // 2, 2, LANES)
    mf = _fox_sample_attention(q16, pk.transpose(0, 2, 3, 1), pv.transpose(0, 2, 3, 1), fk, fv, bias_c, bias_n,
                               gate, s=s)

    lam_init = 0.8 - 0.6 * math.exp(-0.3 * layer_idx)
    pc = min(diff_chunk, p_len)
    assert pc >= MAX_DISTANCE + s
    q_pos = p_len + jnp.arange(s, dtype=jnp.int32)
    far = _rel_bias_of(rel_bias, jnp.full((1,), -2 * p_len, jnp.int32))
    off = jnp.arange(-(s - 1), pc, dtype=jnp.int32)
    bd_c = _chunk_mask(_toeplitz(_rel_bias_of(rel_bias, off - pc) - far, s, pc), q_pos,
                       p_len - pc + jnp.arange(pc, dtype=jnp.int32), False)
    off = jnp.arange(-(s - 1), s, dtype=jnp.int32)
    bd_n = _chunk_mask(_toeplitz(_rel_bias_of(rel_bias, off) - far, s, s), q_pos, q_pos, False)
    bd_n = jnp.concatenate([bd_n, jnp.full((DIFF_HEADS, s, LANES - s), NEG_INF, F32)], axis=-1)
    md = _diff_sample_attention(q16, pdk.reshape(b, p_len * DIFF_HEADS, 2 * DIFF_HD),
                                pdv.reshape(b, p_len * DIFF_HEADS, 2 * DIFF_HD), dk, dv, bd_c, bd_n, gate,
                                subln_g.reshape(1, -1), lams, s=s, pc=pc, lam_init=lam_init)

    y = _out_projection(mf, md, w_out, x2d, g_post, tm=tm)
    return y.reshape(b, s, d), (fk, fv, logf, dk, dv)


def kernel(x_prompt, x_sample, cache_fox_k, cache_fox_v, cache_fox_logf, cache_diff_k, cache_diff_v,
           norm_pre_g, w_in, forget_bias, lambda_q1, lambda_k1, lambda_q2, lambda_k2, subln_g, w_out,
           norm_post_g, rel_bias):
    depth = w_in.shape[0]
    y_p, y_s = x_prompt, x_sample
    rows_p, rows_s = [], []
    for l in range(depth):
        lams = tuple(a[l].reshape(1, -1) for a in (lambda_q1, lambda_k1, lambda_q2, lambda_k2))
        params = (norm_pre_g[l], w_in[l], forget_bias[l], lams, subln_g[l], w_out[l], norm_post_g[l], rel_bias, l)
        y_p, new_p = _prompt_layer(y_p, *params)
        past = (cache_fox_k[l], cache_fox_v[l], cache_fox_logf[l], cache_diff_k[l], cache_diff_v[l])
        y_s, new_s = _sample_layer(y_s, past, *params)
        rows_p.append(new_p)
        rows_s.append(new_s)

    bp, sp = x_prompt.shape[:2]
    bs, ss = x_sample.shape[:2]

    def stack(rows, idx, shape):
        return jnp.stack([r[idx].reshape(shape) for r in rows], axis=0)

    outs = [y_p, y_s]
    for rows, (b, s) in ((rows_p, (bp, sp)), (rows_s, (bs, ss))):
        outs += [stack(rows, 0, (b, s, FOX_HEADS, FOX_HD)), stack(rows, 1, (b, s, FOX_HEADS, FOX_HD)),
                 stack(rows, 2, (b, s, FOX_HEADS)), stack(rows, 3, (b, s, DIFF_HEADS, 2 * DIFF_HD)),
                 stack(rows, 4, (b, s, DIFF_HEADS, 2 * DIFF_HD))]
    return tuple(outs)
```

```python
import functools
import math

import jax
import jax.numpy as jnp
import numpy as np
from jax import lax
from jax.experimental import pallas as pl
from jax.experimental.pallas import tpu as pltpu

F32 = jnp.float32
BF16 = jnp.bfloat16

FOX_HEADS = 8
FOX_HD = 64
DIFF_HEADS = 4
DIFF_HD = 64
FOX_W = FOX_HEADS * FOX_HD
DIFF_W = DIFF_HEADS * 2 * DIFF_HD
CHUNK = 64
NUM_BUCKETS = 32
MAX_DISTANCE = 128
EPS = 1e-6
NEG_INF = -1e30
LOG2E = 1.4426950408889634

LANES = 128
HEAD_PAIR_W = 2 * FOX_HD
VMEM_LIMIT_BYTES = 56 * 1024 * 1024

PIECES = 3
ONES_AT = PIECES * FOX_HEADS
QEXT_ROWS = 32


def _cparams(n_axes):
    return pltpu.CompilerParams(
        dimension_semantics=("arbitrary",) * n_axes,
        vmem_limit_bytes=VMEM_LIMIT_BYTES,
    )


def _split3(a):
    hi = a.astype(BF16)
    r1 = a - hi.astype(F32)
    mid = r1.astype(BF16)
    lo = (r1 - mid.astype(F32)).astype(BF16)
    return hi, mid, lo


def _nt_dot(a, b):
    return lax.dot_general(a, b, (((1,), (1,)), ((), ())), preferred_element_type=F32)


def _dot(a, b):
    return jnp.dot(a, b, preferred_element_type=F32)


def _inproj_body(x_ref, g_ref, wkv_ref, wg_ref, wff_ref, bf_ref, wq_ref, wvt_ref,
                 fk_ref, fv_ref, dk_ref, dv_ref, logf_ref, gate_ref, q_ref, *rest,
                 transposed, tk, qscale):
    x = x_ref[...]
    ms = jnp.mean(x * x, axis=-1, keepdims=True)
    h = (x * lax.rsqrt(ms + EPS)) * g_ref[...]
    hb = h.astype(BF16)

    fk = _dot(hb, wkv_ref[:, 0 * FOX_W:1 * FOX_W])
    fk_ref[...] = fk
    fv_ref[...] = _dot(hb, wkv_ref[:, 1 * FOX_W:2 * FOX_W])
    dk = _dot(hb, wkv_ref[:, 2 * FOX_W:3 * FOX_W])
    dk_ref[...] = dk
    dv_ref[...] = _dot(hb, wkv_ref[:, 3 * FOX_W:4 * FOX_W])

    ff = _dot(hb, wff_ref[...])
    logf = jax.nn.log_sigmoid(ff + bf_ref[...])
    logf_ref[...] = logf[:, :FOX_HEADS]

    gates = _dot(hb, wg_ref[...])
    gate_ref[...] = (gates * jax.nn.sigmoid(gates)).astype(BF16)

    if transposed:
        k16_ref, dk16_ref, vt_ref = rest
        k16_ref[...] = fk.astype(BF16)
        dk16_ref[...] = dk.astype(BF16)
        q_ref[...] = (_nt_dot(wq_ref[...], hb) * qscale).astype(BF16)
        vt = _nt_dot(wvt_ref[...], hb).astype(BF16)
        for c in range(vt.shape[1] // tk):
            vt_ref[c] = vt[:, c * tk:(c + 1) * tk]
    else:
        q_ref[...] = (_dot(hb, wq_ref[...]) * qscale).astype(BF16)


def _in_projection(x2d, g_pre, w_in, b_f, *, transposed, tm, tk):
    n, d = x2d.shape
    tm = min(tm, n)
    assert n % tm == 0 and (not transposed or tm % tk == 0)
    qscale = FOX_HD ** -0.5 * LOG2E
    o = np.cumsum([0, FOX_W, FOX_W, FOX_W, FOX_HEADS, FOX_W, DIFF_W, DIFF_W, DIFF_W, DIFF_W])
    w_fq, w_fk, w_fv, w_ff, w_fg, w_dq, w_dk, w_dv, w_dg = [w_in[:, o[i]:o[i + 1]] for i in range(9)]
    wkv = jnp.concatenate([w_fk, w_fv, w_dk, w_dv], axis=1).astype(BF16)
    wg = jnp.concatenate([w_fg, w_dg], axis=1).astype(BF16)
    wff = jnp.pad(w_ff, ((0, 0), (0, LANES - FOX_HEADS))).astype(BF16)
    bfp = jnp.pad(b_f.reshape(1, FOX_HEADS), ((0, 0), (0, LANES - FOX_HEADS)))
    wq = jnp.concatenate([w_fq, w_dq], axis=1)
    wv = jnp.concatenate([w_fv, w_dv], axis=1)
    if transposed:
        wq = wq.T.astype(BF16)
        wvt = wv.T.astype(BF16)
    else:
        wq = wq.astype(BF16)
        wvt = jnp.zeros((8, LANES), BF16)

    const = lambda shape: pl.BlockSpec(shape, lambda i: (0,) * len(shape))
    rows = lambda w: pl.BlockSpec((tm, w), lambda i: (i, 0))
    in_specs = [rows(d), const((1, d)), const(wkv.shape), const(wg.shape), const(wff.shape),
                const(bfp.shape), const(wq.shape), const(wvt.shape)]
    out_shape = [jax.ShapeDtypeStruct((n, FOX_W), F32)] * 2 + [jax.ShapeDtypeStruct((n, DIFF_W), F32)] * 2
    out_shape += [jax.ShapeDtypeStruct((n, FOX_HEADS), F32), jax.ShapeDtypeStruct((n, FOX_W + DIFF_W), BF16)]
    out_specs = [rows(FOX_W), rows(FOX_W), rows(DIFF_W), rows(DIFF_W), rows(FOX_HEADS), rows(FOX_W + DIFF_W)]
    if transposed:
        out_shape += [jax.ShapeDtypeStruct((FOX_W + DIFF_W, n), BF16),
                      jax.ShapeDtypeStruct((n, FOX_W), BF16), jax.ShapeDtypeStruct((n, DIFF_W), BF16),
                      jax.ShapeDtypeStruct((n // tk, FOX_W + DIFF_W, tk), BF16)]
        out_specs += [pl.BlockSpec((FOX_W + DIFF_W, tm), lambda i: (0, i)), rows(FOX_W), rows(DIFF_W),
                      pl.BlockSpec((tm // tk, FOX_W + DIFF_W, tk), lambda i: (i, 0, 0))]
    else:
        out_shape += [jax.ShapeDtypeStruct((n, FOX_W + DIFF_W), BF16)]
        out_specs += [rows(FOX_W + DIFF_W)]
    return pl.pallas_call(
        functools.partial(_inproj_body, transposed=transposed, tk=tk, qscale=qscale),
        grid=(n // tm,), in_specs=in_specs, out_specs=out_specs, out_shape=out_shape,
        compiler_params=_cparams(1), name="in_projection_t" if transposed else "in_projection_r",
    )(x2d, g_pre.reshape(1, d), wkv, wg, wff, bfp, wq, wvt)


def _cumsum_body(x_ref, cum_ref, hi_ref, mid_ref, lo_ref, *, seg, group, scale):
    x = x_ref[...]
    r = x.shape[0]
    a = lax.broadcasted_iota(jnp.int32, (LANES, LANES), 0)
    b = lax.broadcasted_iota(jnp.int32, (LANES, LANES), 1)
    sh = int(math.log2(seg))
    upper = ((a <= b) & ((a >> sh) == (b >> sh))).astype(BF16)
    pieces = _split3(x)
    w = sum(_dot(p, upper) for p in pieces)
    if group > 1:
        ones = jnp.ones((LANES, LANES), BF16)
        tot = sum(_dot(p, ones) for p in pieces)
        ri = lax.broadcasted_iota(jnp.int32, (r, r), 0)
        ci = lax.broadcasted_iota(jnp.int32, (r, r), 1)
        gsh = int(math.log2(group))
        lower = ((ci < ri) & ((ri >> gsh) == (ci >> gsh))).astype(BF16)
        w = w + sum(_dot(lower, p) for p in _split3(tot))
    cum_ref[...] = w
    hi, mid, lo = _split3(w * scale)
    hi_ref[...] = hi
    mid_ref[...] = mid
    lo_ref[...] = lo


def _prefix_sums(x2d, *, seg, group, rows_per_step, scale):
    r = x2d.shape[0]
    rb = min(rows_per_step, r)
    assert r % rb == 0 and rb % group == 0 and (seg == LANES or group == 1)
    spec = pl.BlockSpec((rb, LANES), lambda i: (i, 0))
    return pl.pallas_call(
        functools.partial(_cumsum_body, seg=seg, group=group, scale=scale),
        grid=(r // rb,), in_specs=[spec], out_specs=[spec] * 4,
        out_shape=[jax.ShapeDtypeStruct((r, LANES), F32)] + [jax.ShapeDtypeStruct((r, LANES), BF16)] * 3,
        compiler_params=_cparams(1), name="prefix_sums",
    )(x2d)


SUM_ROWS = 16


def _with_sum_rows(vt):
    row = lax.broadcasted_iota(jnp.int32, (SUM_ROWS, vt.shape[1]), 0)
    return jnp.concatenate([vt, jnp.where(row == 0, 1.0, 0.0).astype(BF16)], axis=0)


def _online_update_t(s, h, m_ref, acc_ref, vt_aug):
    m = m_ref[h]
    mn = jnp.maximum(m, jnp.max(s, axis=0, keepdims=True))
    alpha = jnp.exp2(m - mn)
    p = jnp.exp2(s - mn)
    acc_ref[h] = alpha * acc_ref[h] + _dot(vt_aug, p.astype(BF16))
    m_ref[h] = mn


def _init_state(m_ref, l_ref, acc_ref):
    m_ref[...] = jnp.full(m_ref.shape, NEG_INF, F32)
    if l_ref is not None:
        l_ref[...] = jnp.zeros(l_ref.shape, F32)
    acc_ref[...] = jnp.zeros(acc_ref.shape, F32)


def _resident(shape, index_map):
    return pl.BlockSpec(shape, index_map, pipeline_mode=pl.Buffered(1))


def _fox_prompt_body(q_ref, qx_ref, k_ref, kx_ref, vt_ref, gate_ref, o_ref, m_ref, acc_ref, sa_ref, sb_ref, *,
                     t, pairs):
    i = pl.program_id(1)
    heads = 2 * pairs
    _init_state(m_ref, None, acc_ref)
    z64 = jnp.zeros((FOX_HD, t), BF16)
    zpad = jnp.zeros((LANES - QEXT_ROWS, t), BF16)
    qa = []
    for p in range(pairs):
        q = q_ref[p * HEAD_PAIR_W:(p + 1) * HEAD_PAIR_W, :]
        qa.append(jnp.concatenate([q[:FOX_HD], z64, qx_ref[2 * p], zpad], axis=0))
        qa.append(jnp.concatenate([z64, q[FOX_HD:], qx_ref[2 * p + 1], zpad], axis=0))

    def keys(j, p):
        off = pl.multiple_of(j * t, t)
        return jnp.concatenate([k_ref[pl.ds(off, t), p * HEAD_PAIR_W:(p + 1) * HEAD_PAIR_W],
                                kx_ref[pl.ds(off, t), :]], axis=1)

    def consume(j, cur_ref, h, diagonal):
        s = cur_ref[h]
        if diagonal:
            kk = lax.broadcasted_iota(jnp.int32, (t, t), 0)
            qq = lax.broadcasted_iota(jnp.int32, (t, t), 1)
            s = jnp.where(kk <= qq, s, NEG_INF)
        _online_update_t(s, h, m_ref, acc_ref, _with_sum_rows(vt_ref[j, h * FOX_HD:(h + 1) * FOX_HD, :]))

    def half_step(j, cur_ref, nxt_ref):
        for p in range(pairs):
            ka = keys(j + 1, p)
            for h in (2 * p, 2 * p + 1):
                nxt_ref[h] = _dot(ka, qa[h])
                consume(j, cur_ref, h, False)

    def last_step(j, cur_ref):
        for h in range(heads):
            consume(j, cur_ref, h, True)

    for p in range(pairs):
        ka = keys(0, p)
        sa_ref[2 * p] = _dot(ka, qa[2 * p])
        sa_ref[2 * p + 1] = _dot(ka, qa[2 * p + 1])

    def pair_step(jj, c):
        half_step(2 * jj, sa_ref, sb_ref)
        half_step(2 * jj + 1, sb_ref, sa_ref)
        return c

    lax.fori_loop(0, i // 2, pair_step, 0)

    @pl.when(i % 2 == 0)
    def _():
        last_step(i, sa_ref)

    @pl.when(i % 2 == 1)
    def _():
        half_step(i - 1, sa_ref, sb_ref)
        last_step(i, sb_ref)

    for p in range(pairs):
        a0, a1 = acc_ref[2 * p], acc_ref[2 * p + 1]
        ot = jnp.concatenate([a0[:FOX_HD] / a0[FOX_HD:FOX_HD + 1], a1[:FOX_HD] / a1[FOX_HD:FOX_HD + 1]], axis=0)
        cols = slice(p * HEAD_PAIR_W, (p + 1) * HEAD_PAIR_W)
        o_ref[:, cols] = (ot.T * gate_ref[:, cols].astype(F32)).astype(BF16)


def _fox_prompt_attention(qt, qext, k16, kext, vt, gate, *, t, pairs):
    n = k16.shape[0]
    nq = n // t
    w = pairs * HEAD_PAIR_W
    heads = 2 * pairs
    assert (FOX_HEADS // 2) % pairs == 0
    return pl.pallas_call(
        functools.partial(_fox_prompt_body, t=t, pairs=pairs),
        grid=(FOX_HEADS // heads, nq),
        in_specs=[pl.BlockSpec((w, t), lambda g, i: (g, i)),
                  pl.BlockSpec((heads, QEXT_ROWS, t), lambda g, i: (g, 0, i)),
                  _resident((n, w), lambda g, i: (0, g)),
                  _resident((n, LANES), lambda g, i: (0, 0)),
                  _resident((nq, w, t), lambda g, i: (0, g, 0)),
                  pl.BlockSpec((t, w), lambda g, i: (i, g))],
        out_specs=pl.BlockSpec((t, w), lambda g, i: (i, g)),
        out_shape=jax.ShapeDtypeStruct((n, FOX_W), BF16),
        scratch_shapes=[pltpu.VMEM((heads, 1, t), F32), pltpu.VMEM((heads, FOX_HD + SUM_ROWS, t), F32),
                        pltpu.VMEM((heads, t, t), F32), pltpu.VMEM((heads, t, t), F32)],
        compiler_params=_cparams(2), name="fox_prompt_attention",
    )(qt, qext, k16, kext, vt, gate)


def _lambda_value(lq1, lk1, lq2, lk2, lam_init):
    a = jnp.sum(lq1 * lk1, axis=-1, keepdims=True)
    b = jnp.sum(lq2 * lk2, axis=-1, keepdims=True)
    return jnp.exp(a) - jnp.exp(b) + lam_init


def _subln_gate(y, g, gate, lam_init):
    ms = jnp.mean(y * y, axis=-1, keepdims=True)
    yn = (y * lax.rsqrt(ms + EPS)) * g
    return ((yn * (1.0 - lam_init)) * gate.astype(F32)).astype(BF16)


def _diff_prompt_body(q_ref, k_ref, vt_ref, bd_ref, bs_ref, gate_ref, sg_ref, lq1_ref, lk1_ref, lq2_ref,
                      lk2_ref, o_ref, m_ref, acc_ref, sa_ref, sb_ref, *, t, heads, lam_init):
    i = pl.program_id(1)
    w = 2 * DIFF_HD
    _init_state(m_ref, None, acc_ref)
    z64 = jnp.zeros((DIFF_HD, t), BF16)
    qa = []
    for h in range(heads):
        q = q_ref[h * w:(h + 1) * w, :]
        qa.append(jnp.concatenate([q[:DIFF_HD], z64], axis=0))
        qa.append(jnp.concatenate([z64, q[DIFF_HD:]], axis=0))

    def keys(j, h):
        return k_ref[pl.ds(pl.multiple_of(j * t, t), t), h * w:(h + 1) * w]

    def consume(j, cur_ref, c, bias_ref):
        h = c // 2
        s = cur_ref[c]
        if bias_ref is not None:
            s = s + bias_ref[h]
        _online_update_t(s, c, m_ref, acc_ref, _with_sum_rows(vt_ref[j, h * w:(h + 1) * w, :]))

    def half_step(j, cur_ref, nxt_ref, bias_ref):
        for h in range(heads):
            kk = keys(j + 1, h)
            for c in (2 * h, 2 * h + 1):
                nxt_ref[c] = _dot(kk, qa[c])
                consume(j, cur_ref, c, bias_ref)

    def last_step(j, cur_ref):
        for c in range(2 * heads):
            consume(j, cur_ref, c, bd_ref)

    for h in range(heads):
        kk = keys(0, h)
        sa_ref[2 * h] = _dot(kk, qa[2 * h])
        sa_ref[2 * h + 1] = _dot(kk, qa[2 * h + 1])

    def pair_step(jj, c):
        half_step(2 * jj, sa_ref, sb_ref, None)
        half_step(2 * jj + 1, sb_ref, sa_ref, None)
        return c

    n_far = jnp.maximum(i - 1, 0)
    lax.fori_loop(0, n_far // 2, pair_step, 0)

    @pl.when(i == 0)
    def _():
        last_step(i, sa_ref)

    @pl.when(i % 2 == 1)
    def _():
        half_step(i - 1, sa_ref, sb_ref, bs_ref)
        last_step(i, sb_ref)

    @pl.when((i >= 2) & (i % 2 == 0))
    def _():
        half_step(i - 2, sa_ref, sb_ref, None)
        half_step(i - 1, sb_ref, sa_ref, bs_ref)
        last_step(i, sa_ref)

    lam = _lambda_value(lq1_ref[...], lk1_ref[...], lq2_ref[...], lk2_ref[...], lam_init)
    for h in range(heads):
        a0, a1 = acc_ref[2 * h], acc_ref[2 * h + 1]
        ot = a0[:w] / a0[w:w + 1] - lam * (a1[:w] / a1[w:w + 1])
        cols = slice(h * w, (h + 1) * w)
        o_ref[:, cols] = _subln_gate(ot.T, sg_ref[...], gate_ref[:, cols], lam_init)


def _diff_prompt_attention(qt, dk16, vt, bias_diag, bias_sub, gate, subln_g, lams, *, t, heads, lam_init):
    n = dk16.shape[0]
    nq = n // t
    w = heads * 2 * DIFF_HD
    assert DIFF_HEADS % heads == 0 and FOX_W % w == 0
    fox_blocks = FOX_W // w
    vec = lambda width: pl.BlockSpec((1, width), lambda g, i: (0, 0))
    return pl.pallas_call(
        functools.partial(_diff_prompt_body, t=t, heads=heads, lam_init=lam_init),
        grid=(DIFF_HEADS // heads, nq),
        in_specs=[pl.BlockSpec((w, t), lambda g, i: (fox_blocks + g, i)),
                  _resident((n, w), lambda g, i: (0, g)),
                  _resident((nq, w, t), lambda g, i: (0, fox_blocks + g, 0)),
                  pl.BlockSpec((heads, t, t), lambda g, i: (g, 0, 0)),
                  pl.BlockSpec((heads, t, t), lambda g, i: (g, 0, 0)),
                  pl.BlockSpec((t, w), lambda g, i: (i, fox_blocks + g)),
                  vec(2 * DIFF_HD), vec(DIFF_HD), vec(DIFF_HD), vec(DIFF_HD), vec(DIFF_HD)],
        out_specs=pl.BlockSpec((t, w), lambda g, i: (i, g)),
        out_shape=jax.ShapeDtypeStruct((n, DIFF_W), BF16),
        scratch_shapes=[pltpu.VMEM((2 * heads, 1, t), F32),
                        pltpu.VMEM((2 * heads, 2 * DIFF_HD + SUM_ROWS, t), F32),
                        pltpu.VMEM((2 * heads, t, t), F32), pltpu.VMEM((2 * heads, t, t), F32)],
        compiler_params=_cparams(2), name="diff_prompt_attention",
    )(qt, dk16, vt, bias_diag, bias_sub, gate, subln_g, *lams)


def _block_diag_queries(qb):
    lane = lax.broadcasted_iota(jnp.int32, qb.shape, 1)
    zero = jnp.zeros_like(qb)
    return jnp.concatenate([jnp.where(lane < FOX_HD, qb, zero), jnp.where(lane >= FOX_HD, qb, zero)], axis=0)


def _fox_sample_body(q_ref, kc_ref, vc_ref, kn_ref, vn_ref, bc_ref, bn_ref, gate_ref, o_ref, *, s):
    qbd = _block_diag_queries(q_ref[...])
    p_len = kc_ref.shape[-1]
    kt = kc_ref[0].reshape(HEAD_PAIR_W, p_len).astype(BF16)
    vt = vc_ref[0].reshape(HEAD_PAIR_W, p_len).astype(BF16)
    zrows = jnp.zeros((LANES - s, HEAD_PAIR_W), BF16)
    kn = jnp.concatenate([kn_ref[...].astype(BF16), zrows], axis=0)
    vn = jnp.concatenate([vn_ref[...].astype(BF16), zrows], axis=0)
    bc = bc_ref[0, 0]
    bn = bn_ref[0, 0]

    def rows_bias(b, width):
        return jnp.concatenate([jnp.broadcast_to(b[0:1], (s, width)), jnp.broadcast_to(b[1:2], (s, width))], axis=0)

    sc = _dot(qbd, kt) + rows_bias(bc, p_len)
    sn = _nt_dot(qbd, kn) + rows_bias(bn, LANES)
    qi = lax.broadcasted_iota(jnp.int32, (2 * s, LANES), 0)
    ki = lax.broadcasted_iota(jnp.int32, (2 * s, LANES), 1)
    qi = jnp.where(qi >= s, qi - s, qi)
    sn = jnp.where(ki <= qi, sn, NEG_INF)
    m = jnp.maximum(jnp.max(sc, axis=1, keepdims=True), jnp.max(sn, axis=1, keepdims=True))
    pc = jnp.exp2(sc - m)
    pn = jnp.exp2(sn - m)
    l = jnp.sum(pc, axis=1, keepdims=True) + jnp.sum(pn, axis=1, keepdims=True)
    o = (_nt_dot(pc.astype(BF16), vt) + _dot(pn.astype(BF16), vn)) / l
    lane = lax.broadcasted_iota(jnp.int32, (s, HEAD_PAIR_W), 1)
    out = jnp.where(lane < FOX_HD, o[:s], o[s:])
    o_ref[...] = (out * gate_ref[...].astype(F32)).astype(BF16)


def _fox_sample_attention(q16, cache_kt, cache_vt, fk32, fv32, bias_c, bias_n, gate, *, s):
    b, _, _, p_len = cache_kt.shape
    pairs = FOX_HEADS // 2
    blk = lambda: pl.BlockSpec((s, HEAD_PAIR_W), lambda bi, p: (bi, p))
    cache = lambda: pl.BlockSpec((1, 2, FOX_HD, p_len), lambda bi, p: (bi, p, 0, 0))
    return pl.pallas_call(
        functools.partial(_fox_sample_body, s=s),
        grid=(b, pairs),
        in_specs=[blk(), cache(), cache(), blk(), blk(),
                  pl.BlockSpec((1, 1, 2, p_len), lambda bi, p: (bi, p, 0, 0)),
                  pl.BlockSpec((1, 1, 2, LANES), lambda bi, p: (bi, p, 0, 0)),
                  blk()],
        out_specs=blk(),
        out_shape=jax.ShapeDtypeStruct((b * s, FOX_W), BF16),
        compiler_params=_cparams(2), name="fox_sample_attention",
    )(q16, cache_kt, cache_vt, fk32, fv32, bias_c, bias_n, gate)


def _diff_sample_body(q_ref, kc_ref, vc_ref, kn_ref, vn_ref, bc_ref, bn_ref, gate_ref, sg_ref, lq1_ref,
                      lk1_ref, lq2_ref, lk2_ref, o_ref, m_ref, l_ref, acc_ref, *, s, pc, lam_init):
    c = pl.program_id(1)
    last = pl.num_programs(1) - 1
    w = 2 * DIFF_HD

    @pl.when(c == 0)
    def _():
        _init_state(m_ref, l_ref, acc_ref)

    qbd = [_block_diag_queries(q_ref[:, h * w:(h + 1) * w]) for h in range(DIFF_HEADS)]

    def update(h, sc, v16):
        m = m_ref[h]
        mn = jnp.maximum(m, jnp.max(sc, axis=1, keepdims=True))
        alpha = jnp.exp2(m - mn)
        p = jnp.exp2(sc - mn)
        l_ref[h] = alpha * l_ref[h] + jnp.sum(p, axis=1, keepdims=True)
        acc_ref[h] = alpha * acc_ref[h] + _dot(p.astype(BF16), v16)
        m_ref[h] = mn

    def chunk(with_bias):
        scores = [_nt_dot(qbd[h], kc_ref[0, pl.ds(h, pc, stride=DIFF_HEADS), :].astype(BF16))
                  for h in range(DIFF_HEADS)]
        for h, sc in enumerate(scores):
            if with_bias:
                sc = sc + jnp.concatenate([bc_ref[h], bc_ref[h]], axis=0)
            update(h, sc, vc_ref[0, pl.ds(h, pc, stride=DIFF_HEADS), :].astype(BF16))

    @pl.when(c != last)
    def _():
        chunk(False)

    @pl.when(c == last)
    def _():
        chunk(True)
        lam = _lambda_value(lq1_ref[...], lk1_ref[...], lq2_ref[...], lk2_ref[...], lam_init)
        zrows = jnp.zeros((LANES - s, w), BF16)
        for h in range(DIFF_HEADS):
            cols = slice(h * w, (h + 1) * w)
            kn = jnp.concatenate([kn_ref[:, cols].astype(BF16), zrows], axis=0)
            vn = jnp.concatenate([vn_ref[:, cols].astype(BF16), zrows], axis=0)
            sn = _nt_dot(qbd[h], kn) + jnp.concatenate([bn_ref[h], bn_ref[h]], axis=0)
            update(h, sn, vn)
            o = acc_ref[h] / l_ref[h]
            o = o[:s] - lam * o[s:]
            o_ref[:, cols] = _subln_gate(o, sg_ref[...], gate_ref[:, cols], lam_init)


def _diff_sample_attention(q16, cache_k, cache_v, dk32, dv32, bias_c, bias_n, gate, subln_g, lams, *, s, pc,
                           lam_init):
    b, rows, w = cache_k.shape
    p_len = rows // DIFF_HEADS
    assert p_len % pc == 0 and bias_c.shape == (DIFF_HEADS, s, pc)
    cache = lambda: pl.BlockSpec((1, DIFF_HEADS * pc, w), lambda bi, c: (bi, c, 0))
    new = lambda: pl.BlockSpec((s, DIFF_W), lambda bi, c: (bi, 0))
    half = lambda: pl.BlockSpec((s, DIFF_W), lambda bi, c: (bi, 1))
    const = lambda shape: pl.BlockSpec(shape, lambda bi, c: (0,) * len(shape))
    return pl.pallas_call(
        functools.partial(_diff_sample_body, s=s, pc=pc, lam_init=lam_init),
        grid=(b, p_len // pc),
        in_specs=[half(), cache(), cache(), new(), new(), const(bias_c.shape), const(bias_n.shape), half(),
                  const((1, w)), const((1, DIFF_HD)), const((1, DIFF_HD)), const((1, DIFF_HD)), const((1, DIFF_HD))],
        out_specs=new(),
        out_shape=jax.ShapeDtypeStruct((b * s, DIFF_W), BF16),
        scratch_shapes=[pltpu.VMEM((DIFF_HEADS, 2 * s, 1), F32), pltpu.VMEM((DIFF_HEADS, 2 * s, 1), F32),
                        pltpu.VMEM((DIFF_HEADS, 2 * s, w), F32)],
        compiler_params=_cparams(2), name="diff_sample_attention",
    )(q16, cache_k, cache_v, dk32, dv32, bias_c, bias_n, gate, subln_g, *lams)


def _outproj_body(mf_ref, md_ref, w_ref, x_ref, g_ref, y_ref):
    out = _dot(mf_ref[...], w_ref[:FOX_W, :]) + _dot(md_ref[...], w_ref[FOX_W:, :])
    ms = jnp.mean(out * out, axis=-1, keepdims=True)
    y_ref[...] = x_ref[...] + (out * lax.rsqrt(ms + EPS)) * g_ref[...]


def _out_projection(mf, md, w_out, x2d, g_post, *, tm):
    n, d = x2d.shape
    tm = min(tm, n)
    assert n % tm == 0
    rows = lambda w: pl.BlockSpec((tm, w), lambda i: (i, 0))
    return pl.pallas_call(
        _outproj_body, grid=(n // tm,),
        in_specs=[rows(FOX_W), rows(DIFF_W), pl.BlockSpec(w_out.shape, lambda i: (0, 0)), rows(d),
                  pl.BlockSpec((1, d), lambda i: (0, 0))],
        out_specs=rows(d), out_shape=jax.ShapeDtypeStruct((n, d), F32),
        compiler_params=_cparams(1), name="out_projection",
    )(mf, md, w_out.astype(BF16), x2d, g_post.reshape(1, d))


def _t5_bucket(rel):
    half = NUM_BUCKETS // 2
    max_exact = half // 2
    ret = jnp.where(rel > 0, half, 0)
    n = jnp.abs(rel)
    nf = jnp.maximum(n, 1).astype(jnp.float32)
    large = max_exact + (jnp.log(nf / max_exact) / math.log(MAX_DISTANCE / max_exact)
                         * (half - max_exact)).astype(jnp.int32)
    large = jnp.minimum(large, half - 1)
    return ret + jnp.where(n < max_exact, n, large)


def _rel_bias_of(rel_bias, rel):
    return rel_bias[_t5_bucket(rel)].astype(F32).T


def _toeplitz(g, rows, cols):
    h = g.shape[0]
    period = rows + cols
    v = jnp.concatenate([g[:, rows - 1:], jnp.zeros((h, 1), g.dtype), g[:, :rows - 1]], axis=1)
    flat = jnp.tile(v, (1, rows))[:, :rows * (period - 1)]
    return flat.reshape(h, rows, period - 1)[:, :, :cols]


def _chunk_mask(bias, row_pos, col_pos, rows_are_keys):
    r = (row_pos // CHUNK)[:, None]
    c = (col_pos // CHUNK)[None, :]
    mask = (r <= c) if rows_are_keys else (c <= r)
    return jnp.where(mask[None], bias * LOG2E, NEG_INF)


def _prompt_layer(x, g_pre, w_in, b_f, lams, subln_g, w_out, g_post, rel_bias, layer_idx, *, t=256, tm=512,
                  fox_pairs=4, diff_heads=4):
    b, s, d = x.shape
    assert b == 1 and s % t == 0 and s // LANES >= 1
    x2d = x.reshape(s, d)
    (fk, fv, dk, dv, logf, gate, qt, k16, dk16, vt) = _in_projection(
        x2d, g_pre, w_in, b_f, transposed=True, tm=tm, tk=t)

    rows = s // LANES
    _, hi, mid, lo = _prefix_sums(logf.T.reshape(FOX_HEADS * rows, LANES), seg=LANES, group=rows,
                                  rows_per_step=FOX_HEADS * rows, scale=LOG2E)
    pieces = jnp.stack([hi, mid, lo], axis=0).reshape(PIECES, FOX_HEADS, s)
    kext = jnp.concatenate([
        (-pieces).transpose(2, 1, 0).reshape(s, ONES_AT), jnp.ones((s, PIECES), BF16),
        jnp.zeros((s, LANES - ONES_AT - PIECES), BF16)], axis=1)
    sel = (jnp.arange(ONES_AT)[None, :] // PIECES == jnp.arange(FOX_HEADS)[:, None]).astype(BF16)
    qext = jnp.concatenate([
        jnp.broadcast_to(sel[:, :, None], (FOX_HEADS, ONES_AT, s)), pieces.transpose(1, 0, 2),
        jnp.zeros((FOX_HEADS, QEXT_ROWS - ONES_AT - PIECES, s), BF16)], axis=1)

    mf = _fox_prompt_attention(qt, qext, k16, kext, vt, gate, t=t, pairs=fox_pairs)

    lam_init = 0.8 - 0.6 * math.exp(-0.3 * layer_idx)
    assert t >= MAX_DISTANCE
    pos = jnp.arange(t, dtype=jnp.int32)
    off = jnp.arange(-(t - 1), t, dtype=jnp.int32)
    far = _rel_bias_of(rel_bias, jnp.full((1,), -2 * t, jnp.int32))
    bias_diag = _chunk_mask(_toeplitz(_rel_bias_of(rel_bias, -off) - far, t, t), pos, pos, True)
    bias_sub = _chunk_mask(_toeplitz(_rel_bias_of(rel_bias, -off - t) - far, t, t), pos, pos + t, True)
    md = _diff_prompt_attention(qt, dk16, vt, bias_diag, bias_sub, gate, subln_g.reshape(1, -1), lams,
                                t=t, heads=diff_heads, lam_init=lam_init)

    y = _out_projection(mf, md, w_out, x2d, g_post, tm=tm)
    return y.reshape(b, s, d), (fk, fv, logf, dk, dv)


def _sample_layer(x, past, g_pre, w_in, b_f, lams, subln_g, w_out, g_post, rel_bias, layer_idx, *, tm=512,
                  diff_chunk=1024):
    b, s, d = x.shape
    pk, pv, plogf, pdk, pdv = past
    p_len = pk.shape[1]
    assert s <= LANES and LANES % s == 0 and p_len % LANES == 0
    x2d = x.reshape(b * s, d)
    fk, fv, dk, dv, logf, gate, q16 = _in_projection(x2d, g_pre, w_in, b_f, transposed=False, tm=tm, tk=tm)

    rows = p_len // LANES
    cum_c, _, _, _ = _prefix_sums(plogf.transpose(0, 2, 1).reshape(b * FOX_HEADS * rows, LANES), seg=LANES,
                                  group=rows, rows_per_step=32 * rows, scale=1.0)
    cum_c = cum_c.reshape(b, FOX_HEADS, p_len)
    bias_c = ((cum_c[:, :, -1:] - cum_c) * LOG2E).reshape(b, FOX_HEADS // 2, 2, p_len)
    cum_n, _, _, _ = _prefix_sums(
        logf.reshape(b, s, FOX_HEADS).transpose(0, 2, 1).reshape(b * FOX_HEADS * s // LANES, LANES),
        seg=s, group=1, rows_per_step=1024, scale=1.0)
    bias_n = jnp.concatenate([-cum_n.reshape(b, FOX_HEADS, s) * LOG2E,
                              jnp.full((b, FOX_HEADS, LANES - s), NEG_INF, F32)], axis=-1)
    bias_n = bias_n.reshape(b, FOX_HEADS // 2, 2, LANES)
    mf = _fox_sample_attention(q16, pk.transpose(0, 2, 3, 1), pv.transpose(0, 2, 3, 1), fk, fv, bias_c, bias_n,
                               gate, s=s)

    lam_init = 0.8 - 0.6 * math.exp(-0.3 * layer_idx)
    pc = min(diff_chunk, p_len)
    assert pc >= MAX_DISTANCE + s
    q_pos = p_len + jnp.arange(s, dtype=jnp.int32)
    far = _rel_bias_of(rel_bias, jnp.full((1,), -2 * p_len, jnp.int32))
    off = jnp.arange(-(s - 1), pc, dtype=jnp.int32)
    bd_c = _chunk_mask(_toeplitz(_rel_bias_of(rel_bias, off - pc) - far, s, pc), q_pos,
                       p_len - pc + jnp.arange(pc, dtype=jnp.int32), False)
    off = jnp.arange(-(s - 1), s, dtype=jnp.int32)
    bd_n = _chunk_mask(_toeplitz(_rel_bias_of(rel_bias, off) - far, s, s), q_pos, q_pos, False)
    bd_n = jnp.concatenate([bd_n, jnp.full((DIFF_HEADS, s, LANES - s), NEG_INF, F32)], axis=-1)
    md = _diff_sample_attention(q16, pdk.reshape(b, p_len * DIFF_HEADS, 2 * DIFF_HD),
                                pdv.reshape(b, p_len * DIFF_HEADS, 2 * DIFF_HD), dk, dv, bd_c, bd_n, gate,
                                subln_g.reshape(1, -1), lams, s=s, pc=pc, lam_init=lam_init)

    y = _out_projection(mf, md, w_out, x2d, g_post, tm=tm)
    return y.reshape(b, s, d), (fk, fv, logf, dk, dv)


def kernel(x_prompt, x_sample, cache_fox_k, cache_fox_v, cache_fox_logf, cache_diff_k, cache_diff_v,
           norm_pre_g, w_in, forget_bias, lambda_q1, lambda_k1, lambda_q2, lambda_k2, subln_g, w_out,
           norm_post_g, rel_bias):
    depth = w_in.shape[0]
    y_p, y_s = x_prompt, x_sample
    rows_p, rows_s = [], []
    for l in range(depth):
        lams = tuple(a[l].reshape(1, -1) for a in (lambda_q1, lambda_k1, lambda_q2, lambda_k2))
        params = (norm_pre_g[l], w_in[l], forget_bias[l], lams, subln_g[l], w_out[l], norm_post_g[l], rel_bias, l)
        y_p, new_p = _prompt_layer(y_p, *params)
        past = (cache_fox_k[l], cache_fox_v[l], cache_fox_logf[l], cache_diff_k[l], cache_diff_v[l])
        y_s, new_s = _sample_layer(y_s, past, *params)
        rows_p.append(new_p)
        rows_s.append(new_s)

    bp, sp = x_prompt.shape[:2]
    bs, ss = x_sample.shape[:2]

    def stack(rows, idx, shape):
        return jnp.stack([r[idx].reshape(shape) for r in rows], axis=0)

    outs = [y_p, y_s]
    for rows, (b, s) in ((rows_p, (bp, sp)), (rows_s, (bs, ss))):
        outs += [stack(rows, 0, (b, s, FOX_HEADS, FOX_HD)), stack(rows, 1, (b, s, FOX_HEADS, FOX_HD)),
                 stack(rows, 2, (b, s, FOX_HEADS)), stack(rows, 3, (b, s, DIFF_HEADS, 2 * DIFF_HD)),
                 stack(rows, 4, (b, s, DIFF_HEADS, 2 * DIFF_HD))]
    return tuple(outs)
```

```python
import functools
import math

import jax
import jax.numpy as jnp
import numpy as np
from jax import lax
from jax.experimental import pallas as pl
from jax.experimental.pallas import tpu as pltpu

F32 = jnp.float32
BF16 = jnp.bfloat16

FOX_HEADS = 8
FOX_HD = 64
DIFF_HEADS = 4
DIFF_HD = 64
FOX_W = FOX_HEADS * FOX_HD
DIFF_W = DIFF_HEADS * 2 * DIFF_HD
CHUNK = 64
NUM_BUCKETS = 32
MAX_DISTANCE = 128
EPS = 1e-6
NEG_INF = -1e30
LOG2E = 1.4426950408889634

LANES = 128
HEAD_PAIR_W = 2 * FOX_HD
VMEM_LIMIT_BYTES = 56 * 1024 * 1024

PIECES = 3
ONES_AT = PIECES * FOX_HEADS
QEXT_ROWS = 32


def _cparams(n_axes):
    return pltpu.CompilerParams(
        dimension_semantics=("arbitrary",) * n_axes,
        vmem_limit_bytes=VMEM_LIMIT_BYTES,
    )


def _split3(a):
    hi = a.astype(BF16)
    r1 = a - hi.astype(F32)
    mid = r1.astype(BF16)
    lo = (r1 - mid.astype(F32)).astype(BF16)
    return hi, mid, lo


def _nt_dot(a, b):
    return lax.dot_general(a, b, (((1,), (1,)), ((), ())), preferred_element_type=F32)


def _dot(a, b):
    return jnp.dot(a, b, preferred_element_type=F32)


def _inproj_body(x_ref, g_ref, wkv_ref, wg_ref, wff_ref, bf_ref, wq_ref, wvt_ref,
                 fk_ref, fv_ref, dk_ref, dv_ref, logf_ref, gate_ref, q_ref, *rest,
                 transposed, tk, qscale):
    x = x_ref[...]
    ms = jnp.mean(x * x, axis=-1, keepdims=True)
    h = (x * lax.rsqrt(ms + EPS)) * g_ref[...]
    hb = h.astype(BF16)

    fk = _dot(hb, wkv_ref[:, 0 * FOX_W:1 * FOX_W])
    fk_ref[...] = fk
    fv_ref[...] = _dot(hb, wkv_ref[:, 1 * FOX_W:2 * FOX_W])
    dk = _dot(hb, wkv_ref[:, 2 * FOX_W:3 * FOX_W])
    dk_ref[...] = dk
    dv_ref[...] = _dot(hb, wkv_ref[:, 3 * FOX_W:4 * FOX_W])

    ff = _dot(hb, wff_ref[...])
    logf = jax.nn.log_sigmoid(ff + bf_ref[...])
    logf_ref[...] = logf[:, :FOX_HEADS]

    gates = _dot(hb, wg_ref[...])
    gate_ref[...] = (gates * jax.nn.sigmoid(gates)).astype(BF16)

    if transposed:
        k16_ref, dk16_ref, vt_ref = rest
        k16_ref[...] = fk.astype(BF16)
        dk16_ref[...] = dk.astype(BF16)
        q_ref[...] = (_nt_dot(wq_ref[...], hb) * qscale).astype(BF16)
        vt = _nt_dot(wvt_ref[...], hb).astype(BF16)
        for c in range(vt.shape[1] // tk):
            vt_ref[c] = vt[:, c * tk:(c + 1) * tk]
    else:
        q_ref[...] = (_dot(hb, wq_ref[...]) * qscale).astype(BF16)


def _in_projection(x2d, g_pre, w_in, b_f, *, transposed, tm, tk):
    n, d = x2d.shape
    tm = min(tm, n)
    assert n % tm == 0 and (not transposed or tm % tk == 0)
    qscale = FOX_HD ** -0.5 * LOG2E
    o = np.cumsum([0, FOX_W, FOX_W, FOX_W, FOX_HEADS, FOX_W, DIFF_W, DIFF_W, DIFF_W, DIFF_W])
    w_fq, w_fk, w_fv, w_ff, w_fg, w_dq, w_dk, w_dv, w_dg = [w_in[:, o[i]:o[i + 1]] for i in range(9)]
    wkv = jnp.concatenate([w_fk, w_fv, w_dk, w_dv], axis=1).astype(BF16)
    wg = jnp.concatenate([w_fg, w_dg], axis=1).astype(BF16)
    wff = jnp.pad(w_ff, ((0, 0), (0, LANES - FOX_HEADS))).astype(BF16)
    bfp = jnp.pad(b_f.reshape(1, FOX_HEADS), ((0, 0), (0, LANES - FOX_HEADS)))
    wq = jnp.concatenate([w_fq, w_dq], axis=1)
    wv = jnp.concatenate([w_fv, w_dv], axis=1)
    if transposed:
        wq = wq.T.astype(BF16)
        wvt = wv.T.astype(BF16)
    else:
        wq = wq.astype(BF16)
        wvt = jnp.zeros((8, LANES), BF16)

    const = lambda shape: pl.BlockSpec(shape, lambda i: (0,) * len(shape))
    rows = lambda w: pl.BlockSpec((tm, w), lambda i: (i, 0))
    in_specs = [rows(d), const((1, d)), const(wkv.shape), const(wg.shape), const(wff.shape),
                const(bfp.shape), const(wq.shape), const(wvt.shape)]
    out_shape = [jax.ShapeDtypeStruct((n, FOX_W), F32)] * 2 + [jax.ShapeDtypeStruct((n, DIFF_W), F32)] * 2
    out_shape += [jax.ShapeDtypeStruct((n, FOX_HEADS), F32), jax.ShapeDtypeStruct((n, FOX_W + DIFF_W), BF16)]
    out_specs = [rows(FOX_W), rows(FOX_W), rows(DIFF_W), rows(DIFF_W), rows(FOX_HEADS), rows(FOX_W + DIFF_W)]
    if transposed:
        out_shape += [jax.ShapeDtypeStruct((FOX_W + DIFF_W, n), BF16),
                      jax.ShapeDtypeStruct((n, FOX_W), BF16), jax.ShapeDtypeStruct((n, DIFF_W), BF16),
                      jax.ShapeDtypeStruct((n // tk, FOX_W + DIFF_W, tk), BF16)]
        out_specs += [pl.BlockSpec((FOX_W + DIFF_W, tm), lambda i: (0, i)), rows(FOX_W), rows(DIFF_W),
                      pl.BlockSpec((tm // tk, FOX_W + DIFF_W, tk), lambda i: (i, 0, 0))]
    else:
        out_shape += [jax.ShapeDtypeStruct((n, FOX_W + DIFF_W), BF16)]
        out_specs += [rows(FOX_W + DIFF_W)]
    return pl.pallas_call(
        functools.partial(_inproj_body, transposed=transposed, tk=tk, qscale=qscale),
        grid=(n // tm,), in_specs=in_specs, out_specs=out_specs, out_shape=out_shape,
        compiler_params=_cparams(1), name="in_projection_t" if transposed else "in_projection_r",
    )(x2d, g_pre.reshape(1, d), wkv, wg, wff, bfp, wq, wvt)


def _cumsum_body(x_ref, cum_ref, hi_ref, mid_ref, lo_ref, *, seg, group, scale):
    x = x_ref[...]
    r = x.shape[0]
    a = lax.broadcasted_iota(jnp.int32, (LANES, LANES), 0)
    b = lax.broadcasted_iota(jnp.int32, (LANES, LANES), 1)
    sh = int(math.log2(seg))
    upper = ((a <= b) & ((a >> sh) == (b >> sh))).astype(BF16)
    pieces = _split3(x)
    w = sum(_dot(p, upper) for p in pieces)
    if group > 1:
        ones = jnp.ones((LANES, LANES), BF16)
        tot = sum(_dot(p, ones) for p in pieces)
        ri = lax.broadcasted_iota(jnp.int32, (r, r), 0)
        ci = lax.broadcasted_iota(jnp.int32, (r, r), 1)
        gsh = int(math.log2(group))
        lower = ((ci < ri) & ((ri >> gsh) == (ci >> gsh))).astype(BF16)
        w = w + sum(_dot(lower, p) for p in _split3(tot))
    cum_ref[...] = w
    hi, mid, lo = _split3(w * scale)
    hi_ref[...] = hi
    mid_ref[...] = mid
    lo_ref[...] = lo


def _prefix_sums(x2d, *, seg, group, rows_per_step, scale):
    r = x2d.shape[0]
    rb = min(rows_per_step, r)
    assert r % rb == 0 and rb % group == 0 and (seg == LANES or group == 1)
    assert seg & (seg - 1) == 0 and group & (group - 1) == 0
    spec = pl.BlockSpec((rb, LANES), lambda i: (i, 0))
    return pl.pallas_call(
        functools.partial(_cumsum_body, seg=seg, group=group, scale=scale),
        grid=(r // rb,), in_specs=[spec], out_specs=[spec] * 4,
        out_shape=[jax.ShapeDtypeStruct((r, LANES), F32)] + [jax.ShapeDtypeStruct((r, LANES), BF16)] * 3,
        compiler_params=_cparams(1), name="prefix_sums",
    )(x2d)


SUM_ROWS = 16
SCORES_AHEAD = 2
FAR_UNROLL = 4


def _with_sum_rows(vt):
    row = lax.broadcasted_iota(jnp.int32, (SUM_ROWS, vt.shape[1]), 0)
    return jnp.concatenate([vt, jnp.where(row == 0, 1.0, 0.0).astype(BF16)], axis=0)


def _online_update_t(s, h, m_ref, acc_ref, vt_aug):
    m = m_ref[h]
    mn = jnp.maximum(m, jnp.max(s, axis=0, keepdims=True))
    alpha = jnp.exp2(m - mn)
    p = jnp.exp2(s - mn)
    acc_ref[h] = alpha * acc_ref[h] + _dot(vt_aug, p.astype(BF16))
    m_ref[h] = mn


def _run_far_tiles(n, half_step, sa_ref, sb_ref):
    def unrolled(base, count):
        bufs = (sa_ref, sb_ref)
        for u in range(count):
            half_step(base + u, bufs[u % 2], bufs[(u + 1) % 2])

    def body(jj, c):
        unrolled(FAR_UNROLL * jj, FAR_UNROLL)
        return c

    lax.fori_loop(0, n // FAR_UNROLL, body, 0)
    for count in range(FAR_UNROLL - 2, 0, -2):
        @pl.when(n % FAR_UNROLL // 2 * 2 == count)
        def _():
            unrolled(n // FAR_UNROLL * FAR_UNROLL, count)


def _init_state(m_ref, l_ref, acc_ref):
    m_ref[...] = jnp.full(m_ref.shape, NEG_INF, F32)
    if l_ref is not None:
        l_ref[...] = jnp.zeros(l_ref.shape, F32)
    acc_ref[...] = jnp.zeros(acc_ref.shape, F32)


def _resident(shape, index_map):
    return pl.BlockSpec(shape, index_map, pipeline_mode=pl.Buffered(1))


def _fox_prompt_body(q_ref, qx_ref, k_ref, kx_ref, vt_ref, gate_ref, o_ref, m_ref, acc_ref, sa_ref, sb_ref, *,
                     t, pairs):
    i = pl.program_id(1)
    heads = 2 * pairs
    _init_state(m_ref, None, acc_ref)
    z64 = jnp.zeros((FOX_HD, t), BF16)
    zpad = jnp.zeros((LANES - QEXT_ROWS, t), BF16)
    qa = []
    for p in range(pairs):
        q = q_ref[p * HEAD_PAIR_W:(p + 1) * HEAD_PAIR_W, :]
        qa.append(jnp.concatenate([q[:FOX_HD], z64, qx_ref[2 * p], zpad], axis=0))
        qa.append(jnp.concatenate([z64, q[FOX_HD:], qx_ref[2 * p + 1], zpad], axis=0))

    def keys(j, p):
        off = pl.multiple_of(j * t, t)
        return jnp.concatenate([k_ref[pl.ds(off, t), p * HEAD_PAIR_W:(p + 1) * HEAD_PAIR_W],
                                kx_ref[pl.ds(off, t), :]], axis=1)

    def consume(j, cur_ref, h, diagonal):
        s = cur_ref[h]
        if diagonal:
            kk = lax.broadcasted_iota(jnp.int32, (t, t), 0)
            qq = lax.broadcasted_iota(jnp.int32, (t, t), 1)
            s = jnp.where(kk <= qq, s, NEG_INF)
        _online_update_t(s, h, m_ref, acc_ref, _with_sum_rows(vt_ref[j, h * FOX_HD:(h + 1) * FOX_HD, :]))

    def half_step(j, cur_ref, nxt_ref):
        ka = {}

        def prefetch(h):
            if h // 2 not in ka:
                ka[h // 2] = keys(j + 1, h // 2)
            nxt_ref[h] = _dot(ka[h // 2], qa[h])

        for h in range(min(SCORES_AHEAD, heads)):
            prefetch(h)
        for h in range(heads):
            consume(j, cur_ref, h, False)
            if h + SCORES_AHEAD < heads:
                prefetch(h + SCORES_AHEAD)

    def last_step(j, cur_ref):
        for h in range(heads):
            consume(j, cur_ref, h, True)

    for p in range(pairs):
        ka = keys(0, p)
        sa_ref[2 * p] = _dot(ka, qa[2 * p])
        sa_ref[2 * p + 1] = _dot(ka, qa[2 * p + 1])

    _run_far_tiles(i, lambda j, cur, nxt: half_step(j, cur, nxt), sa_ref, sb_ref)

    @pl.when(i % 2 == 0)
    def _():
        last_step(i, sa_ref)

    @pl.when(i % 2 == 1)
    def _():
        half_step(i - 1, sa_ref, sb_ref)
        last_step(i, sb_ref)

    for p in range(pairs):
        a0, a1 = acc_ref[2 * p], acc_ref[2 * p + 1]
        ot = jnp.concatenate([a0[:FOX_HD] / a0[FOX_HD:FOX_HD + 1], a1[:FOX_HD] / a1[FOX_HD:FOX_HD + 1]], axis=0)
        cols = slice(p * HEAD_PAIR_W, (p + 1) * HEAD_PAIR_W)
        o_ref[:, cols] = (ot.T * gate_ref[:, cols].astype(F32)).astype(BF16)


def _fox_prompt_attention(qt, qext, k16, kext, vt, gate, *, t, pairs):
    n = k16.shape[0]
    nq = n // t
    w = pairs * HEAD_PAIR_W
    heads = 2 * pairs
    assert (FOX_HEADS // 2) % pairs == 0
    return pl.pallas_call(
        functools.partial(_fox_prompt_body, t=t, pairs=pairs),
        grid=(FOX_HEADS // heads, nq),
        in_specs=[pl.BlockSpec((w, t), lambda g, i: (g, i)),
                  pl.BlockSpec((heads, QEXT_ROWS, t), lambda g, i: (g, 0, i)),
                  _resident((n, w), lambda g, i: (0, g)),
                  _resident((n, LANES), lambda g, i: (0, 0)),
                  _resident((nq, w, t), lambda g, i: (0, g, 0)),
                  pl.BlockSpec((t, w), lambda g, i: (i, g))],
        out_specs=pl.BlockSpec((t, w), lambda g, i: (i, g)),
        out_shape=jax.ShapeDtypeStruct((n, FOX_W), BF16),
        scratch_shapes=[pltpu.VMEM((heads, 1, t), F32), pltpu.VMEM((heads, FOX_HD + SUM_ROWS, t), F32),
                        pltpu.VMEM((heads, t, t), F32), pltpu.VMEM((heads, t, t), F32)],
        compiler_params=_cparams(2), name="fox_prompt_attention",
    )(qt, qext, k16, kext, vt, gate)


def _lambda_value(lq1, lk1, lq2, lk2, lam_init):
    a = jnp.sum(lq1 * lk1, axis=-1, keepdims=True)
    b = jnp.sum(lq2 * lk2, axis=-1, keepdims=True)
    return jnp.exp(a) - jnp.exp(b) + lam_init


def _subln_gate(y, g, gate, lam_init):
    ms = jnp.mean(y * y, axis=-1, keepdims=True)
    yn = (y * lax.rsqrt(ms + EPS)) * g
    return ((yn * (1.0 - lam_init)) * gate.astype(F32)).astype(BF16)


def _diff_prompt_body(q_ref, k_ref, vt_ref, bd_ref, bs_ref, gate_ref, sg_ref, lq1_ref, lk1_ref, lq2_ref,
                      lk2_ref, o_ref, m_ref, acc_ref, sa_ref, sb_ref, *, t, heads, lam_init):
    i = pl.program_id(1)
    w = 2 * DIFF_HD
    _init_state(m_ref, None, acc_ref)
    z64 = jnp.zeros((DIFF_HD, t), BF16)
    qa = []
    for h in range(heads):
        q = q_ref[h * w:(h + 1) * w, :]
        qa.append(jnp.concatenate([q[:DIFF_HD], z64], axis=0))
        qa.append(jnp.concatenate([z64, q[DIFF_HD:]], axis=0))

    def keys(j, h):
        return k_ref[pl.ds(pl.multiple_of(j * t, t), t), h * w:(h + 1) * w]

    def consume(j, cur_ref, c, bias_ref):
        h = c // 2
        s = cur_ref[c]
        if bias_ref is not None:
            s = s + bias_ref[h]
        _online_update_t(s, c, m_ref, acc_ref, _with_sum_rows(vt_ref[j, h * w:(h + 1) * w, :]))

    def half_step(j, cur_ref, nxt_ref, bias_ref):
        kk = {}

        def prefetch(c):
            if c // 2 not in kk:
                kk[c // 2] = keys(j + 1, c // 2)
            nxt_ref[c] = _dot(kk[c // 2], qa[c])

        for c in range(min(SCORES_AHEAD, 2 * heads)):
            prefetch(c)
        for c in range(2 * heads):
            consume(j, cur_ref, c, bias_ref)
            if c + SCORES_AHEAD < 2 * heads:
                prefetch(c + SCORES_AHEAD)

    def last_step(j, cur_ref):
        for c in range(2 * heads):
            consume(j, cur_ref, c, bd_ref)

    for h in range(heads):
        kk = keys(0, h)
        sa_ref[2 * h] = _dot(kk, qa[2 * h])
        sa_ref[2 * h + 1] = _dot(kk, qa[2 * h + 1])

    _run_far_tiles(jnp.maximum(i - 1, 0), lambda j, cur, nxt: half_step(j, cur, nxt, None), sa_ref, sb_ref)

    @pl.when(i == 0)
    def _():
        last_step(i, sa_ref)

    @pl.when(i % 2 == 1)
    def _():
        half_step(i - 1, sa_ref, sb_ref, bs_ref)
        last_step(i, sb_ref)

    @pl.when((i >= 2) & (i % 2 == 0))
    def _():
        half_step(i - 2, sa_ref, sb_ref, None)
        half_step(i - 1, sb_ref, sa_ref, bs_ref)
        last_step(i, sa_ref)

    lam = _lambda_value(lq1_ref[...], lk1_ref[...], lq2_ref[...], lk2_ref[...], lam_init)
    for h in range(heads):
        a0, a1 = acc_ref[2 * h], acc_ref[2 * h + 1]
        ot = a0[:w] / a0[w:w + 1] - lam * (a1[:w] / a1[w:w + 1])
        cols = slice(h * w, (h + 1) * w)
        o_ref[:, cols] = _subln_gate(ot.T, sg_ref[...], gate_ref[:, cols], lam_init)


def _diff_prompt_attention(qt, dk16, vt, bias_diag, bias_sub, gate, subln_g, lams, *, t, heads, lam_init):
    n = dk16.shape[0]
    nq = n // t
    w = heads * 2 * DIFF_HD
    assert DIFF_HEADS % heads == 0 and FOX_W % w == 0
    fox_blocks = FOX_W // w
    vec = lambda width: pl.BlockSpec((1, width), lambda g, i: (0, 0))
    return pl.pallas_call(
        functools.partial(_diff_prompt_body, t=t, heads=heads, lam_init=lam_init),
        grid=(DIFF_HEADS // heads, nq),
        in_specs=[pl.BlockSpec((w, t), lambda g, i: (fox_blocks + g, i)),
                  _resident((n, w), lambda g, i: (0, g)),
                  _resident((nq, w, t), lambda g, i: (0, fox_blocks + g, 0)),
                  pl.BlockSpec((heads, t, t), lambda g, i: (g, 0, 0)),
                  pl.BlockSpec((heads, t, t), lambda g, i: (g, 0, 0)),
                  pl.BlockSpec((t, w), lambda g, i: (i, fox_blocks + g)),
                  vec(2 * DIFF_HD), vec(DIFF_HD), vec(DIFF_HD), vec(DIFF_HD), vec(DIFF_HD)],
        out_specs=pl.BlockSpec((t, w), lambda g, i: (i, g)),
        out_shape=jax.ShapeDtypeStruct((n, DIFF_W), BF16),
        scratch_shapes=[pltpu.VMEM((2 * heads, 1, t), F32),
                        pltpu.VMEM((2 * heads, 2 * DIFF_HD + SUM_ROWS, t), F32),
                        pltpu.VMEM((2 * heads, t, t), F32), pltpu.VMEM((2 * heads, t, t), F32)],
        compiler_params=_cparams(2), name="diff_prompt_attention",
    )(qt, dk16, vt, bias_diag, bias_sub, gate, subln_g, *lams)


def _block_diag_queries(qb):
    lane = lax.broadcasted_iota(jnp.int32, qb.shape, 1)
    zero = jnp.zeros_like(qb)
    return jnp.concatenate([jnp.where(lane < FOX_HD, qb, zero), jnp.where(lane >= FOX_HD, qb, zero)], axis=0)


def _fox_sample_body(q_ref, kc_ref, vc_ref, kn_ref, vn_ref, bc_ref, bn_ref, gate_ref, o_ref, *, s):
    qbd = _block_diag_queries(q_ref[...])
    p_len = kc_ref.shape[-1]
    kt = kc_ref[0].reshape(HEAD_PAIR_W, p_len).astype(BF16)
    vt = vc_ref[0].reshape(HEAD_PAIR_W, p_len).astype(BF16)
    zrows = jnp.zeros((LANES - s, HEAD_PAIR_W), BF16)
    kn = jnp.concatenate([kn_ref[...].astype(BF16), zrows], axis=0)
    vn = jnp.concatenate([vn_ref[...].astype(BF16), zrows], axis=0)
    bc = bc_ref[0, 0]
    bn = bn_ref[0, 0]

    def rows_bias(b, width):
        return jnp.concatenate([jnp.broadcast_to(b[0:1], (s, width)), jnp.broadcast_to(b[1:2], (s, width))], axis=0)

    sc = _dot(qbd, kt) + rows_bias(bc, p_len)
    sn = _nt_dot(qbd, kn) + rows_bias(bn, LANES)
    qi = lax.broadcasted_iota(jnp.int32, (2 * s, LANES), 0)
    ki = lax.broadcasted_iota(jnp.int32, (2 * s, LANES), 1)
    qi = jnp.where(qi >= s, qi - s, qi)
    sn = jnp.where(ki <= qi, sn, NEG_INF)
    m = jnp.maximum(jnp.max(sc, axis=1, keepdims=True), jnp.max(sn, axis=1, keepdims=True))
    pc = jnp.exp2(sc - m)
    pn = jnp.exp2(sn - m)
    l = jnp.sum(pc, axis=1, keepdims=True) + jnp.sum(pn, axis=1, keepdims=True)
    o = (_nt_dot(pc.astype(BF16), vt) + _dot(pn.astype(BF16), vn)) / l
    lane = lax.broadcasted_iota(jnp.int32, (s, HEAD_PAIR_W), 1)
    out = jnp.where(lane < FOX_HD, o[:s], o[s:])
    o_ref[...] = (out * gate_ref[...].astype(F32)).astype(BF16)


def _fox_sample_attention(q16, cache_kt, cache_vt, fk32, fv32, bias_c, bias_n, gate, *, s):
    b, _, _, p_len = cache_kt.shape
    pairs = FOX_HEADS // 2
    blk = lambda: pl.BlockSpec((s, HEAD_PAIR_W), lambda bi, p: (bi, p))
    cache = lambda: pl.BlockSpec((1, 2, FOX_HD, p_len), lambda bi, p: (bi, p, 0, 0))
    return pl.pallas_call(
        functools.partial(_fox_sample_body, s=s),
        grid=(b, pairs),
        in_specs=[blk(), cache(), cache(), blk(), blk(),
                  pl.BlockSpec((1, 1, 2, p_len), lambda bi, p: (bi, p, 0, 0)),
                  pl.BlockSpec((1, 1, 2, LANES), lambda bi, p: (bi, p, 0, 0)),
                  blk()],
        out_specs=blk(),
        out_shape=jax.ShapeDtypeStruct((b * s, FOX_W), BF16),
        compiler_params=_cparams(2), name="fox_sample_attention",
    )(q16, cache_kt, cache_vt, fk32, fv32, bias_c, bias_n, gate)


def _diff_sample_body(q_ref, kc_ref, vc_ref, kn_ref, vn_ref, bc_ref, bn_ref, gate_ref, sg_ref, lq1_ref,
                      lk1_ref, lq2_ref, lk2_ref, o_ref, m_ref, l_ref, acc_ref, *, s, pc, lam_init):
    c = pl.program_id(1)
    last = pl.num_programs(1) - 1
    w = 2 * DIFF_HD

    @pl.when(c == 0)
    def _():
        _init_state(m_ref, l_ref, acc_ref)

    qbd = [_block_diag_queries(q_ref[:, h * w:(h + 1) * w]) for h in range(DIFF_HEADS)]

    def update(h, sc, v16):
        m = m_ref[h]
        mn = jnp.maximum(m, jnp.max(sc, axis=1, keepdims=True))
        alpha = jnp.exp2(m - mn)
        p = jnp.exp2(sc - mn)
        l_ref[h] = alpha * l_ref[h] + jnp.sum(p, axis=1, keepdims=True)
        acc_ref[h] = alpha * acc_ref[h] + _dot(p.astype(BF16), v16)
        m_ref[h] = mn

    def chunk(with_bias):
        scores = [_nt_dot(qbd[h], kc_ref[0, pl.ds(h, pc, stride=DIFF_HEADS), :].astype(BF16))
                  for h in range(DIFF_HEADS)]
        for h, sc in enumerate(scores):
            if with_bias:
                sc = sc + jnp.concatenate([bc_ref[h], bc_ref[h]], axis=0)
            update(h, sc, vc_ref[0, pl.ds(h, pc, stride=DIFF_HEADS), :].astype(BF16))

    @pl.when(c != last)
    def _():
        chunk(False)

    @pl.when(c == last)
    def _():
        chunk(True)
        lam = _lambda_value(lq1_ref[...], lk1_ref[...], lq2_ref[...], lk2_ref[...], lam_init)
        zrows = jnp.zeros((LANES - s, w), BF16)
        for h in range(DIFF_HEADS):
            cols = slice(h * w, (h + 1) * w)
            kn = jnp.concatenate([kn_ref[:, cols].astype(BF16), zrows], axis=0)
            vn = jnp.concatenate([vn_ref[:, cols].astype(BF16), zrows], axis=0)
            sn = _nt_dot(qbd[h], kn) + jnp.concatenate([bn_ref[h], bn_ref[h]], axis=0)
            update(h, sn, vn)
            o = acc_ref[h] / l_ref[h]
            o = o[:s] - lam * o[s:]
            o_ref[:, cols] = _subln_gate(o, sg_ref[...], gate_ref[:, cols], lam_init)


def _diff_sample_attention(q16, cache_k, cache_v, dk32, dv32, bias_c, bias_n, gate, subln_g, lams, *, s, pc,
                           lam_init):
    b, rows, w = cache_k.shape
    p_len = rows // DIFF_HEADS
    assert p_len % pc == 0 and bias_c.shape == (DIFF_HEADS, s, pc)
    cache = lambda: pl.BlockSpec((1, DIFF_HEADS * pc, w), lambda bi, c: (bi, c, 0))
    new = lambda: pl.BlockSpec((s, DIFF_W), lambda bi, c: (bi, 0))
    half = lambda: pl.BlockSpec((s, DIFF_W), lambda bi, c: (bi, 1))
    const = lambda shape: pl.BlockSpec(shape, lambda bi, c: (0,) * len(shape))
    return pl.pallas_call(
        functools.partial(_diff_sample_body, s=s, pc=pc, lam_init=lam_init),
        grid=(b, p_len // pc),
        in_specs=[half(), cache(), cache(), new(), new(), const(bias_c.shape), const(bias_n.shape), half(),
                  const((1, w)), const((1, DIFF_HD)), const((1, DIFF_HD)), const((1, DIFF_HD)), const((1, DIFF_HD))],
        out_specs=new(),
        out_shape=jax.ShapeDtypeStruct((b * s, DIFF_W), BF16),
        scratch_shapes=[pltpu.VMEM((DIFF_HEADS, 2 * s, 1), F32), pltpu.VMEM((DIFF_HEADS, 2 * s, 1), F32),
                        pltpu.VMEM((DIFF_HEADS, 2 * s, w), F32)],
        compiler_params=_cparams(2), name="diff_sample_attention",
    )(q16, cache_k, cache_v, dk32, dv32, bias_c, bias_n, gate, subln_g, *lams)


def _outproj_body(mf_ref, md_ref, w_ref, x_ref, g_ref, y_ref):
    out = _dot(mf_ref[...], w_ref[:FOX_W, :]) + _dot(md_ref[...], w_ref[FOX_W:, :])
    ms = jnp.mean(out * out, axis=-1, keepdims=True)
    y_ref[...] = x_ref[...] + (out * lax.rsqrt(ms + EPS)) * g_ref[...]


def _out_projection(mf, md, w_out, x2d, g_post, *, tm):
    n, d = x2d.shape
    tm = min(tm, n)
    assert n % tm == 0
    rows = lambda w: pl.BlockSpec((tm, w), lambda i: (i, 0))
    return pl.pallas_call(
        _outproj_body, grid=(n // tm,),
        in_specs=[rows(FOX_W), rows(DIFF_W), pl.BlockSpec(w_out.shape, lambda i: (0, 0)), rows(d),
                  pl.BlockSpec((1, d), lambda i: (0, 0))],
        out_specs=rows(d), out_shape=jax.ShapeDtypeStruct((n, d), F32),
        compiler_params=_cparams(1), name="out_projection",
    )(mf, md, w_out.astype(BF16), x2d, g_post.reshape(1, d))


def _t5_bucket(rel):
    half = NUM_BUCKETS // 2
    max_exact = half // 2
    ret = jnp.where(rel > 0, half, 0)
    n = jnp.abs(rel)
    nf = jnp.maximum(n, 1).astype(jnp.float32)
    large = max_exact + (jnp.log(nf / max_exact) / math.log(MAX_DISTANCE / max_exact)
                         * (half - max_exact)).astype(jnp.int32)
    large = jnp.minimum(large, half - 1)
    return ret + jnp.where(n < max_exact, n, large)


def _rel_bias_of(rel_bias, rel):
    return rel_bias[_t5_bucket(rel)].astype(F32).T


def _toeplitz(g, rows, cols):
    h = g.shape[0]
    period = rows + cols
    v = jnp.concatenate([g[:, rows - 1:], jnp.zeros((h, 1), g.dtype), g[:, :rows - 1]], axis=1)
    flat = jnp.tile(v, (1, rows))[:, :rows * (period - 1)]
    return flat.reshape(h, rows, period - 1)[:, :, :cols]


def _chunk_mask(bias, row_pos, col_pos, rows_are_keys):
    r = (row_pos // CHUNK)[:, None]
    c = (col_pos // CHUNK)[None, :]
    mask = (r <= c) if rows_are_keys else (c <= r)
    return jnp.where(mask[None], bias * LOG2E, NEG_INF)


def _prompt_layer(x, g_pre, w_in, b_f, lams, subln_g, w_out, g_post, rel_bias, layer_idx, *, t=256, tm=512,
                  fox_pairs=4, diff_heads=4):
    b, s, d = x.shape
    assert b == 1 and s % t == 0 and s // LANES >= 1
    x2d = x.reshape(s, d)
    (fk, fv, dk, dv, logf, gate, qt, k16, dk16, vt) = _in_projection(
        x2d, g_pre, w_in, b_f, transposed=True, tm=tm, tk=t)

    rows = s // LANES
    _, hi, mid, lo = _prefix_sums(logf.T.reshape(FOX_HEADS * rows, LANES), seg=LANES, group=rows,
                                  rows_per_step=FOX_HEADS * rows, scale=LOG2E)
    pieces = jnp.stack([hi, mid, lo], axis=0).reshape(PIECES, FOX_HEADS, s)
    kext = jnp.concatenate([
        (-pieces).transpose(2, 1, 0).reshape(s, ONES_AT), jnp.ones((s, PIECES), BF16),
        jnp.zeros((s, LANES - ONES_AT - PIECES), BF16)], axis=1)
    sel = (jnp.arange(ONES_AT)[None, :] // PIECES == jnp.arange(FOX_HEADS)[:, None]).astype(BF16)
    qext = jnp.concatenate([
        jnp.broadcast_to(sel[:, :, None], (FOX_HEADS, ONES_AT, s)), pieces.transpose(1, 0, 2),
        jnp.zeros((FOX_HEADS, QEXT_ROWS - ONES_AT - PIECES, s), BF16)], axis=1)

    mf = _fox_prompt_attention(qt, qext, k16, kext, vt, gate, t=t, pairs=fox_pairs)

    lam_init = 0.8 - 0.6 * math.exp(-0.3 * layer_idx)
    assert t >= MAX_DISTANCE
    pos = jnp.arange(t, dtype=jnp.int32)
    off = jnp.arange(-(t - 1), t, dtype=jnp.int32)
    far = _rel_bias_of(rel_bias, jnp.full((1,), -2 * t, jnp.int32))
    bias_diag = _chunk_mask(_toeplitz(_rel_bias_of(rel_bias, -off) - far, t, t), pos, pos, True)
    bias_sub = _chunk_mask(_toeplitz(_rel_bias_of(rel_bias, -off - t) - far, t, t), pos, pos + t, True)
    md = _diff_prompt_attention(qt, dk16, vt, bias_diag, bias_sub, gate, subln_g.reshape(1, -1), lams,
                                t=t, heads=diff_heads, lam_init=lam_init)

    y = _out_projection(mf, md, w_out, x2d, g_post, tm=tm)
    return y.reshape(b, s, d), (fk, fv, logf, dk, dv)


def _sample_layer(x, past, g_pre, w_in, b_f, lams, subln_g, w_out, g_post, rel_bias, layer_idx, *, tm=512,
                  diff_chunk=1024):
    b, s, d = x.shape
    pk, pv, plogf, pdk, pdv = past
    p_len = pk.shape[1]
    assert s <= LANES and LANES % s == 0 and p_len % LANES == 0
    x2d = x.reshape(b * s, d)
    fk, fv, dk, dv, logf, gate, q16 = _in_projection(x2d, g_pre, w_in, b_f, transposed=False, tm=tm, tk=tm)

    rows = p_len // LANES
    cum_c, _, _, _ = _prefix_sums(plogf.transpose(0, 2, 1).reshape(b * FOX_HEADS * rows, LANES), seg=LANES,
                                  group=rows, rows_per_step=32 * rows, scale=1.0)
    cum_c = cum_c.reshape(b, FOX_HEADS, p_len)
    bias_c = ((cum_c[:, :, -1:] - cum_c) * LOG2E).reshape(b, FOX_HEADS // 2, 2, p_len)
    cum_n, _, _, _ = _prefix_sums(
        logf.reshape(b, s, FOX_HEADS).transpose(0, 2, 1).reshape(b * FOX_HEADS * s // LANES, LANES),
        seg=s, group=1, rows_per_step=1024, scale=1.0)
    bias_n = jnp.concatenate([-cum_n.reshape(b, FOX_HEADS, s) * LOG2E,
                              jnp.full((b, FOX_HEADS, LANES - s), NEG_INF, F32)], axis=-1)
    bias_n = bias_n.reshape(b, FOX_HEADS // 2, 2, LANES)
    mf = _fox_sample_attention(q16, pk.transpose(0, 2, 3, 1), pv.transpose(0, 2, 3, 1), fk, fv, bias_c, bias_n,
                               gate, s=s)

    lam_init = 0.8 - 0.6 * math.exp(-0.3 * layer_idx)
    pc = min(diff_chunk, p_len)
    assert pc >= MAX_DISTANCE + s
    q_pos = p_len + jnp.arange(s, dtype=jnp.int32)
    far = _rel_bias_of(rel_bias, jnp.full((1,), -2 * p_len, jnp.int32))
    off = jnp.arange(-(s - 1), pc, dtype=jnp.int32)
    bd_c = _chunk_mask(_toeplitz(_rel_bias_of(rel_bias, off - pc) - far, s, pc), q_pos,
                       p_len - pc + jnp.arange(pc, dtype=jnp.int32), False)
    off = jnp.arange(-(s - 1), s, dtype=jnp.int32)
    bd_n = _chunk_mask(_toeplitz(_rel_bias_of(rel_bias, off) - far, s, s), q_pos, q_pos, False)
    bd_n = jnp.concatenate([bd_n, jnp.full((DIFF_HEADS, s, LANES - s), NEG_INF, F32)], axis=-1)
    md = _diff_sample_attention(q16, pdk.reshape(b, p_len * DIFF_HEADS, 2 * DIFF_HD),
                                pdv.reshape(b, p_len * DIFF_HEADS, 2 * DIFF_HD), dk, dv, bd_c, bd_n, gate,
                                subln_g.reshape(1, -1), lams, s=s, pc=pc, lam_init=lam_init)

    y = _out_projection(mf, md, w_out, x2d, g_post, tm=tm)
    return y.reshape(b, s, d), (fk, fv, logf, dk, dv)


def kernel(x_prompt, x_sample, cache_fox_k, cache_fox_v, cache_fox_logf, cache_diff_k, cache_diff_v,
           norm_pre_g, w_in, forget_bias, lambda_q1, lambda_k1, lambda_q2, lambda_k2, subln_g, w_out,
           norm_post_g, rel_bias):
    depth = w_in.shape[0]
    y_p, y_s = x_prompt, x_sample
    rows_p, rows_s = [], []
    for l in range(depth):
        lams = tuple(a[l].reshape(1, -1) for a in (lambda_q1, lambda_k1, lambda_q2, lambda_k2))
        params = (norm_pre_g[l], w_in[l], forget_bias[l], lams, subln_g[l], w_out[l], norm_post_g[l], rel_bias, l)
        y_p, new_p = _prompt_layer(y_p, *params)
        past = (cache_fox_k[l], cache_fox_v[l], cache_fox_logf[l], cache_diff_k[l], cache_diff_v[l])
        y_s, new_s = _sample_layer(y_s, past, *params)
        rows_p.append(new_p)
        rows_s.append(new_s)

    bp, sp = x_prompt.shape[:2]
    bs, ss = x_sample.shape[:2]

    def stack(rows, idx, shape):
        return jnp.stack([r[idx].reshape(shape) for r in rows], axis=0)

    outs = [y_p, y_s]
    for rows, (b, s) in ((rows_p, (bp, sp)), (rows_s, (bs, ss))):
        outs += [stack(rows, 0, (b, s, FOX_HEADS, FOX_HD)), stack(rows, 1, (b, s, FOX_HEADS, FOX_HD)),
                 stack(rows, 2, (b, s, FOX_HEADS)), stack(rows, 3, (b, s, DIFF_HEADS, 2 * DIFF_HD)),
                 stack(rows, 4, (b, s, DIFF_HEADS, 2 * DIFF_HD))]
    return tuple(outs)
```

```python
import functools
import math

import jax
import jax.numpy as jnp
import numpy as np
from jax import lax
from jax.experimental import pallas as pl
from jax.experimental.pallas import tpu as pltpu

F32 = jnp.float32
BF16 = jnp.bfloat16

FOX_HEADS = 8
FOX_HD = 64
DIFF_HEADS = 4
DIFF_HD = 64
FOX_W = FOX_HEADS * FOX_HD
DIFF_W = DIFF_HEADS * 2 * DIFF_HD
CHUNK = 64
NUM_BUCKETS = 32
MAX_DISTANCE = 128
EPS = 1e-6
NEG_INF = -1e30
LOG2E = 1.4426950408889634

LANES = 128
HEAD_PAIR_W = 2 * FOX_HD
VMEM_LIMIT_BYTES = 56 * 1024 * 1024

PIECES = 3
ONES_AT = PIECES * FOX_HEADS
QEXT_ROWS = 32


def _cparams(n_axes):
    return pltpu.CompilerParams(
        dimension_semantics=("arbitrary",) * n_axes,
        vmem_limit_bytes=VMEM_LIMIT_BYTES,
    )


def _split3(a):
    hi = a.astype(BF16)
    r1 = a - hi.astype(F32)
    mid = r1.astype(BF16)
    lo = (r1 - mid.astype(F32)).astype(BF16)
    return hi, mid, lo


def _nt_dot(a, b):
    return lax.dot_general(a, b, (((1,), (1,)), ((), ())), preferred_element_type=F32)


def _dot(a, b):
    return jnp.dot(a, b, preferred_element_type=F32)


def _inproj_body(x_ref, g_ref, wkv_ref, wg_ref, wff_ref, bf_ref, wq_ref, wvt_ref,
                 fk_ref, fv_ref, dk_ref, dv_ref, logf_ref, gate_ref, q_ref, *rest,
                 transposed, tk, qscale):
    x = x_ref[...]
    ms = jnp.mean(x * x, axis=-1, keepdims=True)
    h = (x * lax.rsqrt(ms + EPS)) * g_ref[...]
    hb = h.astype(BF16)

    fk = _dot(hb, wkv_ref[:, 0 * FOX_W:1 * FOX_W])
    fk_ref[...] = fk
    fv_ref[...] = _dot(hb, wkv_ref[:, 1 * FOX_W:2 * FOX_W])
    dk = _dot(hb, wkv_ref[:, 2 * FOX_W:3 * FOX_W])
    dk_ref[...] = dk
    dv_ref[...] = _dot(hb, wkv_ref[:, 3 * FOX_W:4 * FOX_W])

    ff = _dot(hb, wff_ref[...])
    logf = jax.nn.log_sigmoid(ff + bf_ref[...])
    logf_ref[...] = logf[:, :FOX_HEADS]

    gates = _dot(hb, wg_ref[...])
    gate_ref[...] = (gates * jax.nn.sigmoid(gates)).astype(BF16)

    if transposed:
        k16_ref, dk16_ref, vt_ref = rest
        k16_ref[...] = fk.astype(BF16)
        dk16_ref[...] = dk.astype(BF16)
        q_ref[...] = (_nt_dot(wq_ref[...], hb) * qscale).astype(BF16)
        vt = _nt_dot(wvt_ref[...], hb).astype(BF16)
        for c in range(vt.shape[1] // tk):
            vt_ref[c] = vt[:, c * tk:(c + 1) * tk]
    else:
        q_ref[...] = (_dot(hb, wq_ref[...]) * qscale).astype(BF16)


def _in_projection(x2d, g_pre, w_in, b_f, *, transposed, tm, tk):
    n, d = x2d.shape
    tm = min(tm, n)
    assert n % tm == 0 and (not transposed or tm % tk == 0)
    qscale = FOX_HD ** -0.5 * LOG2E
    o = np.cumsum([0, FOX_W, FOX_W, FOX_W, FOX_HEADS, FOX_W, DIFF_W, DIFF_W, DIFF_W, DIFF_W])
    w_fq, w_fk, w_fv, w_ff, w_fg, w_dq, w_dk, w_dv, w_dg = [w_in[:, o[i]:o[i + 1]] for i in range(9)]
    wkv = jnp.concatenate([w_fk, w_fv, w_dk, w_dv], axis=1).astype(BF16)
    wg = jnp.concatenate([w_fg, w_dg], axis=1).astype(BF16)
    wff = jnp.pad(w_ff, ((0, 0), (0, LANES - FOX_HEADS))).astype(BF16)
    bfp = jnp.pad(b_f.reshape(1, FOX_HEADS), ((0, 0), (0, LANES - FOX_HEADS)))
    wq = jnp.concatenate([w_fq, w_dq], axis=1)
    wv = jnp.concatenate([w_fv, w_dv], axis=1)
    if transposed:
        wq = wq.T.astype(BF16)
        wvt = wv.T.astype(BF16)
    else:
        wq = wq.astype(BF16)
        wvt = jnp.zeros((8, LANES), BF16)

    const = lambda shape: pl.BlockSpec(shape, lambda i: (0,) * len(shape))
    rows = lambda w: pl.BlockSpec((tm, w), lambda i: (i, 0))
    in_specs = [rows(d), const((1, d)), const(wkv.shape), const(wg.shape), const(wff.shape),
                const(bfp.shape), const(wq.shape), const(wvt.shape)]
    out_shape = [jax.ShapeDtypeStruct((n, FOX_W), F32)] * 2 + [jax.ShapeDtypeStruct((n, DIFF_W), F32)] * 2
    out_shape += [jax.ShapeDtypeStruct((n, FOX_HEADS), F32), jax.ShapeDtypeStruct((n, FOX_W + DIFF_W), BF16)]
    out_specs = [rows(FOX_W), rows(FOX_W), rows(DIFF_W), rows(DIFF_W), rows(FOX_HEADS), rows(FOX_W + DIFF_W)]
    if transposed:
        out_shape += [jax.ShapeDtypeStruct((FOX_W + DIFF_W, n), BF16),
                      jax.ShapeDtypeStruct((n, FOX_W), BF16), jax.ShapeDtypeStruct((n, DIFF_W), BF16),
                      jax.ShapeDtypeStruct((n // tk, FOX_W + DIFF_W, tk), BF16)]
        out_specs += [pl.BlockSpec((FOX_W + DIFF_W, tm), lambda i: (0, i)), rows(FOX_W), rows(DIFF_W),
                      pl.BlockSpec((tm // tk, FOX_W + DIFF_W, tk), lambda i: (i, 0, 0))]
    else:
        out_shape += [jax.ShapeDtypeStruct((n, FOX_W + DIFF_W), BF16)]
        out_specs += [rows(FOX_W + DIFF_W)]
    return pl.pallas_call(
        functools.partial(_inproj_body, transposed=transposed, tk=tk, qscale=qscale),
        grid=(n // tm,), in_specs=in_specs, out_specs=out_specs, out_shape=out_shape,
        compiler_params=_cparams(1), name="in_projection_t" if transposed else "in_projection_r",
    )(x2d, g_pre.reshape(1, d), wkv, wg, wff, bfp, wq, wvt)


def _cumsum_body(x_ref, cum_ref, hi_ref, mid_ref, lo_ref, *, seg, group, scale):
    x = x_ref[...]
    r = x.shape[0]
    a = lax.broadcasted_iota(jnp.int32, (LANES, LANES), 0)
    b = lax.broadcasted_iota(jnp.int32, (LANES, LANES), 1)
    sh = int(math.log2(seg))
    upper = ((a <= b) & ((a >> sh) == (b >> sh))).astype(BF16)
    pieces = _split3(x)
    w = sum(_dot(p, upper) for p in pieces)
    if group > 1:
        ones = jnp.ones((LANES, LANES), BF16)
        tot = sum(_dot(p, ones) for p in pieces)
        ri = lax.broadcasted_iota(jnp.int32, (r, r), 0)
        ci = lax.broadcasted_iota(jnp.int32, (r, r), 1)
        gsh = int(math.log2(group))
        lower = ((ci < ri) & ((ri >> gsh) == (ci >> gsh))).astype(BF16)
        w = w + sum(_dot(lower, p) for p in _split3(tot))
    cum_ref[...] = w
    hi, mid, lo = _split3(w * scale)
    hi_ref[...] = hi
    mid_ref[...] = mid
    lo_ref[...] = lo


def _prefix_sums(x2d, *, seg, group, rows_per_step, scale):
    r = x2d.shape[0]
    rb = min(rows_per_step, r)
    assert r % rb == 0 and rb % group == 0 and (seg == LANES or group == 1)
    assert seg & (seg - 1) == 0 and group & (group - 1) == 0
    spec = pl.BlockSpec((rb, LANES), lambda i: (i, 0))
    return pl.pallas_call(
        functools.partial(_cumsum_body, seg=seg, group=group, scale=scale),
        grid=(r // rb,), in_specs=[spec], out_specs=[spec] * 4,
        out_shape=[jax.ShapeDtypeStruct((r, LANES), F32)] + [jax.ShapeDtypeStruct((r, LANES), BF16)] * 3,
        compiler_params=_cparams(1), name="prefix_sums",
    )(x2d)


SUM_ROWS = 16
SCORES_AHEAD = 2
FAR_UNROLL = 4


def _with_sum_rows(vt):
    row = lax.broadcasted_iota(jnp.int32, (SUM_ROWS, vt.shape[1]), 0)
    return jnp.concatenate([vt, jnp.where(row == 0, 1.0, 0.0).astype(BF16)], axis=0)


def _store_scores(buf, h, s):
    buf[0][h] = s
    buf[1][h] = jnp.max(s, axis=0, keepdims=True)


def _online_update_t(s, tile_max, h, m_ref, acc_ref, vt_aug):
    m = m_ref[h]
    mn = jnp.maximum(m, tile_max)
    alpha = jnp.exp2(m - mn)
    p = jnp.exp2(s - mn)
    acc_ref[h] = alpha * acc_ref[h] + _dot(vt_aug, p.astype(BF16))
    m_ref[h] = mn


def _run_far_tiles(n, half_step, buf_a, buf_b):
    def unrolled(base, count):
        bufs = (buf_a, buf_b)
        for u in range(count):
            half_step(base + u, bufs[u % 2], bufs[(u + 1) % 2])

    def body(jj, c):
        unrolled(FAR_UNROLL * jj, FAR_UNROLL)
        return c

    lax.fori_loop(0, n // FAR_UNROLL, body, 0)
    for count in range(FAR_UNROLL - 2, 0, -2):
        @pl.when(n % FAR_UNROLL // 2 * 2 == count)
        def _():
            unrolled(n // FAR_UNROLL * FAR_UNROLL, count)


def _init_state(m_ref, l_ref, acc_ref):
    m_ref[...] = jnp.full(m_ref.shape, NEG_INF, F32)
    if l_ref is not None:
        l_ref[...] = jnp.zeros(l_ref.shape, F32)
    acc_ref[...] = jnp.zeros(acc_ref.shape, F32)


def _resident(shape, index_map):
    return pl.BlockSpec(shape, index_map, pipeline_mode=pl.Buffered(1))


def _fox_prompt_body(q_ref, qx_ref, k_ref, kx_ref, vt_ref, gate_ref, o_ref, m_ref, acc_ref, sa_ref, sb_ref,
                     ma_ref, mb_ref, *, t, pairs):
    i = pl.program_id(1)
    heads = 2 * pairs
    buf_a, buf_b = (sa_ref, ma_ref), (sb_ref, mb_ref)
    _init_state(m_ref, None, acc_ref)
    z64 = jnp.zeros((FOX_HD, t), BF16)
    zpad = jnp.zeros((LANES - QEXT_ROWS, t), BF16)
    qa = []
    for p in range(pairs):
        q = q_ref[p * HEAD_PAIR_W:(p + 1) * HEAD_PAIR_W, :]
        qa.append(jnp.concatenate([q[:FOX_HD], z64, qx_ref[2 * p], zpad], axis=0))
        qa.append(jnp.concatenate([z64, q[FOX_HD:], qx_ref[2 * p + 1], zpad], axis=0))

    def keys(j, p):
        off = pl.multiple_of(j * t, t)
        return jnp.concatenate([k_ref[pl.ds(off, t), p * HEAD_PAIR_W:(p + 1) * HEAD_PAIR_W],
                                kx_ref[pl.ds(off, t), :]], axis=1)

    def consume(j, cur, h, diagonal):
        s = cur[0][h]
        if diagonal:
            kk = lax.broadcasted_iota(jnp.int32, (t, t), 0)
            qq = lax.broadcasted_iota(jnp.int32, (t, t), 1)
            s = jnp.where(kk <= qq, s, NEG_INF)
            tile_max = jnp.max(s, axis=0, keepdims=True)
        else:
            tile_max = cur[1][h]
        _online_update_t(s, tile_max, h, m_ref, acc_ref,
                         _with_sum_rows(vt_ref[j, h * FOX_HD:(h + 1) * FOX_HD, :]))

    def half_step(j, cur, nxt):
        ka = {}

        def prefetch(h):
            if h // 2 not in ka:
                ka[h // 2] = keys(j + 1, h // 2)
            _store_scores(nxt, h, _dot(ka[h // 2], qa[h]))

        for h in range(min(SCORES_AHEAD, heads)):
            prefetch(h)
        for h in range(heads):
            consume(j, cur, h, False)
            if h + SCORES_AHEAD < heads:
                prefetch(h + SCORES_AHEAD)

    def last_step(j, cur):
        for h in range(heads):
            consume(j, cur, h, True)

    for p in range(pairs):
        ka = keys(0, p)
        _store_scores(buf_a, 2 * p, _dot(ka, qa[2 * p]))
        _store_scores(buf_a, 2 * p + 1, _dot(ka, qa[2 * p + 1]))

    _run_far_tiles(i, half_step, buf_a, buf_b)

    @pl.when(i % 2 == 0)
    def _():
        last_step(i, buf_a)

    @pl.when(i % 2 == 1)
    def _():
        half_step(i - 1, buf_a, buf_b)
        last_step(i, buf_b)

    for p in range(pairs):
        a0, a1 = acc_ref[2 * p], acc_ref[2 * p + 1]
        ot = jnp.concatenate([a0[:FOX_HD] / a0[FOX_HD:FOX_HD + 1], a1[:FOX_HD] / a1[FOX_HD:FOX_HD + 1]], axis=0)
        cols = slice(p * HEAD_PAIR_W, (p + 1) * HEAD_PAIR_W)
        o_ref[:, cols] = (ot.T * gate_ref[:, cols].astype(F32)).astype(BF16)


def _fox_prompt_attention(qt, qext, k16, kext, vt, gate, *, t, pairs):
    n = k16.shape[0]
    nq = n // t
    w = pairs * HEAD_PAIR_W
    heads = 2 * pairs
    assert (FOX_HEADS // 2) % pairs == 0
    return pl.pallas_call(
        functools.partial(_fox_prompt_body, t=t, pairs=pairs),
        grid=(FOX_HEADS // heads, nq),
        in_specs=[pl.BlockSpec((w, t), lambda g, i: (g, i)),
                  pl.BlockSpec((heads, QEXT_ROWS, t), lambda g, i: (g, 0, i)),
                  _resident((n, w), lambda g, i: (0, g)),
                  _resident((n, LANES), lambda g, i: (0, 0)),
                  _resident((nq, w, t), lambda g, i: (0, g, 0)),
                  pl.BlockSpec((t, w), lambda g, i: (i, g))],
        out_specs=pl.BlockSpec((t, w), lambda g, i: (i, g)),
        out_shape=jax.ShapeDtypeStruct((n, FOX_W), BF16),
        scratch_shapes=[pltpu.VMEM((heads, 1, t), F32), pltpu.VMEM((heads, FOX_HD + SUM_ROWS, t), F32),
                        pltpu.VMEM((heads, t, t), F32), pltpu.VMEM((heads, t, t), F32),
                        pltpu.VMEM((heads, 1, t), F32), pltpu.VMEM((heads, 1, t), F32)],
        compiler_params=_cparams(2), name="fox_prompt_attention",
    )(qt, qext, k16, kext, vt, gate)


def _lambda_value(lq1, lk1, lq2, lk2, lam_init):
    a = jnp.sum(lq1 * lk1, axis=-1, keepdims=True)
    b = jnp.sum(lq2 * lk2, axis=-1, keepdims=True)
    return jnp.exp(a) - jnp.exp(b) + lam_init


def _subln_gate(y, g, gate, lam_init):
    ms = jnp.mean(y * y, axis=-1, keepdims=True)
    yn = (y * lax.rsqrt(ms + EPS)) * g
    return ((yn * (1.0 - lam_init)) * gate.astype(F32)).astype(BF16)


def _diff_prompt_body(q_ref, k_ref, vt_ref, bd_ref, bs_ref, gate_ref, sg_ref, lq1_ref, lk1_ref, lq2_ref,
                      lk2_ref, o_ref, m_ref, acc_ref, sa_ref, sb_ref, ma_ref, mb_ref, *, t, heads, lam_init):
    i = pl.program_id(1)
    w = 2 * DIFF_HD
    buf_a, buf_b = (sa_ref, ma_ref), (sb_ref, mb_ref)
    _init_state(m_ref, None, acc_ref)
    z64 = jnp.zeros((DIFF_HD, t), BF16)
    qa = []
    for h in range(heads):
        q = q_ref[h * w:(h + 1) * w, :]
        qa.append(jnp.concatenate([q[:DIFF_HD], z64], axis=0))
        qa.append(jnp.concatenate([z64, q[DIFF_HD:]], axis=0))

    def keys(j, h):
        return k_ref[pl.ds(pl.multiple_of(j * t, t), t), h * w:(h + 1) * w]

    def consume(j, cur, c, bias_ref):
        h = c // 2
        s = cur[0][c]
        if bias_ref is not None:
            s = s + bias_ref[h]
            tile_max = jnp.max(s, axis=0, keepdims=True)
        else:
            tile_max = cur[1][c]
        _online_update_t(s, tile_max, c, m_ref, acc_ref, _with_sum_rows(vt_ref[j, h * w:(h + 1) * w, :]))

    def half_step(j, cur, nxt, bias_ref=None):
        kk = {}

        def prefetch(c):
            if c // 2 not in kk:
                kk[c // 2] = keys(j + 1, c // 2)
            _store_scores(nxt, c, _dot(kk[c // 2], qa[c]))

        for c in range(min(SCORES_AHEAD, 2 * heads)):
            prefetch(c)
        for c in range(2 * heads):
            consume(j, cur, c, bias_ref)
            if c + SCORES_AHEAD < 2 * heads:
                prefetch(c + SCORES_AHEAD)

    def last_step(j, cur):
        for c in range(2 * heads):
            consume(j, cur, c, bd_ref)

    for h in range(heads):
        kk = keys(0, h)
        _store_scores(buf_a, 2 * h, _dot(kk, qa[2 * h]))
        _store_scores(buf_a, 2 * h + 1, _dot(kk, qa[2 * h + 1]))

    _run_far_tiles(jnp.maximum(i - 1, 0), half_step, buf_a, buf_b)

    @pl.when(i == 0)
    def _():
        last_step(i, buf_a)

    @pl.when(i % 2 == 1)
    def _():
        half_step(i - 1, buf_a, buf_b, bs_ref)
        last_step(i, buf_b)

    @pl.when((i >= 2) & (i % 2 == 0))
    def _():
        half_step(i - 2, buf_a, buf_b)
        half_step(i - 1, buf_b, buf_a, bs_ref)
        last_step(i, buf_a)

    lam = _lambda_value(lq1_ref[...], lk1_ref[...], lq2_ref[...], lk2_ref[...], lam_init)
    for h in range(heads):
        a0, a1 = acc_ref[2 * h], acc_ref[2 * h + 1]
        ot = a0[:w] / a0[w:w + 1] - lam * (a1[:w] / a1[w:w + 1])
        cols = slice(h * w, (h + 1) * w)
        o_ref[:, cols] = _subln_gate(ot.T, sg_ref[...], gate_ref[:, cols], lam_init)


def _diff_prompt_attention(qt, dk16, vt, bias_diag, bias_sub, gate, subln_g, lams, *, t, heads, lam_init):
    n = dk16.shape[0]
    nq = n // t
    w = heads * 2 * DIFF_HD
    assert DIFF_HEADS % heads == 0 and FOX_W % w == 0
    fox_blocks = FOX_W // w
    vec = lambda width: pl.BlockSpec((1, width), lambda g, i: (0, 0))
    return pl.pallas_call(
        functools.partial(_diff_prompt_body, t=t, heads=heads, lam_init=lam_init),
        grid=(DIFF_HEADS // heads, nq),
        in_specs=[pl.BlockSpec((w, t), lambda g, i: (fox_blocks + g, i)),
                  _resident((n, w), lambda g, i: (0, g)),
                  _resident((nq, w, t), lambda g, i: (0, fox_blocks + g, 0)),
                  pl.BlockSpec((heads, t, t), lambda g, i: (g, 0, 0)),
                  pl.BlockSpec((heads, t, t), lambda g, i: (g, 0, 0)),
                  pl.BlockSpec((t, w), lambda g, i: (i, fox_blocks + g)),
                  vec(2 * DIFF_HD), vec(DIFF_HD), vec(DIFF_HD), vec(DIFF_HD), vec(DIFF_HD)],
        out_specs=pl.BlockSpec((t, w), lambda g, i: (i, g)),
        out_shape=jax.ShapeDtypeStruct((n, DIFF_W), BF16),
        scratch_shapes=[pltpu.VMEM((2 * heads, 1, t), F32),
                        pltpu.VMEM((2 * heads, 2 * DIFF_HD + SUM_ROWS, t), F32),
                        pltpu.VMEM((2 * heads, t, t), F32), pltpu.VMEM((2 * heads, t, t), F32),
                        pltpu.VMEM((2 * heads, 1, t), F32), pltpu.VMEM((2 * heads, 1, t), F32)],
        compiler_params=_cparams(2), name="diff_prompt_attention",
    )(qt, dk16, vt, bias_diag, bias_sub, gate, subln_g, *lams)


def _block_diag_queries(qb):
    lane = lax.broadcasted_iota(jnp.int32, qb.shape, 1)
    zero = jnp.zeros_like(qb)
    return jnp.concatenate([jnp.where(lane < FOX_HD, qb, zero), jnp.where(lane >= FOX_HD, qb, zero)], axis=0)


def _fox_sample_body(q_ref, kc_ref, vc_ref, kn_ref, vn_ref, bc_ref, bn_ref, gate_ref, o_ref, *, s):
    qbd = _block_diag_queries(q_ref[...])
    p_len = kc_ref.shape[-1]
    kt = kc_ref[0].reshape(HEAD_PAIR_W, p_len).astype(BF16)
    vt = vc_ref[0].reshape(HEAD_PAIR_W, p_len).astype(BF16)
    zrows = jnp.zeros((LANES - s, HEAD_PAIR_W), BF16)
    kn = jnp.concatenate([kn_ref[...].astype(BF16), zrows], axis=0)
    vn = jnp.concatenate([vn_ref[...].astype(BF16), zrows], axis=0)
    bc = bc_ref[0, 0]
    bn = bn_ref[0, 0]

    def rows_bias(b, width):
        return jnp.concatenate([jnp.broadcast_to(b[0:1], (s, width)), jnp.broadcast_to(b[1:2], (s, width))], axis=0)

    sc = _dot(qbd, kt) + rows_bias(bc, p_len)
    sn = _nt_dot(qbd, kn) + rows_bias(bn, LANES)
    qi = lax.broadcasted_iota(jnp.int32, (2 * s, LANES), 0)
    ki = lax.broadcasted_iota(jnp.int32, (2 * s, LANES), 1)
    qi = jnp.where(qi >= s, qi - s, qi)
    sn = jnp.where(ki <= qi, sn, NEG_INF)
    m = jnp.maximum(jnp.max(sc, axis=1, keepdims=True), jnp.max(sn, axis=1, keepdims=True))
    pc = jnp.exp2(sc - m)
    pn = jnp.exp2(sn - m)
    l = jnp.sum(pc, axis=1, keepdims=True) + jnp.sum(pn, axis=1, keepdims=True)
    o = (_nt_dot(pc.astype(BF16), vt) + _dot(pn.astype(BF16), vn)) / l
    lane = lax.broadcasted_iota(jnp.int32, (s, HEAD_PAIR_W), 1)
    out = jnp.where(lane < FOX_HD, o[:s], o[s:])
    o_ref[...] = (out * gate_ref[...].astype(F32)).astype(BF16)


def _fox_sample_attention(q16, cache_kt, cache_vt, fk32, fv32, bias_c, bias_n, gate, *, s):
    b, _, _, p_len = cache_kt.shape
    pairs = FOX_HEADS // 2
    blk = lambda: pl.BlockSpec((s, HEAD_PAIR_W), lambda bi, p: (bi, p))
    cache = lambda: pl.BlockSpec((1, 2, FOX_HD, p_len), lambda bi, p: (bi, p, 0, 0))
    return pl.pallas_call(
        functools.partial(_fox_sample_body, s=s),
        grid=(b, pairs),
        in_specs=[blk(), cache(), cache(), blk(), blk(),
                  pl.BlockSpec((1, 1, 2, p_len), lambda bi, p: (bi, p, 0, 0)),
                  pl.BlockSpec((1, 1, 2, LANES), lambda bi, p: (bi, p, 0, 0)),
                  blk()],
        out_specs=blk(),
        out_shape=jax.ShapeDtypeStruct((b * s, FOX_W), BF16),
        compiler_params=_cparams(2), name="fox_sample_attention",
    )(q16, cache_kt, cache_vt, fk32, fv32, bias_c, bias_n, gate)


def _diff_sample_body(q_ref, kc_ref, vc_ref, kn_ref, vn_ref, bc_ref, bn_ref, gate_ref, sg_ref, lq1_ref,
                      lk1_ref, lq2_ref, lk2_ref, o_ref, m_ref, l_ref, acc_ref, *, s, pc, lam_init):
    c = pl.program_id(1)
    last = pl.num_programs(1) - 1
    w = 2 * DIFF_HD

    @pl.when(c == 0)
    def _():
        _init_state(m_ref, l_ref, acc_ref)

    qbd = [_block_diag_queries(q_ref[:, h * w:(h + 1) * w]) for h in range(DIFF_HEADS)]

    def update(h, sc, v16):
        m = m_ref[h]
        mn = jnp.maximum(m, jnp.max(sc, axis=1, keepdims=True))
        alpha = jnp.exp2(m - mn)
        p = jnp.exp2(sc - mn)
        l_ref[h] = alpha * l_ref[h] + jnp.sum(p, axis=1, keepdims=True)
        acc_ref[h] = alpha * acc_ref[h] + _dot(p.astype(BF16), v16)
        m_ref[h] = mn

    def chunk(with_bias):
        scores = [_nt_dot(qbd[h], kc_ref[0, pl.ds(h, pc, stride=DIFF_HEADS), :].astype(BF16))
                  for h in range(DIFF_HEADS)]
        for h, sc in enumerate(scores):
            if with_bias:
                sc = sc + jnp.concatenate([bc_ref[h], bc_ref[h]], axis=0)
            update(h, sc, vc_ref[0, pl.ds(h, pc, stride=DIFF_HEADS), :].astype(BF16))

    @pl.when(c != last)
    def _():
        chunk(False)

    @pl.when(c == last)
    def _():
        chunk(True)
        lam = _lambda_value(lq1_ref[...], lk1_ref[...], lq2_ref[...], lk2_ref[...], lam_init)
        zrows = jnp.zeros((LANES - s, w), BF16)
        for h in range(DIFF_HEADS):
            cols = slice(h * w, (h + 1) * w)
            kn = jnp.concatenate([kn_ref[:, cols].astype(BF16), zrows], axis=0)
            vn = jnp.concatenate([vn_ref[:, cols].astype(BF16), zrows], axis=0)
            sn = _nt_dot(qbd[h], kn) + jnp.concatenate([bn_ref[h], bn_ref[h]], axis=0)
            update(h, sn, vn)
            o = acc_ref[h] / l_ref[h]
            o = o[:s] - lam * o[s:]
            o_ref[:, cols] = _subln_gate(o, sg_ref[...], gate_ref[:, cols], lam_init)


def _diff_sample_attention(q16, cache_k, cache_v, dk32, dv32, bias_c, bias_n, gate, subln_g, lams, *, s, pc,
                           lam_init):
    b, rows, w = cache_k.shape
    p_len = rows // DIFF_HEADS
    assert p_len % pc == 0 and bias_c.shape == (DIFF_HEADS, s, pc)
    cache = lambda: pl.BlockSpec((1, DIFF_HEADS * pc, w), lambda bi, c: (bi, c, 0))
    new = lambda: pl.BlockSpec((s, DIFF_W), lambda bi, c: (bi, 0))
    half = lambda: pl.BlockSpec((s, DIFF_W), lambda bi, c: (bi, 1))
    const = lambda shape: pl.BlockSpec(shape, lambda bi, c: (0,) * len(shape))
    return pl.pallas_call(
        functools.partial(_diff_sample_body, s=s, pc=pc, lam_init=lam_init),
        grid=(b, p_len // pc),
        in_specs=[half(), cache(), cache(), new(), new(), const(bias_c.shape), const(bias_n.shape), half(),
                  const((1, w)), const((1, DIFF_HD)), const((1, DIFF_HD)), const((1, DIFF_HD)), const((1, DIFF_HD))],
        out_specs=new(),
        out_shape=jax.ShapeDtypeStruct((b * s, DIFF_W), BF16),
        scratch_shapes=[pltpu.VMEM((DIFF_HEADS, 2 * s, 1), F32), pltpu.VMEM((DIFF_HEADS, 2 * s, 1), F32),
                        pltpu.VMEM((DIFF_HEADS, 2 * s, w), F32)],
        compiler_params=_cparams(2), name="diff_sample_attention",
    )(q16, cache_k, cache_v, dk32, dv32, bias_c, bias_n, gate, subln_g, *lams)


def _outproj_body(mf_ref, md_ref, w_ref, x_ref, g_ref, y_ref):
    out = _dot(mf_ref[...], w_ref[:FOX_W, :]) + _dot(md_ref[...], w_ref[FOX_W:, :])
    ms = jnp.mean(out * out, axis=-1, keepdims=True)
    y_ref[...] = x_ref[...] + (out * lax.rsqrt(ms + EPS)) * g_ref[...]


def _out_projection(mf, md, w_out, x2d, g_post, *, tm):
    n, d = x2d.shape
    tm = min(tm, n)
    assert n % tm == 0
    rows = lambda w: pl.BlockSpec((tm, w), lambda i: (i, 0))
    return pl.pallas_call(
        _outproj_body, grid=(n // tm,),
        in_specs=[rows(FOX_W), rows(DIFF_W), pl.BlockSpec(w_out.shape, lambda i: (0, 0)), rows(d),
                  pl.BlockSpec((1, d), lambda i: (0, 0))],
        out_specs=rows(d), out_shape=jax.ShapeDtypeStruct((n, d), F32),
        compiler_params=_cparams(1), name="out_projection",
    )(mf, md, w_out.astype(BF16), x2d, g_post.reshape(1, d))


def _t5_bucket(rel):
    half = NUM_BUCKETS // 2
    max_exact = half // 2
    ret = jnp.where(rel > 0, half, 0)
    n = jnp.abs(rel)
    nf = jnp.maximum(n, 1).astype(jnp.float32)
    large = max_exact + (jnp.log(nf / max_exact) / math.log(MAX_DISTANCE / max_exact)
                         * (half - max_exact)).astype(jnp.int32)
    large = jnp.minimum(large, half - 1)
    return ret + jnp.where(n < max_exact, n, large)


def _rel_bias_of(rel_bias, rel):
    return rel_bias[_t5_bucket(rel)].astype(F32).T


def _toeplitz(g, rows, cols):
    h = g.shape[0]
    period = rows + cols
    v = jnp.concatenate([g[:, rows - 1:], jnp.zeros((h, 1), g.dtype), g[:, :rows - 1]], axis=1)
    flat = jnp.tile(v, (1, rows))[:, :rows * (period - 1)]
    return flat.reshape(h, rows, period - 1)[:, :, :cols]


def _chunk_mask(bias, row_pos, col_pos, rows_are_keys):
    r = (row_pos // CHUNK)[:, None]
    c = (col_pos // CHUNK)[None, :]
    mask = (r <= c) if rows_are_keys else (c <= r)
    return jnp.where(mask[None], bias * LOG2E, NEG_INF)


def _prompt_layer(x, g_pre, w_in, b_f, lams, subln_g, w_out, g_post, rel_bias, layer_idx, *, t=256, tm=512,
                  fox_pairs=4, diff_heads=4):
    b, s, d = x.shape
    assert b == 1 and s % t == 0 and s // LANES >= 1
    x2d = x.reshape(s, d)
    (fk, fv, dk, dv, logf, gate, qt, k16, dk16, vt) = _in_projection(
        x2d, g_pre, w_in, b_f, transposed=True, tm=tm, tk=t)

    rows = s // LANES
    _, hi, mid, lo = _prefix_sums(logf.T.reshape(FOX_HEADS * rows, LANES), seg=LANES, group=rows,
                                  rows_per_step=FOX_HEADS * rows, scale=LOG2E)
    pieces = jnp.stack([hi, mid, lo], axis=0).reshape(PIECES, FOX_HEADS, s)
    kext = jnp.concatenate([
        (-pieces).transpose(2, 1, 0).reshape(s, ONES_AT), jnp.ones((s, PIECES), BF16),
        jnp.zeros((s, LANES - ONES_AT - PIECES), BF16)], axis=1)
    sel = (jnp.arange(ONES_AT)[None, :] // PIECES == jnp.arange(FOX_HEADS)[:, None]).astype(BF16)
    qext = jnp.concatenate([
        jnp.broadcast_to(sel[:, :, None], (FOX_HEADS, ONES_AT, s)), pieces.transpose(1, 0, 2),
        jnp.zeros((FOX_HEADS, QEXT_ROWS - ONES_AT - PIECES, s), BF16)], axis=1)

    mf = _fox_prompt_attention(qt, qext, k16, kext, vt, gate, t=t, pairs=fox_pairs)

    lam_init = 0.8 - 0.6 * math.exp(-0.3 * layer_idx)
    assert t >= MAX_DISTANCE
    pos = jnp.arange(t, dtype=jnp.int32)
    off = jnp.arange(-(t - 1), t, dtype=jnp.int32)
    far = _rel_bias_of(rel_bias, jnp.full((1,), -2 * t, jnp.int32))
    bias_diag = _chunk_mask(_toeplitz(_rel_bias_of(rel_bias, -off) - far, t, t), pos, pos, True)
    bias_sub = _chunk_mask(_toeplitz(_rel_bias_of(rel_bias, -off - t) - far, t, t), pos, pos + t, True)
    md = _diff_prompt_attention(qt, dk16, vt, bias_diag, bias_sub, gate, subln_g.reshape(1, -1), lams,
                                t=t, heads=diff_heads, lam_init=lam_init)

    y = _out_projection(mf, md, w_out, x2d, g_post, tm=tm)
    return y.reshape(b, s, d), (fk, fv, logf, dk, dv)


def _sample_layer(x, past, g_pre, w_in, b_f, lams, subln_g, w_out, g_post, rel_bias, layer_idx, *, tm=512,
                  diff_chunk=1024):
    b, s, d = x.shape
    pk, pv, plogf, pdk, pdv = past
    p_len = pk.shape[1]
    assert s <= LANES and LANES % s == 0 and p_len % LANES == 0
    x2d = x.reshape(b * s, d)
    fk, fv, dk, dv, logf, gate, q16 = _in_projection(x2d, g_pre, w_in, b_f, transposed=False, tm=tm, tk=tm)

    rows = p_len // LANES
    cum_c, _, _, _ = _prefix_sums(plogf.transpose(0, 2, 1).reshape(b * FOX_HEADS * rows, LANES), seg=LANES,
                                  group=rows, rows_per_step=32 * rows, scale=1.0)
    cum_c = cum_c.reshape(b, FOX_HEADS, p_len)
    bias_c = ((cum_c[:, :, -1:] - cum_c) * LOG2E).reshape(b, FOX_HEADS // 2, 2, p_len)
    cum_n, _, _, _ = _prefix_sums(
        logf.reshape(b, s, FOX_HEADS).transpose(0, 2, 1).reshape(b * FOX_HEADS * s // LANES, LANES),
        seg=s, group=1, rows_per_step=1024, scale=1.0)
    bias_n = jnp.concatenate([-cum_n.reshape(b, FOX_HEADS, s) * LOG2E,
                              jnp.full((b, FOX_HEADS, LANES - s), NEG_INF, F32)], axis=-1)
    bias_n = bias_n.reshape(b, FOX_HEADS // 2, 2, LANES)
    mf = _fox_sample_attention(q16, pk.transpose(0, 2, 3, 1), pv.transpose(0, 2, 3, 1), fk, fv, bias_c, bias_n,
                               gate, s=s)

    lam_init = 0.8 - 0.6 * math.exp(-0.3 * layer_idx)
    pc = min(diff_chunk, p_len)
    assert pc >= MAX_DISTANCE + s
    q_pos = p_len + jnp.arange(s, dtype=jnp.int32)
    far = _rel_bias_of(rel_bias, jnp.full((1,), -2 * p_len, jnp.int32))
    off = jnp.arange(-(s - 1), pc, dtype=jnp.int32)
    bd_c = _chunk_mask(_toeplitz(_rel_bias_of(rel_bias, off - pc) - far, s, pc), q_pos,
                       p_len - pc + jnp.arange(pc, dtype=jnp.int32), False)
    off = jnp.arange(-(s - 1), s, dtype=jnp.int32)
    bd_n = _chunk_mask(_toeplitz(_rel_bias_of(rel_bias, off) - far, s, s), q_pos, q_pos, False)
    bd_n = jnp.concatenate([bd_n, jnp.full((DIFF_HEADS, s, LANES - s), NEG_INF, F32)], axis=-1)
    md = _diff_sample_attention(q16, pdk.reshape(b, p_len * DIFF_HEADS, 2 * DIFF_HD),
                                pdv.reshape(b, p_len * DIFF_HEADS, 2 * DIFF_HD), dk, dv, bd_c, bd_n, gate,
                                subln_g.reshape(1, -1), lams, s=s, pc=pc, lam_init=lam_init)

    y = _out_projection(mf, md, w_out, x2d, g_post, tm=tm)
    return y.reshape(b, s, d), (fk, fv, logf, dk, dv)


def kernel(x_prompt, x_sample, cache_fox_k, cache_fox_v, cache_fox_logf, cache_diff_k, cache_diff_v,
           norm_pre_g, w_in, forget_bias, lambda_q1, lambda_k1, lambda_q2, lambda_k2, subln_g, w_out,
           norm_post_g, rel_bias):
    depth = w_in.shape[0]
    y_p, y_s = x_prompt, x_sample
    rows_p, rows_s = [], []
    for l in range(depth):
        lams = tuple(a[l].reshape(1, -1) for a in (lambda_q1, lambda_k1, lambda_q2, lambda_k2))
        params = (norm_pre_g[l], w_in[l], forget_bias[l], lams, subln_g[l], w_out[l], norm_post_g[l], rel_bias, l)
        y_p, new_p = _prompt_layer(y_p, *params)
        past = (cache_fox_k[l], cache_fox_v[l], cache_fox_logf[l], cache_diff_k[l], cache_diff_v[l])
        y_s, new_s = _sample_layer(y_s, past, *params)
        rows_p.append(new_p)
        rows_s.append(new_s)

    bp, sp = x_prompt.shape[:2]
    bs, ss = x_sample.shape[:2]

    def stack(rows, idx, shape):
        return jnp.stack([r[idx].reshape(shape) for r in rows], axis=0)

    outs = [y_p, y_s]
    for rows, (b, s) in ((rows_p, (bp, sp)), (rows_s, (bs, ss))):
        outs += [stack(rows, 0, (b, s, FOX_HEADS, FOX_HD)), stack(rows, 1, (b, s, FOX_HEADS, FOX_HD)),
                 stack(rows, 2, (b, s, FOX_HEADS)), stack(rows, 3, (b, s, DIFF_HEADS, 2 * DIFF_HD)),
                 stack(rows, 4, (b, s, DIFF_HEADS, 2 * DIFF_HD))]
    return tuple(outs)
```

```python
import functools
import math

import jax
import jax.numpy as jnp
import numpy as np
from jax import lax
from jax.experimental import pallas as pl
from jax.experimental.pallas import tpu as pltpu

F32 = jnp.float32
BF16 = jnp.bfloat16

FOX_HEADS = 8
FOX_HD = 64
DIFF_HEADS = 4
DIFF_HD = 64
FOX_W = FOX_HEADS * FOX_HD
DIFF_W = DIFF_HEADS * 2 * DIFF_HD
CHUNK = 64
NUM_BUCKETS = 32
MAX_DISTANCE = 128
EPS = 1e-6
NEG_INF = -1e30
LOG2E = 1.4426950408889634

LANES = 128
HEAD_PAIR_W = 2 * FOX_HD
VMEM_LIMIT_BYTES = 56 * 1024 * 1024

PIECES = 3
ONES_AT = PIECES * FOX_HEADS
QEXT_ROWS = 32


def _cparams(n_axes):
    return pltpu.CompilerParams(
        dimension_semantics=("arbitrary",) * n_axes,
        vmem_limit_bytes=VMEM_LIMIT_BYTES,
    )


def _split3(a):
    hi = a.astype(BF16)
    r1 = a - hi.astype(F32)
    mid = r1.astype(BF16)
    lo = (r1 - mid.astype(F32)).astype(BF16)
    return hi, mid, lo


def _nt_dot(a, b):
    return lax.dot_general(a, b, (((1,), (1,)), ((), ())), preferred_element_type=F32)


def _dot(a, b):
    return jnp.dot(a, b, preferred_element_type=F32)


def _inproj_body(x_ref, g_ref, wkv_ref, wg_ref, wff_ref, bf_ref, wq_ref, wvt_ref,
                 fk_ref, fv_ref, dk_ref, dv_ref, logf_ref, gate_ref, q_ref, *rest,
                 transposed, tk, qscale):
    x = x_ref[...]
    ms = jnp.mean(x * x, axis=-1, keepdims=True)
    h = (x * lax.rsqrt(ms + EPS)) * g_ref[...]
    hb = h.astype(BF16)

    fk = _dot(hb, wkv_ref[:, 0 * FOX_W:1 * FOX_W])
    fk_ref[...] = fk
    fv_ref[...] = _dot(hb, wkv_ref[:, 1 * FOX_W:2 * FOX_W])
    dk = _dot(hb, wkv_ref[:, 2 * FOX_W:3 * FOX_W])
    dk_ref[...] = dk
    dv_ref[...] = _dot(hb, wkv_ref[:, 3 * FOX_W:4 * FOX_W])

    ff = _dot(hb, wff_ref[...])
    logf = jax.nn.log_sigmoid(ff + bf_ref[...])
    logf_ref[...] = logf[:, :FOX_HEADS]

    gates = _dot(hb, wg_ref[...])
    gate_ref[...] = (gates * jax.nn.sigmoid(gates)).astype(BF16)

    if transposed:
        k16_ref, dk16_ref, vt_ref = rest
        k16_ref[...] = fk.astype(BF16)
        dk16_ref[...] = dk.astype(BF16)
        q_ref[...] = (_nt_dot(wq_ref[...], hb) * qscale).astype(BF16)
        vt = _nt_dot(wvt_ref[...], hb).astype(BF16)
        for c in range(vt.shape[1] // tk):
            vt_ref[c] = vt[:, c * tk:(c + 1) * tk]
    else:
        q_ref[...] = (_dot(hb, wq_ref[...]) * qscale).astype(BF16)


def _in_projection(x2d, g_pre, w_in, b_f, *, transposed, tm, tk):
    n, d = x2d.shape
    tm = min(tm, n)
    assert n % tm == 0 and (not transposed or tm % tk == 0)
    qscale = FOX_HD ** -0.5 * LOG2E
    o = np.cumsum([0, FOX_W, FOX_W, FOX_W, FOX_HEADS, FOX_W, DIFF_W, DIFF_W, DIFF_W, DIFF_W])
    w_fq, w_fk, w_fv, w_ff, w_fg, w_dq, w_dk, w_dv, w_dg = [w_in[:, o[i]:o[i + 1]] for i in range(9)]
    wkv = jnp.concatenate([w_fk, w_fv, w_dk, w_dv], axis=1).astype(BF16)
    wg = jnp.concatenate([w_fg, w_dg], axis=1).astype(BF16)
    wff = jnp.pad(w_ff, ((0, 0), (0, LANES - FOX_HEADS))).astype(BF16)
    bfp = jnp.pad(b_f.reshape(1, FOX_HEADS), ((0, 0), (0, LANES - FOX_HEADS)))
    wq = jnp.concatenate([w_fq, w_dq], axis=1)
    wv = jnp.concatenate([w_fv, w_dv], axis=1)
    if transposed:
        wq = wq.T.astype(BF16)
        wvt = wv.T.astype(BF16)
    else:
        wq = wq.astype(BF16)
        wvt = jnp.zeros((8, LANES), BF16)

    const = lambda shape: pl.BlockSpec(shape, lambda i: (0,) * len(shape))
    rows = lambda w: pl.BlockSpec((tm, w), lambda i: (i, 0))
    in_specs = [rows(d), const((1, d)), const(wkv.shape), const(wg.shape), const(wff.shape),
                const(bfp.shape), const(wq.shape), const(wvt.shape)]
    out_shape = [jax.ShapeDtypeStruct((n, FOX_W), F32)] * 2 + [jax.ShapeDtypeStruct((n, DIFF_W), F32)] * 2
    out_shape += [jax.ShapeDtypeStruct((n, FOX_HEADS), F32), jax.ShapeDtypeStruct((n, FOX_W + DIFF_W), BF16)]
    out_specs = [rows(FOX_W), rows(FOX_W), rows(DIFF_W), rows(DIFF_W), rows(FOX_HEADS), rows(FOX_W + DIFF_W)]
    if transposed:
        out_shape += [jax.ShapeDtypeStruct((FOX_W + DIFF_W, n), BF16),
                      jax.ShapeDtypeStruct((n, FOX_W), BF16), jax.ShapeDtypeStruct((n, DIFF_W), BF16),
                      jax.ShapeDtypeStruct((n // tk, FOX_W + DIFF_W, tk), BF16)]
        out_specs += [pl.BlockSpec((FOX_W + DIFF_W, tm), lambda i: (0, i)), rows(FOX_W), rows(DIFF_W),
                      pl.BlockSpec((tm // tk, FOX_W + DIFF_W, tk), lambda i: (i, 0, 0))]
    else:
        out_shape += [jax.ShapeDtypeStruct((n, FOX_W + DIFF_W), BF16)]
        out_specs += [rows(FOX_W + DIFF_W)]
    return pl.pallas_call(
        functools.partial(_inproj_body, transposed=transposed, tk=tk, qscale=qscale),
        grid=(n // tm,), in_specs=in_specs, out_specs=out_specs, out_shape=out_shape,
        compiler_params=_cparams(1), name="in_projection_t" if transposed else "in_projection_r",
    )(x2d, g_pre.reshape(1, d), wkv, wg, wff, bfp, wq, wvt)


def _cumsum_body(x_ref, cum_ref, hi_ref, mid_ref, lo_ref, *, seg, group, scale):
    x = x_ref[...]
    r = x.shape[0]
    a = lax.broadcasted_iota(jnp.int32, (LANES, LANES), 0)
    b = lax.broadcasted_iota(jnp.int32, (LANES, LANES), 1)
    sh = int(math.log2(seg))
    upper = ((a <= b) & ((a >> sh) == (b >> sh))).astype(BF16)
    pieces = _split3(x)
    w = sum(_dot(p, upper) for p in pieces)
    if group > 1:
        ones = jnp.ones((LANES, LANES), BF16)
        tot = sum(_dot(p, ones) for p in pieces)
        ri = lax.broadcasted_iota(jnp.int32, (r, r), 0)
        ci = lax.broadcasted_iota(jnp.int32, (r, r), 1)
        gsh = int(math.log2(group))
        lower = ((ci < ri) & ((ri >> gsh) == (ci >> gsh))).astype(BF16)
        w = w + sum(_dot(lower, p) for p in _split3(tot))
    cum_ref[...] = w
    hi, mid, lo = _split3(w * scale)
    hi_ref[...] = hi
    mid_ref[...] = mid
    lo_ref[...] = lo


def _prefix_sums(x2d, *, seg, group, rows_per_step, scale):
    r = x2d.shape[0]
    rb = min(rows_per_step, r)
    assert r % rb == 0 and rb % group == 0 and (seg == LANES or group == 1)
    assert seg & (seg - 1) == 0 and group & (group - 1) == 0
    spec = pl.BlockSpec((rb, LANES), lambda i: (i, 0))
    return pl.pallas_call(
        functools.partial(_cumsum_body, seg=seg, group=group, scale=scale),
        grid=(r // rb,), in_specs=[spec], out_specs=[spec] * 4,
        out_shape=[jax.ShapeDtypeStruct((r, LANES), F32)] + [jax.ShapeDtypeStruct((r, LANES), BF16)] * 3,
        compiler_params=_cparams(1), name="prefix_sums",
    )(x2d)


SUM_ROWS = 16
SCORES_AHEAD = 2
FAR_UNROLL = 4


def _with_sum_rows(vt):
    row = lax.broadcasted_iota(jnp.int32, (SUM_ROWS, vt.shape[1]), 0)
    return jnp.concatenate([vt, jnp.where(row == 0, 1.0, 0.0).astype(BF16)], axis=0)


def _store_scores(buf, h, s):
    buf[0][h] = s
    buf[1][h] = jnp.max(s, axis=0, keepdims=True)


def _online_update_t(s, tile_max, h, m_ref, acc_ref, vt_aug):
    m = m_ref[h]
    mn = jnp.maximum(m, tile_max)
    alpha = jnp.exp2(m - mn)
    p = jnp.exp2(s - mn)
    acc_ref[h] = alpha * acc_ref[h] + _dot(vt_aug, p.astype(BF16))
    m_ref[h] = mn


def _run_far_tiles(n, half_step, buf_a, buf_b):
    def unrolled(base, count):
        bufs = (buf_a, buf_b)
        for u in range(count):
            half_step(base + u, bufs[u % 2], bufs[(u + 1) % 2])

    def body(jj, c):
        unrolled(FAR_UNROLL * jj, FAR_UNROLL)
        return c

    lax.fori_loop(0, n // FAR_UNROLL, body, 0)
    for count in range(FAR_UNROLL - 2, 0, -2):
        @pl.when(n % FAR_UNROLL // 2 * 2 == count)
        def _():
            unrolled(n // FAR_UNROLL * FAR_UNROLL, count)


def _init_state(m_ref, l_ref, acc_ref):
    m_ref[...] = jnp.full(m_ref.shape, NEG_INF, F32)
    if l_ref is not None:
        l_ref[...] = jnp.zeros(l_ref.shape, F32)
    acc_ref[...] = jnp.zeros(acc_ref.shape, F32)


def _resident(shape, index_map):
    return pl.BlockSpec(shape, index_map, pipeline_mode=pl.Buffered(1))


def _fox_prompt_body(q_ref, qx_ref, k_ref, kx_ref, vt_ref, gate_ref, o_ref, m_ref, acc_ref, sa_ref, sb_ref,
                     ma_ref, mb_ref, *, t, pairs):
    i = pl.program_id(1)
    heads = 2 * pairs
    buf_a, buf_b = (sa_ref, ma_ref), (sb_ref, mb_ref)
    _init_state(m_ref, None, acc_ref)
    z64 = jnp.zeros((FOX_HD, t), BF16)
    zpad = jnp.zeros((LANES - QEXT_ROWS, t), BF16)
    qa = []
    for p in range(pairs):
        q = q_ref[p * HEAD_PAIR_W:(p + 1) * HEAD_PAIR_W, :]
        qa.append(jnp.concatenate([q[:FOX_HD], z64, qx_ref[2 * p], zpad], axis=0))
        qa.append(jnp.concatenate([z64, q[FOX_HD:], qx_ref[2 * p + 1], zpad], axis=0))

    def keys(j, p):
        off = pl.multiple_of(j * t, t)
        return jnp.concatenate([k_ref[pl.ds(off, t), p * HEAD_PAIR_W:(p + 1) * HEAD_PAIR_W],
                                kx_ref[pl.ds(off, t), :]], axis=1)

    def consume(j, cur, h, diagonal):
        s = cur[0][h]
        if diagonal:
            kk = lax.broadcasted_iota(jnp.int32, (t, t), 0)
            qq = lax.broadcasted_iota(jnp.int32, (t, t), 1)
            s = jnp.where(kk <= qq, s, NEG_INF)
            tile_max = jnp.max(s, axis=0, keepdims=True)
        else:
            tile_max = cur[1][h]
        _online_update_t(s, tile_max, h, m_ref, acc_ref,
                         _with_sum_rows(vt_ref[j, h * FOX_HD:(h + 1) * FOX_HD, :]))

    def half_step(j, cur, nxt):
        ka = {}

        def prefetch(h):
            if h // 2 not in ka:
                ka[h // 2] = keys(j + 1, h // 2)
            _store_scores(nxt, h, _dot(ka[h // 2], qa[h]))

        for h in range(min(SCORES_AHEAD, heads)):
            prefetch(h)
        for h in range(heads):
            consume(j, cur, h, False)
            if h + SCORES_AHEAD < heads:
                prefetch(h + SCORES_AHEAD)

    def last_step(j, cur):
        for h in range(heads):
            consume(j, cur, h, True)

    for p in range(pairs):
        ka = keys(0, p)
        _store_scores(buf_a, 2 * p, _dot(ka, qa[2 * p]))
        _store_scores(buf_a, 2 * p + 1, _dot(ka, qa[2 * p + 1]))

    _run_far_tiles(i, half_step, buf_a, buf_b)

    @pl.when(i % 2 == 0)
    def _():
        last_step(i, buf_a)

    @pl.when(i % 2 == 1)
    def _():
        half_step(i - 1, buf_a, buf_b)
        last_step(i, buf_b)

    for p in range(pairs):
        a0, a1 = acc_ref[2 * p], acc_ref[2 * p + 1]
        ot = jnp.concatenate([a0[:FOX_HD] / a0[FOX_HD:FOX_HD + 1], a1[:FOX_HD] / a1[FOX_HD:FOX_HD + 1]], axis=0)
        cols = slice(p * HEAD_PAIR_W, (p + 1) * HEAD_PAIR_W)
        o_ref[:, cols] = (ot.T * gate_ref[:, cols].astype(F32)).astype(BF16)


def _fox_prompt_attention(qt, qext, k16, kext, vt, gate, *, t, pairs):
    n = k16.shape[0]
    nq = n // t
    w = pairs * HEAD_PAIR_W
    heads = 2 * pairs
    assert (FOX_HEADS // 2) % pairs == 0
    return pl.pallas_call(
        functools.partial(_fox_prompt_body, t=t, pairs=pairs),
        grid=(FOX_HEADS // heads, nq),
        in_specs=[pl.BlockSpec((w, t), lambda g, i: (g, i)),
                  pl.BlockSpec((heads, QEXT_ROWS, t), lambda g, i: (g, 0, i)),
                  _resident((n, w), lambda g, i: (0, g)),
                  _resident((n, LANES), lambda g, i: (0, 0)),
                  _resident((nq, w, t), lambda g, i: (0, g, 0)),
                  pl.BlockSpec((t, w), lambda g, i: (i, g))],
        out_specs=pl.BlockSpec((t, w), lambda g, i: (i, g)),
        out_shape=jax.ShapeDtypeStruct((n, FOX_W), BF16),
        scratch_shapes=[pltpu.VMEM((heads, 1, t), F32), pltpu.VMEM((heads, FOX_HD + SUM_ROWS, t), F32),
                        pltpu.VMEM((heads, t, t), F32), pltpu.VMEM((heads, t, t), F32),
                        pltpu.VMEM((heads, 1, t), F32), pltpu.VMEM((heads, 1, t), F32)],
        compiler_params=_cparams(2), name="fox_prompt_attention",
    )(qt, qext, k16, kext, vt, gate)


def _lambda_value(lq1, lk1, lq2, lk2, lam_init):
    a = jnp.sum(lq1 * lk1, axis=-1, keepdims=True)
    b = jnp.sum(lq2 * lk2, axis=-1, keepdims=True)
    return jnp.exp(a) - jnp.exp(b) + lam_init


def _subln_gate(y, g, gate, lam_init):
    ms = jnp.mean(y * y, axis=-1, keepdims=True)
    yn = (y * lax.rsqrt(ms + EPS)) * g
    return ((yn * (1.0 - lam_init)) * gate.astype(F32)).astype(BF16)


def _diff_prompt_body(q_ref, k_ref, vt_ref, bd_ref, bs_ref, gate_ref, sg_ref, lq1_ref, lk1_ref, lq2_ref,
                      lk2_ref, o_ref, m_ref, acc_ref, sa_ref, sb_ref, ma_ref, mb_ref, *, t, heads, lam_init):
    i = pl.program_id(1)
    w = 2 * DIFF_HD
    buf_a, buf_b = (sa_ref, ma_ref), (sb_ref, mb_ref)
    _init_state(m_ref, None, acc_ref)
    z64 = jnp.zeros((DIFF_HD, t), BF16)
    qa = []
    for h in range(heads):
        q = q_ref[h * w:(h + 1) * w, :]
        qa.append(jnp.concatenate([q[:DIFF_HD], z64], axis=0))
        qa.append(jnp.concatenate([z64, q[DIFF_HD:]], axis=0))

    def keys(j, h):
        return k_ref[pl.ds(pl.multiple_of(j * t, t), t), h * w:(h + 1) * w]

    def consume(j, cur, c, bias_ref):
        h = c // 2
        s = cur[0][c]
        if bias_ref is not None:
            s = s + bias_ref[h]
            tile_max = jnp.max(s, axis=0, keepdims=True)
        else:
            tile_max = cur[1][c]
        _online_update_t(s, tile_max, c, m_ref, acc_ref, _with_sum_rows(vt_ref[j, h * w:(h + 1) * w, :]))

    def half_step(j, cur, nxt, bias_ref=None):
        kk = {}

        def prefetch(c):
            if c // 2 not in kk:
                kk[c // 2] = keys(j + 1, c // 2)
            _store_scores(nxt, c, _dot(kk[c // 2], qa[c]))

        for c in range(min(SCORES_AHEAD, 2 * heads)):
            prefetch(c)
        for c in range(2 * heads):
            consume(j, cur, c, bias_ref)
            if c + SCORES_AHEAD < 2 * heads:
                prefetch(c + SCORES_AHEAD)

    def last_step(j, cur):
        for c in range(2 * heads):
            consume(j, cur, c, bd_ref)

    for h in range(heads):
        kk = keys(0, h)
        _store_scores(buf_a, 2 * h, _dot(kk, qa[2 * h]))
        _store_scores(buf_a, 2 * h + 1, _dot(kk, qa[2 * h + 1]))

    _run_far_tiles(jnp.maximum(i - 1, 0), half_step, buf_a, buf_b)

    @pl.when(i == 0)
    def _():
        last_step(i, buf_a)

    @pl.when(i % 2 == 1)
    def _():
        half_step(i - 1, buf_a, buf_b, bs_ref)
        last_step(i, buf_b)

    @pl.when((i >= 2) & (i % 2 == 0))
    def _():
        half_step(i - 2, buf_a, buf_b)
        half_step(i - 1, buf_b, buf_a, bs_ref)
        last_step(i, buf_a)

    lam = _lambda_value(lq1_ref[...], lk1_ref[...], lq2_ref[...], lk2_ref[...], lam_init)
    for h in range(heads):
        a0, a1 = acc_ref[2 * h], acc_ref[2 * h + 1]
        ot = a0[:w] / a0[w:w + 1] - lam * (a1[:w] / a1[w:w + 1])
        cols = slice(h * w, (h + 1) * w)
        o_ref[:, cols] = _subln_gate(ot.T, sg_ref[...], gate_ref[:, cols], lam_init)


def _diff_prompt_attention(qt, dk16, vt, bias_diag, bias_sub, gate, subln_g, lams, *, t, heads, lam_init):
    n = dk16.shape[0]
    nq = n // t
    w = heads * 2 * DIFF_HD
    assert DIFF_HEADS % heads == 0 and FOX_W % w == 0
    fox_blocks = FOX_W // w
    vec = lambda width: pl.BlockSpec((1, width), lambda g, i: (0, 0))
    return pl.pallas_call(
        functools.partial(_diff_prompt_body, t=t, heads=heads, lam_init=lam_init),
        grid=(DIFF_HEADS // heads, nq),
        in_specs=[pl.BlockSpec((w, t), lambda g, i: (fox_blocks + g, i)),
                  _resident((n, w), lambda g, i: (0, g)),
                  _resident((nq, w, t), lambda g, i: (0, fox_blocks + g, 0)),
                  pl.BlockSpec((heads, t, t), lambda g, i: (g, 0, 0)),
                  pl.BlockSpec((heads, t, t), lambda g, i: (g, 0, 0)),
                  pl.BlockSpec((t, w), lambda g, i: (i, fox_blocks + g)),
                  vec(2 * DIFF_HD), vec(DIFF_HD), vec(DIFF_HD), vec(DIFF_HD), vec(DIFF_HD)],
        out_specs=pl.BlockSpec((t, w), lambda g, i: (i, g)),
        out_shape=jax.ShapeDtypeStruct((n, DIFF_W), BF16),
        scratch_shapes=[pltpu.VMEM((2 * heads, 1, t), F32),
                        pltpu.VMEM((2 * heads, 2 * DIFF_HD + SUM_ROWS, t), F32),
                        pltpu.VMEM((2 * heads, t, t), F32), pltpu.VMEM((2 * heads, t, t), F32),
                        pltpu.VMEM((2 * heads, 1, t), F32), pltpu.VMEM((2 * heads, 1, t), F32)],
        compiler_params=_cparams(2), name="diff_prompt_attention",
    )(qt, dk16, vt, bias_diag, bias_sub, gate, subln_g, *lams)


def _block_diag_queries(qb):
    lane = lax.broadcasted_iota(jnp.int32, qb.shape, 1)
    zero = jnp.zeros_like(qb)
    return jnp.concatenate([jnp.where(lane < FOX_HD, qb, zero), jnp.where(lane >= FOX_HD, qb, zero)], axis=0)


def _fox_sample_body(q_ref, kc_ref, vc_ref, kn_ref, vn_ref, bc_ref, bn_ref, gate_ref, o_ref, *, s, pairs):
    p_len = kc_ref.shape[-1]
    zrows = jnp.zeros((LANES - s, HEAD_PAIR_W), BF16)
    qi = lax.broadcasted_iota(jnp.int32, (2 * s, LANES), 0)
    ki = lax.broadcasted_iota(jnp.int32, (2 * s, LANES), 1)
    causal = ki <= jnp.where(qi >= s, qi - s, qi)
    lane = lax.broadcasted_iota(jnp.int32, (s, HEAD_PAIR_W), 1)

    def rows_bias(b, width):
        return jnp.concatenate([jnp.broadcast_to(b[0:1], (s, width)), jnp.broadcast_to(b[1:2], (s, width))], axis=0)

    scores = []
    for p in range(pairs):
        cols = slice(p * HEAD_PAIR_W, (p + 1) * HEAD_PAIR_W)
        qbd = _block_diag_queries(q_ref[:, cols])
        kt = kc_ref[0, 2 * p:2 * p + 2].reshape(HEAD_PAIR_W, p_len).astype(BF16)
        kn = jnp.concatenate([kn_ref[:, cols].astype(BF16), zrows], axis=0)
        sc = _dot(qbd, kt) + rows_bias(bc_ref[0, p], p_len)
        sn = _nt_dot(qbd, kn) + rows_bias(bn_ref[0, p], LANES)
        scores.append((sc, jnp.where(causal, sn, NEG_INF)))
    for p, (sc, sn) in enumerate(scores):
        cols = slice(p * HEAD_PAIR_W, (p + 1) * HEAD_PAIR_W)
        vt = vc_ref[0, 2 * p:2 * p + 2].reshape(HEAD_PAIR_W, p_len).astype(BF16)
        vn = jnp.concatenate([vn_ref[:, cols].astype(BF16), zrows], axis=0)
        m = jnp.maximum(jnp.max(sc, axis=1, keepdims=True), jnp.max(sn, axis=1, keepdims=True))
        pc = jnp.exp2(sc - m)
        pn = jnp.exp2(sn - m)
        l = jnp.sum(pc, axis=1, keepdims=True) + jnp.sum(pn, axis=1, keepdims=True)
        o = (_nt_dot(pc.astype(BF16), vt) + _dot(pn.astype(BF16), vn)) / l
        out = jnp.where(lane < FOX_HD, o[:s], o[s:])
        o_ref[:, cols] = (out * gate_ref[:, cols].astype(F32)).astype(BF16)


def _fox_sample_attention(q16, cache_kt, cache_vt, fk32, fv32, bias_c, bias_n, gate, *, s, pairs):
    b, _, _, p_len = cache_kt.shape
    groups = FOX_HEADS // (2 * pairs)
    w = pairs * HEAD_PAIR_W
    blk = lambda: pl.BlockSpec((s, w), lambda bi, g: (bi, g))
    cache = lambda: pl.BlockSpec((1, 2 * pairs, FOX_HD, p_len), lambda bi, g: (bi, g, 0, 0))
    return pl.pallas_call(
        functools.partial(_fox_sample_body, s=s, pairs=pairs),
        grid=(b, groups),
        in_specs=[blk(), cache(), cache(), blk(), blk(),
                  pl.BlockSpec((1, pairs, 2, p_len), lambda bi, g: (bi, g, 0, 0)),
                  pl.BlockSpec((1, pairs, 2, LANES), lambda bi, g: (bi, g, 0, 0)),
                  blk()],
        out_specs=blk(),
        out_shape=jax.ShapeDtypeStruct((b * s, FOX_W), BF16),
        compiler_params=_cparams(2), name="fox_sample_attention",
    )(q16, cache_kt, cache_vt, fk32, fv32, bias_c, bias_n, gate)


def _diff_sample_body(q_ref, kc_ref, vc_ref, kn_ref, vn_ref, bc_ref, bn_ref, gate_ref, sg_ref, lq1_ref,
                      lk1_ref, lq2_ref, lk2_ref, o_ref, m_ref, l_ref, acc_ref, *, s, pc, lam_init):
    c = pl.program_id(1)
    last = pl.num_programs(1) - 1
    w = 2 * DIFF_HD

    @pl.when(c == 0)
    def _():
        _init_state(m_ref, l_ref, acc_ref)

    qbd = [_block_diag_queries(q_ref[:, h * w:(h + 1) * w]) for h in range(DIFF_HEADS)]

    def update(h, sc, v16):
        m = m_ref[h]
        mn = jnp.maximum(m, jnp.max(sc, axis=1, keepdims=True))
        alpha = jnp.exp2(m - mn)
        p = jnp.exp2(sc - mn)
        l_ref[h] = alpha * l_ref[h] + jnp.sum(p, axis=1, keepdims=True)
        acc_ref[h] = alpha * acc_ref[h] + _dot(p.astype(BF16), v16)
        m_ref[h] = mn

    def chunk(with_bias):
        scores = [_nt_dot(qbd[h], kc_ref[0, pl.ds(h, pc, stride=DIFF_HEADS), :].astype(BF16))
                  for h in range(DIFF_HEADS)]
        for h, sc in enumerate(scores):
            if with_bias:
                sc = sc + jnp.concatenate([bc_ref[h], bc_ref[h]], axis=0)
            update(h, sc, vc_ref[0, pl.ds(h, pc, stride=DIFF_HEADS), :].astype(BF16))

    @pl.when(c != last)
    def _():
        chunk(False)

    @pl.when(c == last)
    def _():
        chunk(True)
        lam = _lambda_value(lq1_ref[...], lk1_ref[...], lq2_ref[...], lk2_ref[...], lam_init)
        zrows = jnp.zeros((LANES - s, w), BF16)
        for h in range(DIFF_HEADS):
            cols = slice(h * w, (h + 1) * w)
            kn = jnp.concatenate([kn_ref[:, cols].astype(BF16), zrows], axis=0)
            vn = jnp.concatenate([vn_ref[:, cols].astype(BF16), zrows], axis=0)
            sn = _nt_dot(qbd[h], kn) + jnp.concatenate([bn_ref[h], bn_ref[h]], axis=0)
            update(h, sn, vn)
            o = acc_ref[h] / l_ref[h]
            o = o[:s] - lam * o[s:]
            o_ref[:, cols] = _subln_gate(o, sg_ref[...], gate_ref[:, cols], lam_init)


def _diff_sample_attention(q16, cache_k, cache_v, dk32, dv32, bias_c, bias_n, gate, subln_g, lams, *, s, pc,
                           lam_init):
    b, rows, w = cache_k.shape
    p_len = rows // DIFF_HEADS
    assert p_len % pc == 0 and bias_c.shape == (DIFF_HEADS, s, pc)
    cache = lambda: pl.BlockSpec((1, DIFF_HEADS * pc, w), lambda bi, c: (bi, c, 0))
    new = lambda: pl.BlockSpec((s, DIFF_W), lambda bi, c: (bi, 0))
    half = lambda: pl.BlockSpec((s, DIFF_W), lambda bi, c: (bi, 1))
    const = lambda shape: pl.BlockSpec(shape, lambda bi, c: (0,) * len(shape))
    return pl.pallas_call(
        functools.partial(_diff_sample_body, s=s, pc=pc, lam_init=lam_init),
        grid=(b, p_len // pc),
        in_specs=[half(), cache(), cache(), new(), new(), const(bias_c.shape), const(bias_n.shape), half(),
                  const((1, w)), const((1, DIFF_HD)), const((1, DIFF_HD)), const((1, DIFF_HD)), const((1, DIFF_HD))],
        out_specs=new(),
        out_shape=jax.ShapeDtypeStruct((b * s, DIFF_W), BF16),
        scratch_shapes=[pltpu.VMEM((DIFF_HEADS, 2 * s, 1), F32), pltpu.VMEM((DIFF_HEADS, 2 * s, 1), F32),
                        pltpu.VMEM((DIFF_HEADS, 2 * s, w), F32)],
        compiler_params=_cparams(2), name="diff_sample_attention",
    )(q16, cache_k, cache_v, dk32, dv32, bias_c, bias_n, gate, subln_g, *lams)


def _outproj_body(mf_ref, md_ref, w_ref, x_ref, g_ref, y_ref):
    out = _dot(mf_ref[...], w_ref[:FOX_W, :]) + _dot(md_ref[...], w_ref[FOX_W:, :])
    ms = jnp.mean(out * out, axis=-1, keepdims=True)
    y_ref[...] = x_ref[...] + (out * lax.rsqrt(ms + EPS)) * g_ref[...]


def _out_projection(mf, md, w_out, x2d, g_post, *, tm):
    n, d = x2d.shape
    tm = min(tm, n)
    assert n % tm == 0
    rows = lambda w: pl.BlockSpec((tm, w), lambda i: (i, 0))
    return pl.pallas_call(
        _outproj_body, grid=(n // tm,),
        in_specs=[rows(FOX_W), rows(DIFF_W), pl.BlockSpec(w_out.shape, lambda i: (0, 0)), rows(d),
                  pl.BlockSpec((1, d), lambda i: (0, 0))],
        out_specs=rows(d), out_shape=jax.ShapeDtypeStruct((n, d), F32),
        compiler_params=_cparams(1), name="out_projection",
    )(mf, md, w_out.astype(BF16), x2d, g_post.reshape(1, d))


def _t5_bucket(rel):
    half = NUM_BUCKETS // 2
    max_exact = half // 2
    ret = jnp.where(rel > 0, half, 0)
    n = jnp.abs(rel)
    nf = jnp.maximum(n, 1).astype(jnp.float32)
    large = max_exact + (jnp.log(nf / max_exact) / math.log(MAX_DISTANCE / max_exact)
                         * (half - max_exact)).astype(jnp.int32)
    large = jnp.minimum(large, half - 1)
    return ret + jnp.where(n < max_exact, n, large)


def _rel_bias_of(rel_bias, rel):
    return rel_bias[_t5_bucket(rel)].astype(F32).T


def _toeplitz(g, rows, cols):
    h = g.shape[0]
    period = rows + cols
    v = jnp.concatenate([g[:, rows - 1:], jnp.zeros((h, 1), g.dtype), g[:, :rows - 1]], axis=1)
    flat = jnp.tile(v, (1, rows))[:, :rows * (period - 1)]
    return flat.reshape(h, rows, period - 1)[:, :, :cols]


def _chunk_mask(bias, row_pos, col_pos, rows_are_keys):
    r = (row_pos // CHUNK)[:, None]
    c = (col_pos // CHUNK)[None, :]
    mask = (r <= c) if rows_are_keys else (c <= r)
    return jnp.where(mask[None], bias * LOG2E, NEG_INF)


def _prompt_layer(x, g_pre, w_in, b_f, lams, subln_g, w_out, g_post, rel_bias, layer_idx, *, t=256, tm=512,
                  fox_pairs=4, diff_heads=4):
    b, s, d = x.shape
    assert b == 1 and s % t == 0 and s // LANES >= 1
    x2d = x.reshape(s, d)
    (fk, fv, dk, dv, logf, gate, qt, k16, dk16, vt) = _in_projection(
        x2d, g_pre, w_in, b_f, transposed=True, tm=tm, tk=t)

    rows = s // LANES
    _, hi, mid, lo = _prefix_sums(logf.T.reshape(FOX_HEADS * rows, LANES), seg=LANES, group=rows,
                                  rows_per_step=FOX_HEADS * rows, scale=LOG2E)
    pieces = jnp.stack([hi, mid, lo], axis=0).reshape(PIECES, FOX_HEADS, s)
    kext = jnp.concatenate([
        (-pieces).transpose(2, 1, 0).reshape(s, ONES_AT), jnp.ones((s, PIECES), BF16),
        jnp.zeros((s, LANES - ONES_AT - PIECES), BF16)], axis=1)
    sel = (jnp.arange(ONES_AT)[None, :] // PIECES == jnp.arange(FOX_HEADS)[:, None]).astype(BF16)
    qext = jnp.concatenate([
        jnp.broadcast_to(sel[:, :, None], (FOX_HEADS, ONES_AT, s)), pieces.transpose(1, 0, 2),
        jnp.zeros((FOX_HEADS, QEXT_ROWS - ONES_AT - PIECES, s), BF16)], axis=1)

    mf = _fox_prompt_attention(qt, qext, k16, kext, vt, gate, t=t, pairs=fox_pairs)

    lam_init = 0.8 - 0.6 * math.exp(-0.3 * layer_idx)
    assert t >= MAX_DISTANCE
    pos = jnp.arange(t, dtype=jnp.int32)
    off = jnp.arange(-(t - 1), t, dtype=jnp.int32)
    far = _rel_bias_of(rel_bias, jnp.full((1,), -2 * t, jnp.int32))
    bias_diag = _chunk_mask(_toeplitz(_rel_bias_of(rel_bias, -off) - far, t, t), pos, pos, True)
    bias_sub = _chunk_mask(_toeplitz(_rel_bias_of(rel_bias, -off - t) - far, t, t), pos, pos + t, True)
    md = _diff_prompt_attention(qt, dk16, vt, bias_diag, bias_sub, gate, subln_g.reshape(1, -1), lams,
                                t=t, heads=diff_heads, lam_init=lam_init)

    y = _out_projection(mf, md, w_out, x2d, g_post, tm=tm)
    return y.reshape(b, s, d), (fk, fv, logf, dk, dv)


def _sample_layer(x, past, g_pre, w_in, b_f, lams, subln_g, w_out, g_post, rel_bias, layer_idx, *, tm=512,
                  fox_pairs=2, diff_chunk=2048):
    b, s, d = x.shape
    pk, pv, plogf, pdk, pdv = past
    p_len = pk.shape[1]
    assert s <= LANES and LANES % s == 0 and p_len % LANES == 0
    x2d = x.reshape(b * s, d)
    fk, fv, dk, dv, logf, gate, q16 = _in_projection(x2d, g_pre, w_in, b_f, transposed=False, tm=tm, tk=tm)

    rows = p_len // LANES
    cum_c, _, _, _ = _prefix_sums(plogf.transpose(0, 2, 1).reshape(b * FOX_HEADS * rows, LANES), seg=LANES,
                                  group=rows, rows_per_step=32 * rows, scale=1.0)
    cum_c = cum_c.reshape(b, FOX_HEADS, p_len)
    bias_c = ((cum_c[:, :, -1:] - cum_c) * LOG2E).reshape(b, FOX_HEADS // 2, 2, p_len)
    cum_n, _, _, _ = _prefix_sums(
        logf.reshape(b, s, FOX_HEADS).transpose(0, 2, 1).reshape(b * FOX_HEADS * s // LANES, LANES),
        seg=s, group=1, rows_per_step=1024, scale=1.0)
    bias_n = jnp.concatenate([-cum_n.reshape(b, FOX_HEADS, s) * LOG2E,
                              jnp.full((b, FOX_HEADS, LANES - s), NEG_INF, F32)], axis=-1)
    bias_n = bias_n.reshape(b, FOX_HEADS // 2, 2, LANES)
    mf = _fox_sample_attention(q16, pk.transpose(0, 2, 3, 1), pv.transpose(0, 2, 3, 1), fk, fv, bias_c, bias_n,
                               gate, s=s, pairs=fox_pairs)

    lam_init = 0.8 - 0.6 * math.exp(-0.3 * layer_idx)
    pc = min(diff_chunk, p_len)
    assert pc >= MAX_DISTANCE + s
    q_pos = p_len + jnp.arange(s, dtype=jnp.int32)
    far = _rel_bias_of(rel_bias, jnp.full((1,), -2 * p_len, jnp.int32))
    off = jnp.arange(-(s - 1), pc, dtype=jnp.int32)
    bd_c = _chunk_mask(_toeplitz(_rel_bias_of(rel_bias, off - pc) - far, s, pc), q_pos,
                       p_len - pc + jnp.arange(pc, dtype=jnp.int32), False)
    off = jnp.arange(-(s - 1), s, dtype=jnp.int32)
    bd_n = _chunk_mask(_toeplitz(_rel_bias_of(rel_bias, off) - far, s, s), q_pos, q_pos, False)
    bd_n = jnp.concatenate([bd_n, jnp.full((DIFF_HEADS, s, LANES - s), NEG_INF, F32)], axis=-1)
    md = _diff_sample_attention(q16, pdk.reshape(b, p_len * DIFF_HEADS, 2 * DIFF_HD),
                                pdv.reshape(b, p_len * DIFF_HEADS, 2 * DIFF_HD), dk, dv, bd_c, bd_n, gate,
                                subln_g.reshape(1, -1), lams, s=s, pc=pc, lam_init=lam_init)

    y = _out_projection(mf, md, w_out, x2d, g_post, tm=tm)
    return y.reshape(b, s, d), (fk, fv, logf, dk, dv)


def kernel(x_prompt, x_sample, cache_fox_k, cache_fox_v, cache_fox_logf, cache_diff_k, cache_diff_v,
           norm_pre_g, w_in, forget_bias, lambda_q1, lambda_k1, lambda_q2, lambda_k2, subln_g, w_out,
           norm_post_g, rel_bias):
    depth = w_in.shape[0]
    y_p, y_s = x_prompt, x_sample
    rows_p, rows_s = [], []
    for l in range(depth):
        lams = tuple(a[l].reshape(1, -1) for a in (lambda_q1, lambda_k1, lambda_q2, lambda_k2))
        params = (norm_pre_g[l], w_in[l], forget_bias[l], lams, subln_g[l], w_out[l], norm_post_g[l], rel_bias, l)
        y_p, new_p = _prompt_layer(y_p, *params)
        past = (cache_fox_k[l], cache_fox_v[l], cache_fox_logf[l], cache_diff_k[l], cache_diff_v[l])
        y_s, new_s = _sample_layer(y_s, past, *params)
        rows_p.append(new_p)
        rows_s.append(new_s)

    bp, sp = x_prompt.shape[:2]
    bs, ss = x_sample.shape[:2]

    def stack(rows, idx, shape):
        return jnp.stack([r[idx].reshape(shape) for r in rows], axis=0)

    outs = [y_p, y_s]
    for rows, (b, s) in ((rows_p, (bp, sp)), (rows_s, (bs, ss))):
        outs += [stack(rows, 0, (b, s, FOX_HEADS, FOX_HD)), stack(rows, 1, (b, s, FOX_HEADS, FOX_HD)),
                 stack(rows, 2, (b, s, FOX_HEADS)), stack(rows, 3, (b, s, DIFF_HEADS, 2 * DIFF_HD)),
                 stack(rows, 4, (b, s, DIFF_HEADS, 2 * DIFF_HD))]
    return tuple(outs)
```

```python
import functools
import math

import jax
import jax.numpy as jnp
import numpy as np
from jax import lax
from jax.experimental import pallas as pl
from jax.experimental.pallas import tpu as pltpu

F32 = jnp.float32
BF16 = jnp.bfloat16

FOX_HEADS = 8
FOX_HD = 64
DIFF_HEADS = 4
DIFF_HD = 64
FOX_W = FOX_HEADS * FOX_HD
DIFF_W = DIFF_HEADS * 2 * DIFF_HD
CHUNK = 64
NUM_BUCKETS = 32
MAX_DISTANCE = 128
EPS = 1e-6
NEG_INF = -1e30
LOG2E = 1.4426950408889634

LANES = 128
HEAD_PAIR_W = 2 * FOX_HD
VMEM_LIMIT_BYTES = 56 * 1024 * 1024

PIECES = 3
ONES_AT = PIECES * FOX_HEADS
QEXT_ROWS = 32


def _cparams(n_axes):
    return pltpu.CompilerParams(
        dimension_semantics=("arbitrary",) * n_axes,
        vmem_limit_bytes=VMEM_LIMIT_BYTES,
    )


def _split3(a):
    hi = a.astype(BF16)
    r1 = a - hi.astype(F32)
    mid = r1.astype(BF16)
    lo = (r1 - mid.astype(F32)).astype(BF16)
    return hi, mid, lo


def _nt_dot(a, b):
    return lax.dot_general(a, b, (((1,), (1,)), ((), ())), preferred_element_type=F32)


def _dot(a, b):
    return jnp.dot(a, b, preferred_element_type=F32)


def _inproj_body(x_ref, g_ref, wkv_ref, wg_ref, wff_ref, bf_ref, wq_ref, wvt_ref,
                 fk_ref, fv_ref, dk_ref, dv_ref, logf_ref, gate_ref, q_ref, *rest,
                 transposed, tk, qscale):
    x = x_ref[...]
    ms = jnp.mean(x * x, axis=-1, keepdims=True)
    h = (x * lax.rsqrt(ms + EPS)) * g_ref[...]
    hb = h.astype(BF16)

    fk = _dot(hb, wkv_ref[:, 0 * FOX_W:1 * FOX_W])
    fk_ref[...] = fk
    fv_ref[...] = _dot(hb, wkv_ref[:, 1 * FOX_W:2 * FOX_W])
    dk = _dot(hb, wkv_ref[:, 2 * FOX_W:3 * FOX_W])
    dv = _dot(hb, wkv_ref[:, 3 * FOX_W:4 * FOX_W])
    tm = dk.shape[0]
    for hd in range(DIFF_HEADS):
        cols = slice(hd * 2 * DIFF_HD, (hd + 1) * 2 * DIFF_HD)
        dk_ref[pl.ds(hd, tm, stride=DIFF_HEADS), :] = dk[:, cols]
        dv_ref[pl.ds(hd, tm, stride=DIFF_HEADS), :] = dv[:, cols]

    ff = _dot(hb, wff_ref[...])
    logf = jax.nn.log_sigmoid(ff + bf_ref[...])
    logf_ref[...] = logf[:, :FOX_HEADS]

    gates = _dot(hb, wg_ref[...])
    gate_ref[...] = (gates * jax.nn.sigmoid(gates)).astype(BF16)

    if transposed:
        k16_ref, dk16_ref, vt_ref = rest
        k16_ref[...] = fk.astype(BF16)
        dk16_ref[...] = dk.astype(BF16)
        q_ref[...] = (_nt_dot(wq_ref[...], hb) * qscale).astype(BF16)
        vt = _nt_dot(wvt_ref[...], hb).astype(BF16)
        for c in range(vt.shape[1] // tk):
            vt_ref[c] = vt[:, c * tk:(c + 1) * tk]
    else:
        q_ref[...] = (_dot(hb, wq_ref[...]) * qscale).astype(BF16)


def _in_projection(x2d, g_pre, w_in, b_f, *, transposed, tm, tk):
    n, d = x2d.shape
    tm = min(tm, n)
    assert n % tm == 0 and (not transposed or tm % tk == 0)
    qscale = FOX_HD ** -0.5 * LOG2E
    o = np.cumsum([0, FOX_W, FOX_W, FOX_W, FOX_HEADS, FOX_W, DIFF_W, DIFF_W, DIFF_W, DIFF_W])
    w_fq, w_fk, w_fv, w_ff, w_fg, w_dq, w_dk, w_dv, w_dg = [w_in[:, o[i]:o[i + 1]] for i in range(9)]
    wkv = jnp.concatenate([w_fk, w_fv, w_dk, w_dv], axis=1).astype(BF16)
    wg = jnp.concatenate([w_fg, w_dg], axis=1).astype(BF16)
    wff = jnp.pad(w_ff, ((0, 0), (0, LANES - FOX_HEADS))).astype(BF16)
    bfp = jnp.pad(b_f.reshape(1, FOX_HEADS), ((0, 0), (0, LANES - FOX_HEADS)))
    wq = jnp.concatenate([w_fq, w_dq], axis=1)
    wv = jnp.concatenate([w_fv, w_dv], axis=1)
    if transposed:
        wq = wq.T.astype(BF16)
        wvt = wv.T.astype(BF16)
    else:
        wq = wq.astype(BF16)
        wvt = jnp.zeros((8, LANES), BF16)

    const = lambda shape: pl.BlockSpec(shape, lambda i: (0,) * len(shape))
    rows = lambda w: pl.BlockSpec((tm, w), lambda i: (i, 0))
    in_specs = [rows(d), const((1, d)), const(wkv.shape), const(wg.shape), const(wff.shape),
                const(bfp.shape), const(wq.shape), const(wvt.shape)]
    by_head = jax.ShapeDtypeStruct((DIFF_HEADS * n, 2 * DIFF_HD), F32)
    by_head_spec = pl.BlockSpec((DIFF_HEADS * tm, 2 * DIFF_HD), lambda i: (i, 0))
    out_shape = [jax.ShapeDtypeStruct((n, FOX_W), F32)] * 2 + [by_head] * 2
    out_shape += [jax.ShapeDtypeStruct((n, FOX_HEADS), F32), jax.ShapeDtypeStruct((n, FOX_W + DIFF_W), BF16)]
    out_specs = [rows(FOX_W), rows(FOX_W), by_head_spec, by_head_spec, rows(FOX_HEADS), rows(FOX_W + DIFF_W)]
    if transposed:
        out_shape += [jax.ShapeDtypeStruct((FOX_W + DIFF_W, n), BF16),
                      jax.ShapeDtypeStruct((n, FOX_W), BF16), jax.ShapeDtypeStruct((n, DIFF_W), BF16),
                      jax.ShapeDtypeStruct((n // tk, FOX_W + DIFF_W, tk), BF16)]
        out_specs += [pl.BlockSpec((FOX_W + DIFF_W, tm), lambda i: (0, i)), rows(FOX_W), rows(DIFF_W),
                      pl.BlockSpec((tm // tk, FOX_W + DIFF_W, tk), lambda i: (i, 0, 0))]
    else:
        out_shape += [jax.ShapeDtypeStruct((n, FOX_W + DIFF_W), BF16)]
        out_specs += [rows(FOX_W + DIFF_W)]
    return pl.pallas_call(
        functools.partial(_inproj_body, transposed=transposed, tk=tk, qscale=qscale),
        grid=(n // tm,), in_specs=in_specs, out_specs=out_specs, out_shape=out_shape,
        compiler_params=_cparams(1), name="in_projection_t" if transposed else "in_projection_r",
    )(x2d, g_pre.reshape(1, d), wkv, wg, wff, bfp, wq, wvt)


def _cumsum_body(x_ref, cum_ref, hi_ref, mid_ref, lo_ref, *, seg, group, scale):
    x = x_ref[...]
    r = x.shape[0]
    a = lax.broadcasted_iota(jnp.int32, (LANES, LANES), 0)
    b = lax.broadcasted_iota(jnp.int32, (LANES, LANES), 1)
    sh = int(math.log2(seg))
    upper = ((a <= b) & ((a >> sh) == (b >> sh))).astype(BF16)
    pieces = _split3(x)
    w = sum(_dot(p, upper) for p in pieces)
    if group > 1:
        ones = jnp.ones((LANES, LANES), BF16)
        tot = sum(_dot(p, ones) for p in pieces)
        ri = lax.broadcasted_iota(jnp.int32, (r, r), 0)
        ci = lax.broadcasted_iota(jnp.int32, (r, r), 1)
        gsh = int(math.log2(group))
        lower = ((ci < ri) & ((ri >> gsh) == (ci >> gsh))).astype(BF16)
        w = w + sum(_dot(lower, p) for p in _split3(tot))
    cum_ref[...] = w
    hi, mid, lo = _split3(w * scale)
    hi_ref[...] = hi
    mid_ref[...] = mid
    lo_ref[...] = lo


def _prefix_sums(x2d, *, seg, group, rows_per_step, scale):
    r = x2d.shape[0]
    rb = min(rows_per_step, r)
    assert r % rb == 0 and rb % group == 0 and (seg == LANES or group == 1)
    assert seg & (seg - 1) == 0 and group & (group - 1) == 0
    spec = pl.BlockSpec((rb, LANES), lambda i: (i, 0))
    return pl.pallas_call(
        functools.partial(_cumsum_body, seg=seg, group=group, scale=scale),
        grid=(r // rb,), in_specs=[spec], out_specs=[spec] * 4,
        out_shape=[jax.ShapeDtypeStruct((r, LANES), F32)] + [jax.ShapeDtypeStruct((r, LANES), BF16)] * 3,
        compiler_params=_cparams(1), name="prefix_sums",
    )(x2d)


SUM_ROWS = 16
SCORES_AHEAD = 2
FAR_UNROLL = 4


def _with_sum_rows(vt):
    row = lax.broadcasted_iota(jnp.int32, (SUM_ROWS, vt.shape[1]), 0)
    return jnp.concatenate([vt, jnp.where(row == 0, 1.0, 0.0).astype(BF16)], axis=0)


def _store_scores(buf, h, s):
    buf[0][h] = s
    buf[1][h] = jnp.max(s, axis=0, keepdims=True)


def _online_update_t(s, tile_max, h, m_ref, acc_ref, vt_aug):
    m = m_ref[h]
    mn = jnp.maximum(m, tile_max)
    alpha = jnp.exp2(m - mn)
    p = jnp.exp2(s - mn)
    acc_ref[h] = alpha * acc_ref[h] + _dot(vt_aug, p.astype(BF16))
    m_ref[h] = mn


def _run_far_tiles(n, half_step, buf_a, buf_b):
    def unrolled(base, count):
        bufs = (buf_a, buf_b)
        for u in range(count):
            half_step(base + u, bufs[u % 2], bufs[(u + 1) % 2])

    def body(jj, c):
        unrolled(FAR_UNROLL * jj, FAR_UNROLL)
        return c

    lax.fori_loop(0, n // FAR_UNROLL, body, 0)
    for count in range(FAR_UNROLL - 2, 0, -2):
        @pl.when(n % FAR_UNROLL // 2 * 2 == count)
        def _():
            unrolled(n // FAR_UNROLL * FAR_UNROLL, count)


def _init_state(m_ref, l_ref, acc_ref):
    m_ref[...] = jnp.full(m_ref.shape, NEG_INF, F32)
    if l_ref is not None:
        l_ref[...] = jnp.zeros(l_ref.shape, F32)
    acc_ref[...] = jnp.zeros(acc_ref.shape, F32)


def _resident(shape, index_map):
    return pl.BlockSpec(shape, index_map, pipeline_mode=pl.Buffered(1))


def _fox_prompt_body(q_ref, qx_ref, k_ref, kx_ref, vt_ref, gate_ref, o_ref, m_ref, acc_ref, sa_ref, sb_ref,
                     ma_ref, mb_ref, *, t, pairs):
    i = pl.program_id(1)
    heads = 2 * pairs
    buf_a, buf_b = (sa_ref, ma_ref), (sb_ref, mb_ref)
    _init_state(m_ref, None, acc_ref)
    z64 = jnp.zeros((FOX_HD, t), BF16)
    zpad = jnp.zeros((LANES - QEXT_ROWS, t), BF16)
    qa = []
    for p in range(pairs):
        q = q_ref[p * HEAD_PAIR_W:(p + 1) * HEAD_PAIR_W, :]
        qa.append(jnp.concatenate([q[:FOX_HD], z64, qx_ref[2 * p], zpad], axis=0))
        qa.append(jnp.concatenate([z64, q[FOX_HD:], qx_ref[2 * p + 1], zpad], axis=0))

    def keys(j, p):
        off = pl.multiple_of(j * t, t)
        return jnp.concatenate([k_ref[pl.ds(off, t), p * HEAD_PAIR_W:(p + 1) * HEAD_PAIR_W],
                                kx_ref[pl.ds(off, t), :]], axis=1)

    def consume(j, cur, h, diagonal):
        s = cur[0][h]
        if diagonal:
            kk = lax.broadcasted_iota(jnp.int32, (t, t), 0)
            qq = lax.broadcasted_iota(jnp.int32, (t, t), 1)
            s = jnp.where(kk <= qq, s, NEG_INF)
            tile_max = jnp.max(s, axis=0, keepdims=True)
        else:
            tile_max = cur[1][h]
        _online_update_t(s, tile_max, h, m_ref, acc_ref,
                         _with_sum_rows(vt_ref[j, h * FOX_HD:(h + 1) * FOX_HD, :]))

    def half_step(j, cur, nxt):
        ka = {}

        def prefetch(h):
            if h // 2 not in ka:
                ka[h // 2] = keys(j + 1, h // 2)
            _store_scores(nxt, h, _dot(ka[h // 2], qa[h]))

        for h in range(min(SCORES_AHEAD, heads)):
            prefetch(h)
        for h in range(heads):
            consume(j, cur, h, False)
            if h + SCORES_AHEAD < heads:
                prefetch(h + SCORES_AHEAD)

    def last_step(j, cur):
        for h in range(heads):
            consume(j, cur, h, True)

    for p in range(pairs):
        ka = keys(0, p)
        _store_scores(buf_a, 2 * p, _dot(ka, qa[2 * p]))
        _store_scores(buf_a, 2 * p + 1, _dot(ka, qa[2 * p + 1]))

    _run_far_tiles(i, half_step, buf_a, buf_b)

    @pl.when(i % 2 == 0)
    def _():
        last_step(i, buf_a)

    @pl.when(i % 2 == 1)
    def _():
        half_step(i - 1, buf_a, buf_b)
        last_step(i, buf_b)

    for p in range(pairs):
        a0, a1 = acc_ref[2 * p], acc_ref[2 * p + 1]
        ot = jnp.concatenate([a0[:FOX_HD] / a0[FOX_HD:FOX_HD + 1], a1[:FOX_HD] / a1[FOX_HD:FOX_HD + 1]], axis=0)
        cols = slice(p * HEAD_PAIR_W, (p + 1) * HEAD_PAIR_W)
        o_ref[:, cols] = (ot.T * gate_ref[:, cols].astype(F32)).astype(BF16)


def _fox_prompt_attention(qt, qext, k16, kext, vt, gate, *, t, pairs):
    n = k16.shape[0]
    nq = n // t
    w = pairs * HEAD_PAIR_W
    heads = 2 * pairs
    assert (FOX_HEADS // 2) % pairs == 0
    return pl.pallas_call(
        functools.partial(_fox_prompt_body, t=t, pairs=pairs),
        grid=(FOX_HEADS // heads, nq),
        in_specs=[pl.BlockSpec((w, t), lambda g, i: (g, i)),
                  pl.BlockSpec((heads, QEXT_ROWS, t), lambda g, i: (g, 0, i)),
                  _resident((n, w), lambda g, i: (0, g)),
                  _resident((n, LANES), lambda g, i: (0, 0)),
                  _resident((nq, w, t), lambda g, i: (0, g, 0)),
                  pl.BlockSpec((t, w), lambda g, i: (i, g))],
        out_specs=pl.BlockSpec((t, w), lambda g, i: (i, g)),
        out_shape=jax.ShapeDtypeStruct((n, FOX_W), BF16),
        scratch_shapes=[pltpu.VMEM((heads, 1, t), F32), pltpu.VMEM((heads, FOX_HD + SUM_ROWS, t), F32),
                        pltpu.VMEM((heads, t, t), F32), pltpu.VMEM((heads, t, t), F32),
                        pltpu.VMEM((heads, 1, t), F32), pltpu.VMEM((heads, 1, t), F32)],
        compiler_params=_cparams(2), name="fox_prompt_attention",
    )(qt, qext, k16, kext, vt, gate)


def _lambda_value(lq1, lk1, lq2, lk2, lam_init):
    a = jnp.sum(lq1 * lk1, axis=-1, keepdims=True)
    b = jnp.sum(lq2 * lk2, axis=-1, keepdims=True)
    return jnp.exp(a) - jnp.exp(b) + lam_init


def _subln_gate(y, g, gate, lam_init):
    ms = jnp.mean(y * y, axis=-1, keepdims=True)
    yn = (y * lax.rsqrt(ms + EPS)) * g
    return ((yn * (1.0 - lam_init)) * gate.astype(F32)).astype(BF16)


def _diff_prompt_body(q_ref, k_ref, vt_ref, bd_ref, bs_ref, gate_ref, sg_ref, lq1_ref, lk1_ref, lq2_ref,
                      lk2_ref, o_ref, m_ref, acc_ref, sa_ref, sb_ref, ma_ref, mb_ref, *, t, heads, lam_init):
    i = pl.program_id(1)
    w = 2 * DIFF_HD
    buf_a, buf_b = (sa_ref, ma_ref), (sb_ref, mb_ref)
    _init_state(m_ref, None, acc_ref)
    z64 = jnp.zeros((DIFF_HD, t), BF16)
    qa = []
    for h in range(heads):
        q = q_ref[h * w:(h + 1) * w, :]
        qa.append(jnp.concatenate([q[:DIFF_HD], z64], axis=0))
        qa.append(jnp.concatenate([z64, q[DIFF_HD:]], axis=0))

    def keys(j, h):
        return k_ref[pl.ds(pl.multiple_of(j * t, t), t), h * w:(h + 1) * w]

    def consume(j, cur, c, bias_ref):
        h = c // 2
        s = cur[0][c]
        if bias_ref is not None:
            s = s + bias_ref[h]
            tile_max = jnp.max(s, axis=0, keepdims=True)
        else:
            tile_max = cur[1][c]
        _online_update_t(s, tile_max, c, m_ref, acc_ref, _with_sum_rows(vt_ref[j, h * w:(h + 1) * w, :]))

    def half_step(j, cur, nxt, bias_ref=None):
        kk = {}

        def prefetch(c):
            if c // 2 not in kk:
                kk[c // 2] = keys(j + 1, c // 2)
            _store_scores(nxt, c, _dot(kk[c // 2], qa[c]))

        for c in range(min(SCORES_AHEAD, 2 * heads)):
            prefetch(c)
        for c in range(2 * heads):
            consume(j, cur, c, bias_ref)
            if c + SCORES_AHEAD < 2 * heads:
                prefetch(c + SCORES_AHEAD)

    def last_step(j, cur):
        for c in range(2 * heads):
            consume(j, cur, c, bd_ref)

    for h in range(heads):
        kk = keys(0, h)
        _store_scores(buf_a, 2 * h, _dot(kk, qa[2 * h]))
        _store_scores(buf_a, 2 * h + 1, _dot(kk, qa[2 * h + 1]))

    _run_far_tiles(jnp.maximum(i - 1, 0), half_step, buf_a, buf_b)

    @pl.when(i == 0)
    def _():
        last_step(i, buf_a)

    @pl.when(i % 2 == 1)
    def _():
        half_step(i - 1, buf_a, buf_b, bs_ref)
        last_step(i, buf_b)

    @pl.when((i >= 2) & (i % 2 == 0))
    def _():
        half_step(i - 2, buf_a, buf_b)
        half_step(i - 1, buf_b, buf_a, bs_ref)
        last_step(i, buf_a)

    lam = _lambda_value(lq1_ref[...], lk1_ref[...], lq2_ref[...], lk2_ref[...], lam_init)
    for h in range(heads):
        a0, a1 = acc_ref[2 * h], acc_ref[2 * h + 1]
        ot = a0[:w] / a0[w:w + 1] - lam * (a1[:w] / a1[w:w + 1])
        cols = slice(h * w, (h + 1) * w)
        o_ref[:, cols] = _subln_gate(ot.T, sg_ref[...], gate_ref[:, cols], lam_init)


def _diff_prompt_attention(qt, dk16, vt, bias_diag, bias_sub, gate, subln_g, lams, *, t, heads, lam_init):
    n = dk16.shape[0]
    nq = n // t
    w = heads * 2 * DIFF_HD
    assert DIFF_HEADS % heads == 0 and FOX_W % w == 0
    fox_blocks = FOX_W // w
    vec = lambda width: pl.BlockSpec((1, width), lambda g, i: (0, 0))
    return pl.pallas_call(
        functools.partial(_diff_prompt_body, t=t, heads=heads, lam_init=lam_init),
        grid=(DIFF_HEADS // heads, nq),
        in_specs=[pl.BlockSpec((w, t), lambda g, i: (fox_blocks + g, i)),
                  _resident((n, w), lambda g, i: (0, g)),
                  _resident((nq, w, t), lambda g, i: (0, fox_blocks + g, 0)),
                  pl.BlockSpec((heads, t, t), lambda g, i: (g, 0, 0)),
                  pl.BlockSpec((heads, t, t), lambda g, i: (g, 0, 0)),
                  pl.BlockSpec((t, w), lambda g, i: (i, fox_blocks + g)),
                  vec(2 * DIFF_HD), vec(DIFF_HD), vec(DIFF_HD), vec(DIFF_HD), vec(DIFF_HD)],
        out_specs=pl.BlockSpec((t, w), lambda g, i: (i, g)),
        out_shape=jax.ShapeDtypeStruct((n, DIFF_W), BF16),
        scratch_shapes=[pltpu.VMEM((2 * heads, 1, t), F32),
                        pltpu.VMEM((2 * heads, 2 * DIFF_HD + SUM_ROWS, t), F32),
                        pltpu.VMEM((2 * heads, t, t), F32), pltpu.VMEM((2 * heads, t, t), F32),
                        pltpu.VMEM((2 * heads, 1, t), F32), pltpu.VMEM((2 * heads, 1, t), F32)],
        compiler_params=_cparams(2), name="diff_prompt_attention",
    )(qt, dk16, vt, bias_diag, bias_sub, gate, subln_g, *lams)


def _block_diag_queries(qb):
    lane = lax.broadcasted_iota(jnp.int32, qb.shape, 1)
    zero = jnp.zeros_like(qb)
    return jnp.concatenate([jnp.where(lane < FOX_HD, qb, zero), jnp.where(lane >= FOX_HD, qb, zero)], axis=0)


def _fox_sample_body(q_ref, kc_ref, vc_ref, kn_ref, vn_ref, bc_ref, bn_ref, gate_ref, o_ref, *, s, pairs):
    p_len = kc_ref.shape[-1]
    zrows = jnp.zeros((LANES - s, HEAD_PAIR_W), BF16)
    qi = lax.broadcasted_iota(jnp.int32, (2 * s, LANES), 0)
    ki = lax.broadcasted_iota(jnp.int32, (2 * s, LANES), 1)
    causal = ki <= jnp.where(qi >= s, qi - s, qi)
    lane = lax.broadcasted_iota(jnp.int32, (s, HEAD_PAIR_W), 1)

    def rows_bias(b, width):
        return jnp.concatenate([jnp.broadcast_to(b[0:1], (s, width)), jnp.broadcast_to(b[1:2], (s, width))], axis=0)

    scores = []
    for p in range(pairs):
        cols = slice(p * HEAD_PAIR_W, (p + 1) * HEAD_PAIR_W)
        qbd = _block_diag_queries(q_ref[:, cols])
        kt = kc_ref[0, 2 * p:2 * p + 2].reshape(HEAD_PAIR_W, p_len).astype(BF16)
        kn = jnp.concatenate([kn_ref[:, cols].astype(BF16), zrows], axis=0)
        sc = _dot(qbd, kt) + rows_bias(bc_ref[0, p], p_len)
        sn = _nt_dot(qbd, kn) + rows_bias(bn_ref[0, p], LANES)
        scores.append((sc, jnp.where(causal, sn, NEG_INF)))
    for p, (sc, sn) in enumerate(scores):
        cols = slice(p * HEAD_PAIR_W, (p + 1) * HEAD_PAIR_W)
        vt = vc_ref[0, 2 * p:2 * p + 2].reshape(HEAD_PAIR_W, p_len).astype(BF16)
        vn = jnp.concatenate([vn_ref[:, cols].astype(BF16), zrows], axis=0)
        m = jnp.maximum(jnp.max(sc, axis=1, keepdims=True), jnp.max(sn, axis=1, keepdims=True))
        pc = jnp.exp2(sc - m)
        pn = jnp.exp2(sn - m)
        l = jnp.sum(pc, axis=1, keepdims=True) + jnp.sum(pn, axis=1, keepdims=True)
        o = (_nt_dot(pc.astype(BF16), vt) + _dot(pn.astype(BF16), vn)) / l
        out = jnp.where(lane < FOX_HD, o[:s], o[s:])
        o_ref[:, cols] = (out * gate_ref[:, cols].astype(F32)).astype(BF16)


def _fox_sample_attention(q16, cache_kt, cache_vt, fk32, fv32, bias_c, bias_n, gate, *, s, pairs):
    b, _, _, p_len = cache_kt.shape
    groups = FOX_HEADS // (2 * pairs)
    w = pairs * HEAD_PAIR_W
    blk = lambda: pl.BlockSpec((s, w), lambda bi, g: (bi, g))
    cache = lambda: pl.BlockSpec((1, 2 * pairs, FOX_HD, p_len), lambda bi, g: (bi, g, 0, 0))
    return pl.pallas_call(
        functools.partial(_fox_sample_body, s=s, pairs=pairs),
        grid=(b, groups),
        in_specs=[blk(), cache(), cache(), blk(), blk(),
                  pl.BlockSpec((1, pairs, 2, p_len), lambda bi, g: (bi, g, 0, 0)),
                  pl.BlockSpec((1, pairs, 2, LANES), lambda bi, g: (bi, g, 0, 0)),
                  blk()],
        out_specs=blk(),
        out_shape=jax.ShapeDtypeStruct((b * s, FOX_W), BF16),
        compiler_params=_cparams(2), name="fox_sample_attention",
    )(q16, cache_kt, cache_vt, fk32, fv32, bias_c, bias_n, gate)


def _diff_sample_body(q_ref, kc_ref, vc_ref, kn_ref, vn_ref, bc_ref, bn_ref, gate_ref, sg_ref, lq1_ref,
                      lk1_ref, lq2_ref, lk2_ref, o_ref, m_ref, l_ref, acc_ref, *, s, pc, lam_init):
    c = pl.program_id(1)
    last = pl.num_programs(1) - 1
    w = 2 * DIFF_HD

    @pl.when(c == 0)
    def _():
        _init_state(m_ref, l_ref, acc_ref)

    qbd = [_block_diag_queries(q_ref[:, h * w:(h + 1) * w]) for h in range(DIFF_HEADS)]

    def update(h, sc, v16):
        m = m_ref[h]
        mn = jnp.maximum(m, jnp.max(sc, axis=1, keepdims=True))
        alpha = jnp.exp2(m - mn)
        p = jnp.exp2(sc - mn)
        l_ref[h] = alpha * l_ref[h] + jnp.sum(p, axis=1, keepdims=True)
        acc_ref[h] = alpha * acc_ref[h] + _dot(p.astype(BF16), v16)
        m_ref[h] = mn

    def chunk(with_bias):
        scores = [_nt_dot(qbd[h], kc_ref[0, pl.ds(h, pc, stride=DIFF_HEADS), :].astype(BF16))
                  for h in range(DIFF_HEADS)]
        for h, sc in enumerate(scores):
            if with_bias:
                sc = sc + jnp.concatenate([bc_ref[h], bc_ref[h]], axis=0)
            update(h, sc, vc_ref[0, pl.ds(h, pc, stride=DIFF_HEADS), :].astype(BF16))

    @pl.when(c != last)
    def _():
        chunk(False)

    @pl.when(c == last)
    def _():
        chunk(True)
        lam = _lambda_value(lq1_ref[...], lk1_ref[...], lq2_ref[...], lk2_ref[...], lam_init)
        zrows = jnp.zeros((LANES - s, w), BF16)
        for h in range(DIFF_HEADS):
            cols = slice(h * w, (h + 1) * w)
            new_rows = pl.ds(h, s, stride=DIFF_HEADS)
            kn = jnp.concatenate([kn_ref[new_rows, :].astype(BF16), zrows], axis=0)
            vn = jnp.concatenate([vn_ref[new_rows, :].astype(BF16), zrows], axis=0)
            sn = _nt_dot(qbd[h], kn) + jnp.concatenate([bn_ref[h], bn_ref[h]], axis=0)
            update(h, sn, vn)
            o = acc_ref[h] / l_ref[h]
            o = o[:s] - lam * o[s:]
            o_ref[:, cols] = _subln_gate(o, sg_ref[...], gate_ref[:, cols], lam_init)


def _diff_sample_attention(q16, cache_k, cache_v, dk32, dv32, bias_c, bias_n, gate, subln_g, lams, *, s, pc,
                           lam_init):
    b, rows, w = cache_k.shape
    p_len = rows // DIFF_HEADS
    assert p_len % pc == 0 and bias_c.shape == (DIFF_HEADS, s, pc)
    cache = lambda: pl.BlockSpec((1, DIFF_HEADS * pc, w), lambda bi, c: (bi, c, 0))
    out = lambda: pl.BlockSpec((s, DIFF_W), lambda bi, c: (bi, 0))
    new = lambda: pl.BlockSpec((DIFF_HEADS * s, w), lambda bi, c: (bi, 0))
    half = lambda: pl.BlockSpec((s, DIFF_W), lambda bi, c: (bi, 1))
    const = lambda shape: pl.BlockSpec(shape, lambda bi, c: (0,) * len(shape))
    return pl.pallas_call(
        functools.partial(_diff_sample_body, s=s, pc=pc, lam_init=lam_init),
        grid=(b, p_len // pc),
        in_specs=[half(), cache(), cache(), new(), new(), const(bias_c.shape), const(bias_n.shape), half(),
                  const((1, w)), const((1, DIFF_HD)), const((1, DIFF_HD)), const((1, DIFF_HD)), const((1, DIFF_HD))],
        out_specs=out(),
        out_shape=jax.ShapeDtypeStruct((b * s, DIFF_W), BF16),
        scratch_shapes=[pltpu.VMEM((DIFF_HEADS, 2 * s, 1), F32), pltpu.VMEM((DIFF_HEADS, 2 * s, 1), F32),
                        pltpu.VMEM((DIFF_HEADS, 2 * s, w), F32)],
        compiler_params=_cparams(2), name="diff_sample_attention",
    )(q16, cache_k, cache_v, dk32, dv32, bias_c, bias_n, gate, subln_g, *lams)


def _outproj_body(mf_ref, md_ref, w_ref, x_ref, g_ref, y_ref):
    out = _dot(mf_ref[...], w_ref[:FOX_W, :]) + _dot(md_ref[...], w_ref[FOX_W:, :])
    ms = jnp.mean(out * out, axis=-1, keepdims=True)
    y_ref[...] = x_ref[...] + (out * lax.rsqrt(ms + EPS)) * g_ref[...]


def _out_projection(mf, md, w_out, x2d, g_post, *, tm):
    n, d = x2d.shape
    tm = min(tm, n)
    assert n % tm == 0
    rows = lambda w: pl.BlockSpec((tm, w), lambda i: (i, 0))
    return pl.pallas_call(
        _outproj_body, grid=(n // tm,),
        in_specs=[rows(FOX_W), rows(DIFF_W), pl.BlockSpec(w_out.shape, lambda i: (0, 0)), rows(d),
                  pl.BlockSpec((1, d), lambda i: (0, 0))],
        out_specs=rows(d), out_shape=jax.ShapeDtypeStruct((n, d), F32),
        compiler_params=_cparams(1), name="out_projection",
    )(mf, md, w_out.astype(BF16), x2d, g_post.reshape(1, d))


def _t5_bucket(rel):
    half = NUM_BUCKETS // 2
    max_exact = half // 2
    ret = jnp.where(rel > 0, half, 0)
    n = jnp.abs(rel)
    nf = jnp.maximum(n, 1).astype(jnp.float32)
    large = max_exact + (jnp.log(nf / max_exact) / math.log(MAX_DISTANCE / max_exact)
                         * (half - max_exact)).astype(jnp.int32)
    large = jnp.minimum(large, half - 1)
    return ret + jnp.where(n < max_exact, n, large)


def _rel_bias_of(rel_bias, rel):
    return rel_bias[_t5_bucket(rel)].astype(F32).T


def _toeplitz(g, rows, cols):
    h = g.shape[0]
    period = rows + cols
    v = jnp.concatenate([g[:, rows - 1:], jnp.zeros((h, 1), g.dtype), g[:, :rows - 1]], axis=1)
    flat = jnp.tile(v, (1, rows))[:, :rows * (period - 1)]
    return flat.reshape(h, rows, period - 1)[:, :, :cols]


def _chunk_mask(bias, row_pos, col_pos, rows_are_keys):
    r = (row_pos // CHUNK)[:, None]
    c = (col_pos // CHUNK)[None, :]
    mask = (r <= c) if rows_are_keys else (c <= r)
    return jnp.where(mask[None], bias * LOG2E, NEG_INF)


def _prompt_layer(x, g_pre, w_in, b_f, lams, subln_g, w_out, g_post, rel_bias, layer_idx, *, t=256, tm=512,
                  tm_out=1024, fox_pairs=4, diff_heads=4):
    b, s, d = x.shape
    assert b == 1 and s % t == 0 and s // LANES >= 1
    x2d = x.reshape(s, d)
    (fk, fv, dk, dv, logf, gate, qt, k16, dk16, vt) = _in_projection(
        x2d, g_pre, w_in, b_f, transposed=True, tm=tm, tk=t)

    rows = s // LANES
    _, hi, mid, lo = _prefix_sums(logf.T.reshape(FOX_HEADS * rows, LANES), seg=LANES, group=rows,
                                  rows_per_step=FOX_HEADS * rows, scale=LOG2E)
    pieces = jnp.stack([hi, mid, lo], axis=0).reshape(PIECES, FOX_HEADS, s)
    kext = jnp.concatenate([
        (-pieces).transpose(2, 1, 0).reshape(s, ONES_AT), jnp.ones((s, PIECES), BF16),
        jnp.zeros((s, LANES - ONES_AT - PIECES), BF16)], axis=1)
    sel = (jnp.arange(ONES_AT)[None, :] // PIECES == jnp.arange(FOX_HEADS)[:, None]).astype(BF16)
    qext = jnp.concatenate([
        jnp.broadcast_to(sel[:, :, None], (FOX_HEADS, ONES_AT, s)), pieces.transpose(1, 0, 2),
        jnp.zeros((FOX_HEADS, QEXT_ROWS - ONES_AT - PIECES, s), BF16)], axis=1)

    mf = _fox_prompt_attention(qt, qext, k16, kext, vt, gate, t=t, pairs=fox_pairs)

    lam_init = 0.8 - 0.6 * math.exp(-0.3 * layer_idx)
    assert t >= MAX_DISTANCE
    pos = jnp.arange(t, dtype=jnp.int32)
    off = jnp.arange(-(t - 1), t, dtype=jnp.int32)
    far = _rel_bias_of(rel_bias, jnp.full((1,), -2 * t, jnp.int32))
    bias_diag = _chunk_mask(_toeplitz(_rel_bias_of(rel_bias, -off) - far, t, t), pos, pos, True)
    bias_sub = _chunk_mask(_toeplitz(_rel_bias_of(rel_bias, -off - t) - far, t, t), pos, pos + t, True)
    md = _diff_prompt_attention(qt, dk16, vt, bias_diag, bias_sub, gate, subln_g.reshape(1, -1), lams,
                                t=t, heads=diff_heads, lam_init=lam_init)

    y = _out_projection(mf, md, w_out, x2d, g_post, tm=tm_out)
    return y.reshape(b, s, d), (fk, fv, logf, dk, dv)


def _sample_layer(x, past, g_pre, w_in, b_f, lams, subln_g, w_out, g_post, rel_bias, layer_idx, *, tm=512,
                  fox_pairs=2, diff_chunk=2048):
    b, s, d = x.shape
    pk, pv, plogf, pdk, pdv = past
    p_len = pk.shape[1]
    assert s <= LANES and LANES % s == 0 and p_len % LANES == 0
    x2d = x.reshape(b * s, d)
    fk, fv, dk, dv, logf, gate, q16 = _in_projection(x2d, g_pre, w_in, b_f, transposed=False, tm=tm, tk=tm)

    rows = p_len // LANES
    cum_c, _, _, _ = _prefix_sums(plogf.transpose(0, 2, 1).reshape(b * FOX_HEADS * rows, LANES), seg=LANES,
                                  group=rows, rows_per_step=32 * rows, scale=1.0)
    cum_c = cum_c.reshape(b, FOX_HEADS, p_len)
    bias_c = ((cum_c[:, :, -1:] - cum_c) * LOG2E).reshape(b, FOX_HEADS // 2, 2, p_len)
    cum_n, _, _, _ = _prefix_sums(
        logf.reshape(b, s, FOX_HEADS).transpose(0, 2, 1).reshape(b * FOX_HEADS * s // LANES, LANES),
        seg=s, group=1, rows_per_step=1024, scale=1.0)
    bias_n = jnp.concatenate([-cum_n.reshape(b, FOX_HEADS, s) * LOG2E,
                              jnp.full((b, FOX_HEADS, LANES - s), NEG_INF, F32)], axis=-1)
    bias_n = bias_n.reshape(b, FOX_HEADS // 2, 2, LANES)
    mf = _fox_sample_attention(q16, pk.transpose(0, 2, 3, 1), pv.transpose(0, 2, 3, 1), fk, fv, bias_c, bias_n,
                               gate, s=s, pairs=fox_pairs)

    lam_init = 0.8 - 0.6 * math.exp(-0.3 * layer_idx)
    pc = min(diff_chunk, p_len)
    assert pc >= MAX_DISTANCE + s
    q_pos = p_len + jnp.arange(s, dtype=jnp.int32)
    far = _rel_bias_of(rel_bias, jnp.full((1,), -2 * p_len, jnp.int32))
    off = jnp.arange(-(s - 1), pc, dtype=jnp.int32)
    bd_c = _chunk_mask(_toeplitz(_rel_bias_of(rel_bias, off - pc) - far, s, pc), q_pos,
                       p_len - pc + jnp.arange(pc, dtype=jnp.int32), False)
    off = jnp.arange(-(s - 1), s, dtype=jnp.int32)
    bd_n = _chunk_mask(_toeplitz(_rel_bias_of(rel_bias, off) - far, s, s), q_pos, q_pos, False)
    bd_n = jnp.concatenate([bd_n, jnp.full((DIFF_HEADS, s, LANES - s), NEG_INF, F32)], axis=-1)
    md = _diff_sample_attention(q16, pdk.reshape(b, p_len * DIFF_HEADS, 2 * DIFF_HD),
                                pdv.reshape(b, p_len * DIFF_HEADS, 2 * DIFF_HD), dk, dv, bd_c, bd_n, gate,
                                subln_g.reshape(1, -1), lams, s=s, pc=pc, lam_init=lam_init)

    y = _out_projection(mf, md, w_out, x2d, g_post, tm=tm)
    return y.reshape(b, s, d), (fk, fv, logf, dk, dv)


def kernel(x_prompt, x_sample, cache_fox_k, cache_fox_v, cache_fox_logf, cache_diff_k, cache_diff_v,
           norm_pre_g, w_in, forget_bias, lambda_q1, lambda_k1, lambda_q2, lambda_k2, subln_g, w_out,
           norm_post_g, rel_bias):
    depth = w_in.shape[0]
    y_p, y_s = x_prompt, x_sample
    rows_p, rows_s = [], []
    for l in range(depth):
        lams = tuple(a[l].reshape(1, -1) for a in (lambda_q1, lambda_k1, lambda_q2, lambda_k2))
        params = (norm_pre_g[l], w_in[l], forget_bias[l], lams, subln_g[l], w_out[l], norm_post_g[l], rel_bias, l)
        y_p, new_p = _prompt_layer(y_p, *params)
        past = (cache_fox_k[l], cache_fox_v[l], cache_fox_logf[l], cache_diff_k[l], cache_diff_v[l])
        y_s, new_s = _sample_layer(y_s, past, *params)
        rows_p.append(new_p)
        rows_s.append(new_s)

    bp, sp = x_prompt.shape[:2]
    bs, ss = x_sample.shape[:2]

    def stack(rows, idx, shape):
        return jnp.stack([r[idx].reshape(shape) for r in rows], axis=0)

    outs = [y_p, y_s]
    for rows, (b, s) in ((rows_p, (bp, sp)), (rows_s, (bs, ss))):
        outs += [stack(rows, 0, (b, s, FOX_HEADS, FOX_HD)), stack(rows, 1, (b, s, FOX_HEADS, FOX_HD)),
                 stack(rows, 2, (b, s, FOX_HEADS)), stack(rows, 3, (b, s, DIFF_HEADS, 2 * DIFF_HD)),
                 stack(rows, 4, (b, s, DIFF_HEADS, 2 * DIFF_HD))]
    return tuple(outs)
```

```python
import functools
import math

import jax
import jax.numpy as jnp
import numpy as np
from jax import lax
from jax.experimental import pallas as pl
from jax.experimental.pallas import tpu as pltpu

F32 = jnp.float32
BF16 = jnp.bfloat16

FOX_HEADS = 8
FOX_HD = 64
DIFF_HEADS = 4
DIFF_HD = 64
FOX_W = FOX_HEADS * FOX_HD
DIFF_W = DIFF_HEADS * 2 * DIFF_HD
CHUNK = 64
NUM_BUCKETS = 32
MAX_DISTANCE = 128
EPS = 1e-6
NEG_INF = -1e30
LOG2E = 1.4426950408889634

LANES = 128
HEAD_PAIR_W = 2 * FOX_HD
VMEM_LIMIT_BYTES = 56 * 1024 * 1024

PIECES = 3
ONES_AT = PIECES * FOX_HEADS
QEXT_ROWS = 32


def _cparams(n_axes):
    return pltpu.CompilerParams(
        dimension_semantics=("arbitrary",) * n_axes,
        vmem_limit_bytes=VMEM_LIMIT_BYTES,
    )


def _split3(a):
    hi = a.astype(BF16)
    r1 = a - hi.astype(F32)
    mid = r1.astype(BF16)
    lo = (r1 - mid.astype(F32)).astype(BF16)
    return hi, mid, lo


def _nt_dot(a, b):
    return lax.dot_general(a, b, (((1,), (1,)), ((), ())), preferred_element_type=F32)


def _dot(a, b):
    return jnp.dot(a, b, preferred_element_type=F32)


def _inproj_body(x_ref, g_ref, wkv_ref, wg_ref, wff_ref, bf_ref, wq_ref, wvt_ref,
                 fk_ref, fv_ref, dk_ref, dv_ref, logf_ref, gate_ref, q_ref, *rest,
                 transposed, tk, qscale):
    x = x_ref[...]
    ms = jnp.mean(x * x, axis=-1, keepdims=True)
    h = (x * lax.rsqrt(ms + EPS)) * g_ref[...]
    hb = h.astype(BF16)

    fk = _dot(hb, wkv_ref[:, 0 * FOX_W:1 * FOX_W])
    fk_ref[...] = fk
    fv_ref[...] = _dot(hb, wkv_ref[:, 1 * FOX_W:2 * FOX_W])
    dk = _dot(hb, wkv_ref[:, 2 * FOX_W:3 * FOX_W])
    dv = _dot(hb, wkv_ref[:, 3 * FOX_W:4 * FOX_W])
    tm = dk.shape[0]
    for hd in range(DIFF_HEADS):
        cols = slice(hd * 2 * DIFF_HD, (hd + 1) * 2 * DIFF_HD)
        dk_ref[pl.ds(hd, tm, stride=DIFF_HEADS), :] = dk[:, cols]
        dv_ref[pl.ds(hd, tm, stride=DIFF_HEADS), :] = dv[:, cols]

    ff = _dot(hb, wff_ref[...])
    logf = jax.nn.log_sigmoid(ff + bf_ref[...])
    logf_ref[...] = logf[:, :FOX_HEADS]

    gates = _dot(hb, wg_ref[...])
    gate_ref[...] = (gates * jax.nn.sigmoid(gates)).astype(BF16)

    if transposed:
        k16_ref, dk16_ref, vt_ref = rest
        k16_ref[...] = fk.astype(BF16)
        dk16_ref[...] = dk.astype(BF16)
        q_ref[...] = (_nt_dot(wq_ref[...], hb) * qscale).astype(BF16)
        vt = _nt_dot(wvt_ref[...], hb).astype(BF16)
        for c in range(vt.shape[1] // tk):
            vt_ref[c] = vt[:, c * tk:(c + 1) * tk]
    else:
        q_ref[...] = (_dot(hb, wq_ref[...]) * qscale).astype(BF16)


def _in_projection(x2d, g_pre, w_in, b_f, *, transposed, tm, tk):
    n, d = x2d.shape
    tm = min(tm, n)
    assert n % tm == 0 and (not transposed or tm % tk == 0)
    qscale = FOX_HD ** -0.5 * LOG2E
    o = np.cumsum([0, FOX_W, FOX_W, FOX_W, FOX_HEADS, FOX_W, DIFF_W, DIFF_W, DIFF_W, DIFF_W])
    w_fq, w_fk, w_fv, w_ff, w_fg, w_dq, w_dk, w_dv, w_dg = [w_in[:, o[i]:o[i + 1]] for i in range(9)]
    wkv = jnp.concatenate([w_fk, w_fv, w_dk, w_dv], axis=1).astype(BF16)
    wg = jnp.concatenate([w_fg, w_dg], axis=1).astype(BF16)
    wff = jnp.pad(w_ff, ((0, 0), (0, LANES - FOX_HEADS))).astype(BF16)
    bfp = jnp.pad(b_f.reshape(1, FOX_HEADS), ((0, 0), (0, LANES - FOX_HEADS)))
    wq = jnp.concatenate([w_fq, w_dq], axis=1)
    wv = jnp.concatenate([w_fv, w_dv], axis=1)
    if transposed:
        wq = wq.T.astype(BF16)
        wvt = wv.T.astype(BF16)
    else:
        wq = wq.astype(BF16)
        wvt = jnp.zeros((8, LANES), BF16)

    const = lambda shape: pl.BlockSpec(shape, lambda i: (0,) * len(shape))
    rows = lambda w: pl.BlockSpec((tm, w), lambda i: (i, 0))
    in_specs = [rows(d), const((1, d)), const(wkv.shape), const(wg.shape), const(wff.shape),
                const(bfp.shape), const(wq.shape), const(wvt.shape)]
    by_head = jax.ShapeDtypeStruct((DIFF_HEADS * n, 2 * DIFF_HD), F32)
    by_head_spec = pl.BlockSpec((DIFF_HEADS * tm, 2 * DIFF_HD), lambda i: (i, 0))
    out_shape = [jax.ShapeDtypeStruct((n, FOX_W), F32)] * 2 + [by_head] * 2
    out_shape += [jax.ShapeDtypeStruct((n, FOX_HEADS), F32), jax.ShapeDtypeStruct((n, FOX_W + DIFF_W), BF16)]
    out_specs = [rows(FOX_W), rows(FOX_W), by_head_spec, by_head_spec, rows(FOX_HEADS), rows(FOX_W + DIFF_W)]
    if transposed:
        out_shape += [jax.ShapeDtypeStruct((FOX_W + DIFF_W, n), BF16),
                      jax.ShapeDtypeStruct((n, FOX_W), BF16), jax.ShapeDtypeStruct((n, DIFF_W), BF16),
                      jax.ShapeDtypeStruct((n // tk, FOX_W + DIFF_W, tk), BF16)]
        out_specs += [pl.BlockSpec((FOX_W + DIFF_W, tm), lambda i: (0, i)), rows(FOX_W), rows(DIFF_W),
                      pl.BlockSpec((tm // tk, FOX_W + DIFF_W, tk), lambda i: (i, 0, 0))]
    else:
        out_shape += [jax.ShapeDtypeStruct((n, FOX_W + DIFF_W), BF16)]
        out_specs += [rows(FOX_W + DIFF_W)]
    return pl.pallas_call(
        functools.partial(_inproj_body, transposed=transposed, tk=tk, qscale=qscale),
        grid=(n // tm,), in_specs=in_specs, out_specs=out_specs, out_shape=out_shape,
        compiler_params=_cparams(1), name="in_projection_t" if transposed else "in_projection_r",
    )(x2d, g_pre.reshape(1, d), wkv, wg, wff, bfp, wq, wvt)


def _cumsum_body(x_ref, cum_ref, hi_ref, mid_ref, lo_ref, *, seg, group, scale):
    x = x_ref[...]
    r = x.shape[0]
    a = lax.broadcasted_iota(jnp.int32, (LANES, LANES), 0)
    b = lax.broadcasted_iota(jnp.int32, (LANES, LANES), 1)
    sh = int(math.log2(seg))
    upper = ((a <= b) & ((a >> sh) == (b >> sh))).astype(BF16)
    pieces = _split3(x)
    w = sum(_dot(p, upper) for p in pieces)
    if group > 1:
        ones = jnp.ones((LANES, LANES), BF16)
        tot = sum(_dot(p, ones) for p in pieces)
        ri = lax.broadcasted_iota(jnp.int32, (r, r), 0)
        ci = lax.broadcasted_iota(jnp.int32, (r, r), 1)
        gsh = int(math.log2(group))
        lower = ((ci < ri) & ((ri >> gsh) == (ci >> gsh))).astype(BF16)
        w = w + sum(_dot(lower, p) for p in _split3(tot))
    cum_ref[...] = w
    hi, mid, lo = _split3(w * scale)
    hi_ref[...] = hi
    mid_ref[...] = mid
    lo_ref[...] = lo


def _prefix_sums(x2d, *, seg, group, rows_per_step, scale):
    r = x2d.shape[0]
    rb = min(rows_per_step, r)
    assert r % rb == 0 and rb % group == 0 and (seg == LANES or group == 1)
    assert seg & (seg - 1) == 0 and group & (group - 1) == 0
    spec = pl.BlockSpec((rb, LANES), lambda i: (i, 0))
    return pl.pallas_call(
        functools.partial(_cumsum_body, seg=seg, group=group, scale=scale),
        grid=(r // rb,), in_specs=[spec], out_specs=[spec] * 4,
        out_shape=[jax.ShapeDtypeStruct((r, LANES), F32)] + [jax.ShapeDtypeStruct((r, LANES), BF16)] * 3,
        compiler_params=_cparams(1), name="prefix_sums",
    )(x2d)


SUM_ROWS = 16
SCORES_AHEAD = 2
FAR_UNROLL = 8


def _with_sum_rows(vt):
    row = lax.broadcasted_iota(jnp.int32, (SUM_ROWS, vt.shape[1]), 0)
    return jnp.concatenate([vt, jnp.where(row == 0, 1.0, 0.0).astype(BF16)], axis=0)


def _store_scores(buf, h, s):
    buf[0][h] = s
    buf[1][h] = jnp.max(s, axis=0, keepdims=True)


def _online_update_t(s, tile_max, h, m_ref, acc_ref, vt_aug):
    m = m_ref[h]
    mn = jnp.maximum(m, tile_max)
    alpha = jnp.exp2(m - mn)
    p = jnp.exp2(s - mn)
    acc_ref[h] = alpha * acc_ref[h] + _dot(vt_aug, p.astype(BF16))
    m_ref[h] = mn


def _run_far_tiles(n, half_step, buf_a, buf_b):
    def unrolled(base, count):
        bufs = (buf_a, buf_b)
        for u in range(count):
            half_step(base + u, bufs[u % 2], bufs[(u + 1) % 2])

    def body(jj, c):
        unrolled(FAR_UNROLL * jj, FAR_UNROLL)
        return c

    lax.fori_loop(0, n // FAR_UNROLL, body, 0)
    for count in range(FAR_UNROLL - 2, 0, -2):
        @pl.when(n % FAR_UNROLL // 2 * 2 == count)
        def _():
            unrolled(n // FAR_UNROLL * FAR_UNROLL, count)


def _init_state(m_ref, l_ref, acc_ref):
    m_ref[...] = jnp.full(m_ref.shape, NEG_INF, F32)
    if l_ref is not None:
        l_ref[...] = jnp.zeros(l_ref.shape, F32)
    acc_ref[...] = jnp.zeros(acc_ref.shape, F32)


def _resident(shape, index_map):
    return pl.BlockSpec(shape, index_map, pipeline_mode=pl.Buffered(1))


def _fox_prompt_body(q_ref, qx_ref, k_ref, kx_ref, vt_ref, gate_ref, o_ref, m_ref, acc_ref, sa_ref, sb_ref,
                     ma_ref, mb_ref, *, t, pairs):
    i = pl.program_id(1)
    heads = 2 * pairs
    buf_a, buf_b = (sa_ref, ma_ref), (sb_ref, mb_ref)
    _init_state(m_ref, None, acc_ref)
    z64 = jnp.zeros((FOX_HD, t), BF16)
    zpad = jnp.zeros((LANES - QEXT_ROWS, t), BF16)
    qa = []
    for p in range(pairs):
        q = q_ref[p * HEAD_PAIR_W:(p + 1) * HEAD_PAIR_W, :]
        qa.append(jnp.concatenate([q[:FOX_HD], z64, qx_ref[2 * p], zpad], axis=0))
        qa.append(jnp.concatenate([z64, q[FOX_HD:], qx_ref[2 * p + 1], zpad], axis=0))

    def keys(j, p):
        off = pl.multiple_of(j * t, t)
        return jnp.concatenate([k_ref[pl.ds(off, t), p * HEAD_PAIR_W:(p + 1) * HEAD_PAIR_W],
                                kx_ref[pl.ds(off, t), :]], axis=1)

    def consume(j, cur, h, diagonal):
        s = cur[0][h]
        if diagonal:
            kk = lax.broadcasted_iota(jnp.int32, (t, t), 0)
            qq = lax.broadcasted_iota(jnp.int32, (t, t), 1)
            s = jnp.where(kk <= qq, s, NEG_INF)
            tile_max = jnp.max(s, axis=0, keepdims=True)
        else:
            tile_max = cur[1][h]
        _online_update_t(s, tile_max, h, m_ref, acc_ref,
                         _with_sum_rows(vt_ref[j, h * FOX_HD:(h + 1) * FOX_HD, :]))

    def half_step(j, cur, nxt):
        ka = {}

        def prefetch(h):
            if h // 2 not in ka:
                ka[h // 2] = keys(j + 1, h // 2)
            _store_scores(nxt, h, _dot(ka[h // 2], qa[h]))

        for h in range(min(SCORES_AHEAD, heads)):
            prefetch(h)
        for h in range(heads):
            consume(j, cur, h, False)
            if h + SCORES_AHEAD < heads:
                prefetch(h + SCORES_AHEAD)

    def last_step(j, cur):
        for h in range(heads):
            consume(j, cur, h, True)

    for p in range(pairs):
        ka = keys(0, p)
        _store_scores(buf_a, 2 * p, _dot(ka, qa[2 * p]))
        _store_scores(buf_a, 2 * p + 1, _dot(ka, qa[2 * p + 1]))

    _run_far_tiles(i, half_step, buf_a, buf_b)

    @pl.when(i % 2 == 0)
    def _():
        last_step(i, buf_a)

    @pl.when(i % 2 == 1)
    def _():
        half_step(i - 1, buf_a, buf_b)
        last_step(i, buf_b)

    for p in range(pairs):
        a0, a1 = acc_ref[2 * p], acc_ref[2 * p + 1]
        ot = jnp.concatenate([a0[:FOX_HD] / a0[FOX_HD:FOX_HD + 1], a1[:FOX_HD] / a1[FOX_HD:FOX_HD + 1]], axis=0)
        cols = slice(p * HEAD_PAIR_W, (p + 1) * HEAD_PAIR_W)
        o_ref[:, cols] = (ot.T * gate_ref[:, cols].astype(F32)).astype(BF16)


def _fox_prompt_attention(qt, qext, k16, kext, vt, gate, *, t, pairs):
    n = k16.shape[0]
    nq = n // t
    w = pairs * HEAD_PAIR_W
    heads = 2 * pairs
    assert (FOX_HEADS // 2) % pairs == 0
    return pl.pallas_call(
        functools.partial(_fox_prompt_body, t=t, pairs=pairs),
        grid=(FOX_HEADS // heads, nq),
        in_specs=[pl.BlockSpec((w, t), lambda g, i: (g, i)),
                  pl.BlockSpec((heads, QEXT_ROWS, t), lambda g, i: (g, 0, i)),
                  _resident((n, w), lambda g, i: (0, g)),
                  _resident((n, LANES), lambda g, i: (0, 0)),
                  _resident((nq, w, t), lambda g, i: (0, g, 0)),
                  pl.BlockSpec((t, w), lambda g, i: (i, g))],
        out_specs=pl.BlockSpec((t, w), lambda g, i: (i, g)),
        out_shape=jax.ShapeDtypeStruct((n, FOX_W), BF16),
        scratch_shapes=[pltpu.VMEM((heads, 1, t), F32), pltpu.VMEM((heads, FOX_HD + SUM_ROWS, t), F32),
                        pltpu.VMEM((heads, t, t), F32), pltpu.VMEM((heads, t, t), F32),
                        pltpu.VMEM((heads, 1, t), F32), pltpu.VMEM((heads, 1, t), F32)],
        compiler_params=_cparams(2), name="fox_prompt_attention",
    )(qt, qext, k16, kext, vt, gate)


def _lambda_value(lq1, lk1, lq2, lk2, lam_init):
    a = jnp.sum(lq1 * lk1, axis=-1, keepdims=True)
    b = jnp.sum(lq2 * lk2, axis=-1, keepdims=True)
    return jnp.exp(a) - jnp.exp(b) + lam_init


def _subln_gate(y, g, gate, lam_init):
    ms = jnp.mean(y * y, axis=-1, keepdims=True)
    yn = (y * lax.rsqrt(ms + EPS)) * g
    return ((yn * (1.0 - lam_init)) * gate.astype(F32)).astype(BF16)


def _diff_prompt_body(q_ref, k_ref, vt_ref, bd_ref, bs_ref, gate_ref, sg_ref, lq1_ref, lk1_ref, lq2_ref,
                      lk2_ref, o_ref, m_ref, acc_ref, sa_ref, sb_ref, ma_ref, mb_ref, *, t, heads, lam_init):
    i = pl.program_id(1)
    w = 2 * DIFF_HD
    buf_a, buf_b = (sa_ref, ma_ref), (sb_ref, mb_ref)
    _init_state(m_ref, None, acc_ref)
    z64 = jnp.zeros((DIFF_HD, t), BF16)
    qa = []
    for h in range(heads):
        q = q_ref[h * w:(h + 1) * w, :]
        qa.append(jnp.concatenate([q[:DIFF_HD], z64], axis=0))
        qa.append(jnp.concatenate([z64, q[DIFF_HD:]], axis=0))

    def keys(j, h):
        return k_ref[pl.ds(pl.multiple_of(j * t, t), t), h * w:(h + 1) * w]

    def consume(j, cur, c, bias_ref):
        h = c // 2
        s = cur[0][c]
        if bias_ref is not None:
            s = s + bias_ref[h]
            tile_max = jnp.max(s, axis=0, keepdims=True)
        else:
            tile_max = cur[1][c]
        _online_update_t(s, tile_max, c, m_ref, acc_ref, _with_sum_rows(vt_ref[j, h * w:(h + 1) * w, :]))

    def half_step(j, cur, nxt, bias_ref=None):
        kk = {}

        def prefetch(c):
            if c // 2 not in kk:
                kk[c // 2] = keys(j + 1, c // 2)
            _store_scores(nxt, c, _dot(kk[c // 2], qa[c]))

        for c in range(min(SCORES_AHEAD, 2 * heads)):
            prefetch(c)
        for c in range(2 * heads):
            consume(j, cur, c, bias_ref)
            if c + SCORES_AHEAD < 2 * heads:
                prefetch(c + SCORES_AHEAD)

    def last_step(j, cur):
        for c in range(2 * heads):
            consume(j, cur, c, bd_ref)

    for h in range(heads):
        kk = keys(0, h)
        _store_scores(buf_a, 2 * h, _dot(kk, qa[2 * h]))
        _store_scores(buf_a, 2 * h + 1, _dot(kk, qa[2 * h + 1]))

    _run_far_tiles(jnp.maximum(i - 1, 0), half_step, buf_a, buf_b)

    @pl.when(i == 0)
    def _():
        last_step(i, buf_a)

    @pl.when(i % 2 == 1)
    def _():
        half_step(i - 1, buf_a, buf_b, bs_ref)
        last_step(i, buf_b)

    @pl.when((i >= 2) & (i % 2 == 0))
    def _():
        half_step(i - 2, buf_a, buf_b)
        half_step(i - 1, buf_b, buf_a, bs_ref)
        last_step(i, buf_a)

    lam = _lambda_value(lq1_ref[...], lk1_ref[...], lq2_ref[...], lk2_ref[...], lam_init)
    for h in range(heads):
        a0, a1 = acc_ref[2 * h], acc_ref[2 * h + 1]
        ot = a0[:w] / a0[w:w + 1] - lam * (a1[:w] / a1[w:w + 1])
        cols = slice(h * w, (h + 1) * w)
        o_ref[:, cols] = _subln_gate(ot.T, sg_ref[...], gate_ref[:, cols], lam_init)


def _diff_prompt_attention(qt, dk16, vt, bias_diag, bias_sub, gate, subln_g, lams, *, t, heads, lam_init):
    n = dk16.shape[0]
    nq = n // t
    w = heads * 2 * DIFF_HD
    assert DIFF_HEADS % heads == 0 and FOX_W % w == 0
    fox_blocks = FOX_W // w
    vec = lambda width: pl.BlockSpec((1, width), lambda g, i: (0, 0))
    return pl.pallas_call(
        functools.partial(_diff_prompt_body, t=t, heads=heads, lam_init=lam_init),
        grid=(DIFF_HEADS // heads, nq),
        in_specs=[pl.BlockSpec((w, t), lambda g, i: (fox_blocks + g, i)),
                  _resident((n, w), lambda g, i: (0, g)),
                  _resident((nq, w, t), lambda g, i: (0, fox_blocks + g, 0)),
                  pl.BlockSpec((heads, t, t), lambda g, i: (g, 0, 0)),
                  pl.BlockSpec((heads, t, t), lambda g, i: (g, 0, 0)),
                  pl.BlockSpec((t, w), lambda g, i: (i, fox_blocks + g)),
                  vec(2 * DIFF_HD), vec(DIFF_HD), vec(DIFF_HD), vec(DIFF_HD), vec(DIFF_HD)],
        out_specs=pl.BlockSpec((t, w), lambda g, i: (i, g)),
        out_shape=jax.ShapeDtypeStruct((n, DIFF_W), BF16),
        scratch_shapes=[pltpu.VMEM((2 * heads, 1, t), F32),
                        pltpu.VMEM((2 * heads, 2 * DIFF_HD + SUM_ROWS, t), F32),
                        pltpu.VMEM((2 * heads, t, t), F32), pltpu.VMEM((2 * heads, t, t), F32),
                        pltpu.VMEM((2 * heads, 1, t), F32), pltpu.VMEM((2 * heads, 1, t), F32)],
        compiler_params=_cparams(2), name="diff_prompt_attention",
    )(qt, dk16, vt, bias_diag, bias_sub, gate, subln_g, *lams)


def _block_diag_queries(qb):
    lane = lax.broadcasted_iota(jnp.int32, qb.shape, 1)
    zero = jnp.zeros_like(qb)
    return jnp.concatenate([jnp.where(lane < FOX_HD, qb, zero), jnp.where(lane >= FOX_HD, qb, zero)], axis=0)


def _fox_sample_body(q_ref, kc_ref, vc_ref, kn_ref, vn_ref, bc_ref, bn_ref, gate_ref, o_ref, *, s, pairs):
    p_len = kc_ref.shape[-1]
    zrows = jnp.zeros((LANES - s, HEAD_PAIR_W), BF16)
    qi = lax.broadcasted_iota(jnp.int32, (2 * s, LANES), 0)
    ki = lax.broadcasted_iota(jnp.int32, (2 * s, LANES), 1)
    causal = ki <= jnp.where(qi >= s, qi - s, qi)
    lane = lax.broadcasted_iota(jnp.int32, (s, HEAD_PAIR_W), 1)

    def rows_bias(b, width):
        return jnp.concatenate([jnp.broadcast_to(b[0:1], (s, width)), jnp.broadcast_to(b[1:2], (s, width))], axis=0)

    scores = []
    for p in range(pairs):
        cols = slice(p * HEAD_PAIR_W, (p + 1) * HEAD_PAIR_W)
        qbd = _block_diag_queries(q_ref[:, cols])
        kt = kc_ref[0, 2 * p:2 * p + 2].reshape(HEAD_PAIR_W, p_len).astype(BF16)
        kn = jnp.concatenate([kn_ref[:, cols].astype(BF16), zrows], axis=0)
        sc = _dot(qbd, kt) + rows_bias(bc_ref[0, p], p_len)
        sn = _nt_dot(qbd, kn) + rows_bias(bn_ref[0, p], LANES)
        scores.append((sc, jnp.where(causal, sn, NEG_INF)))
    for p, (sc, sn) in enumerate(scores):
        cols = slice(p * HEAD_PAIR_W, (p + 1) * HEAD_PAIR_W)
        vt = vc_ref[0, 2 * p:2 * p + 2].reshape(HEAD_PAIR_W, p_len).astype(BF16)
        vn = jnp.concatenate([vn_ref[:, cols].astype(BF16), zrows], axis=0)
        m = jnp.maximum(jnp.max(sc, axis=1, keepdims=True), jnp.max(sn, axis=1, keepdims=True))
        pc = jnp.exp2(sc - m)
        pn = jnp.exp2(sn - m)
        l = jnp.sum(pc, axis=1, keepdims=True) + jnp.sum(pn, axis=1, keepdims=True)
        o = (_nt_dot(pc.astype(BF16), vt) + _dot(pn.astype(BF16), vn)) / l
        out = jnp.where(lane < FOX_HD, o[:s], o[s:])
        o_ref[:, cols] = (out * gate_ref[:, cols].astype(F32)).astype(BF16)


def _fox_sample_attention(q16, cache_kt, cache_vt, fk32, fv32, bias_c, bias_n, gate, *, s, pairs):
    b, _, _, p_len = cache_kt.shape
    groups = FOX_HEADS // (2 * pairs)
    w = pairs * HEAD_PAIR_W
    blk = lambda: pl.BlockSpec((s, w), lambda bi, g: (bi, g))
    cache = lambda: pl.BlockSpec((1, 2 * pairs, FOX_HD, p_len), lambda bi, g: (bi, g, 0, 0))
    return pl.pallas_call(
        functools.partial(_fox_sample_body, s=s, pairs=pairs),
        grid=(b, groups),
        in_specs=[blk(), cache(), cache(), blk(), blk(),
                  pl.BlockSpec((1, pairs, 2, p_len), lambda bi, g: (bi, g, 0, 0)),
                  pl.BlockSpec((1, pairs, 2, LANES), lambda bi, g: (bi, g, 0, 0)),
                  blk()],
        out_specs=blk(),
        out_shape=jax.ShapeDtypeStruct((b * s, FOX_W), BF16),
        compiler_params=_cparams(2), name="fox_sample_attention",
    )(q16, cache_kt, cache_vt, fk32, fv32, bias_c, bias_n, gate)


def _diff_sample_body(q_ref, kc_ref, vc_ref, kn_ref, vn_ref, bc_ref, bn_ref, gate_ref, sg_ref, lq1_ref,
                      lk1_ref, lq2_ref, lk2_ref, o_ref, m_ref, l_ref, acc_ref, *, s, pc, lam_init):
    c = pl.program_id(1)
    last = pl.num_programs(1) - 1
    w = 2 * DIFF_HD

    @pl.when(c == 0)
    def _():
        _init_state(m_ref, l_ref, acc_ref)

    qbd = [_block_diag_queries(q_ref[:, h * w:(h + 1) * w]) for h in range(DIFF_HEADS)]

    def update(h, sc, v16):
        m = m_ref[h]
        mn = jnp.maximum(m, jnp.max(sc, axis=1, keepdims=True))
        alpha = jnp.exp2(m - mn)
        p = jnp.exp2(sc - mn)
        l_ref[h] = alpha * l_ref[h] + jnp.sum(p, axis=1, keepdims=True)
        acc_ref[h] = alpha * acc_ref[h] + _dot(p.astype(BF16), v16)
        m_ref[h] = mn

    def chunk(with_bias):
        scores = [_nt_dot(qbd[h], kc_ref[0, pl.ds(h, pc, stride=DIFF_HEADS), :].astype(BF16))
                  for h in range(DIFF_HEADS)]
        for h, sc in enumerate(scores):
            if with_bias:
                sc = sc + jnp.concatenate([bc_ref[h], bc_ref[h]], axis=0)
            update(h, sc, vc_ref[0, pl.ds(h, pc, stride=DIFF_HEADS), :].astype(BF16))

    @pl.when(c != last)
    def _():
        chunk(False)

    @pl.when(c == last)
    def _():
        chunk(True)
        lam = _lambda_value(lq1_ref[...], lk1_ref[...], lq2_ref[...], lk2_ref[...], lam_init)
        zrows = jnp.zeros((LANES - s, w), BF16)
        for h in range(DIFF_HEADS):
            cols = slice(h * w, (h + 1) * w)
            new_rows = pl.ds(h, s, stride=DIFF_HEADS)
            kn = jnp.concatenate([kn_ref[new_rows, :].astype(BF16), zrows], axis=0)
            vn = jnp.concatenate([vn_ref[new_rows, :].astype(BF16), zrows], axis=0)
            sn = _nt_dot(qbd[h], kn) + jnp.concatenate([bn_ref[h], bn_ref[h]], axis=0)
            update(h, sn, vn)
            o = acc_ref[h] / l_ref[h]
            o = o[:s] - lam * o[s:]
            o_ref[:, cols] = _subln_gate(o, sg_ref[...], gate_ref[:, cols], lam_init)


def _diff_sample_attention(q16, cache_k, cache_v, dk32, dv32, bias_c, bias_n, gate, subln_g, lams, *, s, pc,
                           lam_init):
    b, rows, w = cache_k.shape
    p_len = rows // DIFF_HEADS
    assert p_len % pc == 0 and bias_c.shape == (DIFF_HEADS, s, pc)
    cache = lambda: pl.BlockSpec((1, DIFF_HEADS * pc, w), lambda bi, c: (bi, c, 0))
    out = lambda: pl.BlockSpec((s, DIFF_W), lambda bi, c: (bi, 0))
    new = lambda: pl.BlockSpec((DIFF_HEADS * s, w), lambda bi, c: (bi, 0))
    half = lambda: pl.BlockSpec((s, DIFF_W), lambda bi, c: (bi, 1))
    const = lambda shape: pl.BlockSpec(shape, lambda bi, c: (0,) * len(shape))
    return pl.pallas_call(
        functools.partial(_diff_sample_body, s=s, pc=pc, lam_init=lam_init),
        grid=(b, p_len // pc),
        in_specs=[half(), cache(), cache(), new(), new(), const(bias_c.shape), const(bias_n.shape), half(),
                  const((1, w)), const((1, DIFF_HD)), const((1, DIFF_HD)), const((1, DIFF_HD)), const((1, DIFF_HD))],
        out_specs=out(),
        out_shape=jax.ShapeDtypeStruct((b * s, DIFF_W), BF16),
        scratch_shapes=[pltpu.VMEM((DIFF_HEADS, 2 * s, 1), F32), pltpu.VMEM((DIFF_HEADS, 2 * s, 1), F32),
                        pltpu.VMEM((DIFF_HEADS, 2 * s, w), F32)],
        compiler_params=_cparams(2), name="diff_sample_attention",
    )(q16, cache_k, cache_v, dk32, dv32, bias_c, bias_n, gate, subln_g, *lams)


def _outproj_body(mf_ref, md_ref, w_ref, x_ref, g_ref, y_ref):
    out = _dot(mf_ref[...], w_ref[:FOX_W, :]) + _dot(md_ref[...], w_ref[FOX_W:, :])
    ms = jnp.mean(out * out, axis=-1, keepdims=True)
    y_ref[...] = x_ref[...] + (out * lax.rsqrt(ms + EPS)) * g_ref[...]


def _out_projection(mf, md, w_out, x2d, g_post, *, tm):
    n, d = x2d.shape
    tm = min(tm, n)
    assert n % tm == 0
    rows = lambda w: pl.BlockSpec((tm, w), lambda i: (i, 0))
    return pl.pallas_call(
        _outproj_body, grid=(n // tm,),
        in_specs=[rows(FOX_W), rows(DIFF_W), pl.BlockSpec(w_out.shape, lambda i: (0, 0)), rows(d),
                  pl.BlockSpec((1, d), lambda i: (0, 0))],
        out_specs=rows(d), out_shape=jax.ShapeDtypeStruct((n, d), F32),
        compiler_params=_cparams(1), name="out_projection",
    )(mf, md, w_out.astype(BF16), x2d, g_post.reshape(1, d))


def _t5_bucket(rel):
    half = NUM_BUCKETS // 2
    max_exact = half // 2
    ret = jnp.where(rel > 0, half, 0)
    n = jnp.abs(rel)
    nf = jnp.maximum(n, 1).astype(jnp.float32)
    large = max_exact + (jnp.log(nf / max_exact) / math.log(MAX_DISTANCE / max_exact)
                         * (half - max_exact)).astype(jnp.int32)
    large = jnp.minimum(large, half - 1)
    return ret + jnp.where(n < max_exact, n, large)


def _rel_bias_of(rel_bias, rel):
    return rel_bias[_t5_bucket(rel)].astype(F32).T


def _toeplitz(g, rows, cols):
    h = g.shape[0]
    period = rows + cols
    v = jnp.concatenate([g[:, rows - 1:], jnp.zeros((h, 1), g.dtype), g[:, :rows - 1]], axis=1)
    flat = jnp.tile(v, (1, rows))[:, :rows * (period - 1)]
    return flat.reshape(h, rows, period - 1)[:, :, :cols]


def _chunk_mask(bias, row_pos, col_pos, rows_are_keys):
    r = (row_pos // CHUNK)[:, None]
    c = (col_pos // CHUNK)[None, :]
    mask = (r <= c) if rows_are_keys else (c <= r)
    return jnp.where(mask[None], bias * LOG2E, NEG_INF)


def _prompt_layer(x, g_pre, w_in, b_f, lams, subln_g, w_out, g_post, rel_bias, layer_idx, *, t=256, tm=512,
                  tm_out=1024, fox_pairs=4, diff_heads=4):
    b, s, d = x.shape
    assert b == 1 and s % t == 0 and s // LANES >= 1
    x2d = x.reshape(s, d)
    (fk, fv, dk, dv, logf, gate, qt, k16, dk16, vt) = _in_projection(
        x2d, g_pre, w_in, b_f, transposed=True, tm=tm, tk=t)

    rows = s // LANES
    _, hi, mid, lo = _prefix_sums(logf.T.reshape(FOX_HEADS * rows, LANES), seg=LANES, group=rows,
                                  rows_per_step=FOX_HEADS * rows, scale=LOG2E)
    pieces = jnp.stack([hi, mid, lo], axis=0).reshape(PIECES, FOX_HEADS, s)
    kext = jnp.concatenate([
        (-pieces).transpose(2, 1, 0).reshape(s, ONES_AT), jnp.ones((s, PIECES), BF16),
        jnp.zeros((s, LANES - ONES_AT - PIECES), BF16)], axis=1)
    sel = (jnp.arange(ONES_AT)[None, :] // PIECES == jnp.arange(FOX_HEADS)[:, None]).astype(BF16)
    qext = jnp.concatenate([
        jnp.broadcast_to(sel[:, :, None], (FOX_HEADS, ONES_AT, s)), pieces.transpose(1, 0, 2),
        jnp.zeros((FOX_HEADS, QEXT_ROWS - ONES_AT - PIECES, s), BF16)], axis=1)

    mf = _fox_prompt_attention(qt, qext, k16, kext, vt, gate, t=t, pairs=fox_pairs)

    lam_init = 0.8 - 0.6 * math.exp(-0.3 * layer_idx)
    assert t >= MAX_DISTANCE
    pos = jnp.arange(t, dtype=jnp.int32)
    off = jnp.arange(-(t - 1), t, dtype=jnp.int32)
    far = _rel_bias_of(rel_bias, jnp.full((1,), -2 * t, jnp.int32))
    bias_diag = _chunk_mask(_toeplitz(_rel_bias_of(rel_bias, -off) - far, t, t), pos, pos, True)
    bias_sub = _chunk_mask(_toeplitz(_rel_bias_of(rel_bias, -off - t) - far, t, t), pos, pos + t, True)
    md = _diff_prompt_attention(qt, dk16, vt, bias_diag, bias_sub, gate, subln_g.reshape(1, -1), lams,
                                t=t, heads=diff_heads, lam_init=lam_init)

    y = _out_projection(mf, md, w_out, x2d, g_post, tm=tm_out)
    return y.reshape(b, s, d), (fk, fv, logf, dk, dv)


def _sample_layer(x, past, g_pre, w_in, b_f, lams, subln_g, w_out, g_post, rel_bias, layer_idx, *, tm=512,
                  fox_pairs=2, diff_chunk=2048):
    b, s, d = x.shape
    pk, pv, plogf, pdk, pdv = past
    p_len = pk.shape[1]
    assert s <= LANES and LANES % s == 0 and p_len % LANES == 0
    x2d = x.reshape(b * s, d)
    fk, fv, dk, dv, logf, gate, q16 = _in_projection(x2d, g_pre, w_in, b_f, transposed=False, tm=tm, tk=tm)

    rows = p_len // LANES
    cum_c, _, _, _ = _prefix_sums(plogf.transpose(0, 2, 1).reshape(b * FOX_HEADS * rows, LANES), seg=LANES,
                                  group=rows, rows_per_step=32 * rows, scale=1.0)
    cum_c = cum_c.reshape(b, FOX_HEADS, p_len)
    bias_c = ((cum_c[:, :, -1:] - cum_c) * LOG2E).reshape(b, FOX_HEADS // 2, 2, p_len)
    cum_n, _, _, _ = _prefix_sums(
        logf.reshape(b, s, FOX_HEADS).transpose(0, 2, 1).reshape(b * FOX_HEADS * s // LANES, LANES),
        seg=s, group=1, rows_per_step=1024, scale=1.0)
    bias_n = jnp.concatenate([-cum_n.reshape(b, FOX_HEADS, s) * LOG2E,
                              jnp.full((b, FOX_HEADS, LANES - s), NEG_INF, F32)], axis=-1)
    bias_n = bias_n.reshape(b, FOX_HEADS // 2, 2, LANES)
    mf = _fox_sample_attention(q16, pk.transpose(0, 2, 3, 1), pv.transpose(0, 2, 3, 1), fk, fv, bias_c, bias_n,
                               gate, s=s, pairs=fox_pairs)

    lam_init = 0.8 - 0.6 * math.exp(-0.3 * layer_idx)
    pc = min(diff_chunk, p_len)
    assert pc >= MAX_DISTANCE + s
    q_pos = p_len + jnp.arange(s, dtype=jnp.int32)
    far = _rel_bias_of(rel_bias, jnp.full((1,), -2 * p_len, jnp.int32))
    off = jnp.arange(-(s - 1), pc, dtype=jnp.int32)
    bd_c = _chunk_mask(_toeplitz(_rel_bias_of(rel_bias, off - pc) - far, s, pc), q_pos,
                       p_len - pc + jnp.arange(pc, dtype=jnp.int32), False)
    off = jnp.arange(-(s - 1), s, dtype=jnp.int32)
    bd_n = _chunk_mask(_toeplitz(_rel_bias_of(rel_bias, off) - far, s, s), q_pos, q_pos, False)
    bd_n = jnp.concatenate([bd_n, jnp.full((DIFF_HEADS, s, LANES - s), NEG_INF, F32)], axis=-1)
    md = _diff_sample_attention(q16, pdk.reshape(b, p_len * DIFF_HEADS, 2 * DIFF_HD),
                                pdv.reshape(b, p_len * DIFF_HEADS, 2 * DIFF_HD), dk, dv, bd_c, bd_n, gate,
                                subln_g.reshape(1, -1), lams, s=s, pc=pc, lam_init=lam_init)

    y = _out_projection(mf, md, w_out, x2d, g_post, tm=tm)
    return y.reshape(b, s, d), (fk, fv, logf, dk, dv)


def kernel(x_prompt, x_sample, cache_fox_k, cache_fox_v, cache_fox_logf, cache_diff_k, cache_diff_v,
           norm_pre_g, w_in, forget_bias, lambda_q1, lambda_k1, lambda_q2, lambda_k2, subln_g, w_out,
           norm_post_g, rel_bias):
    depth = w_in.shape[0]
    y_p, y_s = x_prompt, x_sample
    rows_p, rows_s = [], []
    for l in range(depth):
        lams = tuple(a[l].reshape(1, -1) for a in (lambda_q1, lambda_k1, lambda_q2, lambda_k2))
        params = (norm_pre_g[l], w_in[l], forget_bias[l], lams, subln_g[l], w_out[l], norm_post_g[l], rel_bias, l)
        y_p, new_p = _prompt_layer(y_p, *params)
        past = (cache_fox_k[l], cache_fox_v[l], cache_fox_logf[l], cache_diff_k[l], cache_diff_v[l])
        y_s, new_s = _sample_layer(y_s, past, *params)
        rows_p.append(new_p)
        rows_s.append(new_s)

    bp, sp = x_prompt.shape[:2]
    bs, ss = x_sample.shape[:2]

    def stack(rows, idx, shape):
        return jnp.stack([r[idx].reshape(shape) for r in rows], axis=0)

    outs = [y_p, y_s]
    for rows, (b, s) in ((rows_p, (bp, sp)), (rows_s, (bs, ss))):
        outs += [stack(rows, 0, (b, s, FOX_HEADS, FOX_HD)), stack(rows, 1, (b, s, FOX_HEADS, FOX_HD)),
                 stack(rows, 2, (b, s, FOX_HEADS)), stack(rows, 3, (b, s, DIFF_HEADS, 2 * DIFF_HD)),
                 stack(rows, 4, (b, s, DIFF_HEADS, 2 * DIFF_HD))]
    return tuple(outs)
```

```python
import functools
import math

import jax
import jax.numpy as jnp
import numpy as np
from jax import lax
from jax.experimental import pallas as pl
from jax.experimental.pallas import tpu as pltpu

F32 = jnp.float32
BF16 = jnp.bfloat16

FOX_HEADS = 8
FOX_HD = 64
DIFF_HEADS = 4
DIFF_HD = 64
FOX_W = FOX_HEADS * FOX_HD
DIFF_W = DIFF_HEADS * 2 * DIFF_HD
CHUNK = 64
NUM_BUCKETS = 32
MAX_DISTANCE = 128
EPS = 1e-6
NEG_INF = -1e30
LOG2E = 1.4426950408889634

LANES = 128
HEAD_PAIR_W = 2 * FOX_HD
VMEM_LIMIT_BYTES = 56 * 1024 * 1024

PIECES = 3
ONES_AT = PIECES * FOX_HEADS
QEXT_ROWS = 32


def _cparams(n_axes):
    return pltpu.CompilerParams(
        dimension_semantics=("arbitrary",) * n_axes,
        vmem_limit_bytes=VMEM_LIMIT_BYTES,
    )


def _split3(a):
    hi = a.astype(BF16)
    r1 = a - hi.astype(F32)
    mid = r1.astype(BF16)
    lo = (r1 - mid.astype(F32)).astype(BF16)
    return hi, mid, lo


def _nt_dot(a, b):
    return lax.dot_general(a, b, (((1,), (1,)), ((), ())), preferred_element_type=F32)


def _dot(a, b):
    return jnp.dot(a, b, preferred_element_type=F32)


def _inproj_body(x_ref, g_ref, wkv_ref, wg_ref, wff_ref, bf_ref, wq_ref, wvt_ref,
                 fk_ref, fv_ref, dk_ref, dv_ref, logf_ref, gate_ref, q_ref, *rest,
                 transposed, tk, qscale):
    x = x_ref[...]
    ms = jnp.mean(x * x, axis=-1, keepdims=True)
    h = (x * lax.rsqrt(ms + EPS)) * g_ref[...]
    hb = h.astype(BF16)

    fk = _dot(hb, wkv_ref[:, 0 * FOX_W:1 * FOX_W])
    fk_ref[...] = fk
    fv_ref[...] = _dot(hb, wkv_ref[:, 1 * FOX_W:2 * FOX_W])
    dk = _dot(hb, wkv_ref[:, 2 * FOX_W:3 * FOX_W])
    dv = _dot(hb, wkv_ref[:, 3 * FOX_W:4 * FOX_W])
    tm = dk.shape[0]
    for hd in range(DIFF_HEADS):
        cols = slice(hd * 2 * DIFF_HD, (hd + 1) * 2 * DIFF_HD)
        dk_ref[pl.ds(hd, tm, stride=DIFF_HEADS), :] = dk[:, cols]
        dv_ref[pl.ds(hd, tm, stride=DIFF_HEADS), :] = dv[:, cols]

    ff = _dot(hb, wff_ref[...])
    logf = jax.nn.log_sigmoid(ff + bf_ref[...])
    logf_ref[...] = logf[:, :FOX_HEADS]

    gates = _dot(hb, wg_ref[...])
    gate_ref[...] = (gates * jax.nn.sigmoid(gates)).astype(BF16)

    if transposed:
        k16_ref, dk16_ref, vt_ref = rest
        k16_ref[...] = fk.astype(BF16)
        dk16_ref[...] = dk.astype(BF16)
        q_ref[...] = (_nt_dot(wq_ref[...], hb) * qscale).astype(BF16)
        vt = _nt_dot(wvt_ref[...], hb).astype(BF16)
        for c in range(vt.shape[1] // tk):
            vt_ref[c] = vt[:, c * tk:(c + 1) * tk]
    else:
        q_ref[...] = (_dot(hb, wq_ref[...]) * qscale).astype(BF16)


def _in_projection(x2d, g_pre, w_in, b_f, *, transposed, tm, tk):
    n, d = x2d.shape
    tm = min(tm, n)
    assert n % tm == 0 and (not transposed or tm % tk == 0)
    qscale = FOX_HD ** -0.5 * LOG2E
    o = np.cumsum([0, FOX_W, FOX_W, FOX_W, FOX_HEADS, FOX_W, DIFF_W, DIFF_W, DIFF_W, DIFF_W])
    w_fq, w_fk, w_fv, w_ff, w_fg, w_dq, w_dk, w_dv, w_dg = [w_in[:, o[i]:o[i + 1]] for i in range(9)]
    wkv = jnp.concatenate([w_fk, w_fv, w_dk, w_dv], axis=1).astype(BF16)
    wg = jnp.concatenate([w_fg, w_dg], axis=1).astype(BF16)
    wff = jnp.pad(w_ff, ((0, 0), (0, LANES - FOX_HEADS))).astype(BF16)
    bfp = jnp.pad(b_f.reshape(1, FOX_HEADS), ((0, 0), (0, LANES - FOX_HEADS)))
    wq = jnp.concatenate([w_fq, w_dq], axis=1)
    wv = jnp.concatenate([w_fv, w_dv], axis=1)
    if transposed:
        wq = wq.T.astype(BF16)
        wvt = wv.T.astype(BF16)
    else:
        wq = wq.astype(BF16)
        wvt = jnp.zeros((8, LANES), BF16)

    const = lambda shape: pl.BlockSpec(shape, lambda i: (0,) * len(shape))
    rows = lambda w: pl.BlockSpec((tm, w), lambda i: (i, 0))
    in_specs = [rows(d), const((1, d)), const(wkv.shape), const(wg.shape), const(wff.shape),
                const(bfp.shape), const(wq.shape), const(wvt.shape)]
    by_head = jax.ShapeDtypeStruct((DIFF_HEADS * n, 2 * DIFF_HD), F32)
    by_head_spec = pl.BlockSpec((DIFF_HEADS * tm, 2 * DIFF_HD), lambda i: (i, 0))
    out_shape = [jax.ShapeDtypeStruct((n, FOX_W), F32)] * 2 + [by_head] * 2
    out_shape += [jax.ShapeDtypeStruct((n, FOX_HEADS), F32), jax.ShapeDtypeStruct((n, FOX_W + DIFF_W), BF16)]
    out_specs = [rows(FOX_W), rows(FOX_W), by_head_spec, by_head_spec, rows(FOX_HEADS), rows(FOX_W + DIFF_W)]
    if transposed:
        out_shape += [jax.ShapeDtypeStruct((FOX_W + DIFF_W, n), BF16),
                      jax.ShapeDtypeStruct((n, FOX_W), BF16), jax.ShapeDtypeStruct((n, DIFF_W), BF16),
                      jax.ShapeDtypeStruct((n // tk, FOX_W + DIFF_W, tk), BF16)]
        out_specs += [pl.BlockSpec((FOX_W + DIFF_W, tm), lambda i: (0, i)), rows(FOX_W), rows(DIFF_W),
                      pl.BlockSpec((tm // tk, FOX_W + DIFF_W, tk), lambda i: (i, 0, 0))]
    else:
        out_shape += [jax.ShapeDtypeStruct((n, FOX_W + DIFF_W), BF16)]
        out_specs += [rows(FOX_W + DIFF_W)]
    return pl.pallas_call(
        functools.partial(_inproj_body, transposed=transposed, tk=tk, qscale=qscale),
        grid=(n // tm,), in_specs=in_specs, out_specs=out_specs, out_shape=out_shape,
        compiler_params=_cparams(1), name="in_projection_t" if transposed else "in_projection_r",
    )(x2d, g_pre.reshape(1, d), wkv, wg, wff, bfp, wq, wvt)


def _cumsum_body(x_ref, cum_ref, hi_ref, mid_ref, lo_ref, *, seg, group, scale):
    x = x_ref[...]
    r = x.shape[0]
    a = lax.broadcasted_iota(jnp.int32, (LANES, LANES), 0)
    b = lax.broadcasted_iota(jnp.int32, (LANES, LANES), 1)
    sh = int(math.log2(seg))
    upper = ((a <= b) & ((a >> sh) == (b >> sh))).astype(BF16)
    pieces = _split3(x)
    w = sum(_dot(p, upper) for p in pieces)
    if group > 1:
        ones = jnp.ones((LANES, LANES), BF16)
        tot = sum(_dot(p, ones) for p in pieces)
        ri = lax.broadcasted_iota(jnp.int32, (r, r), 0)
        ci = lax.broadcasted_iota(jnp.int32, (r, r), 1)
        gsh = int(math.log2(group))
        lower = ((ci < ri) & ((ri >> gsh) == (ci >> gsh))).astype(BF16)
        w = w + sum(_dot(lower, p) for p in _split3(tot))
    cum_ref[...] = w
    hi, mid, lo = _split3(w * scale)
    hi_ref[...] = hi
    mid_ref[...] = mid
    lo_ref[...] = lo


def _prefix_sums(x2d, *, seg, group, rows_per_step, scale):
    r = x2d.shape[0]
    rb = min(rows_per_step, r)
    assert r % rb == 0 and rb % group == 0 and (seg == LANES or group == 1)
    assert seg & (seg - 1) == 0 and group & (group - 1) == 0
    spec = pl.BlockSpec((rb, LANES), lambda i: (i, 0))
    return pl.pallas_call(
        functools.partial(_cumsum_body, seg=seg, group=group, scale=scale),
        grid=(r // rb,), in_specs=[spec], out_specs=[spec] * 4,
        out_shape=[jax.ShapeDtypeStruct((r, LANES), F32)] + [jax.ShapeDtypeStruct((r, LANES), BF16)] * 3,
        compiler_params=_cparams(1), name="prefix_sums",
    )(x2d)


SUM_ROWS = 16
SCORES_AHEAD = 2
FAR_UNROLL = 8


def _with_sum_rows(vt):
    row = lax.broadcasted_iota(jnp.int32, (SUM_ROWS, vt.shape[1]), 0)
    return jnp.concatenate([vt, jnp.where(row == 0, 1.0, 0.0).astype(BF16)], axis=0)


def _store_scores(buf, h, s):
    buf[0][h] = s
    buf[1][h] = jnp.max(s, axis=0, keepdims=True)


def _online_update_t(s, tile_max, h, m_ref, acc_ref, vt_aug):
    m = m_ref[h]
    mn = jnp.maximum(m, tile_max)
    alpha = jnp.exp2(m - mn)
    p = jnp.exp2(s - mn)
    acc_ref[h] = alpha * acc_ref[h] + _dot(vt_aug, p.astype(BF16))
    m_ref[h] = mn


def _run_far_tiles(n, half_step, buf_a, buf_b):
    def unrolled(base, count):
        bufs = (buf_a, buf_b)
        for u in range(count):
            half_step(base + u, bufs[u % 2], bufs[(u + 1) % 2])

    def body(jj, c):
        unrolled(FAR_UNROLL * jj, FAR_UNROLL)
        return c

    lax.fori_loop(0, n // FAR_UNROLL, body, 0)
    for count in range(FAR_UNROLL - 2, 0, -2):
        @pl.when(n % FAR_UNROLL // 2 * 2 == count)
        def _():
            unrolled(n // FAR_UNROLL * FAR_UNROLL, count)


def _init_state(m_ref, l_ref, acc_ref):
    m_ref[...] = jnp.full(m_ref.shape, NEG_INF, F32)
    if l_ref is not None:
        l_ref[...] = jnp.zeros(l_ref.shape, F32)
    acc_ref[...] = jnp.zeros(acc_ref.shape, F32)


def _resident(shape, index_map):
    return pl.BlockSpec(shape, index_map, pipeline_mode=pl.Buffered(1))


def _fox_prompt_body(q_ref, qx_ref, qn_ref, qxn_ref, k_ref, kx_ref, vt_ref, gate_ref, o_ref, m_ref, acc_ref,
                     sa_ref, sb_ref, ma_ref, mb_ref, *, t, pairs):
    i = pl.program_id(1)
    heads = 2 * pairs
    buf_a, buf_b = (sa_ref, ma_ref), (sb_ref, mb_ref)
    _init_state(m_ref, None, acc_ref)
    z64 = jnp.zeros((FOX_HD, t), BF16)
    zpad = jnp.zeros((LANES - QEXT_ROWS, t), BF16)

    def augmented_queries(q_blk_ref, qx_blk_ref):
        out = []
        for p in range(pairs):
            q = q_blk_ref[p * HEAD_PAIR_W:(p + 1) * HEAD_PAIR_W, :]
            out.append(jnp.concatenate([q[:FOX_HD], z64, qx_blk_ref[2 * p], zpad], axis=0))
            out.append(jnp.concatenate([z64, q[FOX_HD:], qx_blk_ref[2 * p + 1], zpad], axis=0))
        return out

    qa = augmented_queries(q_ref, qx_ref)
    qa_next = augmented_queries(qn_ref, qxn_ref)

    def keys(j, p):
        off = pl.multiple_of(j * t, t)
        return jnp.concatenate([k_ref[pl.ds(off, t), p * HEAD_PAIR_W:(p + 1) * HEAD_PAIR_W],
                                kx_ref[pl.ds(off, t), :]], axis=1)

    def consume(j, cur, h, diagonal, between=None):
        s = cur[0][h]
        if between is not None:
            between()
        if diagonal:
            kk = lax.broadcasted_iota(jnp.int32, (t, t), 0)
            qq = lax.broadcasted_iota(jnp.int32, (t, t), 1)
            s = jnp.where(kk <= qq, s, NEG_INF)
            tile_max = jnp.max(s, axis=0, keepdims=True)
        else:
            tile_max = cur[1][h]
        _online_update_t(s, tile_max, h, m_ref, acc_ref,
                         _with_sum_rows(vt_ref[j, h * FOX_HD:(h + 1) * FOX_HD, :]))

    def half_step(j, cur, nxt):
        ka = {}

        def prefetch(h):
            if h // 2 not in ka:
                ka[h // 2] = keys(j + 1, h // 2)
            _store_scores(nxt, h, _dot(ka[h // 2], qa[h]))

        for h in range(min(SCORES_AHEAD, heads)):
            prefetch(h)
        for h in range(heads):
            consume(j, cur, h, False)
            if h + SCORES_AHEAD < heads:
                prefetch(h + SCORES_AHEAD)

    def first_tile_scores(h, queries):
        _store_scores(buf_a, h, _dot(keys(0, h // 2), queries[h]))

    def last_step(j, cur):
        for h in range(heads):
            consume(j, cur, h, True, between=functools.partial(first_tile_scores, h, qa_next))

    @pl.when(i == 0)
    def _():
        for h in range(heads):
            first_tile_scores(h, qa)

    _run_far_tiles(i, half_step, buf_a, buf_b)

    @pl.when(i % 2 == 0)
    def _():
        last_step(i, buf_a)

    @pl.when(i % 2 == 1)
    def _():
        half_step(i - 1, buf_a, buf_b)
        last_step(i, buf_b)

    for p in range(pairs):
        a0, a1 = acc_ref[2 * p], acc_ref[2 * p + 1]
        ot = jnp.concatenate([a0[:FOX_HD] / a0[FOX_HD:FOX_HD + 1], a1[:FOX_HD] / a1[FOX_HD:FOX_HD + 1]], axis=0)
        cols = slice(p * HEAD_PAIR_W, (p + 1) * HEAD_PAIR_W)
        o_ref[:, cols] = (ot.T * gate_ref[:, cols].astype(F32)).astype(BF16)


def _fox_prompt_attention(qt, qext, k16, kext, vt, gate, *, t, pairs):
    n = k16.shape[0]
    nq = n // t
    w = pairs * HEAD_PAIR_W
    heads = 2 * pairs
    assert (FOX_HEADS // 2) % pairs == 0
    return pl.pallas_call(
        functools.partial(_fox_prompt_body, t=t, pairs=pairs),
        grid=(FOX_HEADS // heads, nq),
        in_specs=[pl.BlockSpec((w, t), lambda g, i: (g, i)),
                  pl.BlockSpec((heads, QEXT_ROWS, t), lambda g, i: (g, 0, i)),
                  pl.BlockSpec((w, t), lambda g, i: (g, jnp.minimum(i + 1, nq - 1))),
                  pl.BlockSpec((heads, QEXT_ROWS, t), lambda g, i: (g, 0, jnp.minimum(i + 1, nq - 1))),
                  _resident((n, w), lambda g, i: (0, g)),
                  _resident((n, LANES), lambda g, i: (0, 0)),
                  _resident((nq, w, t), lambda g, i: (0, g, 0)),
                  pl.BlockSpec((t, w), lambda g, i: (i, g))],
        out_specs=pl.BlockSpec((t, w), lambda g, i: (i, g)),
        out_shape=jax.ShapeDtypeStruct((n, FOX_W), BF16),
        scratch_shapes=[pltpu.VMEM((heads, 1, t), F32), pltpu.VMEM((heads, FOX_HD + SUM_ROWS, t), F32),
                        pltpu.VMEM((heads, t, t), F32), pltpu.VMEM((heads, t, t), F32),
                        pltpu.VMEM((heads, 1, t), F32), pltpu.VMEM((heads, 1, t), F32)],
        compiler_params=_cparams(2), name="fox_prompt_attention",
    )(qt, qext, qt, qext, k16, kext, vt, gate)


def _lambda_value(lq1, lk1, lq2, lk2, lam_init):
    a = jnp.sum(lq1 * lk1, axis=-1, keepdims=True)
    b = jnp.sum(lq2 * lk2, axis=-1, keepdims=True)
    return jnp.exp(a) - jnp.exp(b) + lam_init


def _subln_gate(y, g, gate, lam_init):
    ms = jnp.mean(y * y, axis=-1, keepdims=True)
    yn = (y * lax.rsqrt(ms + EPS)) * g
    return ((yn * (1.0 - lam_init)) * gate.astype(F32)).astype(BF16)


def _diff_prompt_body(q_ref, qn_ref, k_ref, vt_ref, bd_ref, bs_ref, gate_ref, sg_ref, lq1_ref, lk1_ref, lq2_ref,
                      lk2_ref, o_ref, m_ref, acc_ref, sa_ref, sb_ref, ma_ref, mb_ref, *, t, heads, lam_init):
    i = pl.program_id(1)
    w = 2 * DIFF_HD
    buf_a, buf_b = (sa_ref, ma_ref), (sb_ref, mb_ref)
    _init_state(m_ref, None, acc_ref)
    z64 = jnp.zeros((DIFF_HD, t), BF16)

    def padded_queries(q_blk_ref):
        out = []
        for h in range(heads):
            q = q_blk_ref[h * w:(h + 1) * w, :]
            out.append(jnp.concatenate([q[:DIFF_HD], z64], axis=0))
            out.append(jnp.concatenate([z64, q[DIFF_HD:]], axis=0))
        return out

    qa = padded_queries(q_ref)
    qa_next = padded_queries(qn_ref)

    def keys(j, h):
        return k_ref[pl.ds(pl.multiple_of(j * t, t), t), h * w:(h + 1) * w]

    def consume(j, cur, c, bias_ref, between=None):
        h = c // 2
        s = cur[0][c]
        if between is not None:
            between()
        if bias_ref is not None:
            s = s + bias_ref[h]
            tile_max = jnp.max(s, axis=0, keepdims=True)
        else:
            tile_max = cur[1][c]
        _online_update_t(s, tile_max, c, m_ref, acc_ref, _with_sum_rows(vt_ref[j, h * w:(h + 1) * w, :]))

    def half_step(j, cur, nxt, bias_ref=None):
        kk = {}

        def prefetch(c):
            if c // 2 not in kk:
                kk[c // 2] = keys(j + 1, c // 2)
            _store_scores(nxt, c, _dot(kk[c // 2], qa[c]))

        for c in range(min(SCORES_AHEAD, 2 * heads)):
            prefetch(c)
        for c in range(2 * heads):
            consume(j, cur, c, bias_ref)
            if c + SCORES_AHEAD < 2 * heads:
                prefetch(c + SCORES_AHEAD)

    def first_tile_scores(c, queries):
        _store_scores(buf_a, c, _dot(keys(0, c // 2), queries[c]))

    def last_step(j, cur):
        for c in range(2 * heads):
            consume(j, cur, c, bd_ref, between=functools.partial(first_tile_scores, c, qa_next))

    @pl.when(i == 0)
    def _():
        for c in range(2 * heads):
            first_tile_scores(c, qa)

    _run_far_tiles(jnp.maximum(i - 1, 0), half_step, buf_a, buf_b)

    @pl.when(i == 0)
    def _():
        last_step(i, buf_a)

    @pl.when(i % 2 == 1)
    def _():
        half_step(i - 1, buf_a, buf_b, bs_ref)
        last_step(i, buf_b)

    @pl.when((i >= 2) & (i % 2 == 0))
    def _():
        half_step(i - 2, buf_a, buf_b)
        half_step(i - 1, buf_b, buf_a, bs_ref)
        last_step(i, buf_a)

    lam = _lambda_value(lq1_ref[...], lk1_ref[...], lq2_ref[...], lk2_ref[...], lam_init)
    for h in range(heads):
        a0, a1 = acc_ref[2 * h], acc_ref[2 * h + 1]
        ot = a0[:w] / a0[w:w + 1] - lam * (a1[:w] / a1[w:w + 1])
        cols = slice(h * w, (h + 1) * w)
        o_ref[:, cols] = _subln_gate(ot.T, sg_ref[...], gate_ref[:, cols], lam_init)


def _diff_prompt_attention(qt, dk16, vt, bias_diag, bias_sub, gate, subln_g, lams, *, t, heads, lam_init):
    n = dk16.shape[0]
    nq = n // t
    w = heads * 2 * DIFF_HD
    assert DIFF_HEADS % heads == 0 and FOX_W % w == 0
    fox_blocks = FOX_W // w
    vec = lambda width: pl.BlockSpec((1, width), lambda g, i: (0, 0))
    return pl.pallas_call(
        functools.partial(_diff_prompt_body, t=t, heads=heads, lam_init=lam_init),
        grid=(DIFF_HEADS // heads, nq),
        in_specs=[pl.BlockSpec((w, t), lambda g, i: (fox_blocks + g, i)),
                  pl.BlockSpec((w, t), lambda g, i: (fox_blocks + g, jnp.minimum(i + 1, nq - 1))),
                  _resident((n, w), lambda g, i: (0, g)),
                  _resident((nq, w, t), lambda g, i: (0, fox_blocks + g, 0)),
                  pl.BlockSpec((heads, t, t), lambda g, i: (g, 0, 0)),
                  pl.BlockSpec((heads, t, t), lambda g, i: (g, 0, 0)),
                  pl.BlockSpec((t, w), lambda g, i: (i, fox_blocks + g)),
                  vec(2 * DIFF_HD), vec(DIFF_HD), vec(DIFF_HD), vec(DIFF_HD), vec(DIFF_HD)],
        out_specs=pl.BlockSpec((t, w), lambda g, i: (i, g)),
        out_shape=jax.ShapeDtypeStruct((n, DIFF_W), BF16),
        scratch_shapes=[pltpu.VMEM((2 * heads, 1, t), F32),
                        pltpu.VMEM((2 * heads, 2 * DIFF_HD + SUM_ROWS, t), F32),
                        pltpu.VMEM((2 * heads, t, t), F32), pltpu.VMEM((2 * heads, t, t), F32),
                        pltpu.VMEM((2 * heads, 1, t), F32), pltpu.VMEM((2 * heads, 1, t), F32)],
        compiler_params=_cparams(2), name="diff_prompt_attention",
    )(qt, qt, dk16, vt, bias_diag, bias_sub, gate, subln_g, *lams)


def _block_diag_queries(qb):
    lane = lax.broadcasted_iota(jnp.int32, qb.shape, 1)
    zero = jnp.zeros_like(qb)
    return jnp.concatenate([jnp.where(lane < FOX_HD, qb, zero), jnp.where(lane >= FOX_HD, qb, zero)], axis=0)


def _fox_sample_body(q_ref, kc_ref, vc_ref, kn_ref, vn_ref, bc_ref, bn_ref, gate_ref, o_ref, *, s, pairs):
    p_len = kc_ref.shape[-1]
    zrows = jnp.zeros((LANES - s, HEAD_PAIR_W), BF16)
    qi = lax.broadcasted_iota(jnp.int32, (2 * s, LANES), 0)
    ki = lax.broadcasted_iota(jnp.int32, (2 * s, LANES), 1)
    causal = ki <= jnp.where(qi >= s, qi - s, qi)
    lane = lax.broadcasted_iota(jnp.int32, (s, HEAD_PAIR_W), 1)

    def rows_bias(b, width):
        return jnp.concatenate([jnp.broadcast_to(b[0:1], (s, width)), jnp.broadcast_to(b[1:2], (s, width))], axis=0)

    scores = []
    for p in range(pairs):
        cols = slice(p * HEAD_PAIR_W, (p + 1) * HEAD_PAIR_W)
        qbd = _block_diag_queries(q_ref[:, cols])
        kt = kc_ref[0, 2 * p:2 * p + 2].reshape(HEAD_PAIR_W, p_len).astype(BF16)
        kn = jnp.concatenate([kn_ref[:, cols].astype(BF16), zrows], axis=0)
        sc = _dot(qbd, kt) + rows_bias(bc_ref[0, p], p_len)
        sn = _nt_dot(qbd, kn) + rows_bias(bn_ref[0, p], LANES)
        scores.append((sc, jnp.where(causal, sn, NEG_INF)))
    for p, (sc, sn) in enumerate(scores):
        cols = slice(p * HEAD_PAIR_W, (p + 1) * HEAD_PAIR_W)
        vt = vc_ref[0, 2 * p:2 * p + 2].reshape(HEAD_PAIR_W, p_len).astype(BF16)
        vn = jnp.concatenate([vn_ref[:, cols].astype(BF16), zrows], axis=0)
        m = jnp.maximum(jnp.max(sc, axis=1, keepdims=True), jnp.max(sn, axis=1, keepdims=True))
        pc = jnp.exp2(sc - m)
        pn = jnp.exp2(sn - m)
        l = jnp.sum(pc, axis=1, keepdims=True) + jnp.sum(pn, axis=1, keepdims=True)
        o = (_nt_dot(pc.astype(BF16), vt) + _dot(pn.astype(BF16), vn)) / l
        out = jnp.where(lane < FOX_HD, o[:s], o[s:])
        o_ref[:, cols] = (out * gate_ref[:, cols].astype(F32)).astype(BF16)


def _fox_sample_attention(q16, cache_kt, cache_vt, fk32, fv32, bias_c, bias_n, gate, *, s, pairs):
    b, _, _, p_len = cache_kt.shape
    groups = FOX_HEADS // (2 * pairs)
    w = pairs * HEAD_PAIR_W
    blk = lambda: pl.BlockSpec((s, w), lambda bi, g: (bi, g))
    cache = lambda: pl.BlockSpec((1, 2 * pairs, FOX_HD, p_len), lambda bi, g: (bi, g, 0, 0))
    return pl.pallas_call(
        functools.partial(_fox_sample_body, s=s, pairs=pairs),
        grid=(b, groups),
        in_specs=[blk(), cache(), cache(), blk(), blk(),
                  pl.BlockSpec((1, pairs, 2, p_len), lambda bi, g: (bi, g, 0, 0)),
                  pl.BlockSpec((1, pairs, 2, LANES), lambda bi, g: (bi, g, 0, 0)),
                  blk()],
        out_specs=blk(),
        out_shape=jax.ShapeDtypeStruct((b * s, FOX_W), BF16),
        compiler_params=_cparams(2), name="fox_sample_attention",
    )(q16, cache_kt, cache_vt, fk32, fv32, bias_c, bias_n, gate)


def _diff_sample_body(q_ref, kc_ref, vc_ref, kn_ref, vn_ref, bc_ref, bn_ref, gate_ref, sg_ref, lq1_ref,
                      lk1_ref, lq2_ref, lk2_ref, o_ref, m_ref, l_ref, acc_ref, *, s, pc, lam_init):
    c = pl.program_id(1)
    last = pl.num_programs(1) - 1
    w = 2 * DIFF_HD

    @pl.when(c == 0)
    def _():
        _init_state(m_ref, l_ref, acc_ref)

    qbd = [_block_diag_queries(q_ref[:, h * w:(h + 1) * w]) for h in range(DIFF_HEADS)]

    def update(h, sc, v16):
        m = m_ref[h]
        mn = jnp.maximum(m, jnp.max(sc, axis=1, keepdims=True))
        alpha = jnp.exp2(m - mn)
        p = jnp.exp2(sc - mn)
        l_ref[h] = alpha * l_ref[h] + jnp.sum(p, axis=1, keepdims=True)
        acc_ref[h] = alpha * acc_ref[h] + _dot(p.astype(BF16), v16)
        m_ref[h] = mn

    def chunk(with_bias):
        scores = [_nt_dot(qbd[h], kc_ref[0, pl.ds(h, pc, stride=DIFF_HEADS), :].astype(BF16))
                  for h in range(DIFF_HEADS)]
        for h, sc in enumerate(scores):
            if with_bias:
                sc = sc + jnp.concatenate([bc_ref[h], bc_ref[h]], axis=0)
            update(h, sc, vc_ref[0, pl.ds(h, pc, stride=DIFF_HEADS), :].astype(BF16))

    @pl.when(c != last)
    def _():
        chunk(False)

    @pl.when(c == last)
    def _():
        chunk(True)
        lam = _lambda_value(lq1_ref[...], lk1_ref[...], lq2_ref[...], lk2_ref[...], lam_init)
        zrows = jnp.zeros((LANES - s, w), BF16)
        for h in range(DIFF_HEADS):
            cols = slice(h * w, (h + 1) * w)
            new_rows = pl.ds(h, s, stride=DIFF_HEADS)
            kn = jnp.concatenate([kn_ref[new_rows, :].astype(BF16), zrows], axis=0)
            vn = jnp.concatenate([vn_ref[new_rows, :].astype(BF16), zrows], axis=0)
            sn = _nt_dot(qbd[h], kn) + jnp.concatenate([bn_ref[h], bn_ref[h]], axis=0)
            update(h, sn, vn)
            o = acc_ref[h] / l_ref[h]
            o = o[:s] - lam * o[s:]
            o_ref[:, cols] = _subln_gate(o, sg_ref[...], gate_ref[:, cols], lam_init)


def _diff_sample_attention(q16, cache_k, cache_v, dk32, dv32, bias_c, bias_n, gate, subln_g, lams, *, s, pc,
                           lam_init):
    b, rows, w = cache_k.shape
    p_len = rows // DIFF_HEADS
    assert p_len % pc == 0 and bias_c.shape == (DIFF_HEADS, s, pc)
    cache = lambda: pl.BlockSpec((1, DIFF_HEADS * pc, w), lambda bi, c: (bi, c, 0))
    out = lambda: pl.BlockSpec((s, DIFF_W), lambda bi, c: (bi, 0))
    new = lambda: pl.BlockSpec((DIFF_HEADS * s, w), lambda bi, c: (bi, 0))
    half = lambda: pl.BlockSpec((s, DIFF_W), lambda bi, c: (bi, 1))
    const = lambda shape: pl.BlockSpec(shape, lambda bi, c: (0,) * len(shape))
    return pl.pallas_call(
        functools.partial(_diff_sample_body, s=s, pc=pc, lam_init=lam_init),
        grid=(b, p_len // pc),
        in_specs=[half(), cache(), cache(), new(), new(), const(bias_c.shape), const(bias_n.shape), half(),
                  const((1, w)), const((1, DIFF_HD)), const((1, DIFF_HD)), const((1, DIFF_HD)), const((1, DIFF_HD))],
        out_specs=out(),
        out_shape=jax.ShapeDtypeStruct((b * s, DIFF_W), BF16),
        scratch_shapes=[pltpu.VMEM((DIFF_HEADS, 2 * s, 1), F32), pltpu.VMEM((DIFF_HEADS, 2 * s, 1), F32),
                        pltpu.VMEM((DIFF_HEADS, 2 * s, w), F32)],
        compiler_params=_cparams(2), name="diff_sample_attention",
    )(q16, cache_k, cache_v, dk32, dv32, bias_c, bias_n, gate, subln_g, *lams)


def _outproj_body(mf_ref, md_ref, w_ref, x_ref, g_ref, y_ref):
    out = _dot(mf_ref[...], w_ref[:FOX_W, :]) + _dot(md_ref[...], w_ref[FOX_W:, :])
    ms = jnp.mean(out * out, axis=-1, keepdims=True)
    y_ref[...] = x_ref[...] + (out * lax.rsqrt(ms + EPS)) * g_ref[...]


def _out_projection(mf, md, w_out, x2d, g_post, *, tm):
    n, d = x2d.shape
    tm = min(tm, n)
    assert n % tm == 0
    rows = lambda w: pl.BlockSpec((tm, w), lambda i: (i, 0))
    return pl.pallas_call(
        _outproj_body, grid=(n // tm,),
        in_specs=[rows(FOX_W), rows(DIFF_W), pl.BlockSpec(w_out.shape, lambda i: (0, 0)), rows(d),
                  pl.BlockSpec((1, d), lambda i: (0, 0))],
        out_specs=rows(d), out_shape=jax.ShapeDtypeStruct((n, d), F32),
        compiler_params=_cparams(1), name="out_projection",
    )(mf, md, w_out.astype(BF16), x2d, g_post.reshape(1, d))


def _t5_bucket(rel):
    half = NUM_BUCKETS // 2
    max_exact = half // 2
    ret = jnp.where(rel > 0, half, 0)
    n = jnp.abs(rel)
    nf = jnp.maximum(n, 1).astype(jnp.float32)
    large = max_exact + (jnp.log(nf / max_exact) / math.log(MAX_DISTANCE / max_exact)
                         * (half - max_exact)).astype(jnp.int32)
    large = jnp.minimum(large, half - 1)
    return ret + jnp.where(n < max_exact, n, large)


def _rel_bias_of(rel_bias, rel):
    return rel_bias[_t5_bucket(rel)].astype(F32).T


def _toeplitz(g, rows, cols):
    h = g.shape[0]
    period = rows + cols
    v = jnp.concatenate([g[:, rows - 1:], jnp.zeros((h, 1), g.dtype), g[:, :rows - 1]], axis=1)
    flat = jnp.tile(v, (1, rows))[:, :rows * (period - 1)]
    return flat.reshape(h, rows, period - 1)[:, :, :cols]


def _chunk_mask(bias, row_pos, col_pos, rows_are_keys):
    r = (row_pos // CHUNK)[:, None]
    c = (col_pos // CHUNK)[None, :]
    mask = (r <= c) if rows_are_keys else (c <= r)
    return jnp.where(mask[None], bias * LOG2E, NEG_INF)


def _prompt_layer(x, g_pre, w_in, b_f, lams, subln_g, w_out, g_post, rel_bias, layer_idx, *, t=256, tm=512,
                  tm_out=1024, fox_pairs=4, diff_heads=4):
    b, s, d = x.shape
    assert b == 1 and s % t == 0 and s // LANES >= 1
    x2d = x.reshape(s, d)
    (fk, fv, dk, dv, logf, gate, qt, k16, dk16, vt) = _in_projection(
        x2d, g_pre, w_in, b_f, transposed=True, tm=tm, tk=t)

    rows = s // LANES
    _, hi, mid, lo = _prefix_sums(logf.T.reshape(FOX_HEADS * rows, LANES), seg=LANES, group=rows,
                                  rows_per_step=FOX_HEADS * rows, scale=LOG2E)
    pieces = jnp.stack([hi, mid, lo], axis=0).reshape(PIECES, FOX_HEADS, s)
    kext = jnp.concatenate([
        (-pieces).transpose(2, 1, 0).reshape(s, ONES_AT), jnp.ones((s, PIECES), BF16),
        jnp.zeros((s, LANES - ONES_AT - PIECES), BF16)], axis=1)
    sel = (jnp.arange(ONES_AT)[None, :] // PIECES == jnp.arange(FOX_HEADS)[:, None]).astype(BF16)
    qext = jnp.concatenate([
        jnp.broadcast_to(sel[:, :, None], (FOX_HEADS, ONES_AT, s)), pieces.transpose(1, 0, 2),
        jnp.zeros((FOX_HEADS, QEXT_ROWS - ONES_AT - PIECES, s), BF16)], axis=1)

    mf = _fox_prompt_attention(qt, qext, k16, kext, vt, gate, t=t, pairs=fox_pairs)

    lam_init = 0.8 - 0.6 * math.exp(-0.3 * layer_idx)
    assert t >= MAX_DISTANCE
    pos = jnp.arange(t, dtype=jnp.int32)
    off = jnp.arange(-(t - 1), t, dtype=jnp.int32)
    far = _rel_bias_of(rel_bias, jnp.full((1,), -2 * t, jnp.int32))
    bias_diag = _chunk_mask(_toeplitz(_rel_bias_of(rel_bias, -off) - far, t, t), pos, pos, True)
    bias_sub = _chunk_mask(_toeplitz(_rel_bias_of(rel_bias, -off - t) - far, t, t), pos, pos + t, True)
    md = _diff_prompt_attention(qt, dk16, vt, bias_diag, bias_sub, gate, subln_g.reshape(1, -1), lams,
                                t=t, heads=diff_heads, lam_init=lam_init)

    y = _out_projection(mf, md, w_out, x2d, g_post, tm=tm_out)
    return y.reshape(b, s, d), (fk, fv, logf, dk, dv)


def _sample_layer(x, past, g_pre, w_in, b_f, lams, subln_g, w_out, g_post, rel_bias, layer_idx, *, tm=512,
                  fox_pairs=2, diff_chunk=2048):
    b, s, d = x.shape
    pk, pv, plogf, pdk, pdv = past
    p_len = pk.shape[1]
    assert s <= LANES and LANES % s == 0 and p_len % LANES == 0
    x2d = x.reshape(b * s, d)
    fk, fv, dk, dv, logf, gate, q16 = _in_projection(x2d, g_pre, w_in, b_f, transposed=False, tm=tm, tk=tm)

    rows = p_len // LANES
    cum_c, _, _, _ = _prefix_sums(plogf.transpose(0, 2, 1).reshape(b * FOX_HEADS * rows, LANES), seg=LANES,
                                  group=rows, rows_per_step=32 * rows, scale=1.0)
    cum_c = cum_c.reshape(b, FOX_HEADS, p_len)
    bias_c = ((cum_c[:, :, -1:] - cum_c) * LOG2E).reshape(b, FOX_HEADS // 2, 2, p_len)
    cum_n, _, _, _ = _prefix_sums(
        logf.reshape(b, s, FOX_HEADS).transpose(0, 2, 1).reshape(b * FOX_HEADS * s // LANES, LANES),
        seg=s, group=1, rows_per_step=1024, scale=1.0)
    bias_n = jnp.concatenate([-cum_n.reshape(b, FOX_HEADS, s) * LOG2E,
                              jnp.full((b, FOX_HEADS, LANES - s), NEG_INF, F32)], axis=-1)
    bias_n = bias_n.reshape(b, FOX_HEADS // 2, 2, LANES)
    mf = _fox_sample_attention(q16, pk.transpose(0, 2, 3, 1), pv.transpose(0, 2, 3, 1), fk, fv, bias_c, bias_n,
                               gate, s=s, pairs=fox_pairs)

    lam_init = 0.8 - 0.6 * math.exp(-0.3 * layer_idx)
    pc = min(diff_chunk, p_len)
    assert pc >= MAX_DISTANCE + s
    q_pos = p_len + jnp.arange(s, dtype=jnp.int32)
    far = _rel_bias_of(rel_bias, jnp.full((1,), -2 * p_len, jnp.int32))
    off = jnp.arange(-(s - 1), pc, dtype=jnp.int32)
    bd_c = _chunk_mask(_toeplitz(_rel_bias_of(rel_bias, off - pc) - far, s, pc), q_pos,
                       p_len - pc + jnp.arange(pc, dtype=jnp.int32), False)
    off = jnp.arange(-(s - 1), s, dtype=jnp.int32)
    bd_n = _chunk_mask(_toeplitz(_rel_bias_of(rel_bias, off) - far, s, s), q_pos, q_pos, False)
    bd_n = jnp.concatenate([bd_n, jnp.full((DIFF_HEADS, s, LANES - s), NEG_INF, F32)], axis=-1)
    md = _diff_sample_attention(q16, pdk.reshape(b, p_len * DIFF_HEADS, 2 * DIFF_HD),
                                pdv.reshape(b, p_len * DIFF_HEADS, 2 * DIFF_HD), dk, dv, bd_c, bd_n, gate,
                                subln_g.reshape(1, -1), lams, s=s, pc=pc, lam_init=lam_init)

    y = _out_projection(mf, md, w_out, x2d, g_post, tm=tm)
    return y.reshape(b, s, d), (fk, fv, logf, dk, dv)


def kernel(x_prompt, x_sample, cache_fox_k, cache_fox_v, cache_fox_logf, cache_diff_k, cache_diff_v,
           norm_pre_g, w_in, forget_bias, lambda_q1, lambda_k1, lambda_q2, lambda_k2, subln_g, w_out,
           norm_post_g, rel_bias):
    depth = w_in.shape[0]
    y_p, y_s = x_prompt, x_sample
    rows_p, rows_s = [], []
    for l in range(depth):
        lams = tuple(a[l].reshape(1, -1) for a in (lambda_q1, lambda_k1, lambda_q2, lambda_k2))
        params = (norm_pre_g[l], w_in[l], forget_bias[l], lams, subln_g[l], w_out[l], norm_post_g[l], rel_bias, l)
        y_p, new_p = _prompt_layer(y_p, *params)
        past = (cache_fox_k[l], cache_fox_v[l], cache_fox_logf[l], cache_diff_k[l], cache_diff_v[l])
        y_s, new_s = _sample_layer(y_s, past, *params)
        rows_p.append(new_p)
        rows_s.append(new_s)

    bp, sp = x_prompt.shape[:2]
    bs, ss = x_sample.shape[:2]

    def stack(rows, idx, shape):
        return jnp.stack([r[idx].reshape(shape) for r in rows], axis=0)

    outs = [y_p, y_s]
    for rows, (b, s) in ((rows_p, (bp, sp)), (rows_s, (bs, ss))):
        outs += [stack(rows, 0, (b, s, FOX_HEADS, FOX_HD)), stack(rows, 1, (b, s, FOX_HEADS, FOX_HD)),
                 stack(rows, 2, (b, s, FOX_HEADS)), stack(rows, 3, (b, s, DIFF_HEADS, 2 * DIFF_HD)),
                 stack(rows, 4, (b, s, DIFF_HEADS, 2 * DIFF_HD))]
    return tuple(outs)
```

```python
import functools
import math

import jax
import jax.numpy as jnp
import numpy as np
from jax import lax
from jax.experimental import pallas as pl
from jax.experimental.pallas import tpu as pltpu

F32 = jnp.float32
BF16 = jnp.bfloat16

FOX_HEADS = 8
FOX_HD = 64
DIFF_HEADS = 4
DIFF_HD = 64
FOX_W = FOX_HEADS * FOX_HD
DIFF_W = DIFF_HEADS * 2 * DIFF_HD
CHUNK = 64
NUM_BUCKETS = 32
MAX_DISTANCE = 128
EPS = 1e-6
NEG_INF = -1e30
LOG2E = 1.4426950408889634

LANES = 128
HEAD_PAIR_W = 2 * FOX_HD
VMEM_LIMIT_BYTES = 56 * 1024 * 1024

PIECES = 3
ONES_AT = PIECES * FOX_HEADS
QEXT_ROWS = 32


def _cparams(n_axes):
    return pltpu.CompilerParams(
        dimension_semantics=("arbitrary",) * n_axes,
        vmem_limit_bytes=VMEM_LIMIT_BYTES,
    )


def _split3(a):
    hi = a.astype(BF16)
    r1 = a - hi.astype(F32)
    mid = r1.astype(BF16)
    lo = (r1 - mid.astype(F32)).astype(BF16)
    return hi, mid, lo


def _nt_dot(a, b):
    return lax.dot_general(a, b, (((1,), (1,)), ((), ())), preferred_element_type=F32)


def _dot(a, b):
    return jnp.dot(a, b, preferred_element_type=F32)


def _inproj_body(x_ref, g_ref, wkv_ref, wg_ref, wff_ref, bf_ref, wq_ref,
                 fk_ref, fv_ref, dk_ref, dv_ref, logf_ref, gate_ref, q_ref, *rest,
                 transposed, tk, qscale):
    x = x_ref[...]
    ms = jnp.mean(x * x, axis=-1, keepdims=True)
    h = (x * lax.rsqrt(ms + EPS)) * g_ref[...]
    hb = h.astype(BF16)

    fk = _dot(hb, wkv_ref[:, 0 * FOX_W:1 * FOX_W])
    fk_ref[...] = fk
    fv = _dot(hb, wkv_ref[:, 1 * FOX_W:2 * FOX_W])
    fv_ref[...] = fv
    dk = _dot(hb, wkv_ref[:, 2 * FOX_W:3 * FOX_W])
    dv = _dot(hb, wkv_ref[:, 3 * FOX_W:4 * FOX_W])
    tm = dk.shape[0]
    for hd in range(DIFF_HEADS):
        cols = slice(hd * 2 * DIFF_HD, (hd + 1) * 2 * DIFF_HD)
        dk_ref[pl.ds(hd, tm, stride=DIFF_HEADS), :] = dk[:, cols]
        dv_ref[pl.ds(hd, tm, stride=DIFF_HEADS), :] = dv[:, cols]

    ff = _dot(hb, wff_ref[...])
    logf = jax.nn.log_sigmoid(ff + bf_ref[...])
    logf_ref[...] = logf[:, :FOX_HEADS]

    gates = _dot(hb, wg_ref[...])
    gate_ref[...] = (gates * jax.nn.sigmoid(gates)).astype(BF16)

    q = _dot(hb, wq_ref[...]) * qscale
    if transposed:
        k16_ref, dk16_ref, vt_ref = rest
        k16_ref[...] = fk.astype(BF16)
        dk16_ref[...] = dk.astype(BF16)
        q_ref[...] = q.T.astype(BF16)
        vt = jnp.concatenate([fv, dv], axis=1).T.astype(BF16)
        for c in range(vt.shape[1] // tk):
            vt_ref[c] = vt[:, c * tk:(c + 1) * tk]
    else:
        q_ref[...] = q.astype(BF16)


def _in_projection(x2d, g_pre, w_in, b_f, *, transposed, tm, tk):
    n, d = x2d.shape
    tm = min(tm, n)
    assert n % tm == 0 and (not transposed or tm % tk == 0)
    qscale = FOX_HD ** -0.5 * LOG2E
    o = np.cumsum([0, FOX_W, FOX_W, FOX_W, FOX_HEADS, FOX_W, DIFF_W, DIFF_W, DIFF_W, DIFF_W])
    w_fq, w_fk, w_fv, w_ff, w_fg, w_dq, w_dk, w_dv, w_dg = [w_in[:, o[i]:o[i + 1]] for i in range(9)]
    wkv = jnp.concatenate([w_fk, w_fv, w_dk, w_dv], axis=1).astype(BF16)
    wg = jnp.concatenate([w_fg, w_dg], axis=1).astype(BF16)
    wff = jnp.pad(w_ff, ((0, 0), (0, LANES - FOX_HEADS))).astype(BF16)
    bfp = jnp.pad(b_f.reshape(1, FOX_HEADS), ((0, 0), (0, LANES - FOX_HEADS)))
    wq = jnp.concatenate([w_fq, w_dq], axis=1).astype(BF16)

    const = lambda shape: pl.BlockSpec(shape, lambda i: (0,) * len(shape))
    rows = lambda w: pl.BlockSpec((tm, w), lambda i: (i, 0))
    in_specs = [rows(d), const((1, d)), const(wkv.shape), const(wg.shape), const(wff.shape),
                const(bfp.shape), const(wq.shape)]
    by_head = jax.ShapeDtypeStruct((DIFF_HEADS * n, 2 * DIFF_HD), F32)
    by_head_spec = pl.BlockSpec((DIFF_HEADS * tm, 2 * DIFF_HD), lambda i: (i, 0))
    out_shape = [jax.ShapeDtypeStruct((n, FOX_W), F32)] * 2 + [by_head] * 2
    out_shape += [jax.ShapeDtypeStruct((n, FOX_HEADS), F32), jax.ShapeDtypeStruct((n, FOX_W + DIFF_W), BF16)]
    out_specs = [rows(FOX_W), rows(FOX_W), by_head_spec, by_head_spec, rows(FOX_HEADS), rows(FOX_W + DIFF_W)]
    if transposed:
        out_shape += [jax.ShapeDtypeStruct((FOX_W + DIFF_W, n), BF16),
                      jax.ShapeDtypeStruct((n, FOX_W), BF16), jax.ShapeDtypeStruct((n, DIFF_W), BF16),
                      jax.ShapeDtypeStruct((n // tk, FOX_W + DIFF_W, tk), BF16)]
        out_specs += [pl.BlockSpec((FOX_W + DIFF_W, tm), lambda i: (0, i)), rows(FOX_W), rows(DIFF_W),
                      pl.BlockSpec((tm // tk, FOX_W + DIFF_W, tk), lambda i: (i, 0, 0))]
    else:
        out_shape += [jax.ShapeDtypeStruct((n, FOX_W + DIFF_W), BF16)]
        out_specs += [rows(FOX_W + DIFF_W)]
    return pl.pallas_call(
        functools.partial(_inproj_body, transposed=transposed, tk=tk, qscale=qscale),
        grid=(n // tm,), in_specs=in_specs, out_specs=out_specs, out_shape=out_shape,
        compiler_params=_cparams(1), name="in_projection_t" if transposed else "in_projection_r",
    )(x2d, g_pre.reshape(1, d), wkv, wg, wff, bfp, wq)


def _cumsum_body(x_ref, cum_ref, hi_ref, mid_ref, lo_ref, *, seg, group, scale):
    x = x_ref[...]
    r = x.shape[0]
    a = lax.broadcasted_iota(jnp.int32, (LANES, LANES), 0)
    b = lax.broadcasted_iota(jnp.int32, (LANES, LANES), 1)
    sh = int(math.log2(seg))
    upper = ((a <= b) & ((a >> sh) == (b >> sh))).astype(BF16)
    pieces = _split3(x)
    w = sum(_dot(p, upper) for p in pieces)
    if group > 1:
        ones = jnp.ones((LANES, LANES), BF16)
        tot = sum(_dot(p, ones) for p in pieces)
        ri = lax.broadcasted_iota(jnp.int32, (r, r), 0)
        ci = lax.broadcasted_iota(jnp.int32, (r, r), 1)
        gsh = int(math.log2(group))
        lower = ((ci < ri) & ((ri >> gsh) == (ci >> gsh))).astype(BF16)
        w = w + sum(_dot(lower, p) for p in _split3(tot))
    cum_ref[...] = w
    hi, mid, lo = _split3(w * scale)
    hi_ref[...] = hi
    mid_ref[...] = mid
    lo_ref[...] = lo


def _prefix_sums(x2d, *, seg, group, rows_per_step, scale):
    r = x2d.shape[0]
    rb = min(rows_per_step, r)
    assert r % rb == 0 and rb % group == 0 and (seg == LANES or group == 1)
    assert seg & (seg - 1) == 0 and group & (group - 1) == 0
    spec = pl.BlockSpec((rb, LANES), lambda i: (i, 0))
    return pl.pallas_call(
        functools.partial(_cumsum_body, seg=seg, group=group, scale=scale),
        grid=(r // rb,), in_specs=[spec], out_specs=[spec] * 4,
        out_shape=[jax.ShapeDtypeStruct((r, LANES), F32)] + [jax.ShapeDtypeStruct((r, LANES), BF16)] * 3,
        compiler_params=_cparams(1), name="prefix_sums",
    )(x2d)


SUM_ROWS = 16
SCORES_AHEAD = 2
FAR_UNROLL = 8


def _with_sum_rows(vt):
    row = lax.broadcasted_iota(jnp.int32, (SUM_ROWS, vt.shape[1]), 0)
    return jnp.concatenate([vt, jnp.where(row == 0, 1.0, 0.0).astype(BF16)], axis=0)


def _store_scores(buf, h, s):
    buf[0][h] = s
    buf[1][h] = jnp.max(s, axis=0, keepdims=True)


def _online_update_t(s, tile_max, h, m_ref, acc_ref, vt_aug):
    m = m_ref[h]
    mn = jnp.maximum(m, tile_max)
    alpha = jnp.exp2(m - mn)
    p = jnp.exp2(s - mn)
    acc_ref[h] = alpha * acc_ref[h] + _dot(vt_aug, p.astype(BF16))
    m_ref[h] = mn


def _run_far_tiles(n, half_step, buf_a, buf_b):
    def unrolled(base, count):
        bufs = (buf_a, buf_b)
        for u in range(count):
            half_step(base + u, bufs[u % 2], bufs[(u + 1) % 2])

    def body(jj, c):
        unrolled(FAR_UNROLL * jj, FAR_UNROLL)
        return c

    lax.fori_loop(0, n // FAR_UNROLL, body, 0)
    for count in range(FAR_UNROLL - 2, 0, -2):
        @pl.when(n % FAR_UNROLL // 2 * 2 == count)
        def _():
            unrolled(n // FAR_UNROLL * FAR_UNROLL, count)


def _init_state(m_ref, l_ref, acc_ref):
    m_ref[...] = jnp.full(m_ref.shape, NEG_INF, F32)
    if l_ref is not None:
        l_ref[...] = jnp.zeros(l_ref.shape, F32)
    acc_ref[...] = jnp.zeros(acc_ref.shape, F32)


def _resident(shape, index_map):
    return pl.BlockSpec(shape, index_map, pipeline_mode=pl.Buffered(1))


def _fox_prompt_body(q_ref, qx_ref, qn_ref, qxn_ref, k_ref, kx_ref, vt_ref, gate_ref, o_ref, m_ref, acc_ref,
                     sa_ref, sb_ref, ma_ref, mb_ref, *, t, pairs):
    i = pl.program_id(1)
    heads = 2 * pairs
    buf_a, buf_b = (sa_ref, ma_ref), (sb_ref, mb_ref)
    _init_state(m_ref, None, acc_ref)
    z64 = jnp.zeros((FOX_HD, t), BF16)
    zpad = jnp.zeros((LANES - QEXT_ROWS, t), BF16)

    def augmented_queries(q_blk_ref, qx_blk_ref):
        out = []
        for p in range(pairs):
            q = q_blk_ref[p * HEAD_PAIR_W:(p + 1) * HEAD_PAIR_W, :]
            out.append(jnp.concatenate([q[:FOX_HD], z64, qx_blk_ref[2 * p], zpad], axis=0))
            out.append(jnp.concatenate([z64, q[FOX_HD:], qx_blk_ref[2 * p + 1], zpad], axis=0))
        return out

    qa = augmented_queries(q_ref, qx_ref)
    qa_next = augmented_queries(qn_ref, qxn_ref)

    def keys(j, p):
        off = pl.multiple_of(j * t, t)
        return jnp.concatenate([k_ref[pl.ds(off, t), p * HEAD_PAIR_W:(p + 1) * HEAD_PAIR_W],
                                kx_ref[pl.ds(off, t), :]], axis=1)

    def consume(j, cur, h, diagonal, between=None):
        s = cur[0][h]
        if between is not None:
            between()
        if diagonal:
            kk = lax.broadcasted_iota(jnp.int32, (t, t), 0)
            qq = lax.broadcasted_iota(jnp.int32, (t, t), 1)
            s = jnp.where(kk <= qq, s, NEG_INF)
            tile_max = jnp.max(s, axis=0, keepdims=True)
        else:
            tile_max = cur[1][h]
        _online_update_t(s, tile_max, h, m_ref, acc_ref,
                         _with_sum_rows(vt_ref[j, h * FOX_HD:(h + 1) * FOX_HD, :]))

    def half_step(j, cur, nxt):
        ka = {}

        def prefetch(h):
            if h // 2 not in ka:
                ka[h // 2] = keys(j + 1, h // 2)
            _store_scores(nxt, h, _dot(ka[h // 2], qa[h]))

        for h in range(min(SCORES_AHEAD, heads)):
            prefetch(h)
        for h in range(heads):
            consume(j, cur, h, False)
            if h + SCORES_AHEAD < heads:
                prefetch(h + SCORES_AHEAD)

    def first_tile_scores(h, queries):
        _store_scores(buf_a, h, _dot(keys(0, h // 2), queries[h]))

    def last_step(j, cur):
        for h in range(heads):
            consume(j, cur, h, True, between=functools.partial(first_tile_scores, h, qa_next))

    @pl.when(i == 0)
    def _():
        for h in range(heads):
            first_tile_scores(h, qa)

    _run_far_tiles(i, half_step, buf_a, buf_b)

    @pl.when(i % 2 == 0)
    def _():
        last_step(i, buf_a)

    @pl.when(i % 2 == 1)
    def _():
        half_step(i - 1, buf_a, buf_b)
        last_step(i, buf_b)

    for p in range(pairs):
        a0, a1 = acc_ref[2 * p], acc_ref[2 * p + 1]
        ot = jnp.concatenate([a0[:FOX_HD] / a0[FOX_HD:FOX_HD + 1], a1[:FOX_HD] / a1[FOX_HD:FOX_HD + 1]], axis=0)
        cols = slice(p * HEAD_PAIR_W, (p + 1) * HEAD_PAIR_W)
        o_ref[:, cols] = (ot.T * gate_ref[:, cols].astype(F32)).astype(BF16)


def _fox_prompt_attention(qt, qext, k16, kext, vt, gate, *, t, pairs):
    n = k16.shape[0]
    nq = n // t
    w = pairs * HEAD_PAIR_W
    heads = 2 * pairs
    assert (FOX_HEADS // 2) % pairs == 0
    return pl.pallas_call(
        functools.partial(_fox_prompt_body, t=t, pairs=pairs),
        grid=(FOX_HEADS // heads, nq),
        in_specs=[pl.BlockSpec((w, t), lambda g, i: (g, i)),
                  pl.BlockSpec((heads, QEXT_ROWS, t), lambda g, i: (g, 0, i)),
                  pl.BlockSpec((w, t), lambda g, i: (g, jnp.minimum(i + 1, nq - 1))),
                  pl.BlockSpec((heads, QEXT_ROWS, t), lambda g, i: (g, 0, jnp.minimum(i + 1, nq - 1))),
                  _resident((n, w), lambda g, i: (0, g)),
                  _resident((n, LANES), lambda g, i: (0, 0)),
                  _resident((nq, w, t), lambda g, i: (0, g, 0)),
                  pl.BlockSpec((t, w), lambda g, i: (i, g))],
        out_specs=pl.BlockSpec((t, w), lambda g, i: (i, g)),
        out_shape=jax.ShapeDtypeStruct((n, FOX_W), BF16),
        scratch_shapes=[pltpu.VMEM((heads, 1, t), F32), pltpu.VMEM((heads, FOX_HD + SUM_ROWS, t), F32),
                        pltpu.VMEM((heads, t, t), F32), pltpu.VMEM((heads, t, t), F32),
                        pltpu.VMEM((heads, 1, t), F32), pltpu.VMEM((heads, 1, t), F32)],
        compiler_params=_cparams(2), name="fox_prompt_attention",
    )(qt, qext, qt, qext, k16, kext, vt, gate)


def _lambda_value(lq1, lk1, lq2, lk2, lam_init):
    a = jnp.sum(lq1 * lk1, axis=-1, keepdims=True)
    b = jnp.sum(lq2 * lk2, axis=-1, keepdims=True)
    return jnp.exp(a) - jnp.exp(b) + lam_init


def _subln_gate(y, g, gate, lam_init):
    ms = jnp.mean(y * y, axis=-1, keepdims=True)
    yn = (y * lax.rsqrt(ms + EPS)) * g
    return ((yn * (1.0 - lam_init)) * gate.astype(F32)).astype(BF16)


def _diff_prompt_body(q_ref, qn_ref, k_ref, vt_ref, bd_ref, bs_ref, gate_ref, sg_ref, lq1_ref, lk1_ref, lq2_ref,
                      lk2_ref, o_ref, m_ref, acc_ref, sa_ref, sb_ref, ma_ref, mb_ref, *, t, heads, lam_init):
    i = pl.program_id(1)
    w = 2 * DIFF_HD
    buf_a, buf_b = (sa_ref, ma_ref), (sb_ref, mb_ref)
    _init_state(m_ref, None, acc_ref)
    z64 = jnp.zeros((DIFF_HD, t), BF16)

    def padded_queries(q_blk_ref):
        out = []
        for h in range(heads):
            q = q_blk_ref[h * w:(h + 1) * w, :]
            out.append(jnp.concatenate([q[:DIFF_HD], z64], axis=0))
            out.append(jnp.concatenate([z64, q[DIFF_HD:]], axis=0))
        return out

    qa = padded_queries(q_ref)
    qa_next = padded_queries(qn_ref)

    def keys(j, h):
        return k_ref[pl.ds(pl.multiple_of(j * t, t), t), h * w:(h + 1) * w]

    def consume(j, cur, c, bias_ref, between=None):
        h = c // 2
        s = cur[0][c]
        if between is not None:
            between()
        if bias_ref is not None:
            s = s + bias_ref[h]
            tile_max = jnp.max(s, axis=0, keepdims=True)
        else:
            tile_max = cur[1][c]
        _online_update_t(s, tile_max, c, m_ref, acc_ref, _with_sum_rows(vt_ref[j, h * w:(h + 1) * w, :]))

    def half_step(j, cur, nxt, bias_ref=None):
        kk = {}

        def prefetch(c):
            if c // 2 not in kk:
                kk[c // 2] = keys(j + 1, c // 2)
            _store_scores(nxt, c, _dot(kk[c // 2], qa[c]))

        for c in range(min(SCORES_AHEAD, 2 * heads)):
            prefetch(c)
        for c in range(2 * heads):
            consume(j, cur, c, bias_ref)
            if c + SCORES_AHEAD < 2 * heads:
                prefetch(c + SCORES_AHEAD)

    def first_tile_scores(c, queries):
        _store_scores(buf_a, c, _dot(keys(0, c // 2), queries[c]))

    def last_step(j, cur):
        for c in range(2 * heads):
            consume(j, cur, c, bd_ref, between=functools.partial(first_tile_scores, c, qa_next))

    @pl.when(i == 0)
    def _():
        for c in range(2 * heads):
            first_tile_scores(c, qa)

    _run_far_tiles(jnp.maximum(i - 1, 0), half_step, buf_a, buf_b)

    @pl.when(i == 0)
    def _():
        last_step(i, buf_a)

    @pl.when(i % 2 == 1)
    def _():
        half_step(i - 1, buf_a, buf_b, bs_ref)
        last_step(i, buf_b)

    @pl.when((i >= 2) & (i % 2 == 0))
    def _():
        half_step(i - 2, buf_a, buf_b)
        half_step(i - 1, buf_b, buf_a, bs_ref)
        last_step(i, buf_a)

    lam = _lambda_value(lq1_ref[...], lk1_ref[...], lq2_ref[...], lk2_ref[...], lam_init)
    for h in range(heads):
        a0, a1 = acc_ref[2 * h], acc_ref[2 * h + 1]
        ot = a0[:w] / a0[w:w + 1] - lam * (a1[:w] / a1[w:w + 1])
        cols = slice(h * w, (h + 1) * w)
        o_ref[:, cols] = _subln_gate(ot.T, sg_ref[...], gate_ref[:, cols], lam_init)


def _diff_prompt_attention(qt, dk16, vt, bias_diag, bias_sub, gate, subln_g, lams, *, t, heads, lam_init):
    n = dk16.shape[0]
    nq = n // t
    w = heads * 2 * DIFF_HD
    assert DIFF_HEADS % heads == 0 and FOX_W % w == 0
    fox_blocks = FOX_W // w
    vec = lambda width: pl.BlockSpec((1, width), lambda g, i: (0, 0))
    return pl.pallas_call(
        functools.partial(_diff_prompt_body, t=t, heads=heads, lam_init=lam_init),
        grid=(DIFF_HEADS // heads, nq),
        in_specs=[pl.BlockSpec((w, t), lambda g, i: (fox_blocks + g, i)),
                  pl.BlockSpec((w, t), lambda g, i: (fox_blocks + g, jnp.minimum(i + 1, nq - 1))),
                  _resident((n, w), lambda g, i: (0, g)),
                  _resident((nq, w, t), lambda g, i: (0, fox_blocks + g, 0)),
                  pl.BlockSpec((heads, t, t), lambda g, i: (g, 0, 0)),
                  pl.BlockSpec((heads, t, t), lambda g, i: (g, 0, 0)),
                  pl.BlockSpec((t, w), lambda g, i: (i, fox_blocks + g)),
                  vec(2 * DIFF_HD), vec(DIFF_HD), vec(DIFF_HD), vec(DIFF_HD), vec(DIFF_HD)],
        out_specs=pl.BlockSpec((t, w), lambda g, i: (i, g)),
        out_shape=jax.ShapeDtypeStruct((n, DIFF_W), BF16),
        scratch_shapes=[pltpu.VMEM((2 * heads, 1, t), F32),
                        pltpu.VMEM((2 * heads, 2 * DIFF_HD + SUM_ROWS, t), F32),
                        pltpu.VMEM((2 * heads, t, t), F32), pltpu.VMEM((2 * heads, t, t), F32),
                        pltpu.VMEM((2 * heads, 1, t), F32), pltpu.VMEM((2 * heads, 1, t), F32)],
        compiler_params=_cparams(2), name="diff_prompt_attention",
    )(qt, qt, dk16, vt, bias_diag, bias_sub, gate, subln_g, *lams)


def _block_diag_queries(qb):
    lane = lax.broadcasted_iota(jnp.int32, qb.shape, 1)
    zero = jnp.zeros_like(qb)
    return jnp.concatenate([jnp.where(lane < FOX_HD, qb, zero), jnp.where(lane >= FOX_HD, qb, zero)], axis=0)


def _fox_sample_body(q_ref, kc_ref, vc_ref, kn_ref, vn_ref, bc_ref, bn_ref, gate_ref, o_ref, *, s, pairs):
    p_len = kc_ref.shape[-1]
    zrows = jnp.zeros((LANES - s, HEAD_PAIR_W), BF16)
    qi = lax.broadcasted_iota(jnp.int32, (2 * s, LANES), 0)
    ki = lax.broadcasted_iota(jnp.int32, (2 * s, LANES), 1)
    causal = ki <= jnp.where(qi >= s, qi - s, qi)
    lane = lax.broadcasted_iota(jnp.int32, (s, HEAD_PAIR_W), 1)

    def rows_bias(b, width):
        return jnp.concatenate([jnp.broadcast_to(b[0:1], (s, width)), jnp.broadcast_to(b[1:2], (s, width))], axis=0)

    scores = []
    for p in range(pairs):
        cols = slice(p * HEAD_PAIR_W, (p + 1) * HEAD_PAIR_W)
        qbd = _block_diag_queries(q_ref[:, cols])
        kt = kc_ref[0, 2 * p:2 * p + 2].reshape(HEAD_PAIR_W, p_len).astype(BF16)
        kn = jnp.concatenate([kn_ref[:, cols].astype(BF16), zrows], axis=0)
        sc = _dot(qbd, kt) + rows_bias(bc_ref[0, p], p_len)
        sn = _nt_dot(qbd, kn) + rows_bias(bn_ref[0, p], LANES)
        scores.append((sc, jnp.where(causal, sn, NEG_INF)))
    for p, (sc, sn) in enumerate(scores):
        cols = slice(p * HEAD_PAIR_W, (p + 1) * HEAD_PAIR_W)
        vt = vc_ref[0, 2 * p:2 * p + 2].reshape(HEAD_PAIR_W, p_len).astype(BF16)
        vn = jnp.concatenate([vn_ref[:, cols].astype(BF16), zrows], axis=0)
        m = jnp.maximum(jnp.max(sc, axis=1, keepdims=True), jnp.max(sn, axis=1, keepdims=True))
        pc = jnp.exp2(sc - m)
        pn = jnp.exp2(sn - m)
        l = jnp.sum(pc, axis=1, keepdims=True) + jnp.sum(pn, axis=1, keepdims=True)
        o = (_nt_dot(pc.astype(BF16), vt) + _dot(pn.astype(BF16), vn)) / l
        out = jnp.where(lane < FOX_HD, o[:s], o[s:])
        o_ref[:, cols] = (out * gate_ref[:, cols].astype(F32)).astype(BF16)


def _fox_sample_attention(q16, cache_kt, cache_vt, fk32, fv32, bias_c, bias_n, gate, *, s, pairs):
    b, _, _, p_len = cache_kt.shape
    groups = FOX_HEADS // (2 * pairs)
    w = pairs * HEAD_PAIR_W
    blk = lambda: pl.BlockSpec((s, w), lambda bi, g: (bi, g))
    cache = lambda: pl.BlockSpec((1, 2 * pairs, FOX_HD, p_len), lambda bi, g: (bi, g, 0, 0))
    return pl.pallas_call(
        functools.partial(_fox_sample_body, s=s, pairs=pairs),
        grid=(b, groups),
        in_specs=[blk(), cache(), cache(), blk(), blk(),
                  pl.BlockSpec((1, pairs, 2, p_len), lambda bi, g: (bi, g, 0, 0)),
                  pl.BlockSpec((1, pairs, 2, LANES), lambda bi, g: (bi, g, 0, 0)),
                  blk()],
        out_specs=blk(),
        out_shape=jax.ShapeDtypeStruct((b * s, FOX_W), BF16),
        compiler_params=_cparams(2), name="fox_sample_attention",
    )(q16, cache_kt, cache_vt, fk32, fv32, bias_c, bias_n, gate)


def _diff_sample_body(q_ref, kc_ref, vc_ref, kn_ref, vn_ref, bc_ref, bn_ref, gate_ref, sg_ref, lq1_ref,
                      lk1_ref, lq2_ref, lk2_ref, o_ref, m_ref, l_ref, acc_ref, *, s, pc, lam_init):
    c = pl.program_id(1)
    last = pl.num_programs(1) - 1
    w = 2 * DIFF_HD

    @pl.when(c == 0)
    def _():
        _init_state(m_ref, l_ref, acc_ref)

    qbd = [_block_diag_queries(q_ref[:, h * w:(h + 1) * w]) for h in range(DIFF_HEADS)]

    def update(h, sc, v16):
        m = m_ref[h]
        mn = jnp.maximum(m, jnp.max(sc, axis=1, keepdims=True))
        alpha = jnp.exp2(m - mn)
        p = jnp.exp2(sc - mn)
        l_ref[h] = alpha * l_ref[h] + jnp.sum(p, axis=1, keepdims=True)
        acc_ref[h] = alpha * acc_ref[h] + _dot(p.astype(BF16), v16)
        m_ref[h] = mn

    def chunk(with_bias):
        scores = [_nt_dot(qbd[h], kc_ref[0, pl.ds(h, pc, stride=DIFF_HEADS), :].astype(BF16))
                  for h in range(DIFF_HEADS)]
        for h, sc in enumerate(scores):
            if with_bias:
                sc = sc + jnp.concatenate([bc_ref[h], bc_ref[h]], axis=0)
            update(h, sc, vc_ref[0, pl.ds(h, pc, stride=DIFF_HEADS), :].astype(BF16))

    @pl.when(c != last)
    def _():
        chunk(False)

    @pl.when(c == last)
    def _():
        chunk(True)
        lam = _lambda_value(lq1_ref[...], lk1_ref[...], lq2_ref[...], lk2_ref[...], lam_init)
        zrows = jnp.zeros((LANES - s, w), BF16)
        for h in range(DIFF_HEADS):
            cols = slice(h * w, (h + 1) * w)
            new_rows = pl.ds(h, s, stride=DIFF_HEADS)
            kn = jnp.concatenate([kn_ref[new_rows, :].astype(BF16), zrows], axis=0)
            vn = jnp.concatenate([vn_ref[new_rows, :].astype(BF16), zrows], axis=0)
            sn = _nt_dot(qbd[h], kn) + jnp.concatenate([bn_ref[h], bn_ref[h]], axis=0)
            update(h, sn, vn)
            o = acc_ref[h] / l_ref[h]
            o = o[:s] - lam * o[s:]
            o_ref[:, cols] = _subln_gate(o, sg_ref[...], gate_ref[:, cols], lam_init)


def _diff_sample_attention(q16, cache_k, cache_v, dk32, dv32, bias_c, bias_n, gate, subln_g, lams, *, s, pc,
                           lam_init):
    b, rows, w = cache_k.shape
    p_len = rows // DIFF_HEADS
    assert p_len % pc == 0 and bias_c.shape == (DIFF_HEADS, s, pc)
    cache = lambda: pl.BlockSpec((1, DIFF_HEADS * pc, w), lambda bi, c: (bi, c, 0))
    out = lambda: pl.BlockSpec((s, DIFF_W), lambda bi, c: (bi, 0))
    new = lambda: pl.BlockSpec((DIFF_HEADS * s, w), lambda bi, c: (bi, 0))
    half = lambda: pl.BlockSpec((s, DIFF_W), lambda bi, c: (bi, 1))
    const = lambda shape: pl.BlockSpec(shape, lambda bi, c: (0,) * len(shape))
    return pl.pallas_call(
        functools.partial(_diff_sample_body, s=s, pc=pc, lam_init=lam_init),
        grid=(b, p_len // pc),
        in_specs=[half(), cache(), cache(), new(), new(), const(bias_c.shape), const(bias_n.shape), half(),
                  const((1, w)), const((1, DIFF_HD)), const((1, DIFF_HD)), const((1, DIFF_HD)), const((1, DIFF_HD))],
        out_specs=out(),
        out_shape=jax.ShapeDtypeStruct((b * s, DIFF_W), BF16),
        scratch_shapes=[pltpu.VMEM((DIFF_HEADS, 2 * s, 1), F32), pltpu.VMEM((DIFF_HEADS, 2 * s, 1), F32),
                        pltpu.VMEM((DIFF_HEADS, 2 * s, w), F32)],
        compiler_params=_cparams(2), name="diff_sample_attention",
    )(q16, cache_k, cache_v, dk32, dv32, bias_c, bias_n, gate, subln_g, *lams)


def _outproj_body(mf_ref, md_ref, w_ref, x_ref, g_ref, y_ref):
    out = _dot(mf_ref[...], w_ref[:FOX_W, :]) + _dot(md_ref[...], w_ref[FOX_W:, :])
    ms = jnp.mean(out * out, axis=-1, keepdims=True)
    y_ref[...] = x_ref[...] + (out * lax.rsqrt(ms + EPS)) * g_ref[...]


def _out_projection(mf, md, w_out, x2d, g_post, *, tm):
    n, d = x2d.shape
    tm = min(tm, n)
    assert n % tm == 0
    rows = lambda w: pl.BlockSpec((tm, w), lambda i: (i, 0))
    return pl.pallas_call(
        _outproj_body, grid=(n // tm,),
        in_specs=[rows(FOX_W), rows(DIFF_W), pl.BlockSpec(w_out.shape, lambda i: (0, 0)), rows(d),
                  pl.BlockSpec((1, d), lambda i: (0, 0))],
        out_specs=rows(d), out_shape=jax.ShapeDtypeStruct((n, d), F32),
        compiler_params=_cparams(1), name="out_projection",
    )(mf, md, w_out.astype(BF16), x2d, g_post.reshape(1, d))


def _t5_bucket(rel):
    half = NUM_BUCKETS // 2
    max_exact = half // 2
    ret = jnp.where(rel > 0, half, 0)
    n = jnp.abs(rel)
    nf = jnp.maximum(n, 1).astype(jnp.float32)
    large = max_exact + (jnp.log(nf / max_exact) / math.log(MAX_DISTANCE / max_exact)
                         * (half - max_exact)).astype(jnp.int32)
    large = jnp.minimum(large, half - 1)
    return ret + jnp.where(n < max_exact, n, large)


def _rel_bias_of(rel_bias, rel):
    return rel_bias[_t5_bucket(rel)].astype(F32).T


def _toeplitz(g, rows, cols):
    h = g.shape[0]
    period = rows + cols
    v = jnp.concatenate([g[:, rows - 1:], jnp.zeros((h, 1), g.dtype), g[:, :rows - 1]], axis=1)
    flat = jnp.tile(v, (1, rows))[:, :rows * (period - 1)]
    return flat.reshape(h, rows, period - 1)[:, :, :cols]


def _chunk_mask(bias, row_pos, col_pos, rows_are_keys):
    r = (row_pos // CHUNK)[:, None]
    c = (col_pos // CHUNK)[None, :]
    mask = (r <= c) if rows_are_keys else (c <= r)
    return jnp.where(mask[None], bias * LOG2E, NEG_INF)


def _prompt_layer(x, g_pre, w_in, b_f, lams, subln_g, w_out, g_post, rel_bias, layer_idx, *, t=256, tm=512,
                  tm_out=1024, fox_pairs=4, diff_heads=4):
    b, s, d = x.shape
    assert b == 1 and s % t == 0 and s // LANES >= 1
    x2d = x.reshape(s, d)
    (fk, fv, dk, dv, logf, gate, qt, k16, dk16, vt) = _in_projection(
        x2d, g_pre, w_in, b_f, transposed=True, tm=tm, tk=t)

    rows = s // LANES
    _, hi, mid, lo = _prefix_sums(logf.T.reshape(FOX_HEADS * rows, LANES), seg=LANES, group=rows,
                                  rows_per_step=FOX_HEADS * rows, scale=LOG2E)
    pieces = jnp.stack([hi, mid, lo], axis=0).reshape(PIECES, FOX_HEADS, s)
    kext = jnp.concatenate([
        (-pieces).transpose(2, 1, 0).reshape(s, ONES_AT), jnp.ones((s, PIECES), BF16),
        jnp.zeros((s, LANES - ONES_AT - PIECES), BF16)], axis=1)
    sel = (jnp.arange(ONES_AT)[None, :] // PIECES == jnp.arange(FOX_HEADS)[:, None]).astype(BF16)
    qext = jnp.concatenate([
        jnp.broadcast_to(sel[:, :, None], (FOX_HEADS, ONES_AT, s)), pieces.transpose(1, 0, 2),
        jnp.zeros((FOX_HEADS, QEXT_ROWS - ONES_AT - PIECES, s), BF16)], axis=1)

    mf = _fox_prompt_attention(qt, qext, k16, kext, vt, gate, t=t, pairs=fox_pairs)

    lam_init = 0.8 - 0.6 * math.exp(-0.3 * layer_idx)
    assert t >= MAX_DISTANCE
    pos = jnp.arange(t, dtype=jnp.int32)
    off = jnp.arange(-(t - 1), t, dtype=jnp.int32)
    far = _rel_bias_of(rel_bias, jnp.full((1,), -2 * t, jnp.int32))
    bias_diag = _chunk_mask(_toeplitz(_rel_bias_of(rel_bias, -off) - far, t, t), pos, pos, True)
    bias_sub = _chunk_mask(_toeplitz(_rel_bias_of(rel_bias, -off - t) - far, t, t), pos, pos + t, True)
    md = _diff_prompt_attention(qt, dk16, vt, bias_diag, bias_sub, gate, subln_g.reshape(1, -1), lams,
                                t=t, heads=diff_heads, lam_init=lam_init)

    y = _out_projection(mf, md, w_out, x2d, g_post, tm=tm_out)
    return y.reshape(b, s, d), (fk, fv, logf, dk, dv)


def _sample_layer(x, past, g_pre, w_in, b_f, lams, subln_g, w_out, g_post, rel_bias, layer_idx, *, tm=512,
                  fox_pairs=2, diff_chunk=2048):
    b, s, d = x.shape
    pk, pv, plogf, pdk, pdv = past
    p_len = pk.shape[1]
    assert s <= LANES and LANES % s == 0 and p_len % LANES == 0
    x2d = x.reshape(b * s, d)
    fk, fv, dk, dv, logf, gate, q16 = _in_projection(x2d, g_pre, w_in, b_f, transposed=False, tm=tm, tk=tm)

    rows = p_len // LANES
    cum_c, _, _, _ = _prefix_sums(plogf.transpose(0, 2, 1).reshape(b * FOX_HEADS * rows, LANES), seg=LANES,
                                  group=rows, rows_per_step=32 * rows, scale=1.0)
    cum_c = cum_c.reshape(b, FOX_HEADS, p_len)
    bias_c = ((cum_c[:, :, -1:] - cum_c) * LOG2E).reshape(b, FOX_HEADS // 2, 2, p_len)
    cum_n, _, _, _ = _prefix_sums(
        logf.reshape(b, s, FOX_HEADS).transpose(0, 2, 1).reshape(b * FOX_HEADS * s // LANES, LANES),
        seg=s, group=1, rows_per_step=1024, scale=1.0)
    bias_n = jnp.concatenate([-cum_n.reshape(b, FOX_HEADS, s) * LOG2E,
                              jnp.full((b, FOX_HEADS, LANES - s), NEG_INF, F32)], axis=-1)
    bias_n = bias_n.reshape(b, FOX_HEADS // 2, 2, LANES)
    mf = _fox_sample_attention(q16, pk.transpose(0, 2, 3, 1), pv.transpose(0, 2, 3, 1), fk, fv, bias_c, bias_n,
                               gate, s=s, pairs=fox_pairs)

    lam_init = 0.8 - 0.6 * math.exp(-0.3 * layer_idx)
    pc = min(diff_chunk, p_len)
    assert pc >= MAX_DISTANCE + s
    q_pos = p_len + jnp.arange(s, dtype=jnp.int32)
    far = _rel_bias_of(rel_bias, jnp.full((1,), -2 * p_len, jnp.int32))
    off = jnp.arange(-(s - 1), pc, dtype=jnp.int32)
    bd_c = _chunk_mask(_toeplitz(_rel_bias_of(rel_bias, off - pc) - far, s, pc), q_pos,
                       p_len - pc + jnp.arange(pc, dtype=jnp.int32), False)
    off = jnp.arange(-(s - 1), s, dtype=jnp.int32)
    bd_n = _chunk_mask(_toeplitz(_rel_bias_of(rel_bias, off) - far, s, s), q_pos, q_pos, False)
    bd_n = jnp.concatenate([bd_n, jnp.full((DIFF_HEADS, s, LANES - s), NEG_INF, F32)], axis=-1)
    md = _diff_sample_attention(q16, pdk.reshape(b, p_len * DIFF_HEADS, 2 * DIFF_HD),
                                pdv.reshape(b, p_len * DIFF_HEADS, 2 * DIFF_HD), dk, dv, bd_c, bd_n, gate,
                                subln_g.reshape(1, -1), lams, s=s, pc=pc, lam_init=lam_init)

    y = _out_projection(mf, md, w_out, x2d, g_post, tm=tm)
    return y.reshape(b, s, d), (fk, fv, logf, dk, dv)


def kernel(x_prompt, x_sample, cache_fox_k, cache_fox_v, cache_fox_logf, cache_diff_k, cache_diff_v,
           norm_pre_g, w_in, forget_bias, lambda_q1, lambda_k1, lambda_q2, lambda_k2, subln_g, w_out,
           norm_post_g, rel_bias):
    depth = w_in.shape[0]
    y_p, y_s = x_prompt, x_sample
    rows_p, rows_s = [], []
    for l in range(depth):
        lams = tuple(a[l].reshape(1, -1) for a in (lambda_q1, lambda_k1, lambda_q2, lambda_k2))
        params = (norm_pre_g[l], w_in[l], forget_bias[l], lams, subln_g[l], w_out[l], norm_post_g[l], rel_bias, l)
        y_p, new_p = _prompt_layer(y_p, *params)
        past = (cache_fox_k[l], cache_fox_v[l], cache_fox_logf[l], cache_diff_k[l], cache_diff_v[l])
        y_s, new_s = _sample_layer(y_s, past, *params)
        rows_p.append(new_p)
        rows_s.append(new_s)

    bp, sp = x_prompt.shape[:2]
    bs, ss = x_sample.shape[:2]

    def stack(rows, idx, shape):
        return jnp.stack([r[idx].reshape(shape) for r in rows], axis=0)

    outs = [y_p, y_s]
    for rows, (b, s) in ((rows_p, (bp, sp)), (rows_s, (bs, ss))):
        outs += [stack(rows, 0, (b, s, FOX_HEADS, FOX_HD)), stack(rows, 1, (b, s, FOX_HEADS, FOX_HD)),
                 stack(rows, 2, (b, s, FOX_HEADS)), stack(rows, 3, (b, s, DIFF_HEADS, 2 * DIFF_HD)),
                 stack(rows, 4, (b, s, DIFF_HEADS, 2 * DIFF_HD))]
    return tuple(outs)
```

```python
import functools
import math

import jax
import jax.numpy as jnp
import numpy as np
from jax import lax
from jax.experimental import pallas as pl
from jax.experimental.pallas import tpu as pltpu

F32 = jnp.float32
BF16 = jnp.bfloat16

FOX_HEADS = 8
FOX_HD = 64
DIFF_HEADS = 4
DIFF_HD = 64
FOX_W = FOX_HEADS * FOX_HD
DIFF_W = DIFF_HEADS * 2 * DIFF_HD
CHUNK = 64
NUM_BUCKETS = 32
MAX_DISTANCE = 128
EPS = 1e-6
NEG_INF = -1e30
LOG2E = 1.4426950408889634

LANES = 128
HEAD_PAIR_W = 2 * FOX_HD
VMEM_LIMIT_BYTES = 56 * 1024 * 1024

PIECES = 3
ONES_AT = PIECES * FOX_HEADS
QEXT_ROWS = 32


def _cparams(n_axes):
    return pltpu.CompilerParams(
        dimension_semantics=("arbitrary",) * n_axes,
        vmem_limit_bytes=VMEM_LIMIT_BYTES,
    )


def _split3(a):
    hi = a.astype(BF16)
    r1 = a - hi.astype(F32)
    mid = r1.astype(BF16)
    lo = (r1 - mid.astype(F32)).astype(BF16)
    return hi, mid, lo


def _nt_dot(a, b):
    return lax.dot_general(a, b, (((1,), (1,)), ((), ())), preferred_element_type=F32)


def _dot(a, b):
    return jnp.dot(a, b, preferred_element_type=F32)


def _inproj_body(x_ref, g_ref, wkv_ref, wg_ref, wff_ref, bf_ref, wq_ref,
                 fk_ref, fv_ref, dk_ref, dv_ref, logf_ref, gate_ref, q_ref, *rest,
                 transposed, tk, qscale):
    x = x_ref[...]
    ms = jnp.mean(x * x, axis=-1, keepdims=True)
    h = (x * lax.rsqrt(ms + EPS)) * g_ref[...]
    hb = h.astype(BF16)

    fk = _dot(hb, wkv_ref[:, 0 * FOX_W:1 * FOX_W])
    fk_ref[...] = fk
    fv = _dot(hb, wkv_ref[:, 1 * FOX_W:2 * FOX_W])
    fv_ref[...] = fv
    dk = _dot(hb, wkv_ref[:, 2 * FOX_W:3 * FOX_W])
    dv = _dot(hb, wkv_ref[:, 3 * FOX_W:4 * FOX_W])
    tm = dk.shape[0]
    for hd in range(DIFF_HEADS):
        cols = slice(hd * 2 * DIFF_HD, (hd + 1) * 2 * DIFF_HD)
        dk_ref[pl.ds(hd, tm, stride=DIFF_HEADS), :] = dk[:, cols]
        dv_ref[pl.ds(hd, tm, stride=DIFF_HEADS), :] = dv[:, cols]

    ff = _dot(hb, wff_ref[...])
    logf = jax.nn.log_sigmoid(ff + bf_ref[...])
    logf_ref[...] = logf[:, :FOX_HEADS]

    gates = _dot(hb, wg_ref[...])
    gate_ref[...] = (gates * jax.nn.sigmoid(gates)).astype(BF16)

    q = _dot(hb, wq_ref[...]) * qscale
    if transposed:
        k16_ref, dk16_ref, vt_ref = rest
        k16_ref[...] = fk.astype(BF16)
        dk16_ref[...] = dk.astype(BF16)
        q_ref[...] = q.T.astype(BF16)
        vt = jnp.concatenate([fv, dv], axis=1).T.astype(BF16)
        for c in range(vt.shape[1] // tk):
            vt_ref[c] = vt[:, c * tk:(c + 1) * tk]
    else:
        q_ref[...] = q.astype(BF16)


def _in_projection(x2d, g_pre, w_in, b_f, *, transposed, tm, tk):
    n, d = x2d.shape
    tm = min(tm, n)
    assert n % tm == 0 and (not transposed or tm % tk == 0)
    qscale = FOX_HD ** -0.5 * LOG2E
    o = np.cumsum([0, FOX_W, FOX_W, FOX_W, FOX_HEADS, FOX_W, DIFF_W, DIFF_W, DIFF_W, DIFF_W])
    w_fq, w_fk, w_fv, w_ff, w_fg, w_dq, w_dk, w_dv, w_dg = [w_in[:, o[i]:o[i + 1]] for i in range(9)]
    wkv = jnp.concatenate([w_fk, w_fv, w_dk, w_dv], axis=1).astype(BF16)
    wg = jnp.concatenate([w_fg, w_dg], axis=1).astype(BF16)
    wff = jnp.pad(w_ff, ((0, 0), (0, LANES - FOX_HEADS))).astype(BF16)
    bfp = jnp.pad(b_f.reshape(1, FOX_HEADS), ((0, 0), (0, LANES - FOX_HEADS)))
    wq = jnp.concatenate([w_fq, w_dq], axis=1).astype(BF16)

    const = lambda shape: pl.BlockSpec(shape, lambda i: (0,) * len(shape))
    rows = lambda w: pl.BlockSpec((tm, w), lambda i: (i, 0))
    in_specs = [rows(d), const((1, d)), const(wkv.shape), const(wg.shape), const(wff.shape),
                const(bfp.shape), const(wq.shape)]
    by_head = jax.ShapeDtypeStruct((DIFF_HEADS * n, 2 * DIFF_HD), F32)
    by_head_spec = pl.BlockSpec((DIFF_HEADS * tm, 2 * DIFF_HD), lambda i: (i, 0))
    out_shape = [jax.ShapeDtypeStruct((n, FOX_W), F32)] * 2 + [by_head] * 2
    out_shape += [jax.ShapeDtypeStruct((n, FOX_HEADS), F32), jax.ShapeDtypeStruct((n, FOX_W + DIFF_W), BF16)]
    out_specs = [rows(FOX_W), rows(FOX_W), by_head_spec, by_head_spec, rows(FOX_HEADS), rows(FOX_W + DIFF_W)]
    if transposed:
        out_shape += [jax.ShapeDtypeStruct((FOX_W + DIFF_W, n), BF16),
                      jax.ShapeDtypeStruct((n, FOX_W), BF16), jax.ShapeDtypeStruct((n, DIFF_W), BF16),
                      jax.ShapeDtypeStruct((n // tk, FOX_W + DIFF_W, tk), BF16)]
        out_specs += [pl.BlockSpec((FOX_W + DIFF_W, tm), lambda i: (0, i)), rows(FOX_W), rows(DIFF_W),
                      pl.BlockSpec((tm // tk, FOX_W + DIFF_W, tk), lambda i: (i, 0, 0))]
    else:
        out_shape += [jax.ShapeDtypeStruct((n, FOX_W + DIFF_W), BF16)]
        out_specs += [rows(FOX_W + DIFF_W)]
    return pl.pallas_call(
        functools.partial(_inproj_body, transposed=transposed, tk=tk, qscale=qscale),
        grid=(n // tm,), in_specs=in_specs, out_specs=out_specs, out_shape=out_shape,
        compiler_params=_cparams(1), name="in_projection_t" if transposed else "in_projection_r",
    )(x2d, g_pre.reshape(1, d), wkv, wg, wff, bfp, wq)


def _cumsum_body(x_ref, cum_ref, hi_ref, mid_ref, lo_ref, *, seg, group, scale):
    x = x_ref[...]
    r = x.shape[0]
    a = lax.broadcasted_iota(jnp.int32, (LANES, LANES), 0)
    b = lax.broadcasted_iota(jnp.int32, (LANES, LANES), 1)
    sh = int(math.log2(seg))
    upper = ((a <= b) & ((a >> sh) == (b >> sh))).astype(BF16)
    pieces = _split3(x)
    w = sum(_dot(p, upper) for p in pieces)
    if group > 1:
        ones = jnp.ones((LANES, LANES), BF16)
        tot = sum(_dot(p, ones) for p in pieces)
        ri = lax.broadcasted_iota(jnp.int32, (r, r), 0)
        ci = lax.broadcasted_iota(jnp.int32, (r, r), 1)
        gsh = int(math.log2(group))
        lower = ((ci < ri) & ((ri >> gsh) == (ci >> gsh))).astype(BF16)
        w = w + sum(_dot(lower, p) for p in _split3(tot))
    cum_ref[...] = w
    hi, mid, lo = _split3(w * scale)
    hi_ref[...] = hi
    mid_ref[...] = mid
    lo_ref[...] = lo


PREFIX_ROWS_PER_STEP = 1024


def _prefix_sums(x2d, *, seg, group, scale):
    r = x2d.shape[0]
    rb = min(max(PREFIX_ROWS_PER_STEP, group), r)
    assert r % rb == 0 and rb % group == 0 and (seg == LANES or group == 1)
    assert seg & (seg - 1) == 0 and group & (group - 1) == 0
    spec = pl.BlockSpec((rb, LANES), lambda i: (i, 0))
    return pl.pallas_call(
        functools.partial(_cumsum_body, seg=seg, group=group, scale=scale),
        grid=(r // rb,), in_specs=[spec], out_specs=[spec] * 4,
        out_shape=[jax.ShapeDtypeStruct((r, LANES), F32)] + [jax.ShapeDtypeStruct((r, LANES), BF16)] * 3,
        compiler_params=_cparams(1), name="prefix_sums",
    )(x2d)


SUM_ROWS = 16
SCORES_AHEAD = 2
FAR_UNROLL = 8


def _with_sum_rows(vt):
    row = lax.broadcasted_iota(jnp.int32, (SUM_ROWS, vt.shape[1]), 0)
    return jnp.concatenate([vt, jnp.where(row == 0, 1.0, 0.0).astype(BF16)], axis=0)


def _store_scores(buf, h, s):
    buf[0][h] = s
    buf[1][h] = jnp.max(s, axis=0, keepdims=True)


def _online_update_t(s, tile_max, h, m_ref, acc_ref, vt_aug):
    m = m_ref[h]
    mn = jnp.maximum(m, tile_max)
    alpha = jnp.exp2(m - mn)
    p = jnp.exp2(s - mn)
    acc_ref[h] = alpha * acc_ref[h] + _dot(vt_aug, p.astype(BF16))
    m_ref[h] = mn


def _run_far_tiles(n, half_step, buf_a, buf_b):
    def unrolled(base, count):
        bufs = (buf_a, buf_b)
        for u in range(count):
            half_step(base + u, bufs[u % 2], bufs[(u + 1) % 2])

    def body(jj, c):
        unrolled(FAR_UNROLL * jj, FAR_UNROLL)
        return c

    lax.fori_loop(0, n // FAR_UNROLL, body, 0)
    for count in range(FAR_UNROLL - 2, 0, -2):
        @pl.when(n % FAR_UNROLL // 2 * 2 == count)
        def _():
            unrolled(n // FAR_UNROLL * FAR_UNROLL, count)


def _init_state(m_ref, l_ref, acc_ref):
    m_ref[...] = jnp.full(m_ref.shape, NEG_INF, F32)
    if l_ref is not None:
        l_ref[...] = jnp.zeros(l_ref.shape, F32)
    acc_ref[...] = jnp.zeros(acc_ref.shape, F32)


def _resident(shape, index_map):
    return pl.BlockSpec(shape, index_map, pipeline_mode=pl.Buffered(1))


def _fox_prompt_body(q_ref, qx_ref, qn_ref, qxn_ref, k_ref, kx_ref, vt_ref, gate_ref, o_ref, m_ref, acc_ref,
                     sa_ref, sb_ref, ma_ref, mb_ref, *, t, pairs):
    i = pl.program_id(1)
    heads = 2 * pairs
    buf_a, buf_b = (sa_ref, ma_ref), (sb_ref, mb_ref)
    _init_state(m_ref, None, acc_ref)
    z64 = jnp.zeros((FOX_HD, t), BF16)
    zpad = jnp.zeros((LANES - QEXT_ROWS, t), BF16)

    def augmented_queries(q_blk_ref, qx_blk_ref):
        out = []
        for p in range(pairs):
            q = q_blk_ref[p * HEAD_PAIR_W:(p + 1) * HEAD_PAIR_W, :]
            out.append(jnp.concatenate([q[:FOX_HD], z64, qx_blk_ref[2 * p], zpad], axis=0))
            out.append(jnp.concatenate([z64, q[FOX_HD:], qx_blk_ref[2 * p + 1], zpad], axis=0))
        return out

    qa = augmented_queries(q_ref, qx_ref)
    qa_next = augmented_queries(qn_ref, qxn_ref)

    def keys(j, p):
        off = pl.multiple_of(j * t, t)
        return jnp.concatenate([k_ref[pl.ds(off, t), p * HEAD_PAIR_W:(p + 1) * HEAD_PAIR_W],
                                kx_ref[pl.ds(off, t), :]], axis=1)

    def consume(j, cur, h, diagonal, between=None):
        s = cur[0][h]
        if between is not None:
            between()
        if diagonal:
            kk = lax.broadcasted_iota(jnp.int32, (t, t), 0)
            qq = lax.broadcasted_iota(jnp.int32, (t, t), 1)
            s = jnp.where(kk <= qq, s, NEG_INF)
            tile_max = jnp.max(s, axis=0, keepdims=True)
        else:
            tile_max = cur[1][h]
        _online_update_t(s, tile_max, h, m_ref, acc_ref,
                         _with_sum_rows(vt_ref[j, h * FOX_HD:(h + 1) * FOX_HD, :]))

    def half_step(j, cur, nxt):
        ka = {}

        def prefetch(h):
            if h // 2 not in ka:
                ka[h // 2] = keys(j + 1, h // 2)
            _store_scores(nxt, h, _dot(ka[h // 2], qa[h]))

        for h in range(min(SCORES_AHEAD, heads)):
            prefetch(h)
        for h in range(heads):
            consume(j, cur, h, False)
            if h + SCORES_AHEAD < heads:
                prefetch(h + SCORES_AHEAD)

    def first_tile_scores(h, queries):
        _store_scores(buf_a, h, _dot(keys(0, h // 2), queries[h]))

    def last_step(j, cur):
        for h in range(heads):
            consume(j, cur, h, True, between=functools.partial(first_tile_scores, h, qa_next))

    @pl.when(i == 0)
    def _():
        for h in range(heads):
            first_tile_scores(h, qa)

    _run_far_tiles(i, half_step, buf_a, buf_b)

    @pl.when(i % 2 == 0)
    def _():
        last_step(i, buf_a)

    @pl.when(i % 2 == 1)
    def _():
        half_step(i - 1, buf_a, buf_b)
        last_step(i, buf_b)

    for p in range(pairs):
        a0, a1 = acc_ref[2 * p], acc_ref[2 * p + 1]
        ot = jnp.concatenate([a0[:FOX_HD] / a0[FOX_HD:FOX_HD + 1], a1[:FOX_HD] / a1[FOX_HD:FOX_HD + 1]], axis=0)
        cols = slice(p * HEAD_PAIR_W, (p + 1) * HEAD_PAIR_W)
        o_ref[:, cols] = (ot.T * gate_ref[:, cols].astype(F32)).astype(BF16)


def _fox_prompt_attention(qt, qext, k16, kext, vt, gate, *, t, pairs):
    n = k16.shape[0]
    nq = n // t
    w = pairs * HEAD_PAIR_W
    heads = 2 * pairs
    assert (FOX_HEADS // 2) % pairs == 0
    return pl.pallas_call(
        functools.partial(_fox_prompt_body, t=t, pairs=pairs),
        grid=(FOX_HEADS // heads, nq),
        in_specs=[pl.BlockSpec((w, t), lambda g, i: (g, i)),
                  pl.BlockSpec((heads, QEXT_ROWS, t), lambda g, i: (g, 0, i)),
                  pl.BlockSpec((w, t), lambda g, i: (g, jnp.minimum(i + 1, nq - 1))),
                  pl.BlockSpec((heads, QEXT_ROWS, t), lambda g, i: (g, 0, jnp.minimum(i + 1, nq - 1))),
                  _resident((n, w), lambda g, i: (0, g)),
                  _resident((n, LANES), lambda g, i: (0, 0)),
                  _resident((nq, w, t), lambda g, i: (0, g, 0)),
                  pl.BlockSpec((t, w), lambda g, i: (i, g))],
        out_specs=pl.BlockSpec((t, w), lambda g, i: (i, g)),
        out_shape=jax.ShapeDtypeStruct((n, FOX_W), BF16),
        scratch_shapes=[pltpu.VMEM((heads, 1, t), F32), pltpu.VMEM((heads, FOX_HD + SUM_ROWS, t), F32),
                        pltpu.VMEM((heads, t, t), F32), pltpu.VMEM((heads, t, t), F32),
                        pltpu.VMEM((heads, 1, t), F32), pltpu.VMEM((heads, 1, t), F32)],
        compiler_params=_cparams(2), name="fox_prompt_attention",
    )(qt, qext, qt, qext, k16, kext, vt, gate)


def _lambda_value(lq1, lk1, lq2, lk2, lam_init):
    a = jnp.sum(lq1 * lk1, axis=-1, keepdims=True)
    b = jnp.sum(lq2 * lk2, axis=-1, keepdims=True)
    return jnp.exp(a) - jnp.exp(b) + lam_init


def _subln_gate(y, g, gate, lam_init):
    ms = jnp.mean(y * y, axis=-1, keepdims=True)
    yn = (y * lax.rsqrt(ms + EPS)) * g
    return ((yn * (1.0 - lam_init)) * gate.astype(F32)).astype(BF16)


def _diff_prompt_body(q_ref, qn_ref, k_ref, vt_ref, bd_ref, bs_ref, gate_ref, sg_ref, lq1_ref, lk1_ref, lq2_ref,
                      lk2_ref, o_ref, m_ref, acc_ref, sa_ref, sb_ref, ma_ref, mb_ref, *, t, heads, lam_init):
    i = pl.program_id(1)
    w = 2 * DIFF_HD
    buf_a, buf_b = (sa_ref, ma_ref), (sb_ref, mb_ref)
    _init_state(m_ref, None, acc_ref)
    z64 = jnp.zeros((DIFF_HD, t), BF16)

    def padded_queries(q_blk_ref):
        out = []
        for h in range(heads):
            q = q_blk_ref[h * w:(h + 1) * w, :]
            out.append(jnp.concatenate([q[:DIFF_HD], z64], axis=0))
            out.append(jnp.concatenate([z64, q[DIFF_HD:]], axis=0))
        return out

    qa = padded_queries(q_ref)
    qa_next = padded_queries(qn_ref)

    def keys(j, h):
        return k_ref[pl.ds(pl.multiple_of(j * t, t), t), h * w:(h + 1) * w]

    def consume(j, cur, c, bias_ref, between=None):
        h = c // 2
        s = cur[0][c]
        if between is not None:
            between()
        if bias_ref is not None:
            s = s + bias_ref[h]
            tile_max = jnp.max(s, axis=0, keepdims=True)
        else:
            tile_max = cur[1][c]
        _online_update_t(s, tile_max, c, m_ref, acc_ref, _with_sum_rows(vt_ref[j, h * w:(h + 1) * w, :]))

    def half_step(j, cur, nxt, bias_ref=None):
        kk = {}

        def prefetch(c):
            if c // 2 not in kk:
                kk[c // 2] = keys(j + 1, c // 2)
            _store_scores(nxt, c, _dot(kk[c // 2], qa[c]))

        for c in range(min(SCORES_AHEAD, 2 * heads)):
            prefetch(c)
        for c in range(2 * heads):
            consume(j, cur, c, bias_ref)
            if c + SCORES_AHEAD < 2 * heads:
                prefetch(c + SCORES_AHEAD)

    def first_tile_scores(c, queries):
        _store_scores(buf_a, c, _dot(keys(0, c // 2), queries[c]))

    def last_step(j, cur):
        for c in range(2 * heads):
            consume(j, cur, c, bd_ref, between=functools.partial(first_tile_scores, c, qa_next))

    @pl.when(i == 0)
    def _():
        for c in range(2 * heads):
            first_tile_scores(c, qa)

    _run_far_tiles(jnp.maximum(i - 1, 0), half_step, buf_a, buf_b)

    @pl.when(i == 0)
    def _():
        last_step(i, buf_a)

    @pl.when(i % 2 == 1)
    def _():
        half_step(i - 1, buf_a, buf_b, bs_ref)
        last_step(i, buf_b)

    @pl.when((i >= 2) & (i % 2 == 0))
    def _():
        half_step(i - 2, buf_a, buf_b)
        half_step(i - 1, buf_b, buf_a, bs_ref)
        last_step(i, buf_a)

    lam = _lambda_value(lq1_ref[...], lk1_ref[...], lq2_ref[...], lk2_ref[...], lam_init)
    for h in range(heads):
        a0, a1 = acc_ref[2 * h], acc_ref[2 * h + 1]
        ot = a0[:w] / a0[w:w + 1] - lam * (a1[:w] / a1[w:w + 1])
        cols = slice(h * w, (h + 1) * w)
        o_ref[:, cols] = _subln_gate(ot.T, sg_ref[...], gate_ref[:, cols], lam_init)


def _diff_prompt_attention(qt, dk16, vt, bias_diag, bias_sub, gate, subln_g, lams, *, t, heads, lam_init):
    n = dk16.shape[0]
    nq = n // t
    w = heads * 2 * DIFF_HD
    assert DIFF_HEADS % heads == 0 and FOX_W % w == 0
    fox_blocks = FOX_W // w
    vec = lambda width: pl.BlockSpec((1, width), lambda g, i: (0, 0))
    return pl.pallas_call(
        functools.partial(_diff_prompt_body, t=t, heads=heads, lam_init=lam_init),
        grid=(DIFF_HEADS // heads, nq),
        in_specs=[pl.BlockSpec((w, t), lambda g, i: (fox_blocks + g, i)),
                  pl.BlockSpec((w, t), lambda g, i: (fox_blocks + g, jnp.minimum(i + 1, nq - 1))),
                  _resident((n, w), lambda g, i: (0, g)),
                  _resident((nq, w, t), lambda g, i: (0, fox_blocks + g, 0)),
                  pl.BlockSpec((heads, t, t), lambda g, i: (g, 0, 0)),
                  pl.BlockSpec((heads, t, t), lambda g, i: (g, 0, 0)),
                  pl.BlockSpec((t, w), lambda g, i: (i, fox_blocks + g)),
                  vec(2 * DIFF_HD), vec(DIFF_HD), vec(DIFF_HD), vec(DIFF_HD), vec(DIFF_HD)],
        out_specs=pl.BlockSpec((t, w), lambda g, i: (i, g)),
        out_shape=jax.ShapeDtypeStruct((n, DIFF_W), BF16),
        scratch_shapes=[pltpu.VMEM((2 * heads, 1, t), F32),
                        pltpu.VMEM((2 * heads, 2 * DIFF_HD + SUM_ROWS, t), F32),
                        pltpu.VMEM((2 * heads, t, t), F32), pltpu.VMEM((2 * heads, t, t), F32),
                        pltpu.VMEM((2 * heads, 1, t), F32), pltpu.VMEM((2 * heads, 1, t), F32)],
        compiler_params=_cparams(2), name="diff_prompt_attention",
    )(qt, qt, dk16, vt, bias_diag, bias_sub, gate, subln_g, *lams)


def _block_diag_queries(qb):
    lane = lax.broadcasted_iota(jnp.int32, qb.shape, 1)
    zero = jnp.zeros_like(qb)
    return jnp.concatenate([jnp.where(lane < FOX_HD, qb, zero), jnp.where(lane >= FOX_HD, qb, zero)], axis=0)


def _fox_sample_body(q_ref, kc_ref, vc_ref, kn_ref, vn_ref, bc_ref, bn_ref, gate_ref, o_ref, *, s, pairs):
    p_len = kc_ref.shape[-1]
    zrows = jnp.zeros((LANES - s, HEAD_PAIR_W), BF16)
    qi = lax.broadcasted_iota(jnp.int32, (2 * s, LANES), 0)
    ki = lax.broadcasted_iota(jnp.int32, (2 * s, LANES), 1)
    causal = ki <= jnp.where(qi >= s, qi - s, qi)
    lane = lax.broadcasted_iota(jnp.int32, (s, HEAD_PAIR_W), 1)

    def rows_bias(b, width):
        return jnp.concatenate([jnp.broadcast_to(b[0:1], (s, width)), jnp.broadcast_to(b[1:2], (s, width))], axis=0)

    def pair_bias(ref, p):
        rows = ref[0, 2 * p:2 * p + 2, :]
        for g in range(1, FOX_HEADS // (2 * pairs)):
            first = 2 * (g * pairs + p)
            rows = jnp.where(pl.program_id(1) == g, ref[0, first:first + 2, :], rows)
        return rows

    scores = []
    for p in range(pairs):
        cols = slice(p * HEAD_PAIR_W, (p + 1) * HEAD_PAIR_W)
        qbd = _block_diag_queries(q_ref[:, cols])
        kt = kc_ref[0, 2 * p:2 * p + 2].reshape(HEAD_PAIR_W, p_len).astype(BF16)
        kn = jnp.concatenate([kn_ref[:, cols].astype(BF16), zrows], axis=0)
        sc = _dot(qbd, kt) + rows_bias(pair_bias(bc_ref, p), p_len)
        sn = _nt_dot(qbd, kn) + rows_bias(pair_bias(bn_ref, p), LANES)
        scores.append((sc, jnp.where(causal, sn, NEG_INF)))
    for p, (sc, sn) in enumerate(scores):
        cols = slice(p * HEAD_PAIR_W, (p + 1) * HEAD_PAIR_W)
        vt = vc_ref[0, 2 * p:2 * p + 2].reshape(HEAD_PAIR_W, p_len).astype(BF16)
        vn = jnp.concatenate([vn_ref[:, cols].astype(BF16), zrows], axis=0)
        m = jnp.maximum(jnp.max(sc, axis=1, keepdims=True), jnp.max(sn, axis=1, keepdims=True))
        pc = jnp.exp2(sc - m)
        pn = jnp.exp2(sn - m)
        l = jnp.sum(pc, axis=1, keepdims=True) + jnp.sum(pn, axis=1, keepdims=True)
        o = (_nt_dot(pc.astype(BF16), vt) + _dot(pn.astype(BF16), vn)) / l
        out = jnp.where(lane < FOX_HD, o[:s], o[s:])
        o_ref[:, cols] = (out * gate_ref[:, cols].astype(F32)).astype(BF16)


def _fox_sample_attention(q16, cache_kt, cache_vt, fk32, fv32, bias_c, bias_n, gate, *, s, pairs):
    b, _, _, p_len = cache_kt.shape
    groups = FOX_HEADS // (2 * pairs)
    w = pairs * HEAD_PAIR_W
    blk = lambda: pl.BlockSpec((s, w), lambda bi, g: (bi, g))
    cache = lambda: pl.BlockSpec((1, 2 * pairs, FOX_HD, p_len), lambda bi, g: (bi, g, 0, 0))
    return pl.pallas_call(
        functools.partial(_fox_sample_body, s=s, pairs=pairs),
        grid=(b, groups),
        in_specs=[blk(), cache(), cache(), blk(), blk(),
                  pl.BlockSpec((1, FOX_HEADS, p_len), lambda bi, g: (bi, 0, 0)),
                  pl.BlockSpec((1, FOX_HEADS, LANES), lambda bi, g: (bi, 0, 0)),
                  blk()],
        out_specs=blk(),
        out_shape=jax.ShapeDtypeStruct((b * s, FOX_W), BF16),
        compiler_params=_cparams(2), name="fox_sample_attention",
    )(q16, cache_kt, cache_vt, fk32, fv32, bias_c, bias_n, gate)


def _diff_sample_body(q_ref, kc_ref, vc_ref, kn_ref, vn_ref, bc_ref, bn_ref, gate_ref, sg_ref, lq1_ref,
                      lk1_ref, lq2_ref, lk2_ref, o_ref, m_ref, l_ref, acc_ref, *, s, pc, lam_init):
    c = pl.program_id(1)
    last = pl.num_programs(1) - 1
    w = 2 * DIFF_HD

    @pl.when(c == 0)
    def _():
        _init_state(m_ref, l_ref, acc_ref)

    qbd = [_block_diag_queries(q_ref[:, h * w:(h + 1) * w]) for h in range(DIFF_HEADS)]

    def update(h, sc, v16):
        m = m_ref[h]
        mn = jnp.maximum(m, jnp.max(sc, axis=1, keepdims=True))
        alpha = jnp.exp2(m - mn)
        p = jnp.exp2(sc - mn)
        l_ref[h] = alpha * l_ref[h] + jnp.sum(p, axis=1, keepdims=True)
        acc_ref[h] = alpha * acc_ref[h] + _dot(p.astype(BF16), v16)
        m_ref[h] = mn

    def chunk(with_bias):
        scores = [_nt_dot(qbd[h], kc_ref[0, pl.ds(h, pc, stride=DIFF_HEADS), :].astype(BF16))
                  for h in range(DIFF_HEADS)]
        for h, sc in enumerate(scores):
            if with_bias:
                sc = sc + jnp.concatenate([bc_ref[h], bc_ref[h]], axis=0)
            update(h, sc, vc_ref[0, pl.ds(h, pc, stride=DIFF_HEADS), :].astype(BF16))

    @pl.when(c != last)
    def _():
        chunk(False)

    @pl.when(c == last)
    def _():
        chunk(True)
        lam = _lambda_value(lq1_ref[...], lk1_ref[...], lq2_ref[...], lk2_ref[...], lam_init)
        zrows = jnp.zeros((LANES - s, w), BF16)
        for h in range(DIFF_HEADS):
            cols = slice(h * w, (h + 1) * w)
            new_rows = pl.ds(h, s, stride=DIFF_HEADS)
            kn = jnp.concatenate([kn_ref[new_rows, :].astype(BF16), zrows], axis=0)
            vn = jnp.concatenate([vn_ref[new_rows, :].astype(BF16), zrows], axis=0)
            sn = _nt_dot(qbd[h], kn) + jnp.concatenate([bn_ref[h], bn_ref[h]], axis=0)
            update(h, sn, vn)
            o = acc_ref[h] / l_ref[h]
            o = o[:s] - lam * o[s:]
            o_ref[:, cols] = _subln_gate(o, sg_ref[...], gate_ref[:, cols], lam_init)


def _diff_sample_attention(q16, cache_k, cache_v, dk32, dv32, bias_c, bias_n, gate, subln_g, lams, *, s, pc,
                           lam_init):
    b, rows, w = cache_k.shape
    p_len = rows // DIFF_HEADS
    assert p_len % pc == 0 and bias_c.shape == (DIFF_HEADS, s, pc)
    cache = lambda: pl.BlockSpec((1, DIFF_HEADS * pc, w), lambda bi, c: (bi, c, 0))
    out = lambda: pl.BlockSpec((s, DIFF_W), lambda bi, c: (bi, 0))
    new = lambda: pl.BlockSpec((DIFF_HEADS * s, w), lambda bi, c: (bi, 0))
    half = lambda: pl.BlockSpec((s, DIFF_W), lambda bi, c: (bi, 1))
    const = lambda shape: pl.BlockSpec(shape, lambda bi, c: (0,) * len(shape))
    return pl.pallas_call(
        functools.partial(_diff_sample_body, s=s, pc=pc, lam_init=lam_init),
        grid=(b, p_len // pc),
        in_specs=[half(), cache(), cache(), new(), new(), const(bias_c.shape), const(bias_n.shape), half(),
                  const((1, w)), const((1, DIFF_HD)), const((1, DIFF_HD)), const((1, DIFF_HD)), const((1, DIFF_HD))],
        out_specs=out(),
        out_shape=jax.ShapeDtypeStruct((b * s, DIFF_W), BF16),
        scratch_shapes=[pltpu.VMEM((DIFF_HEADS, 2 * s, 1), F32), pltpu.VMEM((DIFF_HEADS, 2 * s, 1), F32),
                        pltpu.VMEM((DIFF_HEADS, 2 * s, w), F32)],
        compiler_params=_cparams(2), name="diff_sample_attention",
    )(q16, cache_k, cache_v, dk32, dv32, bias_c, bias_n, gate, subln_g, *lams)


def _outproj_body(mf_ref, md_ref, w_ref, x_ref, g_ref, y_ref):
    out = _dot(mf_ref[...], w_ref[:FOX_W, :]) + _dot(md_ref[...], w_ref[FOX_W:, :])
    ms = jnp.mean(out * out, axis=-1, keepdims=True)
    y_ref[...] = x_ref[...] + (out * lax.rsqrt(ms + EPS)) * g_ref[...]


def _out_projection(mf, md, w_out, x2d, g_post, *, tm):
    n, d = x2d.shape
    tm = min(tm, n)
    assert n % tm == 0
    rows = lambda w: pl.BlockSpec((tm, w), lambda i: (i, 0))
    return pl.pallas_call(
        _outproj_body, grid=(n // tm,),
        in_specs=[rows(FOX_W), rows(DIFF_W), pl.BlockSpec(w_out.shape, lambda i: (0, 0)), rows(d),
                  pl.BlockSpec((1, d), lambda i: (0, 0))],
        out_specs=rows(d), out_shape=jax.ShapeDtypeStruct((n, d), F32),
        compiler_params=_cparams(1), name="out_projection",
    )(mf, md, w_out.astype(BF16), x2d, g_post.reshape(1, d))


def _t5_bucket(rel):
    half = NUM_BUCKETS // 2
    max_exact = half // 2
    ret = jnp.where(rel > 0, half, 0)
    n = jnp.abs(rel)
    nf = jnp.maximum(n, 1).astype(jnp.float32)
    large = max_exact + (jnp.log(nf / max_exact) / math.log(MAX_DISTANCE / max_exact)
                         * (half - max_exact)).astype(jnp.int32)
    large = jnp.minimum(large, half - 1)
    return ret + jnp.where(n < max_exact, n, large)


def _rel_bias_of(rel_bias, rel):
    return rel_bias[_t5_bucket(rel)].astype(F32).T


def _toeplitz(g, rows, cols):
    h = g.shape[0]
    period = rows + cols
    v = jnp.concatenate([g[:, rows - 1:], jnp.zeros((h, 1), g.dtype), g[:, :rows - 1]], axis=1)
    flat = jnp.tile(v, (1, rows))[:, :rows * (period - 1)]
    return flat.reshape(h, rows, period - 1)[:, :, :cols]


def _chunk_mask(bias, row_pos, col_pos, rows_are_keys):
    r = (row_pos // CHUNK)[:, None]
    c = (col_pos // CHUNK)[None, :]
    mask = (r <= c) if rows_are_keys else (c <= r)
    return jnp.where(mask[None], bias * LOG2E, NEG_INF)


def _prompt_layer(x, g_pre, w_in, b_f, lams, subln_g, w_out, g_post, rel_bias, layer_idx, *, t=256, tm=512,
                  tm_out=1024, fox_pairs=4, diff_heads=4):
    b, s, d = x.shape
    assert b == 1 and s % t == 0 and s // LANES >= 1
    x2d = x.reshape(s, d)
    (fk, fv, dk, dv, logf, gate, qt, k16, dk16, vt) = _in_projection(
        x2d, g_pre, w_in, b_f, transposed=True, tm=tm, tk=t)

    rows = s // LANES
    _, hi, mid, lo = _prefix_sums(logf.T.reshape(FOX_HEADS * rows, LANES), seg=LANES, group=rows, scale=LOG2E)
    pieces = jnp.stack([hi, mid, lo], axis=0).reshape(PIECES, FOX_HEADS, s)
    kext = jnp.concatenate([
        (-pieces).transpose(2, 1, 0).reshape(s, ONES_AT), jnp.ones((s, PIECES), BF16),
        jnp.zeros((s, LANES - ONES_AT - PIECES), BF16)], axis=1)
    sel = (jnp.arange(ONES_AT)[None, :] // PIECES == jnp.arange(FOX_HEADS)[:, None]).astype(BF16)
    qext = jnp.concatenate([
        jnp.broadcast_to(sel[:, :, None], (FOX_HEADS, ONES_AT, s)), pieces.transpose(1, 0, 2),
        jnp.zeros((FOX_HEADS, QEXT_ROWS - ONES_AT - PIECES, s), BF16)], axis=1)

    mf = _fox_prompt_attention(qt, qext, k16, kext, vt, gate, t=t, pairs=fox_pairs)

    lam_init = 0.8 - 0.6 * math.exp(-0.3 * layer_idx)
    assert t >= MAX_DISTANCE
    pos = jnp.arange(t, dtype=jnp.int32)
    off = jnp.arange(-(t - 1), t, dtype=jnp.int32)
    far = _rel_bias_of(rel_bias, jnp.full((1,), -2 * t, jnp.int32))
    bias_diag = _chunk_mask(_toeplitz(_rel_bias_of(rel_bias, -off) - far, t, t), pos, pos, True)
    bias_sub = _chunk_mask(_toeplitz(_rel_bias_of(rel_bias, -off - t) - far, t, t), pos, pos + t, True)
    md = _diff_prompt_attention(qt, dk16, vt, bias_diag, bias_sub, gate, subln_g.reshape(1, -1), lams,
                                t=t, heads=diff_heads, lam_init=lam_init)

    y = _out_projection(mf, md, w_out, x2d, g_post, tm=tm_out)
    return y.reshape(b, s, d), (fk, fv, logf, dk, dv)


def _sample_layer(x, past, g_pre, w_in, b_f, lams, subln_g, w_out, g_post, rel_bias, layer_idx, *, tm=512,
                  fox_pairs=2, diff_chunk=2048):
    b, s, d = x.shape
    pk, pv, plogf, pdk, pdv = past
    p_len = pk.shape[1]
    assert s <= LANES and LANES % s == 0 and p_len % LANES == 0
    x2d = x.reshape(b * s, d)
    fk, fv, dk, dv, logf, gate, q16 = _in_projection(x2d, g_pre, w_in, b_f, transposed=False, tm=tm, tk=tm)

    rows = p_len // LANES
    cum_c, _, _, _ = _prefix_sums(plogf.transpose(0, 2, 1).reshape(b * FOX_HEADS * rows, LANES), seg=LANES,
                                  group=rows, scale=1.0)
    cum_c = cum_c.reshape(b, FOX_HEADS, p_len)
    bias_c = (cum_c[:, :, -1:] - cum_c) * LOG2E
    cum_n, _, _, _ = _prefix_sums(
        logf.reshape(b, s, FOX_HEADS).transpose(0, 2, 1).reshape(b * FOX_HEADS * s // LANES, LANES),
        seg=s, group=1, scale=1.0)
    bias_n = jnp.concatenate([-cum_n.reshape(b, FOX_HEADS, s) * LOG2E,
                              jnp.full((b, FOX_HEADS, LANES - s), NEG_INF, F32)], axis=-1)
    mf = _fox_sample_attention(q16, pk.transpose(0, 2, 3, 1), pv.transpose(0, 2, 3, 1), fk, fv, bias_c, bias_n,
                               gate, s=s, pairs=fox_pairs)

    lam_init = 0.8 - 0.6 * math.exp(-0.3 * layer_idx)
    pc = min(diff_chunk, p_len)
    assert pc >= MAX_DISTANCE + s
    q_pos = p_len + jnp.arange(s, dtype=jnp.int32)
    far = _rel_bias_of(rel_bias, jnp.full((1,), -2 * p_len, jnp.int32))
    off = jnp.arange(-(s - 1), pc, dtype=jnp.int32)
    bd_c = _chunk_mask(_toeplitz(_rel_bias_of(rel_bias, off - pc) - far, s, pc), q_pos,
                       p_len - pc + jnp.arange(pc, dtype=jnp.int32), False)
    off = jnp.arange(-(s - 1), s, dtype=jnp.int32)
    bd_n = _chunk_mask(_toeplitz(_rel_bias_of(rel_bias, off) - far, s, s), q_pos, q_pos, False)
    bd_n = jnp.concatenate([bd_n, jnp.full((DIFF_HEADS, s, LANES - s), NEG_INF, F32)], axis=-1)
    md = _diff_sample_attention(q16, pdk.reshape(b, p_len * DIFF_HEADS, 2 * DIFF_HD),
                                pdv.reshape(b, p_len * DIFF_HEADS, 2 * DIFF_HD), dk, dv, bd_c, bd_n, gate,
                                subln_g.reshape(1, -1), lams, s=s, pc=pc, lam_init=lam_init)

    y = _out_projection(mf, md, w_out, x2d, g_post, tm=tm)
    return y.reshape(b, s, d), (fk, fv, logf, dk, dv)


def kernel(x_prompt, x_sample, cache_fox_k, cache_fox_v, cache_fox_logf, cache_diff_k, cache_diff_v,
           norm_pre_g, w_in, forget_bias, lambda_q1, lambda_k1, lambda_q2, lambda_k2, subln_g, w_out,
           norm_post_g, rel_bias):
    depth = w_in.shape[0]
    y_p, y_s = x_prompt, x_sample
    rows_p, rows_s = [], []
    for l in range(depth):
        lams = tuple(a[l].reshape(1, -1) for a in (lambda_q1, lambda_k1, lambda_q2, lambda_k2))
        params = (norm_pre_g[l], w_in[l], forget_bias[l], lams, subln_g[l], w_out[l], norm_post_g[l], rel_bias, l)
        y_p, new_p = _prompt_layer(y_p, *params)
        past = (cache_fox_k[l], cache_fox_v[l], cache_fox_logf[l], cache_diff_k[l], cache_diff_v[l])
        y_s, new_s = _sample_layer(y_s, past, *params)
        rows_p.append(new_p)
        rows_s.append(new_s)

    bp, sp = x_prompt.shape[:2]
    bs, ss = x_sample.shape[:2]

    def stack(rows, idx, shape):
        return jnp.stack([r[idx].reshape(shape) for r in rows], axis=0)

    outs = [y_p, y_s]
    for rows, (b, s) in ((rows_p, (bp, sp)), (rows_s, (bs, ss))):
        outs += [stack(rows, 0, (b, s, FOX_HEADS, FOX_HD)), stack(rows, 1, (b, s, FOX_HEADS, FOX_HD)),
                 stack(rows, 2, (b, s, FOX_HEADS)), stack(rows, 3, (b, s, DIFF_HEADS, 2 * DIFF_HD)),
                 stack(rows, 4, (b, s, DIFF_HEADS, 2 * DIFF_HD))]
    return tuple(outs)
```

```python
import functools
import math

import jax
import jax.numpy as jnp
import numpy as np
from jax import lax
from jax.experimental import pallas as pl
from jax.experimental.pallas import tpu as pltpu

F32 = jnp.float32
BF16 = jnp.bfloat16

FOX_HEADS = 8
FOX_HD = 64
DIFF_HEADS = 4
DIFF_HD = 64
FOX_W = FOX_HEADS * FOX_HD
DIFF_W = DIFF_HEADS * 2 * DIFF_HD
CHUNK = 64
NUM_BUCKETS = 32
MAX_DISTANCE = 128
EPS = 1e-6
NEG_INF = -1e30
LOG2E = 1.4426950408889634

LANES = 128
HEAD_PAIR_W = 2 * FOX_HD
VMEM_LIMIT_BYTES = 56 * 1024 * 1024

PIECES = 3
ONES_AT = PIECES * FOX_HEADS
QEXT_ROWS = 32


def _cparams(n_axes):
    return pltpu.CompilerParams(
        dimension_semantics=("arbitrary",) * n_axes,
        vmem_limit_bytes=VMEM_LIMIT_BYTES,
    )


def _split3(a):
    hi = a.astype(BF16)
    r1 = a - hi.astype(F32)
    mid = r1.astype(BF16)
    lo = (r1 - mid.astype(F32)).astype(BF16)
    return hi, mid, lo


def _nt_dot(a, b):
    return lax.dot_general(a, b, (((1,), (1,)), ((), ())), preferred_element_type=F32)


def _dot(a, b):
    return jnp.dot(a, b, preferred_element_type=F32)


def _inproj_body(x_ref, g_ref, wkv_ref, wg_ref, wff_ref, bf_ref, wq_ref,
                 fk_ref, fv_ref, dk_ref, dv_ref, logf_ref, gate_ref, q_ref, *rest,
                 transposed, tk, qscale):
    x = x_ref[...]
    ms = jnp.mean(x * x, axis=-1, keepdims=True)
    h = (x * lax.rsqrt(ms + EPS)) * g_ref[...]
    hb = h.astype(BF16)

    fk = _dot(hb, wkv_ref[:, 0 * FOX_W:1 * FOX_W])
    fk_ref[...] = fk
    fv = _dot(hb, wkv_ref[:, 1 * FOX_W:2 * FOX_W])
    fv_ref[...] = fv
    dk = _dot(hb, wkv_ref[:, 2 * FOX_W:3 * FOX_W])
    dv = _dot(hb, wkv_ref[:, 3 * FOX_W:4 * FOX_W])
    tm = dk.shape[0]
    for hd in range(DIFF_HEADS):
        cols = slice(hd * 2 * DIFF_HD, (hd + 1) * 2 * DIFF_HD)
        dk_ref[pl.ds(hd, tm, stride=DIFF_HEADS), :] = dk[:, cols]
        dv_ref[pl.ds(hd, tm, stride=DIFF_HEADS), :] = dv[:, cols]

    ff = _dot(hb, wff_ref[...])
    logf = jax.nn.log_sigmoid(ff + bf_ref[...])
    logf_ref[...] = logf[:, :FOX_HEADS]

    gates = _dot(hb, wg_ref[...])
    gate_ref[...] = (gates * jax.nn.sigmoid(gates)).astype(BF16)

    q = _dot(hb, wq_ref[...]) * qscale
    if transposed:
        k16_ref, dk16_ref, vt_ref = rest
        k16_ref[...] = fk.astype(BF16)
        dk16_ref[...] = dk.astype(BF16)
        q_ref[...] = q.T.astype(BF16)
        vt = jnp.concatenate([fv, dv], axis=1).T.astype(BF16)
        for c in range(vt.shape[1] // tk):
            vt_ref[c] = vt[:, c * tk:(c + 1) * tk]
    else:
        q_ref[...] = q.astype(BF16)


def _in_projection(x2d, g_pre, w_in, b_f, *, transposed, tm, tk):
    n, d = x2d.shape
    tm = min(tm, n)
    assert n % tm == 0 and (not transposed or tm % tk == 0)
    qscale = FOX_HD ** -0.5 * LOG2E
    o = np.cumsum([0, FOX_W, FOX_W, FOX_W, FOX_HEADS, FOX_W, DIFF_W, DIFF_W, DIFF_W, DIFF_W])
    w_fq, w_fk, w_fv, w_ff, w_fg, w_dq, w_dk, w_dv, w_dg = [w_in[:, o[i]:o[i + 1]] for i in range(9)]
    wkv = jnp.concatenate([w_fk, w_fv, w_dk, w_dv], axis=1).astype(BF16)
    wg = jnp.concatenate([w_fg, w_dg], axis=1).astype(BF16)
    wff = jnp.pad(w_ff, ((0, 0), (0, LANES - FOX_HEADS))).astype(BF16)
    bfp = jnp.pad(b_f.reshape(1, FOX_HEADS), ((0, 0), (0, LANES - FOX_HEADS)))
    wq = jnp.concatenate([w_fq, w_dq], axis=1).astype(BF16)

    const = lambda shape: pl.BlockSpec(shape, lambda i: (0,) * len(shape))
    rows = lambda w: pl.BlockSpec((tm, w), lambda i: (i, 0))
    in_specs = [rows(d), const((1, d)), const(wkv.shape), const(wg.shape), const(wff.shape),
                const(bfp.shape), const(wq.shape)]
    by_head = jax.ShapeDtypeStruct((DIFF_HEADS * n, 2 * DIFF_HD), F32)
    by_head_spec = pl.BlockSpec((DIFF_HEADS * tm, 2 * DIFF_HD), lambda i: (i, 0))
    out_shape = [jax.ShapeDtypeStruct((n, FOX_W), F32)] * 2 + [by_head] * 2
    out_shape += [jax.ShapeDtypeStruct((n, FOX_HEADS), F32), jax.ShapeDtypeStruct((n, FOX_W + DIFF_W), BF16)]
    out_specs = [rows(FOX_W), rows(FOX_W), by_head_spec, by_head_spec, rows(FOX_HEADS), rows(FOX_W + DIFF_W)]
    if transposed:
        out_shape += [jax.ShapeDtypeStruct((FOX_W + DIFF_W, n), BF16),
                      jax.ShapeDtypeStruct((n, FOX_W), BF16), jax.ShapeDtypeStruct((n, DIFF_W), BF16),
                      jax.ShapeDtypeStruct((n // tk, FOX_W + DIFF_W, tk), BF16)]
        out_specs += [pl.BlockSpec((FOX_W + DIFF_W, tm), lambda i: (0, i)), rows(FOX_W), rows(DIFF_W),
                      pl.BlockSpec((tm // tk, FOX_W + DIFF_W, tk), lambda i: (i, 0, 0))]
    else:
        out_shape += [jax.ShapeDtypeStruct((n, FOX_W + DIFF_W), BF16)]
        out_specs += [rows(FOX_W + DIFF_W)]
    return pl.pallas_call(
        functools.partial(_inproj_body, transposed=transposed, tk=tk, qscale=qscale),
        grid=(n // tm,), in_specs=in_specs, out_specs=out_specs, out_shape=out_shape,
        compiler_params=_cparams(1), name="in_projection_t" if transposed else "in_projection_r",
    )(x2d, g_pre.reshape(1, d), wkv, wg, wff, bfp, wq)


def _cumsum_body(x_ref, cum_ref, hi_ref, mid_ref, lo_ref, *, seg, group, scale):
    x = x_ref[...]
    r = x.shape[0]
    a = lax.broadcasted_iota(jnp.int32, (LANES, LANES), 0)
    b = lax.broadcasted_iota(jnp.int32, (LANES, LANES), 1)
    sh = int(math.log2(seg))
    upper = ((a <= b) & ((a >> sh) == (b >> sh))).astype(BF16)
    pieces = _split3(x)
    w = sum(_dot(p, upper) for p in pieces)
    if group > 1:
        ones = jnp.ones((LANES, LANES), BF16)
        tot = sum(_dot(p, ones) for p in pieces)
        ri = lax.broadcasted_iota(jnp.int32, (r, r), 0)
        ci = lax.broadcasted_iota(jnp.int32, (r, r), 1)
        gsh = int(math.log2(group))
        lower = ((ci < ri) & ((ri >> gsh) == (ci >> gsh))).astype(BF16)
        w = w + sum(_dot(lower, p) for p in _split3(tot))
    cum_ref[...] = w
    hi, mid, lo = _split3(w * scale)
    hi_ref[...] = hi
    mid_ref[...] = mid
    lo_ref[...] = lo


PREFIX_ROWS_PER_STEP = 1024


def _prefix_sums(x2d, *, seg, group, scale):
    r = x2d.shape[0]
    rb = min(max(PREFIX_ROWS_PER_STEP, group), r)
    assert r % rb == 0 and rb % group == 0 and (seg == LANES or group == 1)
    assert seg & (seg - 1) == 0 and group & (group - 1) == 0
    spec = pl.BlockSpec((rb, LANES), lambda i: (i, 0))
    return pl.pallas_call(
        functools.partial(_cumsum_body, seg=seg, group=group, scale=scale),
        grid=(r // rb,), in_specs=[spec], out_specs=[spec] * 4,
        out_shape=[jax.ShapeDtypeStruct((r, LANES), F32)] + [jax.ShapeDtypeStruct((r, LANES), BF16)] * 3,
        compiler_params=_cparams(1), name="prefix_sums",
    )(x2d)


SUM_ROWS = 16
SCORES_AHEAD = 2
FAR_UNROLL = 8


def _with_sum_rows(vt):
    row = lax.broadcasted_iota(jnp.int32, (SUM_ROWS, vt.shape[1]), 0)
    return jnp.concatenate([vt, jnp.where(row == 0, 1.0, 0.0).astype(BF16)], axis=0)


def _store_scores(buf, h, s):
    buf[0][h] = s
    buf[1][h] = jnp.max(s, axis=0, keepdims=True)


def _online_update_t(s, tile_max, h, m_ref, acc_ref, vt_aug):
    m = m_ref[h]
    mn = jnp.maximum(m, tile_max)
    alpha = jnp.exp2(m - mn)
    p = jnp.exp2(s - mn)
    acc_ref[h] = alpha * acc_ref[h] + _dot(vt_aug, p.astype(BF16))
    m_ref[h] = mn


def _run_far_tiles(n, half_step, buf_a, buf_b):
    def unrolled(base, count):
        bufs = (buf_a, buf_b)
        for u in range(count):
            half_step(base + u, bufs[u % 2], bufs[(u + 1) % 2])

    def body(jj, c):
        unrolled(FAR_UNROLL * jj, FAR_UNROLL)
        return c

    lax.fori_loop(0, n // FAR_UNROLL, body, 0)
    for count in range(FAR_UNROLL - 2, 0, -2):
        @pl.when(n % FAR_UNROLL // 2 * 2 == count)
        def _():
            unrolled(n // FAR_UNROLL * FAR_UNROLL, count)


def _init_state(m_ref, l_ref, acc_ref):
    m_ref[...] = jnp.full(m_ref.shape, NEG_INF, F32)
    if l_ref is not None:
        l_ref[...] = jnp.zeros(l_ref.shape, F32)
    acc_ref[...] = jnp.zeros(acc_ref.shape, F32)


def _resident(shape, index_map):
    return pl.BlockSpec(shape, index_map, pipeline_mode=pl.Buffered(1))


def _fox_prompt_body(q_ref, qx_ref, qn_ref, qxn_ref, k_ref, kx_ref, vt_ref, gate_ref, o_ref, m_ref, acc_ref,
                     sa_ref, sb_ref, ma_ref, mb_ref, *, t, pairs):
    i = pl.program_id(1)
    heads = 2 * pairs
    buf_a, buf_b = (sa_ref, ma_ref), (sb_ref, mb_ref)
    _init_state(m_ref, None, acc_ref)
    z64 = jnp.zeros((FOX_HD, t), BF16)
    zpad = jnp.zeros((LANES - QEXT_ROWS, t), BF16)

    def augmented_queries(q_blk_ref, qx_blk_ref):
        out = []
        for p in range(pairs):
            q = q_blk_ref[p * HEAD_PAIR_W:(p + 1) * HEAD_PAIR_W, :]
            out.append(jnp.concatenate([q[:FOX_HD], z64, qx_blk_ref[2 * p], zpad], axis=0))
            out.append(jnp.concatenate([z64, q[FOX_HD:], qx_blk_ref[2 * p + 1], zpad], axis=0))
        return out

    qa = augmented_queries(q_ref, qx_ref)
    qa_next = augmented_queries(qn_ref, qxn_ref)

    def keys(j, p):
        off = pl.multiple_of(j * t, t)
        return jnp.concatenate([k_ref[pl.ds(off, t), p * HEAD_PAIR_W:(p + 1) * HEAD_PAIR_W],
                                kx_ref[pl.ds(off, t), :]], axis=1)

    def consume(j, cur, h, diagonal, between=None):
        s = cur[0][h]
        if between is not None:
            between()
        if diagonal:
            kk = lax.broadcasted_iota(jnp.int32, (t, t), 0)
            qq = lax.broadcasted_iota(jnp.int32, (t, t), 1)
            s = jnp.where(kk <= qq, s, NEG_INF)
            tile_max = jnp.max(s, axis=0, keepdims=True)
        else:
            tile_max = cur[1][h]
        _online_update_t(s, tile_max, h, m_ref, acc_ref,
                         _with_sum_rows(vt_ref[j, h * FOX_HD:(h + 1) * FOX_HD, :]))

    def half_step(j, cur, nxt):
        ka = {}

        def prefetch(h):
            if h // 2 not in ka:
                ka[h // 2] = keys(j + 1, h // 2)
            _store_scores(nxt, h, _dot(ka[h // 2], qa[h]))

        for h in range(min(SCORES_AHEAD, heads)):
            prefetch(h)
        for h in range(heads):
            consume(j, cur, h, False)
            if h + SCORES_AHEAD < heads:
                prefetch(h + SCORES_AHEAD)

    def first_tile_scores(h, queries):
        _store_scores(buf_a, h, _dot(keys(0, h // 2), queries[h]))

    def last_step(j, cur):
        for h in range(heads):
            consume(j, cur, h, True, between=functools.partial(first_tile_scores, h, qa_next))

    @pl.when(i == 0)
    def _():
        for h in range(heads):
            first_tile_scores(h, qa)

    _run_far_tiles(i, half_step, buf_a, buf_b)

    @pl.when(i % 2 == 0)
    def _():
        last_step(i, buf_a)

    @pl.when(i % 2 == 1)
    def _():
        half_step(i - 1, buf_a, buf_b)
        last_step(i, buf_b)

    for p in range(pairs):
        a0, a1 = acc_ref[2 * p], acc_ref[2 * p + 1]
        ot = jnp.concatenate([a0[:FOX_HD] / a0[FOX_HD:FOX_HD + 1], a1[:FOX_HD] / a1[FOX_HD:FOX_HD + 1]], axis=0)
        cols = slice(p * HEAD_PAIR_W, (p + 1) * HEAD_PAIR_W)
        o_ref[:, cols] = (ot.T * gate_ref[:, cols].astype(F32)).astype(BF16)


def _fox_prompt_attention(qt, qext, k16, kext, vt, gate, *, t, pairs):
    n = k16.shape[0]
    nq = n // t
    w = pairs * HEAD_PAIR_W
    heads = 2 * pairs
    assert (FOX_HEADS // 2) % pairs == 0
    return pl.pallas_call(
        functools.partial(_fox_prompt_body, t=t, pairs=pairs),
        grid=(FOX_HEADS // heads, nq),
        in_specs=[pl.BlockSpec((w, t), lambda g, i: (g, i)),
                  pl.BlockSpec((heads, QEXT_ROWS, t), lambda g, i: (g, 0, i)),
                  pl.BlockSpec((w, t), lambda g, i: (g, jnp.minimum(i + 1, nq - 1))),
                  pl.BlockSpec((heads, QEXT_ROWS, t), lambda g, i: (g, 0, jnp.minimum(i + 1, nq - 1))),
                  _resident((n, w), lambda g, i: (0, g)),
                  _resident((n, LANES), lambda g, i: (0, 0)),
                  _resident((nq, w, t), lambda g, i: (0, g, 0)),
                  pl.BlockSpec((t, w), lambda g, i: (i, g))],
        out_specs=pl.BlockSpec((t, w), lambda g, i: (i, g)),
        out_shape=jax.ShapeDtypeStruct((n, FOX_W), BF16),
        scratch_shapes=[pltpu.VMEM((heads, 1, t), F32), pltpu.VMEM((heads, FOX_HD + SUM_ROWS, t), F32),
                        pltpu.VMEM((heads, t, t), F32), pltpu.VMEM((heads, t, t), F32),
                        pltpu.VMEM((heads, 1, t), F32), pltpu.VMEM((heads, 1, t), F32)],
        compiler_params=_cparams(2), name="fox_prompt_attention",
    )(qt, qext, qt, qext, k16, kext, vt, gate)


def _lambda_value(lq1, lk1, lq2, lk2, lam_init):
    a = jnp.sum(lq1 * lk1, axis=-1, keepdims=True)
    b = jnp.sum(lq2 * lk2, axis=-1, keepdims=True)
    return jnp.exp(a) - jnp.exp(b) + lam_init


def _subln_gate(y, g, gate, lam_init):
    ms = jnp.mean(y * y, axis=-1, keepdims=True)
    yn = (y * lax.rsqrt(ms + EPS)) * g
    return ((yn * (1.0 - lam_init)) * gate.astype(F32)).astype(BF16)


def _diff_prompt_body(q_ref, qn_ref, k_ref, vt_ref, bd_ref, bs_ref, gate_ref, sg_ref, lq1_ref, lk1_ref, lq2_ref,
                      lk2_ref, o_ref, m_ref, acc_ref, sa_ref, sb_ref, ma_ref, mb_ref, *, t, heads, lam_init):
    i = pl.program_id(1)
    w = 2 * DIFF_HD
    buf_a, buf_b = (sa_ref, ma_ref), (sb_ref, mb_ref)
    _init_state(m_ref, None, acc_ref)
    z64 = jnp.zeros((DIFF_HD, t), BF16)

    def padded_queries(q_blk_ref):
        out = []
        for h in range(heads):
            q = q_blk_ref[h * w:(h + 1) * w, :]
            out.append(jnp.concatenate([q[:DIFF_HD], z64], axis=0))
            out.append(jnp.concatenate([z64, q[DIFF_HD:]], axis=0))
        return out

    qa = padded_queries(q_ref)
    qa_next = padded_queries(qn_ref)

    def keys(j, h):
        return k_ref[pl.ds(pl.multiple_of(j * t, t), t), h * w:(h + 1) * w]

    def consume(j, cur, c, bias_ref, between=None):
        h = c // 2
        s = cur[0][c]
        if between is not None:
            between()
        if bias_ref is not None:
            s = s + bias_ref[h]
            tile_max = jnp.max(s, axis=0, keepdims=True)
        else:
            tile_max = cur[1][c]
        _online_update_t(s, tile_max, c, m_ref, acc_ref, _with_sum_rows(vt_ref[j, h * w:(h + 1) * w, :]))

    def half_step(j, cur, nxt, bias_ref=None):
        kk = {}

        def prefetch(c):
            if c // 2 not in kk:
                kk[c // 2] = keys(j + 1, c // 2)
            _store_scores(nxt, c, _dot(kk[c // 2], qa[c]))

        for c in range(min(SCORES_AHEAD, 2 * heads)):
            prefetch(c)
        for c in range(2 * heads):
            consume(j, cur, c, bias_ref)
            if c + SCORES_AHEAD < 2 * heads:
                prefetch(c + SCORES_AHEAD)

    def first_tile_scores(c, queries):
        _store_scores(buf_a, c, _dot(keys(0, c // 2), queries[c]))

    def last_step(j, cur):
        for c in range(2 * heads):
            consume(j, cur, c, bd_ref, between=functools.partial(first_tile_scores, c, qa_next))

    @pl.when(i == 0)
    def _():
        for c in range(2 * heads):
            first_tile_scores(c, qa)

    _run_far_tiles(jnp.maximum(i - 1, 0), half_step, buf_a, buf_b)

    @pl.when(i == 0)
    def _():
        last_step(i, buf_a)

    @pl.when(i % 2 == 1)
    def _():
        half_step(i - 1, buf_a, buf_b, bs_ref)
        last_step(i, buf_b)

    @pl.when((i >= 2) & (i % 2 == 0))
    def _():
        half_step(i - 2, buf_a, buf_b)
        half_step(i - 1, buf_b, buf_a, bs_ref)
        last_step(i, buf_a)

    lam = _lambda_value(lq1_ref[...], lk1_ref[...], lq2_ref[...], lk2_ref[...], lam_init)
    for h in range(heads):
        a0, a1 = acc_ref[2 * h], acc_ref[2 * h + 1]
        ot = a0[:w] / a0[w:w + 1] - lam * (a1[:w] / a1[w:w + 1])
        cols = slice(h * w, (h + 1) * w)
        o_ref[:, cols] = _subln_gate(ot.T, sg_ref[...], gate_ref[:, cols], lam_init)


def _diff_prompt_attention(qt, dk16, vt, bias_diag, bias_sub, gate, subln_g, lams, *, t, heads, lam_init):
    n = dk16.shape[0]
    nq = n // t
    w = heads * 2 * DIFF_HD
    assert DIFF_HEADS % heads == 0 and FOX_W % w == 0
    fox_blocks = FOX_W // w
    vec = lambda width: pl.BlockSpec((1, width), lambda g, i: (0, 0))
    return pl.pallas_call(
        functools.partial(_diff_prompt_body, t=t, heads=heads, lam_init=lam_init),
        grid=(DIFF_HEADS // heads, nq),
        in_specs=[pl.BlockSpec((w, t), lambda g, i: (fox_blocks + g, i)),
                  pl.BlockSpec((w, t), lambda g, i: (fox_blocks + g, jnp.minimum(i + 1, nq - 1))),
                  _resident((n, w), lambda g, i: (0, g)),
                  _resident((nq, w, t), lambda g, i: (0, fox_blocks + g, 0)),
                  pl.BlockSpec((heads, t, t), lambda g, i: (g, 0, 0)),
                  pl.BlockSpec((heads, t, t), lambda g, i: (g, 0, 0)),
                  pl.BlockSpec((t, w), lambda g, i: (i, fox_blocks + g)),
                  vec(2 * DIFF_HD), vec(DIFF_HD), vec(DIFF_HD), vec(DIFF_HD), vec(DIFF_HD)],
        out_specs=pl.BlockSpec((t, w), lambda g, i: (i, g)),
        out_shape=jax.ShapeDtypeStruct((n, DIFF_W), BF16),
        scratch_shapes=[pltpu.VMEM((2 * heads, 1, t), F32),
                        pltpu.VMEM((2 * heads, 2 * DIFF_HD + SUM_ROWS, t), F32),
                        pltpu.VMEM((2 * heads, t, t), F32), pltpu.VMEM((2 * heads, t, t), F32),
                        pltpu.VMEM((2 * heads, 1, t), F32), pltpu.VMEM((2 * heads, 1, t), F32)],
        compiler_params=_cparams(2), name="diff_prompt_attention",
    )(qt, qt, dk16, vt, bias_diag, bias_sub, gate, subln_g, *lams)


def _block_diag_queries(qb):
    lane = lax.broadcasted_iota(jnp.int32, qb.shape, 1)
    zero = jnp.zeros_like(qb)
    return jnp.concatenate([jnp.where(lane < FOX_HD, qb, zero), jnp.where(lane >= FOX_HD, qb, zero)], axis=0)


def _fox_sample_body(q_ref, kc_ref, vc_ref, kn_ref, vn_ref, bc_ref, bn_ref, gate_ref, o_ref, *, s, pairs):
    p_len = kc_ref.shape[-1]
    zrows = jnp.zeros((LANES - s, HEAD_PAIR_W), BF16)
    qi = lax.broadcasted_iota(jnp.int32, (2 * s, LANES), 0)
    ki = lax.broadcasted_iota(jnp.int32, (2 * s, LANES), 1)
    causal = ki <= jnp.where(qi >= s, qi - s, qi)
    lane = lax.broadcasted_iota(jnp.int32, (s, HEAD_PAIR_W), 1)

    def rows_bias(b, width):
        return jnp.concatenate([jnp.broadcast_to(b[0:1], (s, width)), jnp.broadcast_to(b[1:2], (s, width))], axis=0)

    def pair_bias(ref, p):
        rows = ref[0, 2 * p:2 * p + 2, :]
        for g in range(1, FOX_HEADS // (2 * pairs)):
            first = 2 * (g * pairs + p)
            rows = jnp.where(pl.program_id(1) == g, ref[0, first:first + 2, :], rows)
        return rows

    scores = []
    for p in range(pairs):
        cols = slice(p * HEAD_PAIR_W, (p + 1) * HEAD_PAIR_W)
        qbd = _block_diag_queries(q_ref[:, cols])
        kt = kc_ref[0, 2 * p:2 * p + 2].reshape(HEAD_PAIR_W, p_len).astype(BF16)
        kn = jnp.concatenate([kn_ref[:, cols].astype(BF16), zrows], axis=0)
        sc = _dot(qbd, kt) + rows_bias(pair_bias(bc_ref, p), p_len)
        sn = _nt_dot(qbd, kn) + rows_bias(pair_bias(bn_ref, p), LANES)
        scores.append((sc, jnp.where(causal, sn, NEG_INF)))
    for p, (sc, sn) in enumerate(scores):
        cols = slice(p * HEAD_PAIR_W, (p + 1) * HEAD_PAIR_W)
        vt = vc_ref[0, 2 * p:2 * p + 2].reshape(HEAD_PAIR_W, p_len).astype(BF16)
        vn = jnp.concatenate([vn_ref[:, cols].astype(BF16), zrows], axis=0)
        m = jnp.maximum(jnp.max(sc, axis=1, keepdims=True), jnp.max(sn, axis=1, keepdims=True))
        pc = jnp.exp2(sc - m)
        pn = jnp.exp2(sn - m)
        l = jnp.sum(pc, axis=1, keepdims=True) + jnp.sum(pn, axis=1, keepdims=True)
        o = (_nt_dot(pc.astype(BF16), vt) + _dot(pn.astype(BF16), vn)) / l
        out = jnp.where(lane < FOX_HD, o[:s], o[s:])
        o_ref[:, cols] = (out * gate_ref[:, cols].astype(F32)).astype(BF16)


def _fox_sample_attention(q16, cache_kt, cache_vt, fk32, fv32, bias_c, bias_n, gate, *, s, pairs):
    b, _, _, p_len = cache_kt.shape
    groups = FOX_HEADS // (2 * pairs)
    w = pairs * HEAD_PAIR_W
    blk = lambda: pl.BlockSpec((s, w), lambda bi, g: (bi, g))
    cache = lambda: pl.BlockSpec((1, 2 * pairs, FOX_HD, p_len), lambda bi, g: (bi, g, 0, 0))
    return pl.pallas_call(
        functools.partial(_fox_sample_body, s=s, pairs=pairs),
        grid=(b, groups),
        in_specs=[blk(), cache(), cache(), blk(), blk(),
                  pl.BlockSpec((1, FOX_HEADS, p_len), lambda bi, g: (bi, 0, 0)),
                  pl.BlockSpec((1, FOX_HEADS, LANES), lambda bi, g: (bi, 0, 0)),
                  blk()],
        out_specs=blk(),
        out_shape=jax.ShapeDtypeStruct((b * s, FOX_W), BF16),
        compiler_params=_cparams(2), name="fox_sample_attention",
    )(q16, cache_kt, cache_vt, fk32, fv32, bias_c, bias_n, gate)


def _diff_sample_body(q_ref, kc_ref, vc_ref, kn_ref, vn_ref, bc_ref, bn_ref, gate_ref, sg_ref, lq1_ref,
                      lk1_ref, lq2_ref, lk2_ref, o_ref, m_ref, l_ref, acc_ref, *, s, pc, lam_init):
    c = pl.program_id(1)
    last = pl.num_programs(1) - 1
    w = 2 * DIFF_HD

    @pl.when(c == 0)
    def _():
        _init_state(m_ref, l_ref, acc_ref)

    qbd = [_block_diag_queries(q_ref[:, h * w:(h + 1) * w]) for h in range(DIFF_HEADS)]

    def update(h, sc, v16):
        m = m_ref[h]
        mn = jnp.maximum(m, jnp.max(sc, axis=1, keepdims=True))
        alpha = jnp.exp2(m - mn)
        p = jnp.exp2(sc - mn)
        l_ref[h] = alpha * l_ref[h] + jnp.sum(p, axis=1, keepdims=True)
        acc_ref[h] = alpha * acc_ref[h] + _dot(p.astype(BF16), v16)
        m_ref[h] = mn

    def chunk(with_bias):
        scores = [_nt_dot(qbd[h], kc_ref[0, pl.ds(h, pc, stride=DIFF_HEADS), :].astype(BF16))
                  for h in range(DIFF_HEADS)]
        for h, sc in enumerate(scores):
            if with_bias:
                sc = sc + jnp.concatenate([bc_ref[h], bc_ref[h]], axis=0)
            update(h, sc, vc_ref[0, pl.ds(h, pc, stride=DIFF_HEADS), :].astype(BF16))

    @pl.when(c != last)
    def _():
        chunk(False)

    @pl.when(c == last)
    def _():
        chunk(True)
        lam = _lambda_value(lq1_ref[...], lk1_ref[...], lq2_ref[...], lk2_ref[...], lam_init)
        zrows = jnp.zeros((LANES - s, w), BF16)
        for h in range(DIFF_HEADS):
            cols = slice(h * w, (h + 1) * w)
            new_rows = pl.ds(h, s, stride=DIFF_HEADS)
            kn = jnp.concatenate([kn_ref[new_rows, :].astype(BF16), zrows], axis=0)
            vn = jnp.concatenate([vn_ref[new_rows, :].astype(BF16), zrows], axis=0)
            sn = _nt_dot(qbd[h], kn) + jnp.concatenate([bn_ref[h], bn_ref[h]], axis=0)
            update(h, sn, vn)
            o = acc_ref[h] / l_ref[h]
            o = o[:s] - lam * o[s:]
            o_ref[:, cols] = _subln_gate(o, sg_ref[...], gate_ref[:, cols], lam_init)


def _diff_sample_attention(q16, cache_k, cache_v, dk32, dv32, bias_c, bias_n, gate, subln_g, lams, *, s, pc,
                           lam_init):
    b, rows, w = cache_k.shape
    p_len = rows // DIFF_HEADS
    assert p_len % pc == 0 and bias_c.shape == (DIFF_HEADS, s, pc)
    cache = lambda: pl.BlockSpec((1, DIFF_HEADS * pc, w), lambda bi, c: (bi, c, 0))
    out = lambda: pl.BlockSpec((s, DIFF_W), lambda bi, c: (bi, 0))
    new = lambda: pl.BlockSpec((DIFF_HEADS * s, w), lambda bi, c: (bi, 0))
    half = lambda: pl.BlockSpec((s, DIFF_W), lambda bi, c: (bi, 1))
    const = lambda shape: pl.BlockSpec(shape, lambda bi, c: (0,) * len(shape))
    return pl.pallas_call(
        functools.partial(_diff_sample_body, s=s, pc=pc, lam_init=lam_init),
        grid=(b, p_len // pc),
        in_specs=[half(), cache(), cache(), new(), new(), const(bias_c.shape), const(bias_n.shape), half(),
                  const((1, w)), const((1, DIFF_HD)), const((1, DIFF_HD)), const((1, DIFF_HD)), const((1, DIFF_HD))],
        out_specs=out(),
        out_shape=jax.ShapeDtypeStruct((b * s, DIFF_W), BF16),
        scratch_shapes=[pltpu.VMEM((DIFF_HEADS, 2 * s, 1), F32), pltpu.VMEM((DIFF_HEADS, 2 * s, 1), F32),
                        pltpu.VMEM((DIFF_HEADS, 2 * s, w), F32)],
        compiler_params=_cparams(2), name="diff_sample_attention",
    )(q16, cache_k, cache_v, dk32, dv32, bias_c, bias_n, gate, subln_g, *lams)


def _outproj_body(mf_ref, md_ref, w_ref, x_ref, g_ref, y_ref):
    out = _dot(mf_ref[...], w_ref[:FOX_W, :]) + _dot(md_ref[...], w_ref[FOX_W:, :])
    ms = jnp.mean(out * out, axis=-1, keepdims=True)
    y_ref[...] = x_ref[...] + (out * lax.rsqrt(ms + EPS)) * g_ref[...]


def _out_projection(mf, md, w_out, x2d, g_post, *, tm):
    n, d = x2d.shape
    tm = min(tm, n)
    assert n % tm == 0
    rows = lambda w: pl.BlockSpec((tm, w), lambda i: (i, 0))
    return pl.pallas_call(
        _outproj_body, grid=(n // tm,),
        in_specs=[rows(FOX_W), rows(DIFF_W), pl.BlockSpec(w_out.shape, lambda i: (0, 0)), rows(d),
                  pl.BlockSpec((1, d), lambda i: (0, 0))],
        out_specs=rows(d), out_shape=jax.ShapeDtypeStruct((n, d), F32),
        compiler_params=_cparams(1), name="out_projection",
    )(mf, md, w_out.astype(BF16), x2d, g_post.reshape(1, d))


def _t5_bucket(rel):
    half = NUM_BUCKETS // 2
    max_exact = half // 2
    ret = jnp.where(rel > 0, half, 0)
    n = jnp.abs(rel)
    nf = jnp.maximum(n, 1).astype(jnp.float32)
    large = max_exact + (jnp.log(nf / max_exact) / math.log(MAX_DISTANCE / max_exact)
                         * (half - max_exact)).astype(jnp.int32)
    large = jnp.minimum(large, half - 1)
    return ret + jnp.where(n < max_exact, n, large)


def _rel_bias_of(rel_bias, rel):
    return rel_bias[_t5_bucket(rel)].astype(F32).T


def _toeplitz(g, rows, cols):
    h = g.shape[0]
    period = rows + cols
    v = jnp.concatenate([g[:, rows - 1:], jnp.zeros((h, 1), g.dtype), g[:, :rows - 1]], axis=1)
    flat = jnp.tile(v, (1, rows))[:, :rows * (period - 1)]
    return flat.reshape(h, rows, period - 1)[:, :, :cols]


def _chunk_mask(bias, row_pos, col_pos, rows_are_keys):
    r = (row_pos // CHUNK)[:, None]
    c = (col_pos // CHUNK)[None, :]
    mask = (r <= c) if rows_are_keys else (c <= r)
    return jnp.where(mask[None], bias * LOG2E, NEG_INF)


def _prompt_layer(x, g_pre, w_in, b_f, lams, subln_g, w_out, g_post, rel_bias, layer_idx, *, t=256, tm=512,
                  tm_out=1024, fox_pairs=4, diff_heads=4):
    b, s, d = x.shape
    assert b == 1 and s % t == 0 and s // LANES >= 1
    x2d = x.reshape(s, d)
    (fk, fv, dk, dv, logf, gate, qt, k16, dk16, vt) = _in_projection(
        x2d, g_pre, w_in, b_f, transposed=True, tm=tm, tk=t)

    rows = s // LANES
    _, hi, mid, lo = _prefix_sums(logf.T.reshape(FOX_HEADS * rows, LANES), seg=LANES, group=rows, scale=LOG2E)
    pieces = jnp.stack([hi, mid, lo], axis=0).reshape(PIECES, FOX_HEADS, s)
    kext = jnp.concatenate([
        (-pieces).transpose(2, 1, 0).reshape(s, ONES_AT), jnp.ones((s, PIECES), BF16),
        jnp.zeros((s, LANES - ONES_AT - PIECES), BF16)], axis=1)
    sel = (jnp.arange(ONES_AT)[None, :] // PIECES == jnp.arange(FOX_HEADS)[:, None]).astype(BF16)
    qext = jnp.concatenate([
        jnp.broadcast_to(sel[:, :, None], (FOX_HEADS, ONES_AT, s)), pieces.transpose(1, 0, 2),
        jnp.zeros((FOX_HEADS, QEXT_ROWS - ONES_AT - PIECES, s), BF16)], axis=1)

    mf = _fox_prompt_attention(qt, qext, k16, kext, vt, gate, t=t, pairs=fox_pairs)

    lam_init = 0.8 - 0.6 * math.exp(-0.3 * layer_idx)
    assert t >= MAX_DISTANCE
    pos = jnp.arange(t, dtype=jnp.int32)
    off = jnp.arange(-(t - 1), t, dtype=jnp.int32)
    far = _rel_bias_of(rel_bias, jnp.full((1,), -2 * t, jnp.int32))
    bias_diag = _chunk_mask(_toeplitz(_rel_bias_of(rel_bias, -off) - far, t, t), pos, pos, True)
    bias_sub = _chunk_mask(_toeplitz(_rel_bias_of(rel_bias, -off - t) - far, t, t), pos, pos + t, True)
    md = _diff_prompt_attention(qt, dk16, vt, bias_diag, bias_sub, gate, subln_g.reshape(1, -1), lams,
                                t=t, heads=diff_heads, lam_init=lam_init)

    y = _out_projection(mf, md, w_out, x2d, g_post, tm=tm_out)
    return y.reshape(b, s, d), (fk, fv, logf, dk, dv)


def _sample_layer(x, past, g_pre, w_in, b_f, lams, subln_g, w_out, g_post, rel_bias, layer_idx, *, tm=512,
                  fox_pairs=4, diff_chunk=2048):
    b, s, d = x.shape
    pk, pv, plogf, pdk, pdv = past
    p_len = pk.shape[1]
    assert s <= LANES and LANES % s == 0 and p_len % LANES == 0
    x2d = x.reshape(b * s, d)
    fk, fv, dk, dv, logf, gate, q16 = _in_projection(x2d, g_pre, w_in, b_f, transposed=False, tm=tm, tk=tm)

    rows = p_len // LANES
    cum_c, _, _, _ = _prefix_sums(plogf.transpose(0, 2, 1).reshape(b * FOX_HEADS * rows, LANES), seg=LANES,
                                  group=rows, scale=1.0)
    cum_c = cum_c.reshape(b, FOX_HEADS, p_len)
    bias_c = (cum_c[:, :, -1:] - cum_c) * LOG2E
    cum_n, _, _, _ = _prefix_sums(
        logf.reshape(b, s, FOX_HEADS).transpose(0, 2, 1).reshape(b * FOX_HEADS * s // LANES, LANES),
        seg=s, group=1, scale=1.0)
    bias_n = jnp.concatenate([-cum_n.reshape(b, FOX_HEADS, s) * LOG2E,
                              jnp.full((b, FOX_HEADS, LANES - s), NEG_INF, F32)], axis=-1)
    mf = _fox_sample_attention(q16, pk.transpose(0, 2, 3, 1), pv.transpose(0, 2, 3, 1), fk, fv, bias_c, bias_n,
                               gate, s=s, pairs=fox_pairs)

    lam_init = 0.8 - 0.6 * math.exp(-0.3 * layer_idx)
    pc = min(diff_chunk, p_len)
    assert pc >= MAX_DISTANCE + s
    q_pos = p_len + jnp.arange(s, dtype=jnp.int32)
    far = _rel_bias_of(rel_bias, jnp.full((1,), -2 * p_len, jnp.int32))
    off = jnp.arange(-(s - 1), pc, dtype=jnp.int32)
    bd_c = _chunk_mask(_toeplitz(_rel_bias_of(rel_bias, off - pc) - far, s, pc), q_pos,
                       p_len - pc + jnp.arange(pc, dtype=jnp.int32), False)
    off = jnp.arange(-(s - 1), s, dtype=jnp.int32)
    bd_n = _chunk_mask(_toeplitz(_rel_bias_of(rel_bias, off) - far, s, s), q_pos, q_pos, False)
    bd_n = jnp.concatenate([bd_n, jnp.full((DIFF_HEADS, s, LANES - s), NEG_INF, F32)], axis=-1)
    md = _diff_sample_attention(q16, pdk.reshape(b, p_len * DIFF_HEADS, 2 * DIFF_HD),
                                pdv.reshape(b, p_len * DIFF_HEADS, 2 * DIFF_HD), dk, dv, bd_c, bd_n, gate,
                                subln_g.reshape(1, -1), lams, s=s, pc=pc, lam_init=lam_init)

    y = _out_projection(mf, md, w_out, x2d, g_post, tm=tm)
    return y.reshape(b, s, d), (fk, fv, logf, dk, dv)


def kernel(x_prompt, x_sample, cache_fox_k, cache_fox_v, cache_fox_logf, cache_diff_k, cache_diff_v,
           norm_pre_g, w_in, forget_bias, lambda_q1, lambda_k1, lambda_q2, lambda_k2, subln_g, w_out,
           norm_post_g, rel_bias):
    depth = w_in.shape[0]
    y_p, y_s = x_prompt, x_sample
    rows_p, rows_s = [], []
    for l in range(depth):
        lams = tuple(a[l].reshape(1, -1) for a in (lambda_q1, lambda_k1, lambda_q2, lambda_k2))
        params = (norm_pre_g[l], w_in[l], forget_bias[l], lams, subln_g[l], w_out[l], norm_post_g[l], rel_bias, l)
        y_p, new_p = _prompt_layer(y_p, *params)
        past = (cache_fox_k[l], cache_fox_v[l], cache_fox_logf[l], cache_diff_k[l], cache_diff_v[l])
        y_s, new_s = _sample_layer(y_s, past, *params)
        rows_p.append(new_p)
        rows_s.append(new_s)

    bp, sp = x_prompt.shape[:2]
    bs, ss = x_sample.shape[:2]

    def stack(rows, idx, shape):
        return jnp.stack([r[idx].reshape(shape) for r in rows], axis=0)

    outs = [y_p, y_s]
    for rows, (b, s) in ((rows_p, (bp, sp)), (rows_s, (bs, ss))):
        outs += [stack(rows, 0, (b, s, FOX_HEADS, FOX_HD)), stack(rows, 1, (b, s, FOX_HEADS, FOX_HD)),
                 stack(rows, 2, (b, s, FOX_HEADS)), stack(rows, 3, (b, s, DIFF_HEADS, 2 * DIFF_HD)),
                 stack(rows, 4, (b, s, DIFF_HEADS, 2 * DIFF_HD))]
    return tuple(outs)
```

```python
import functools
import math

import jax
import jax.numpy as jnp
import numpy as np
from jax import lax
from jax.experimental import pallas as pl
from jax.experimental.pallas import tpu as pltpu

F32 = jnp.float32
BF16 = jnp.bfloat16

FOX_HEADS = 8
FOX_HD = 64
DIFF_HEADS = 4
DIFF_HD = 64
FOX_W = FOX_HEADS * FOX_HD
DIFF_W = DIFF_HEADS * 2 * DIFF_HD
CHUNK = 64
NUM_BUCKETS = 32
MAX_DISTANCE = 128
EPS = 1e-6
NEG_INF = -1e30
LOG2E = 1.4426950408889634

LANES = 128
HEAD_PAIR_W = 2 * FOX_HD
VMEM_LIMIT_BYTES = 56 * 1024 * 1024

PIECES = 3
ONES_AT = PIECES * FOX_HEADS
QEXT_ROWS = 32


def _cparams(n_axes):
    return pltpu.CompilerParams(
        dimension_semantics=("arbitrary",) * n_axes,
        vmem_limit_bytes=VMEM_LIMIT_BYTES,
    )


def _split3(a):
    hi = a.astype(BF16)
    r1 = a - hi.astype(F32)
    mid = r1.astype(BF16)
    lo = (r1 - mid.astype(F32)).astype(BF16)
    return hi, mid, lo


def _nt_dot(a, b):
    return lax.dot_general(a, b, (((1,), (1,)), ((), ())), preferred_element_type=F32)


def _dot(a, b):
    return jnp.dot(a, b, preferred_element_type=F32)


def _inproj_body(x_ref, g_ref, wkv_ref, wg_ref, wff_ref, bf_ref, wq_ref,
                 fk_ref, fv_ref, dk_ref, dv_ref, logf_ref, gate_ref, q_ref, *rest,
                 transposed, tk, qscale):
    x = x_ref[...]
    ms = jnp.mean(x * x, axis=-1, keepdims=True)
    h = (x * lax.rsqrt(ms + EPS)) * g_ref[...]
    hb = h.astype(BF16)

    fk = _dot(hb, wkv_ref[:, 0 * FOX_W:1 * FOX_W])
    fk_ref[...] = fk
    fv = _dot(hb, wkv_ref[:, 1 * FOX_W:2 * FOX_W])
    fv_ref[...] = fv
    dk = _dot(hb, wkv_ref[:, 2 * FOX_W:3 * FOX_W])
    dv = _dot(hb, wkv_ref[:, 3 * FOX_W:4 * FOX_W])
    tm = dk.shape[0]
    for hd in range(DIFF_HEADS):
        cols = slice(hd * 2 * DIFF_HD, (hd + 1) * 2 * DIFF_HD)
        dk_ref[pl.ds(hd, tm, stride=DIFF_HEADS), :] = dk[:, cols]
        dv_ref[pl.ds(hd, tm, stride=DIFF_HEADS), :] = dv[:, cols]

    ff = _dot(hb, wff_ref[...])
    logf = jax.nn.log_sigmoid(ff + bf_ref[...])
    logf_ref[...] = logf[:, :FOX_HEADS]

    gates = _dot(hb, wg_ref[...])
    gate_ref[...] = (gates * jax.nn.sigmoid(gates)).astype(BF16)

    q = _dot(hb, wq_ref[...]) * qscale
    if transposed:
        k16_ref, dk16_ref, vt_ref = rest
        k16_ref[...] = fk.astype(BF16)
        dk16_ref[...] = dk.astype(BF16)
        q_ref[...] = q.T.astype(BF16)
        vt = jnp.concatenate([fv, dv], axis=1).T.astype(BF16)
        for c in range(vt.shape[1] // tk):
            vt_ref[c] = vt[:, c * tk:(c + 1) * tk]
    else:
        q_ref[...] = q.astype(BF16)


def _in_projection(x2d, g_pre, w_in, b_f, *, transposed, tm, tk):
    n, d = x2d.shape
    tm = min(tm, n)
    assert n % tm == 0 and (not transposed or tm % tk == 0)
    qscale = FOX_HD ** -0.5 * LOG2E
    o = np.cumsum([0, FOX_W, FOX_W, FOX_W, FOX_HEADS, FOX_W, DIFF_W, DIFF_W, DIFF_W, DIFF_W])
    w_fq, w_fk, w_fv, w_ff, w_fg, w_dq, w_dk, w_dv, w_dg = [w_in[:, o[i]:o[i + 1]] for i in range(9)]
    wkv = jnp.concatenate([w_fk, w_fv, w_dk, w_dv], axis=1).astype(BF16)
    wg = jnp.concatenate([w_fg, w_dg], axis=1).astype(BF16)
    wff = jnp.pad(w_ff, ((0, 0), (0, LANES - FOX_HEADS))).astype(BF16)
    bfp = jnp.pad(b_f.reshape(1, FOX_HEADS), ((0, 0), (0, LANES - FOX_HEADS)))
    wq = jnp.concatenate([w_fq, w_dq], axis=1).astype(BF16)

    const = lambda shape: pl.BlockSpec(shape, lambda i: (0,) * len(shape))
    rows = lambda w: pl.BlockSpec((tm, w), lambda i: (i, 0))
    in_specs = [rows(d), const((1, d)), const(wkv.shape), const(wg.shape), const(wff.shape),
                const(bfp.shape), const(wq.shape)]
    by_head = jax.ShapeDtypeStruct((DIFF_HEADS * n, 2 * DIFF_HD), F32)
    by_head_spec = pl.BlockSpec((DIFF_HEADS * tm, 2 * DIFF_HD), lambda i: (i, 0))
    out_shape = [jax.ShapeDtypeStruct((n, FOX_W), F32)] * 2 + [by_head] * 2
    out_shape += [jax.ShapeDtypeStruct((n, FOX_HEADS), F32), jax.ShapeDtypeStruct((n, FOX_W + DIFF_W), BF16)]
    out_specs = [rows(FOX_W), rows(FOX_W), by_head_spec, by_head_spec, rows(FOX_HEADS), rows(FOX_W + DIFF_W)]
    if transposed:
        out_shape += [jax.ShapeDtypeStruct((FOX_W + DIFF_W, n), BF16),
                      jax.ShapeDtypeStruct((n, FOX_W), BF16), jax.ShapeDtypeStruct((n, DIFF_W), BF16),
                      jax.ShapeDtypeStruct((n // tk, FOX_W + DIFF_W, tk), BF16)]
        out_specs += [pl.BlockSpec((FOX_W + DIFF_W, tm), lambda i: (0, i)), rows(FOX_W), rows(DIFF_W),
                      pl.BlockSpec((tm // tk, FOX_W + DIFF_W, tk), lambda i: (i, 0, 0))]
    else:
        out_shape += [jax.ShapeDtypeStruct((n, FOX_W + DIFF_W), BF16)]
        out_specs += [rows(FOX_W + DIFF_W)]
    return pl.pallas_call(
        functools.partial(_inproj_body, transposed=transposed, tk=tk, qscale=qscale),
        grid=(n // tm,), in_specs=in_specs, out_specs=out_specs, out_shape=out_shape,
        compiler_params=_cparams(1), name="in_projection_t" if transposed else "in_projection_r",
    )(x2d, g_pre.reshape(1, d), wkv, wg, wff, bfp, wq)


def _cumsum_body(x_ref, cum_ref, hi_ref, mid_ref, lo_ref, *, seg, group, scale):
    x = x_ref[...]
    r = x.shape[0]
    a = lax.broadcasted_iota(jnp.int32, (LANES, LANES), 0)
    b = lax.broadcasted_iota(jnp.int32, (LANES, LANES), 1)
    sh = int(math.log2(seg))
    upper = ((a <= b) & ((a >> sh) == (b >> sh))).astype(BF16)
    pieces = _split3(x)
    w = sum(_dot(p, upper) for p in pieces)
    if group > 1:
        ones = jnp.ones((LANES, LANES), BF16)
        tot = sum(_dot(p, ones) for p in pieces)
        ri = lax.broadcasted_iota(jnp.int32, (r, r), 0)
        ci = lax.broadcasted_iota(jnp.int32, (r, r), 1)
        gsh = int(math.log2(group))
        lower = ((ci < ri) & ((ri >> gsh) == (ci >> gsh))).astype(BF16)
        w = w + sum(_dot(lower, p) for p in _split3(tot))
    cum_ref[...] = w
    hi, mid, lo = _split3(w * scale)
    hi_ref[...] = hi
    mid_ref[...] = mid
    lo_ref[...] = lo


PREFIX_ROWS_PER_STEP = 1024


def _prefix_sums(x2d, *, seg, group, scale):
    r = x2d.shape[0]
    rb = min(max(PREFIX_ROWS_PER_STEP, group), r)
    assert r % rb == 0 and rb % group == 0 and (seg == LANES or group == 1)
    assert seg & (seg - 1) == 0 and group & (group - 1) == 0
    spec = pl.BlockSpec((rb, LANES), lambda i: (i, 0))
    return pl.pallas_call(
        functools.partial(_cumsum_body, seg=seg, group=group, scale=scale),
        grid=(r // rb,), in_specs=[spec], out_specs=[spec] * 4,
        out_shape=[jax.ShapeDtypeStruct((r, LANES), F32)] + [jax.ShapeDtypeStruct((r, LANES), BF16)] * 3,
        compiler_params=_cparams(1), name="prefix_sums",
    )(x2d)


SUM_ROWS = 16
SCORES_AHEAD = 2
FAR_UNROLL = 8


def _with_sum_rows(vt):
    row = lax.broadcasted_iota(jnp.int32, (SUM_ROWS, vt.shape[1]), 0)
    return jnp.concatenate([vt, jnp.where(row == 0, 1.0, 0.0).astype(BF16)], axis=0)


def _store_scores(buf, h, s):
    buf[0][h] = s
    buf[1][h] = jnp.max(s, axis=0, keepdims=True)


def _online_update_t(s, tile_max, h, m_ref, acc_ref, vt_aug):
    m = m_ref[h]
    mn = jnp.maximum(m, tile_max)
    alpha = jnp.exp2(m - mn)
    p = jnp.exp2(s - mn)
    acc_ref[h] = alpha * acc_ref[h] + _dot(vt_aug, p.astype(BF16))
    m_ref[h] = mn


def _run_far_tiles(n, half_step, buf_a, buf_b):
    def unrolled(base, count):
        bufs = (buf_a, buf_b)
        for u in range(count):
            half_step(base + u, bufs[u % 2], bufs[(u + 1) % 2])

    def body(jj, c):
        unrolled(FAR_UNROLL * jj, FAR_UNROLL)
        return c

    lax.fori_loop(0, n // FAR_UNROLL, body, 0)
    for count in range(FAR_UNROLL - 2, 0, -2):
        @pl.when(n % FAR_UNROLL // 2 * 2 == count)
        def _():
            unrolled(n // FAR_UNROLL * FAR_UNROLL, count)


def _init_state(m_ref, l_ref, acc_ref):
    m_ref[...] = jnp.full(m_ref.shape, NEG_INF, F32)
    if l_ref is not None:
        l_ref[...] = jnp.zeros(l_ref.shape, F32)
    acc_ref[...] = jnp.zeros(acc_ref.shape, F32)


def _resident(shape, index_map):
    return pl.BlockSpec(shape, index_map, pipeline_mode=pl.Buffered(1))


def _fox_prompt_body(q_ref, qx_ref, qn_ref, qxn_ref, k_ref, kx_ref, vt_ref, gate_ref, o_ref, m_ref, acc_ref,
                     sa_ref, sb_ref, ma_ref, mb_ref, *, t, pairs):
    i = pl.program_id(1)
    heads = 2 * pairs
    buf_a, buf_b = (sa_ref, ma_ref), (sb_ref, mb_ref)
    _init_state(m_ref, None, acc_ref)
    z64 = jnp.zeros((FOX_HD, t), BF16)
    zpad = jnp.zeros((LANES - QEXT_ROWS, t), BF16)

    def augmented_queries(q_blk_ref, qx_blk_ref):
        out = []
        for p in range(pairs):
            q = q_blk_ref[p * HEAD_PAIR_W:(p + 1) * HEAD_PAIR_W, :]
            out.append(jnp.concatenate([q[:FOX_HD], z64, qx_blk_ref[2 * p], zpad], axis=0))
            out.append(jnp.concatenate([z64, q[FOX_HD:], qx_blk_ref[2 * p + 1], zpad], axis=0))
        return out

    qa = augmented_queries(q_ref, qx_ref)
    qa_next = augmented_queries(qn_ref, qxn_ref)

    def keys(j, p):
        off = pl.multiple_of(j * t, t)
        return jnp.concatenate([k_ref[pl.ds(off, t), p * HEAD_PAIR_W:(p + 1) * HEAD_PAIR_W],
                                kx_ref[pl.ds(off, t), :]], axis=1)

    def consume(j, cur, h, diagonal, between=None):
        s = cur[0][h]
        if between is not None:
            between()
        if diagonal:
            kk = lax.broadcasted_iota(jnp.int32, (t, t), 0)
            qq = lax.broadcasted_iota(jnp.int32, (t, t), 1)
            s = jnp.where(kk <= qq, s, NEG_INF)
            tile_max = jnp.max(s, axis=0, keepdims=True)
        else:
            tile_max = cur[1][h]
        _online_update_t(s, tile_max, h, m_ref, acc_ref,
                         _with_sum_rows(vt_ref[j, h * FOX_HD:(h + 1) * FOX_HD, :]))

    def half_step(j, cur, nxt):
        ka = {}

        def prefetch(h):
            if h // 2 not in ka:
                ka[h // 2] = keys(j + 1, h // 2)
            _store_scores(nxt, h, _dot(ka[h // 2], qa[h]))

        for h in range(min(SCORES_AHEAD, heads)):
            prefetch(h)
        for h in range(heads):
            consume(j, cur, h, False)
            if h + SCORES_AHEAD < heads:
                prefetch(h + SCORES_AHEAD)

    def first_tile_scores(h, queries):
        _store_scores(buf_a, h, _dot(keys(0, h // 2), queries[h]))

    def last_step(j, cur):
        for h in range(heads):
            consume(j, cur, h, True, between=functools.partial(first_tile_scores, h, qa_next))

    @pl.when(i == 0)
    def _():
        for h in range(heads):
            first_tile_scores(h, qa)

    _run_far_tiles(i, half_step, buf_a, buf_b)

    @pl.when(i % 2 == 0)
    def _():
        last_step(i, buf_a)

    @pl.when(i % 2 == 1)
    def _():
        half_step(i - 1, buf_a, buf_b)
        last_step(i, buf_b)

    for p in range(pairs):
        a0, a1 = acc_ref[2 * p], acc_ref[2 * p + 1]
        ot = jnp.concatenate([a0[:FOX_HD] / a0[FOX_HD:FOX_HD + 1], a1[:FOX_HD] / a1[FOX_HD:FOX_HD + 1]], axis=0)
        cols = slice(p * HEAD_PAIR_W, (p + 1) * HEAD_PAIR_W)
        o_ref[:, cols] = (ot.T * gate_ref[:, cols].astype(F32)).astype(BF16)


def _fox_prompt_attention(qt, qext, k16, kext, vt, gate, *, t, pairs):
    n = k16.shape[0]
    nq = n // t
    w = pairs * HEAD_PAIR_W
    heads = 2 * pairs
    assert (FOX_HEADS // 2) % pairs == 0
    return pl.pallas_call(
        functools.partial(_fox_prompt_body, t=t, pairs=pairs),
        grid=(FOX_HEADS // heads, nq),
        in_specs=[pl.BlockSpec((w, t), lambda g, i: (g, i)),
                  pl.BlockSpec((heads, QEXT_ROWS, t), lambda g, i: (g, 0, i)),
                  pl.BlockSpec((w, t), lambda g, i: (g, jnp.minimum(i + 1, nq - 1))),
                  pl.BlockSpec((heads, QEXT_ROWS, t), lambda g, i: (g, 0, jnp.minimum(i + 1, nq - 1))),
                  _resident((n, w), lambda g, i: (0, g)),
                  _resident((n, LANES), lambda g, i: (0, 0)),
                  _resident((nq, w, t), lambda g, i: (0, g, 0)),
                  pl.BlockSpec((t, w), lambda g, i: (i, g))],
        out_specs=pl.BlockSpec((t, w), lambda g, i: (i, g)),
        out_shape=jax.ShapeDtypeStruct((n, FOX_W), BF16),
        scratch_shapes=[pltpu.VMEM((heads, 1, t), F32), pltpu.VMEM((heads, FOX_HD + SUM_ROWS, t), F32),
                        pltpu.VMEM((heads, t, t), F32), pltpu.VMEM((heads, t, t), F32),
                        pltpu.VMEM((heads, 1, t), F32), pltpu.VMEM((heads, 1, t), F32)],
        compiler_params=_cparams(2), name="fox_prompt_attention",
    )(qt, qext, qt, qext, k16, kext, vt, gate)


def _lambda_value(lq1, lk1, lq2, lk2, lam_init):
    a = jnp.sum(lq1 * lk1, axis=-1, keepdims=True)
    b = jnp.sum(lq2 * lk2, axis=-1, keepdims=True)
    return jnp.exp(a) - jnp.exp(b) + lam_init


def _subln_gate(y, g, gate, lam_init):
    ms = jnp.mean(y * y, axis=-1, keepdims=True)
    yn = (y * lax.rsqrt(ms + EPS)) * g
    return ((yn * (1.0 - lam_init)) * gate.astype(F32)).astype(BF16)


def _diff_prompt_body(q_ref, qn_ref, k_ref, vt_ref, bd_ref, bs_ref, gate_ref, sg_ref, lq1_ref, lk1_ref, lq2_ref,
                      lk2_ref, o_ref, m_ref, acc_ref, sa_ref, sb_ref, ma_ref, mb_ref, *, t, heads, lam_init):
    i = pl.program_id(1)
    w = 2 * DIFF_HD
    buf_a, buf_b = (sa_ref, ma_ref), (sb_ref, mb_ref)
    _init_state(m_ref, None, acc_ref)
    z64 = jnp.zeros((DIFF_HD, t), BF16)

    def padded_queries(q_blk_ref):
        out = []
        for h in range(heads):
            q = q_blk_ref[h * w:(h + 1) * w, :]
            out.append(jnp.concatenate([q[:DIFF_HD], z64], axis=0))
            out.append(jnp.concatenate([z64, q[DIFF_HD:]], axis=0))
        return out

    qa = padded_queries(q_ref)
    qa_next = padded_queries(qn_ref)

    def keys(j, h):
        return k_ref[pl.ds(pl.multiple_of(j * t, t), t), h * w:(h + 1) * w]

    def consume(j, cur, c, bias_ref, between=None):
        h = c // 2
        s = cur[0][c]
        if between is not None:
            between()
        if bias_ref is not None:
            s = s + bias_ref[h]
            tile_max = jnp.max(s, axis=0, keepdims=True)
        else:
            tile_max = cur[1][c]
        _online_update_t(s, tile_max, c, m_ref, acc_ref, _with_sum_rows(vt_ref[j, h * w:(h + 1) * w, :]))

    def half_step(j, cur, nxt, bias_ref=None):
        kk = {}

        def prefetch(c):
            if c // 2 not in kk:
                kk[c // 2] = keys(j + 1, c // 2)
            _store_scores(nxt, c, _dot(kk[c // 2], qa[c]))

        for c in range(min(SCORES_AHEAD, 2 * heads)):
            prefetch(c)
        for c in range(2 * heads):
            consume(j, cur, c, bias_ref)
            if c + SCORES_AHEAD < 2 * heads:
                prefetch(c + SCORES_AHEAD)

    def first_tile_scores(c, queries):
        _store_scores(buf_a, c, _dot(keys(0, c // 2), queries[c]))

    def last_step(j, cur):
        for c in range(2 * heads):
            consume(j, cur, c, bd_ref, between=functools.partial(first_tile_scores, c, qa_next))

    @pl.when(i == 0)
    def _():
        for c in range(2 * heads):
            first_tile_scores(c, qa)

    _run_far_tiles(jnp.maximum(i - 1, 0), half_step, buf_a, buf_b)

    @pl.when(i == 0)
    def _():
        last_step(i, buf_a)

    @pl.when(i % 2 == 1)
    def _():
        half_step(i - 1, buf_a, buf_b, bs_ref)
        last_step(i, buf_b)

    @pl.when((i >= 2) & (i % 2 == 0))
    def _():
        half_step(i - 2, buf_a, buf_b)
        half_step(i - 1, buf_b, buf_a, bs_ref)
        last_step(i, buf_a)

    lam = _lambda_value(lq1_ref[...], lk1_ref[...], lq2_ref[...], lk2_ref[...], lam_init)
    for h in range(heads):
        a0, a1 = acc_ref[2 * h], acc_ref[2 * h + 1]
        ot = a0[:w] / a0[w:w + 1] - lam * (a1[:w] / a1[w:w + 1])
        cols = slice(h * w, (h + 1) * w)
        o_ref[:, cols] = _subln_gate(ot.T, sg_ref[...], gate_ref[:, cols], lam_init)


def _diff_prompt_attention(qt, dk16, vt, bias_diag, bias_sub, gate, subln_g, lams, *, t, heads, lam_init):
    n = dk16.shape[0]
    nq = n // t
    w = heads * 2 * DIFF_HD
    assert DIFF_HEADS % heads == 0 and FOX_W % w == 0
    fox_blocks = FOX_W // w
    vec = lambda width: pl.BlockSpec((1, width), lambda g, i: (0, 0))
    return pl.pallas_call(
        functools.partial(_diff_prompt_body, t=t, heads=heads, lam_init=lam_init),
        grid=(DIFF_HEADS // heads, nq),
        in_specs=[pl.BlockSpec((w, t), lambda g, i: (fox_blocks + g, i)),
                  pl.BlockSpec((w, t), lambda g, i: (fox_blocks + g, jnp.minimum(i + 1, nq - 1))),
                  _resident((n, w), lambda g, i: (0, g)),
                  _resident((nq, w, t), lambda g, i: (0, fox_blocks + g, 0)),
                  pl.BlockSpec((heads, t, t), lambda g, i: (g, 0, 0)),
                  pl.BlockSpec((heads, t, t), lambda g, i: (g, 0, 0)),
                  pl.BlockSpec((t, w), lambda g, i: (i, fox_blocks + g)),
                  vec(2 * DIFF_HD), vec(DIFF_HD), vec(DIFF_HD), vec(DIFF_HD), vec(DIFF_HD)],
        out_specs=pl.BlockSpec((t, w), lambda g, i: (i, g)),
        out_shape=jax.ShapeDtypeStruct((n, DIFF_W), BF16),
        scratch_shapes=[pltpu.VMEM((2 * heads, 1, t), F32),
                        pltpu.VMEM((2 * heads, 2 * DIFF_HD + SUM_ROWS, t), F32),
                        pltpu.VMEM((2 * heads, t, t), F32), pltpu.VMEM((2 * heads, t, t), F32),
                        pltpu.VMEM((2 * heads, 1, t), F32), pltpu.VMEM((2 * heads, 1, t), F32)],
        compiler_params=_cparams(2), name="diff_prompt_attention",
    )(qt, qt, dk16, vt, bias_diag, bias_sub, gate, subln_g, *lams)


def _block_diag_queries(qb):
    lane = lax.broadcasted_iota(jnp.int32, qb.shape, 1)
    zero = jnp.zeros_like(qb)
    return jnp.concatenate([jnp.where(lane < FOX_HD, qb, zero), jnp.where(lane >= FOX_HD, qb, zero)], axis=0)


def _fox_sample_body(q_ref, kc_ref, vc_ref, kn_ref, vn_ref, bc_ref, bn_ref, gate_ref, o_ref, *, s, pairs):
    p_len = kc_ref.shape[-1]
    zrows = jnp.zeros((LANES - s, HEAD_PAIR_W), BF16)
    qi = lax.broadcasted_iota(jnp.int32, (2 * s, LANES), 0)
    ki = lax.broadcasted_iota(jnp.int32, (2 * s, LANES), 1)
    causal = ki <= jnp.where(qi >= s, qi - s, qi)
    lane = lax.broadcasted_iota(jnp.int32, (s, HEAD_PAIR_W), 1)

    def rows_bias(b, width):
        return jnp.concatenate([jnp.broadcast_to(b[0:1], (s, width)), jnp.broadcast_to(b[1:2], (s, width))], axis=0)

    def pair_bias(ref, p):
        rows = ref[0, 2 * p:2 * p + 2, :]
        for g in range(1, FOX_HEADS // (2 * pairs)):
            first = 2 * (g * pairs + p)
            rows = jnp.where(pl.program_id(1) == g, ref[0, first:first + 2, :], rows)
        return rows

    scores = []
    for p in range(pairs):
        cols = slice(p * HEAD_PAIR_W, (p + 1) * HEAD_PAIR_W)
        qbd = _block_diag_queries(q_ref[:, cols])
        kt = kc_ref[0, 2 * p:2 * p + 2].reshape(HEAD_PAIR_W, p_len).astype(BF16)
        kn = jnp.concatenate([kn_ref[:, cols].astype(BF16), zrows], axis=0)
        sc = _dot(qbd, kt) + rows_bias(pair_bias(bc_ref, p), p_len)
        sn = _nt_dot(qbd, kn) + rows_bias(pair_bias(bn_ref, p), LANES)
        scores.append((sc, jnp.where(causal, sn, NEG_INF)))
    for p, (sc, sn) in enumerate(scores):
        cols = slice(p * HEAD_PAIR_W, (p + 1) * HEAD_PAIR_W)
        vt = vc_ref[0, 2 * p:2 * p + 2].reshape(HEAD_PAIR_W, p_len).astype(BF16)
        vn = jnp.concatenate([vn_ref[:, cols].astype(BF16), zrows], axis=0)
        m = jnp.maximum(jnp.max(sc, axis=1, keepdims=True), jnp.max(sn, axis=1, keepdims=True))
        pc = jnp.exp2(sc - m)
        pn = jnp.exp2(sn - m)
        l = jnp.sum(pc, axis=1, keepdims=True) + jnp.sum(pn, axis=1, keepdims=True)
        o = (_nt_dot(pc.astype(BF16), vt) + _dot(pn.astype(BF16), vn)) / l
        out = jnp.where(lane < FOX_HD, o[:s], o[s:])
        o_ref[:, cols] = (out * gate_ref[:, cols].astype(F32)).astype(BF16)


def _fox_sample_attention(q16, cache_kt, cache_vt, fk32, fv32, bias_c, bias_n, gate, *, s, pairs):
    b, _, _, p_len = cache_kt.shape
    groups = FOX_HEADS // (2 * pairs)
    w = pairs * HEAD_PAIR_W
    blk = lambda: pl.BlockSpec((s, w), lambda bi, g: (bi, g))
    cache = lambda: pl.BlockSpec((1, 2 * pairs, FOX_HD, p_len), lambda bi, g: (bi, g, 0, 0))
    return pl.pallas_call(
        functools.partial(_fox_sample_body, s=s, pairs=pairs),
        grid=(b, groups),
        in_specs=[blk(), cache(), cache(), blk(), blk(),
                  pl.BlockSpec((1, FOX_HEADS, p_len), lambda bi, g: (bi, 0, 0)),
                  pl.BlockSpec((1, FOX_HEADS, LANES), lambda bi, g: (bi, 0, 0)),
                  blk()],
        out_specs=blk(),
        out_shape=jax.ShapeDtypeStruct((b * s, FOX_W), BF16),
        compiler_params=_cparams(2), name="fox_sample_attention",
    )(q16, cache_kt, cache_vt, fk32, fv32, bias_c, bias_n, gate)


def _diff_sample_body(q_ref, kc_ref, vc_ref, kn_ref, vn_ref, bc_ref, bn_ref, gate_ref, sg_ref, lq1_ref,
                      lk1_ref, lq2_ref, lk2_ref, o_ref, m_ref, l_ref, acc_ref, *, s, pc, lam_init):
    c = pl.program_id(1)
    last = pl.num_programs(1) - 1
    w = 2 * DIFF_HD

    @pl.when(c == 0)
    def _():
        _init_state(m_ref, l_ref, acc_ref)

    qbd = [_block_diag_queries(q_ref[:, h * w:(h + 1) * w]) for h in range(DIFF_HEADS)]

    def update(h, sc, v16):
        m = m_ref[h]
        mn = jnp.maximum(m, jnp.max(sc, axis=1, keepdims=True))
        alpha = jnp.exp2(m - mn)
        p = jnp.exp2(sc - mn)
        l_ref[h] = alpha * l_ref[h] + jnp.sum(p, axis=1, keepdims=True)
        acc_ref[h] = alpha * acc_ref[h] + _dot(p.astype(BF16), v16)
        m_ref[h] = mn

    def chunk(with_bias):
        scores = [_nt_dot(qbd[h], kc_ref[0, pl.ds(h, pc, stride=DIFF_HEADS), :].astype(BF16))
                  for h in range(DIFF_HEADS)]
        for h, sc in enumerate(scores):
            if with_bias:
                tail = bc_ref.shape[-1]
                near = sc[:, pc - tail:] + jnp.concatenate([bc_ref[h], bc_ref[h]], axis=0)
                sc = jnp.concatenate([sc[:, :pc - tail], near], axis=1) if pc > tail else near
            update(h, sc, vc_ref[0, pl.ds(h, pc, stride=DIFF_HEADS), :].astype(BF16))

    @pl.when(c != last)
    def _():
        chunk(False)

    @pl.when(c == last)
    def _():
        chunk(True)
        lam = _lambda_value(lq1_ref[...], lk1_ref[...], lq2_ref[...], lk2_ref[...], lam_init)
        zrows = jnp.zeros((LANES - s, w), BF16)
        for h in range(DIFF_HEADS):
            cols = slice(h * w, (h + 1) * w)
            new_rows = pl.ds(h, s, stride=DIFF_HEADS)
            kn = jnp.concatenate([kn_ref[new_rows, :].astype(BF16), zrows], axis=0)
            vn = jnp.concatenate([vn_ref[new_rows, :].astype(BF16), zrows], axis=0)
            sn = _nt_dot(qbd[h], kn) + jnp.concatenate([bn_ref[h], bn_ref[h]], axis=0)
            update(h, sn, vn)
            o = acc_ref[h] / l_ref[h]
            o = o[:s] - lam * o[s:]
            o_ref[:, cols] = _subln_gate(o, sg_ref[...], gate_ref[:, cols], lam_init)


def _diff_sample_attention(q16, cache_k, cache_v, dk32, dv32, bias_c, bias_n, gate, subln_g, lams, *, s, pc,
                           lam_init):
    b, rows, w = cache_k.shape
    p_len = rows // DIFF_HEADS
    assert p_len % pc == 0 and bias_c.shape[:2] == (DIFF_HEADS, s) and bias_c.shape[2] <= pc
    cache = lambda: pl.BlockSpec((1, DIFF_HEADS * pc, w), lambda bi, c: (bi, c, 0))
    out = lambda: pl.BlockSpec((s, DIFF_W), lambda bi, c: (bi, 0))
    new = lambda: pl.BlockSpec((DIFF_HEADS * s, w), lambda bi, c: (bi, 0))
    half = lambda: pl.BlockSpec((s, DIFF_W), lambda bi, c: (bi, 1))
    const = lambda shape: pl.BlockSpec(shape, lambda bi, c: (0,) * len(shape))
    return pl.pallas_call(
        functools.partial(_diff_sample_body, s=s, pc=pc, lam_init=lam_init),
        grid=(b, p_len // pc),
        in_specs=[half(), cache(), cache(), new(), new(), const(bias_c.shape), const(bias_n.shape), half(),
                  const((1, w)), const((1, DIFF_HD)), const((1, DIFF_HD)), const((1, DIFF_HD)), const((1, DIFF_HD))],
        out_specs=out(),
        out_shape=jax.ShapeDtypeStruct((b * s, DIFF_W), BF16),
        scratch_shapes=[pltpu.VMEM((DIFF_HEADS, 2 * s, 1), F32), pltpu.VMEM((DIFF_HEADS, 2 * s, 1), F32),
                        pltpu.VMEM((DIFF_HEADS, 2 * s, w), F32)],
        compiler_params=_cparams(2), name="diff_sample_attention",
    )(q16, cache_k, cache_v, dk32, dv32, bias_c, bias_n, gate, subln_g, *lams)


def _outproj_body(mf_ref, md_ref, w_ref, x_ref, g_ref, y_ref):
    out = _dot(mf_ref[...], w_ref[:FOX_W, :]) + _dot(md_ref[...], w_ref[FOX_W:, :])
    ms = jnp.mean(out * out, axis=-1, keepdims=True)
    y_ref[...] = x_ref[...] + (out * lax.rsqrt(ms + EPS)) * g_ref[...]


def _out_projection(mf, md, w_out, x2d, g_post, *, tm):
    n, d = x2d.shape
    tm = min(tm, n)
    assert n % tm == 0
    rows = lambda w: pl.BlockSpec((tm, w), lambda i: (i, 0))
    return pl.pallas_call(
        _outproj_body, grid=(n // tm,),
        in_specs=[rows(FOX_W), rows(DIFF_W), pl.BlockSpec(w_out.shape, lambda i: (0, 0)), rows(d),
                  pl.BlockSpec((1, d), lambda i: (0, 0))],
        out_specs=rows(d), out_shape=jax.ShapeDtypeStruct((n, d), F32),
        compiler_params=_cparams(1), name="out_projection",
    )(mf, md, w_out.astype(BF16), x2d, g_post.reshape(1, d))


def _t5_bucket(rel):
    half = NUM_BUCKETS // 2
    max_exact = half // 2
    ret = jnp.where(rel > 0, half, 0)
    n = jnp.abs(rel)
    nf = jnp.maximum(n, 1).astype(jnp.float32)
    large = max_exact + (jnp.log(nf / max_exact) / math.log(MAX_DISTANCE / max_exact)
                         * (half - max_exact)).astype(jnp.int32)
    large = jnp.minimum(large, half - 1)
    return ret + jnp.where(n < max_exact, n, large)


def _rel_bias_of(rel_bias, rel):
    return rel_bias[_t5_bucket(rel)].astype(F32).T


def _toeplitz(g, rows, cols):
    h = g.shape[0]
    period = rows + cols
    v = jnp.concatenate([g[:, rows - 1:], jnp.zeros((h, 1), g.dtype), g[:, :rows - 1]], axis=1)
    flat = jnp.tile(v, (1, rows))[:, :rows * (period - 1)]
    return flat.reshape(h, rows, period - 1)[:, :, :cols]


def _chunk_mask(bias, row_pos, col_pos, rows_are_keys):
    r = (row_pos // CHUNK)[:, None]
    c = (col_pos // CHUNK)[None, :]
    mask = (r <= c) if rows_are_keys else (c <= r)
    return jnp.where(mask[None], bias * LOG2E, NEG_INF)


def _prompt_layer(x, g_pre, w_in, b_f, lams, subln_g, w_out, g_post, rel_bias, layer_idx, *, t=256, tm=512,
                  tm_out=1024, fox_pairs=4, diff_heads=4):
    b, s, d = x.shape
    assert b == 1 and s % t == 0 and s // LANES >= 1
    x2d = x.reshape(s, d)
    (fk, fv, dk, dv, logf, gate, qt, k16, dk16, vt) = _in_projection(
        x2d, g_pre, w_in, b_f, transposed=True, tm=tm, tk=t)

    rows = s // LANES
    _, hi, mid, lo = _prefix_sums(logf.T.reshape(FOX_HEADS * rows, LANES), seg=LANES, group=rows, scale=LOG2E)
    pieces = jnp.stack([hi, mid, lo], axis=0).reshape(PIECES, FOX_HEADS, s)
    kext = jnp.concatenate([
        (-pieces).transpose(2, 1, 0).reshape(s, ONES_AT), jnp.ones((s, PIECES), BF16),
        jnp.zeros((s, LANES - ONES_AT - PIECES), BF16)], axis=1)
    sel = (jnp.arange(ONES_AT)[None, :] // PIECES == jnp.arange(FOX_HEADS)[:, None]).astype(BF16)
    qext = jnp.concatenate([
        jnp.broadcast_to(sel[:, :, None], (FOX_HEADS, ONES_AT, s)), pieces.transpose(1, 0, 2),
        jnp.zeros((FOX_HEADS, QEXT_ROWS - ONES_AT - PIECES, s), BF16)], axis=1)

    mf = _fox_prompt_attention(qt, qext, k16, kext, vt, gate, t=t, pairs=fox_pairs)

    lam_init = 0.8 - 0.6 * math.exp(-0.3 * layer_idx)
    assert t >= MAX_DISTANCE
    pos = jnp.arange(t, dtype=jnp.int32)
    off = jnp.arange(-(t - 1), t, dtype=jnp.int32)
    far = _rel_bias_of(rel_bias, jnp.full((1,), -2 * t, jnp.int32))
    bias_diag = _chunk_mask(_toeplitz(_rel_bias_of(rel_bias, -off) - far, t, t), pos, pos, True)
    bias_sub = _chunk_mask(_toeplitz(_rel_bias_of(rel_bias, -off - t) - far, t, t), pos, pos + t, True)
    md = _diff_prompt_attention(qt, dk16, vt, bias_diag, bias_sub, gate, subln_g.reshape(1, -1), lams,
                                t=t, heads=diff_heads, lam_init=lam_init)

    y = _out_projection(mf, md, w_out, x2d, g_post, tm=tm_out)
    return y.reshape(b, s, d), (fk, fv, logf, dk, dv)


def _sample_layer(x, past, g_pre, w_in, b_f, lams, subln_g, w_out, g_post, rel_bias, layer_idx, *, tm=512,
                  fox_pairs=4, diff_chunk=4096):
    b, s, d = x.shape
    pk, pv, plogf, pdk, pdv = past
    p_len = pk.shape[1]
    assert s <= LANES and LANES % s == 0 and p_len % LANES == 0
    x2d = x.reshape(b * s, d)
    fk, fv, dk, dv, logf, gate, q16 = _in_projection(x2d, g_pre, w_in, b_f, transposed=False, tm=tm, tk=tm)

    rows = p_len // LANES
    cum_c, _, _, _ = _prefix_sums(plogf.transpose(0, 2, 1).reshape(b * FOX_HEADS * rows, LANES), seg=LANES,
                                  group=rows, scale=1.0)
    cum_c = cum_c.reshape(b, FOX_HEADS, p_len)
    bias_c = (cum_c[:, :, -1:] - cum_c) * LOG2E
    cum_n, _, _, _ = _prefix_sums(
        logf.reshape(b, s, FOX_HEADS).transpose(0, 2, 1).reshape(b * FOX_HEADS * s // LANES, LANES),
        seg=s, group=1, scale=1.0)
    bias_n = jnp.concatenate([-cum_n.reshape(b, FOX_HEADS, s) * LOG2E,
                              jnp.full((b, FOX_HEADS, LANES - s), NEG_INF, F32)], axis=-1)
    mf = _fox_sample_attention(q16, pk.transpose(0, 2, 3, 1), pv.transpose(0, 2, 3, 1), fk, fv, bias_c, bias_n,
                               gate, s=s, pairs=fox_pairs)

    lam_init = 0.8 - 0.6 * math.exp(-0.3 * layer_idx)
    pc = min(diff_chunk, p_len)
    tail = MAX_DISTANCE
    assert tail % LANES == 0 and pc >= tail
    q_pos = p_len + jnp.arange(s, dtype=jnp.int32)
    far = _rel_bias_of(rel_bias, jnp.full((1,), -2 * p_len, jnp.int32))
    off = jnp.arange(-(s - 1), tail, dtype=jnp.int32)
    bd_c = _chunk_mask(_toeplitz(_rel_bias_of(rel_bias, off - tail) - far, s, tail), q_pos,
                       p_len - tail + jnp.arange(tail, dtype=jnp.int32), False)
    off = jnp.arange(-(s - 1), s, dtype=jnp.int32)
    bd_n = _chunk_mask(_toeplitz(_rel_bias_of(rel_bias, off) - far, s, s), q_pos, q_pos, False)
    bd_n = jnp.concatenate([bd_n, jnp.full((DIFF_HEADS, s, LANES - s), NEG_INF, F32)], axis=-1)
    md = _diff_sample_attention(q16, pdk.reshape(b, p_len * DIFF_HEADS, 2 * DIFF_HD),
                                pdv.reshape(b, p_len * DIFF_HEADS, 2 * DIFF_HD), dk, dv, bd_c, bd_n, gate,
                                subln_g.reshape(1, -1), lams, s=s, pc=pc, lam_init=lam_init)

    y = _out_projection(mf, md, w_out, x2d, g_post, tm=tm)
    return y.reshape(b, s, d), (fk, fv, logf, dk, dv)


def kernel(x_prompt, x_sample, cache_fox_k, cache_fox_v, cache_fox_logf, cache_diff_k, cache_diff_v,
           norm_pre_g, w_in, forget_bias, lambda_q1, lambda_k1, lambda_q2, lambda_k2, subln_g, w_out,
           norm_post_g, rel_bias):
    depth = w_in.shape[0]
    y_p, y_s = x_prompt, x_sample
    rows_p, rows_s = [], []
    for l in range(depth):
        lams = tuple(a[l].reshape(1, -1) for a in (lambda_q1, lambda_k1, lambda_q2, lambda_k2))
        params = (norm_pre_g[l], w_in[l], forget_bias[l], lams, subln_g[l], w_out[l], norm_post_g[l], rel_bias, l)
        y_p, new_p = _prompt_layer(y_p, *params)
        past = (cache_fox_k[l], cache_fox_v[l], cache_fox_logf[l], cache_diff_k[l], cache_diff_v[l])
        y_s, new_s = _sample_layer(y_s, past, *params)
        rows_p.append(new_p)
        rows_s.append(new_s)

    bp, sp = x_prompt.shape[:2]
    bs, ss = x_sample.shape[:2]

    def stack(rows, idx, shape):
        return jnp.stack([r[idx].reshape(shape) for r in rows], axis=0)

    outs = [y_p, y_s]
    for rows, (b, s) in ((rows_p, (bp, sp)), (rows_s, (bs, ss))):
        outs += [stack(rows, 0, (b, s, FOX_HEADS, FOX_HD)), stack(rows, 1, (b, s, FOX_HEADS, FOX_HD)),
                 stack(rows, 2, (b, s, FOX_HEADS)), stack(rows, 3, (b, s, DIFF_HEADS, 2 * DIFF_HD)),
                 stack(rows, 4, (b, s, DIFF_HEADS, 2 * DIFF_HD))]
    return tuple(outs)
```

```python
import functools
import math

import jax
import jax.numpy as jnp
import numpy as np
from jax import lax
from jax.experimental import pallas as pl
from jax.experimental.pallas import tpu as pltpu

F32 = jnp.float32
BF16 = jnp.bfloat16

FOX_HEADS = 8
FOX_HD = 64
DIFF_HEADS = 4
DIFF_HD = 64
FOX_W = FOX_HEADS * FOX_HD
DIFF_W = DIFF_HEADS * 2 * DIFF_HD
CHUNK = 64
NUM_BUCKETS = 32
MAX_DISTANCE = 128
EPS = 1e-6
NEG_INF = -1e30
LOG2E = 1.4426950408889634

LANES = 128
BF16_SUBLANES = 16
HEAD_PAIR_W = 2 * FOX_HD
VMEM_LIMIT_BYTES = 56 * 1024 * 1024

PIECES = 3
ONES_AT = PIECES * FOX_HEADS
QEXT_ROWS = 32


def _cparams(n_axes):
    return pltpu.CompilerParams(
        dimension_semantics=("arbitrary",) * n_axes,
        vmem_limit_bytes=VMEM_LIMIT_BYTES,
    )


def _split3(a):
    hi = a.astype(BF16)
    r1 = a - hi.astype(F32)
    mid = r1.astype(BF16)
    lo = (r1 - mid.astype(F32)).astype(BF16)
    return hi, mid, lo


def _nt_dot(a, b):
    return lax.dot_general(a, b, (((1,), (1,)), ((), ())), preferred_element_type=F32)


def _dot(a, b):
    return jnp.dot(a, b, preferred_element_type=F32)


def _inproj_body(x_ref, g_ref, wkv_ref, wg_ref, wff_ref, bf_ref, wq_ref,
                 fk_ref, fv_ref, dk_ref, dv_ref, logf_ref, gate_ref, q_ref, *rest,
                 transposed, tk, qscale):
    x = x_ref[...]
    ms = jnp.mean(x * x, axis=-1, keepdims=True)
    h = (x * lax.rsqrt(ms + EPS)) * g_ref[...]
    hb = h.astype(BF16)

    fk = _dot(hb, wkv_ref[:, 0 * FOX_W:1 * FOX_W])
    fk_ref[...] = fk
    fv = _dot(hb, wkv_ref[:, 1 * FOX_W:2 * FOX_W])
    fv_ref[...] = fv
    dk = _dot(hb, wkv_ref[:, 2 * FOX_W:3 * FOX_W])
    dv = _dot(hb, wkv_ref[:, 3 * FOX_W:4 * FOX_W])
    tm = dk.shape[0]
    for hd in range(DIFF_HEADS):
        cols = slice(hd * 2 * DIFF_HD, (hd + 1) * 2 * DIFF_HD)
        dk_ref[pl.ds(hd, tm, stride=DIFF_HEADS), :] = dk[:, cols]
        dv_ref[pl.ds(hd, tm, stride=DIFF_HEADS), :] = dv[:, cols]

    fft = _nt_dot(wff_ref[...], hb)
    logf_ref[...] = jax.nn.log_sigmoid(fft + bf_ref[...])[:FOX_HEADS]

    gates = _dot(hb, wg_ref[...])
    gate_ref[...] = (gates * jax.nn.sigmoid(gates)).astype(BF16)

    q = _dot(hb, wq_ref[...]) * qscale
    if transposed:
        k16_ref, dk16_ref, vt_ref = rest
        k16_ref[...] = fk.astype(BF16)
        dk16_ref[...] = dk.astype(BF16)
        q_ref[...] = q.T.astype(BF16)
        vt = jnp.concatenate([fv, dv], axis=1).T.astype(BF16)
        for c in range(vt.shape[1] // tk):
            vt_ref[c] = vt[:, c * tk:(c + 1) * tk]
    else:
        q_ref[...] = q.astype(BF16)


def _in_projection(x2d, g_pre, w_in, b_f, *, transposed, tm, tk):
    n, d = x2d.shape
    tm = min(tm, n)
    assert n % tm == 0 and (not transposed or tm % tk == 0)
    qscale = FOX_HD ** -0.5 * LOG2E
    o = np.cumsum([0, FOX_W, FOX_W, FOX_W, FOX_HEADS, FOX_W, DIFF_W, DIFF_W, DIFF_W, DIFF_W])
    w_fq, w_fk, w_fv, w_ff, w_fg, w_dq, w_dk, w_dv, w_dg = [w_in[:, o[i]:o[i + 1]] for i in range(9)]
    wkv = jnp.concatenate([w_fk, w_fv, w_dk, w_dv], axis=1).astype(BF16)
    wg = jnp.concatenate([w_fg, w_dg], axis=1).astype(BF16)
    wff = jnp.pad(w_ff.T, ((0, BF16_SUBLANES - FOX_HEADS), (0, 0))).astype(BF16)
    bfp = jnp.pad(b_f.reshape(FOX_HEADS, 1), ((0, BF16_SUBLANES - FOX_HEADS), (0, 0)))
    wq = jnp.concatenate([w_fq, w_dq], axis=1).astype(BF16)

    const = lambda shape: pl.BlockSpec(shape, lambda i: (0,) * len(shape))
    rows = lambda w: pl.BlockSpec((tm, w), lambda i: (i, 0))
    in_specs = [rows(d), const((1, d)), const(wkv.shape), const(wg.shape), const(wff.shape),
                const(bfp.shape), const(wq.shape)]
    by_head = jax.ShapeDtypeStruct((DIFF_HEADS * n, 2 * DIFF_HD), F32)
    by_head_spec = pl.BlockSpec((DIFF_HEADS * tm, 2 * DIFF_HD), lambda i: (i, 0))
    out_shape = [jax.ShapeDtypeStruct((n, FOX_W), F32)] * 2 + [by_head] * 2
    out_shape += [jax.ShapeDtypeStruct((FOX_HEADS, n), F32), jax.ShapeDtypeStruct((n, FOX_W + DIFF_W), BF16)]
    out_specs = [rows(FOX_W), rows(FOX_W), by_head_spec, by_head_spec,
                 pl.BlockSpec((FOX_HEADS, tm), lambda i: (0, i)), rows(FOX_W + DIFF_W)]
    if transposed:
        out_shape += [jax.ShapeDtypeStruct((FOX_W + DIFF_W, n), BF16),
                      jax.ShapeDtypeStruct((n, FOX_W), BF16), jax.ShapeDtypeStruct((n, DIFF_W), BF16),
                      jax.ShapeDtypeStruct((n // tk, FOX_W + DIFF_W, tk), BF16)]
        out_specs += [pl.BlockSpec((FOX_W + DIFF_W, tm), lambda i: (0, i)), rows(FOX_W), rows(DIFF_W),
                      pl.BlockSpec((tm // tk, FOX_W + DIFF_W, tk), lambda i: (i, 0, 0))]
    else:
        out_shape += [jax.ShapeDtypeStruct((n, FOX_W + DIFF_W), BF16)]
        out_specs += [rows(FOX_W + DIFF_W)]
    return pl.pallas_call(
        functools.partial(_inproj_body, transposed=transposed, tk=tk, qscale=qscale),
        grid=(n // tm,), in_specs=in_specs, out_specs=out_specs, out_shape=out_shape,
        compiler_params=_cparams(1), name="in_projection_t" if transposed else "in_projection_r",
    )(x2d, g_pre.reshape(1, d), wkv, wg, wff, bfp, wq)


PREFIX_ROWS_PER_STEP = 512


def _prefix_body(x_ref, val_ref, hi_ref, mid_ref, lo_ref, *, seg, carry, suffix, scale):
    nb, heads, length = x_ref.shape
    slabs = length // LANES
    rows = nb * slabs * heads
    x = jnp.concatenate([x_ref[g, :, c * LANES:(c + 1) * LANES] for g in range(nb) for c in range(slabs)], axis=0)
    a = lax.broadcasted_iota(jnp.int32, (LANES, LANES), 0)
    b = lax.broadcasted_iota(jnp.int32, (LANES, LANES), 1)
    sh = int(math.log2(seg))
    upper = ((a <= b) & ((a >> sh) == (b >> sh))).astype(BF16)
    pieces = _split3(x)
    w = sum(_dot(p, upper) for p in pieces)
    if carry:
        ones = jnp.ones((LANES, LANES), BF16)
        tot3 = _split3(sum(_dot(p, ones) for p in pieces))
        ri = lax.broadcasted_iota(jnp.int32, (rows, rows), 0)
        ci = lax.broadcasted_iota(jnp.int32, (rows, rows), 1)
        gsh = int(math.log2(slabs * heads))
        same_seq = ((ri >> gsh) == (ci >> gsh)) & ((ri & (heads - 1)) == (ci & (heads - 1)))
        w = w + sum(_dot((same_seq & (ci < ri)).astype(BF16), p) for p in tot3)
        if suffix:
            w = sum(_dot(same_seq.astype(BF16), p) for p in tot3) - w
    val = w * scale
    hi, mid, lo = _split3(val)
    for g in range(nb):
        for c in range(slabs):
            r0 = (g * slabs + c) * heads
            lanes = slice(c * LANES, (c + 1) * LANES)
            val_ref[g, :, lanes] = val[r0:r0 + heads]
            hi_ref[g, :, lanes] = hi[r0:r0 + heads]
            mid_ref[g, :, lanes] = mid[r0:r0 + heads]
            lo_ref[g, :, lanes] = lo[r0:r0 + heads]


def _prefix_sums(x, *, seg, carry, suffix=False, scale=1.0):
    groups, heads, length = x.shape
    slabs = length // LANES
    assert length % LANES == 0 and seg & (seg - 1) == 0 and (seg == LANES if carry else seg <= LANES)
    assert heads == FOX_HEADS and (slabs * heads) & (slabs * heads - 1) == 0
    nb = max(1, min(groups, PREFIX_ROWS_PER_STEP // (slabs * heads)))
    assert groups % nb == 0
    spec = pl.BlockSpec((nb, heads, length), lambda i: (i, 0, 0))
    return pl.pallas_call(
        functools.partial(_prefix_body, seg=seg, carry=carry, suffix=suffix, scale=scale),
        grid=(groups // nb,), in_specs=[spec], out_specs=[spec] * 4,
        out_shape=[jax.ShapeDtypeStruct(x.shape, F32)] + [jax.ShapeDtypeStruct(x.shape, BF16)] * 3,
        compiler_params=_cparams(1), name="prefix_sums",
    )(x)


SUM_ROWS = 16
SCORES_AHEAD = 2
FAR_UNROLL = 8


def _with_sum_rows(vt):
    row = lax.broadcasted_iota(jnp.int32, (SUM_ROWS, vt.shape[1]), 0)
    return jnp.concatenate([vt, jnp.where(row == 0, 1.0, 0.0).astype(BF16)], axis=0)


def _store_scores(buf, h, s):
    buf[0][h] = s
    buf[1][h] = jnp.max(s, axis=0, keepdims=True)


def _online_update_t(s, tile_max, h, m_ref, acc_ref, vt_aug):
    m = m_ref[h]
    mn = jnp.maximum(m, tile_max)
    alpha = jnp.exp2(m - mn)
    p = jnp.exp2(s - mn)
    acc_ref[h] = alpha * acc_ref[h] + _dot(vt_aug, p.astype(BF16))
    m_ref[h] = mn


def _run_far_tiles(n, half_step, buf_a, buf_b):
    def unrolled(base, count):
        bufs = (buf_a, buf_b)
        for u in range(count):
            half_step(base + u, bufs[u % 2], bufs[(u + 1) % 2])

    def body(jj, c):
        unrolled(FAR_UNROLL * jj, FAR_UNROLL)
        return c

    lax.fori_loop(0, n // FAR_UNROLL, body, 0)
    for count in range(FAR_UNROLL - 2, 0, -2):
        @pl.when(n % FAR_UNROLL // 2 * 2 == count)
        def _():
            unrolled(n // FAR_UNROLL * FAR_UNROLL, count)


def _init_state(m_ref, l_ref, acc_ref):
    m_ref[...] = jnp.full(m_ref.shape, NEG_INF, F32)
    if l_ref is not None:
        l_ref[...] = jnp.zeros(l_ref.shape, F32)
    acc_ref[...] = jnp.zeros(acc_ref.shape, F32)


def _resident(shape, index_map):
    return pl.BlockSpec(shape, index_map, pipeline_mode=pl.Buffered(1))


def _fox_prompt_body(q_ref, qx_ref, qn_ref, qxn_ref, k_ref, kx_ref, vt_ref, gate_ref, o_ref, m_ref, acc_ref,
                     sa_ref, sb_ref, ma_ref, mb_ref, *, t, pairs):
    i = pl.program_id(1)
    heads = 2 * pairs
    buf_a, buf_b = (sa_ref, ma_ref), (sb_ref, mb_ref)
    _init_state(m_ref, None, acc_ref)
    z64 = jnp.zeros((FOX_HD, t), BF16)
    zpad = jnp.zeros((LANES - QEXT_ROWS, t), BF16)

    def augmented_queries(q_blk_ref, qx_blk_ref):
        out = []
        for p in range(pairs):
            q = q_blk_ref[p * HEAD_PAIR_W:(p + 1) * HEAD_PAIR_W, :]
            out.append(jnp.concatenate([q[:FOX_HD], z64, qx_blk_ref[2 * p], zpad], axis=0))
            out.append(jnp.concatenate([z64, q[FOX_HD:], qx_blk_ref[2 * p + 1], zpad], axis=0))
        return out

    qa = augmented_queries(q_ref, qx_ref)
    qa_next = augmented_queries(qn_ref, qxn_ref)

    def keys(j, p):
        off = pl.multiple_of(j * t, t)
        return jnp.concatenate([k_ref[pl.ds(off, t), p * HEAD_PAIR_W:(p + 1) * HEAD_PAIR_W],
                                kx_ref[pl.ds(off, t), :]], axis=1)

    def consume(j, cur, h, diagonal, between=None):
        s = cur[0][h]
        if between is not None:
            between()
        if diagonal:
            kk = lax.broadcasted_iota(jnp.int32, (t, t), 0)
            qq = lax.broadcasted_iota(jnp.int32, (t, t), 1)
            s = jnp.where(kk <= qq, s, NEG_INF)
            tile_max = jnp.max(s, axis=0, keepdims=True)
        else:
            tile_max = cur[1][h]
        _online_update_t(s, tile_max, h, m_ref, acc_ref,
                         _with_sum_rows(vt_ref[j, h * FOX_HD:(h + 1) * FOX_HD, :]))

    def half_step(j, cur, nxt):
        ka = {}

        def prefetch(h):
            if h // 2 not in ka:
                ka[h // 2] = keys(j + 1, h // 2)
            _store_scores(nxt, h, _dot(ka[h // 2], qa[h]))

        for h in range(min(SCORES_AHEAD, heads)):
            prefetch(h)
        for h in range(heads):
            consume(j, cur, h, False)
            if h + SCORES_AHEAD < heads:
                prefetch(h + SCORES_AHEAD)

    def first_tile_scores(h, queries):
        _store_scores(buf_a, h, _dot(keys(0, h // 2), queries[h]))

    def last_step(j, cur):
        for h in range(heads):
            consume(j, cur, h, True, between=functools.partial(first_tile_scores, h, qa_next))

    @pl.when(i == 0)
    def _():
        for h in range(heads):
            first_tile_scores(h, qa)

    _run_far_tiles(i, half_step, buf_a, buf_b)

    @pl.when(i % 2 == 0)
    def _():
        last_step(i, buf_a)

    @pl.when(i % 2 == 1)
    def _():
        half_step(i - 1, buf_a, buf_b)
        last_step(i, buf_b)

    for p in range(pairs):
        a0, a1 = acc_ref[2 * p], acc_ref[2 * p + 1]
        ot = jnp.concatenate([a0[:FOX_HD] / a0[FOX_HD:FOX_HD + 1], a1[:FOX_HD] / a1[FOX_HD:FOX_HD + 1]], axis=0)
        cols = slice(p * HEAD_PAIR_W, (p + 1) * HEAD_PAIR_W)
        o_ref[:, cols] = (ot.T * gate_ref[:, cols].astype(F32)).astype(BF16)


def _fox_prompt_attention(qt, qext, k16, kext, vt, gate, *, t, pairs):
    n = k16.shape[0]
    nq = n // t
    w = pairs * HEAD_PAIR_W
    heads = 2 * pairs
    assert (FOX_HEADS // 2) % pairs == 0
    return pl.pallas_call(
        functools.partial(_fox_prompt_body, t=t, pairs=pairs),
        grid=(FOX_HEADS // heads, nq),
        in_specs=[pl.BlockSpec((w, t), lambda g, i: (g, i)),
                  pl.BlockSpec((heads, QEXT_ROWS, t), lambda g, i: (g, 0, i)),
                  pl.BlockSpec((w, t), lambda g, i: (g, jnp.minimum(i + 1, nq - 1))),
                  pl.BlockSpec((heads, QEXT_ROWS, t), lambda g, i: (g, 0, jnp.minimum(i + 1, nq - 1))),
                  _resident((n, w), lambda g, i: (0, g)),
                  _resident((n, LANES), lambda g, i: (0, 0)),
                  _resident((nq, w, t), lambda g, i: (0, g, 0)),
                  pl.BlockSpec((t, w), lambda g, i: (i, g))],
        out_specs=pl.BlockSpec((t, w), lambda g, i: (i, g)),
        out_shape=jax.ShapeDtypeStruct((n, FOX_W), BF16),
        scratch_shapes=[pltpu.VMEM((heads, 1, t), F32), pltpu.VMEM((heads, FOX_HD + SUM_ROWS, t), F32),
                        pltpu.VMEM((heads, t, t), F32), pltpu.VMEM((heads, t, t), F32),
                        pltpu.VMEM((heads, 1, t), F32), pltpu.VMEM((heads, 1, t), F32)],
        compiler_params=_cparams(2), name="fox_prompt_attention",
    )(qt, qext, qt, qext, k16, kext, vt, gate)


def _lambda_value(lq1, lk1, lq2, lk2, lam_init):
    a = jnp.sum(lq1 * lk1, axis=-1, keepdims=True)
    b = jnp.sum(lq2 * lk2, axis=-1, keepdims=True)
    return jnp.exp(a) - jnp.exp(b) + lam_init


def _subln_gate(y, g, gate, lam_init):
    ms = jnp.mean(y * y, axis=-1, keepdims=True)
    yn = (y * lax.rsqrt(ms + EPS)) * g
    return ((yn * (1.0 - lam_init)) * gate.astype(F32)).astype(BF16)


def _diff_prompt_body(q_ref, qn_ref, k_ref, vt_ref, bd_ref, bs_ref, gate_ref, sg_ref, lq1_ref, lk1_ref, lq2_ref,
                      lk2_ref, o_ref, m_ref, acc_ref, sa_ref, sb_ref, ma_ref, mb_ref, *, t, heads, lam_init):
    i = pl.program_id(1)
    w = 2 * DIFF_HD
    buf_a, buf_b = (sa_ref, ma_ref), (sb_ref, mb_ref)
    _init_state(m_ref, None, acc_ref)
    z64 = jnp.zeros((DIFF_HD, t), BF16)

    def padded_queries(q_blk_ref):
        out = []
        for h in range(heads):
            q = q_blk_ref[h * w:(h + 1) * w, :]
            out.append(jnp.concatenate([q[:DIFF_HD], z64], axis=0))
            out.append(jnp.concatenate([z64, q[DIFF_HD:]], axis=0))
        return out

    qa = padded_queries(q_ref)
    qa_next = padded_queries(qn_ref)

    def keys(j, h):
        return k_ref[pl.ds(pl.multiple_of(j * t, t), t), h * w:(h + 1) * w]

    def consume(j, cur, c, bias_ref, between=None):
        h = c // 2
        s = cur[0][c]
        if between is not None:
            between()
        if bias_ref is not None:
            s = s + bias_ref[h]
            tile_max = jnp.max(s, axis=0, keepdims=True)
        else:
            tile_max = cur[1][c]
        _online_update_t(s, tile_max, c, m_ref, acc_ref, _with_sum_rows(vt_ref[j, h * w:(h + 1) * w, :]))

    def half_step(j, cur, nxt, bias_ref=None):
        kk = {}

        def prefetch(c):
            if c // 2 not in kk:
                kk[c // 2] = keys(j + 1, c // 2)
            _store_scores(nxt, c, _dot(kk[c // 2], qa[c]))

        for c in range(min(SCORES_AHEAD, 2 * heads)):
            prefetch(c)
        for c in range(2 * heads):
            consume(j, cur, c, bias_ref)
            if c + SCORES_AHEAD < 2 * heads:
                prefetch(c + SCORES_AHEAD)

    def first_tile_scores(c, queries):
        _store_scores(buf_a, c, _dot(keys(0, c // 2), queries[c]))

    def last_step(j, cur):
        for c in range(2 * heads):
            consume(j, cur, c, bd_ref, between=functools.partial(first_tile_scores, c, qa_next))

    @pl.when(i == 0)
    def _():
        for c in range(2 * heads):
            first_tile_scores(c, qa)

    _run_far_tiles(jnp.maximum(i - 1, 0), half_step, buf_a, buf_b)

    @pl.when(i == 0)
    def _():
        last_step(i, buf_a)

    @pl.when(i % 2 == 1)
    def _():
        half_step(i - 1, buf_a, buf_b, bs_ref)
        last_step(i, buf_b)

    @pl.when((i >= 2) & (i % 2 == 0))
    def _():
        half_step(i - 2, buf_a, buf_b)
        half_step(i - 1, buf_b, buf_a, bs_ref)
        last_step(i, buf_a)

    lam = _lambda_value(lq1_ref[...], lk1_ref[...], lq2_ref[...], lk2_ref[...], lam_init)
    for h in range(heads):
        a0, a1 = acc_ref[2 * h], acc_ref[2 * h + 1]
        ot = a0[:w] / a0[w:w + 1] - lam * (a1[:w] / a1[w:w + 1])
        cols = slice(h * w, (h + 1) * w)
        o_ref[:, cols] = _subln_gate(ot.T, sg_ref[...], gate_ref[:, cols], lam_init)


def _diff_prompt_attention(qt, dk16, vt, bias_diag, bias_sub, gate, subln_g, lams, *, t, heads, lam_init):
    n = dk16.shape[0]
    nq = n // t
    w = heads * 2 * DIFF_HD
    assert DIFF_HEADS % heads == 0 and FOX_W % w == 0
    fox_blocks = FOX_W // w
    vec = lambda width: pl.BlockSpec((1, width), lambda g, i: (0, 0))
    return pl.pallas_call(
        functools.partial(_diff_prompt_body, t=t, heads=heads, lam_init=lam_init),
        grid=(DIFF_HEADS // heads, nq),
        in_specs=[pl.BlockSpec((w, t), lambda g, i: (fox_blocks + g, i)),
                  pl.BlockSpec((w, t), lambda g, i: (fox_blocks + g, jnp.minimum(i + 1, nq - 1))),
                  _resident((n, w), lambda g, i: (0, g)),
                  _resident((nq, w, t), lambda g, i: (0, fox_blocks + g, 0)),
                  pl.BlockSpec((heads, t, t), lambda g, i: (g, 0, 0)),
                  pl.BlockSpec((heads, t, t), lambda g, i: (g, 0, 0)),
                  pl.BlockSpec((t, w), lambda g, i: (i, fox_blocks + g)),
                  vec(2 * DIFF_HD), vec(DIFF_HD), vec(DIFF_HD), vec(DIFF_HD), vec(DIFF_HD)],
        out_specs=pl.BlockSpec((t, w), lambda g, i: (i, g)),
        out_shape=jax.ShapeDtypeStruct((n, DIFF_W), BF16),
        scratch_shapes=[pltpu.VMEM((2 * heads, 1, t), F32),
                        pltpu.VMEM((2 * heads, 2 * DIFF_HD + SUM_ROWS, t), F32),
                        pltpu.VMEM((2 * heads, t, t), F32), pltpu.VMEM((2 * heads, t, t), F32),
                        pltpu.VMEM((2 * heads, 1, t), F32), pltpu.VMEM((2 * heads, 1, t), F32)],
        compiler_params=_cparams(2), name="diff_prompt_attention",
    )(qt, qt, dk16, vt, bias_diag, bias_sub, gate, subln_g, *lams)


def _block_diag_queries(qb):
    lane = lax.broadcasted_iota(jnp.int32, qb.shape, 1)
    zero = jnp.zeros_like(qb)
    return jnp.concatenate([jnp.where(lane < FOX_HD, qb, zero), jnp.where(lane >= FOX_HD, qb, zero)], axis=0)


def _fox_sample_body(q_ref, kc_ref, vc_ref, kn_ref, vn_ref, bc_ref, bn_ref, gate_ref, o_ref, *, s, pairs):
    p_len = kc_ref.shape[-1]
    zrows = jnp.zeros((LANES - s, HEAD_PAIR_W), BF16)
    qi = lax.broadcasted_iota(jnp.int32, (2 * s, LANES), 0)
    ki = lax.broadcasted_iota(jnp.int32, (2 * s, LANES), 1)
    causal = ki <= jnp.where(qi >= s, qi - s, qi)
    lane = lax.broadcasted_iota(jnp.int32, (s, HEAD_PAIR_W), 1)

    def rows_bias(b, width):
        return jnp.concatenate([jnp.broadcast_to(b[0:1], (s, width)), jnp.broadcast_to(b[1:2], (s, width))], axis=0)

    def pair_bias(ref, p):
        rows = ref[0, 2 * p:2 * p + 2, :]
        for g in range(1, FOX_HEADS // (2 * pairs)):
            first = 2 * (g * pairs + p)
            rows = jnp.where(pl.program_id(1) == g, ref[0, first:first + 2, :], rows)
        return rows

    scores = []
    for p in range(pairs):
        cols = slice(p * HEAD_PAIR_W, (p + 1) * HEAD_PAIR_W)
        qbd = _block_diag_queries(q_ref[:, cols])
        kt = kc_ref[0, 2 * p:2 * p + 2].reshape(HEAD_PAIR_W, p_len).astype(BF16)
        kn = jnp.concatenate([kn_ref[:, cols].astype(BF16), zrows], axis=0)
        sc = _dot(qbd, kt) + rows_bias(pair_bias(bc_ref, p), p_len)
        sn = _nt_dot(qbd, kn) + rows_bias(pair_bias(bn_ref, p), LANES)
        scores.append((sc, jnp.where(causal, sn, NEG_INF)))
    for p, (sc, sn) in enumerate(scores):
        cols = slice(p * HEAD_PAIR_W, (p + 1) * HEAD_PAIR_W)
        vt = vc_ref[0, 2 * p:2 * p + 2].reshape(HEAD_PAIR_W, p_len).astype(BF16)
        vn = jnp.concatenate([vn_ref[:, cols].astype(BF16), zrows], axis=0)
        m = jnp.maximum(jnp.max(sc, axis=1, keepdims=True), jnp.max(sn, axis=1, keepdims=True))
        pc = jnp.exp2(sc - m)
        pn = jnp.exp2(sn - m)
        l = jnp.sum(pc, axis=1, keepdims=True) + jnp.sum(pn, axis=1, keepdims=True)
        o = (_nt_dot(pc.astype(BF16), vt) + _dot(pn.astype(BF16), vn)) / l
        out = jnp.where(lane < FOX_HD, o[:s], o[s:])
        o_ref[:, cols] = (out * gate_ref[:, cols].astype(F32)).astype(BF16)


def _fox_sample_attention(q16, cache_kt, cache_vt, fk32, fv32, bias_c, bias_n, gate, *, s, pairs):
    b, _, _, p_len = cache_kt.shape
    groups = FOX_HEADS // (2 * pairs)
    w = pairs * HEAD_PAIR_W
    blk = lambda: pl.BlockSpec((s, w), lambda bi, g: (bi, g))
    cache = lambda: pl.BlockSpec((1, 2 * pairs, FOX_HD, p_len), lambda bi, g: (bi, g, 0, 0))
    return pl.pallas_call(
        functools.partial(_fox_sample_body, s=s, pairs=pairs),
        grid=(b, groups),
        in_specs=[blk(), cache(), cache(), blk(), blk(),
                  pl.BlockSpec((1, FOX_HEADS, p_len), lambda bi, g: (bi, 0, 0)),
                  pl.BlockSpec((1, FOX_HEADS, LANES), lambda bi, g: (bi, 0, 0)),
                  blk()],
        out_specs=blk(),
        out_shape=jax.ShapeDtypeStruct((b * s, FOX_W), BF16),
        compiler_params=_cparams(2), name="fox_sample_attention",
    )(q16, cache_kt, cache_vt, fk32, fv32, bias_c, bias_n, gate)


def _diff_sample_body(q_ref, kc_ref, vc_ref, kn_ref, vn_ref, bc_ref, bn_ref, gate_ref, sg_ref, lq1_ref,
                      lk1_ref, lq2_ref, lk2_ref, o_ref, m_ref, l_ref, acc_ref, *, s, pc, lam_init):
    c = pl.program_id(1)
    last = pl.num_programs(1) - 1
    w = 2 * DIFF_HD

    @pl.when(c == 0)
    def _():
        _init_state(m_ref, l_ref, acc_ref)

    qbd = [_block_diag_queries(q_ref[:, h * w:(h + 1) * w]) for h in range(DIFF_HEADS)]

    def update(h, sc, v16):
        m = m_ref[h]
        mn = jnp.maximum(m, jnp.max(sc, axis=1, keepdims=True))
        alpha = jnp.exp2(m - mn)
        p = jnp.exp2(sc - mn)
        l_ref[h] = alpha * l_ref[h] + jnp.sum(p, axis=1, keepdims=True)
        acc_ref[h] = alpha * acc_ref[h] + _dot(p.astype(BF16), v16)
        m_ref[h] = mn

    def chunk(with_bias):
        scores = [_nt_dot(qbd[h], kc_ref[0, pl.ds(h, pc, stride=DIFF_HEADS), :].astype(BF16))
                  for h in range(DIFF_HEADS)]
        for h, sc in enumerate(scores):
            if with_bias:
                tail = bc_ref.shape[-1]
                near = sc[:, pc - tail:] + jnp.concatenate([bc_ref[h], bc_ref[h]], axis=0)
                sc = jnp.concatenate([sc[:, :pc - tail], near], axis=1) if pc > tail else near
            update(h, sc, vc_ref[0, pl.ds(h, pc, stride=DIFF_HEADS), :].astype(BF16))

    @pl.when(c != last)
    def _():
        chunk(False)

    @pl.when(c == last)
    def _():
        chunk(True)
        lam = _lambda_value(lq1_ref[...], lk1_ref[...], lq2_ref[...], lk2_ref[...], lam_init)
        zrows = jnp.zeros((LANES - s, w), BF16)
        for h in range(DIFF_HEADS):
            cols = slice(h * w, (h + 1) * w)
            new_rows = pl.ds(h, s, stride=DIFF_HEADS)
            kn = jnp.concatenate([kn_ref[new_rows, :].astype(BF16), zrows], axis=0)
            vn = jnp.concatenate([vn_ref[new_rows, :].astype(BF16), zrows], axis=0)
            sn = _nt_dot(qbd[h], kn) + jnp.concatenate([bn_ref[h], bn_ref[h]], axis=0)
            update(h, sn, vn)
            o = acc_ref[h] / l_ref[h]
            o = o[:s] - lam * o[s:]
            o_ref[:, cols] = _subln_gate(o, sg_ref[...], gate_ref[:, cols], lam_init)


def _diff_sample_attention(q16, cache_k, cache_v, dk32, dv32, bias_c, bias_n, gate, subln_g, lams, *, s, pc,
                           lam_init):
    b, rows, w = cache_k.shape
    p_len = rows // DIFF_HEADS
    assert p_len % pc == 0 and bias_c.shape[:2] == (DIFF_HEADS, s) and bias_c.shape[2] <= pc
    cache = lambda: pl.BlockSpec((1, DIFF_HEADS * pc, w), lambda bi, c: (bi, c, 0))
    out = lambda: pl.BlockSpec((s, DIFF_W), lambda bi, c: (bi, 0))
    new = lambda: pl.BlockSpec((DIFF_HEADS * s, w), lambda bi, c: (bi, 0))
    half = lambda: pl.BlockSpec((s, DIFF_W), lambda bi, c: (bi, 1))
    const = lambda shape: pl.BlockSpec(shape, lambda bi, c: (0,) * len(shape))
    return pl.pallas_call(
        functools.partial(_diff_sample_body, s=s, pc=pc, lam_init=lam_init),
        grid=(b, p_len // pc),
        in_specs=[half(), cache(), cache(), new(), new(), const(bias_c.shape), const(bias_n.shape), half(),
                  const((1, w)), const((1, DIFF_HD)), const((1, DIFF_HD)), const((1, DIFF_HD)), const((1, DIFF_HD))],
        out_specs=out(),
        out_shape=jax.ShapeDtypeStruct((b * s, DIFF_W), BF16),
        scratch_shapes=[pltpu.VMEM((DIFF_HEADS, 2 * s, 1), F32), pltpu.VMEM((DIFF_HEADS, 2 * s, 1), F32),
                        pltpu.VMEM((DIFF_HEADS, 2 * s, w), F32)],
        compiler_params=_cparams(2), name="diff_sample_attention",
    )(q16, cache_k, cache_v, dk32, dv32, bias_c, bias_n, gate, subln_g, *lams)


def _outproj_body(mf_ref, md_ref, w_ref, x_ref, g_ref, y_ref):
    out = _dot(mf_ref[...], w_ref[:FOX_W, :]) + _dot(md_ref[...], w_ref[FOX_W:, :])
    ms = jnp.mean(out * out, axis=-1, keepdims=True)
    y_ref[...] = x_ref[...] + (out * lax.rsqrt(ms + EPS)) * g_ref[...]


def _out_projection(mf, md, w_out, x2d, g_post, *, tm):
    n, d = x2d.shape
    tm = min(tm, n)
    assert n % tm == 0
    rows = lambda w: pl.BlockSpec((tm, w), lambda i: (i, 0))
    return pl.pallas_call(
        _outproj_body, grid=(n // tm,),
        in_specs=[rows(FOX_W), rows(DIFF_W), pl.BlockSpec(w_out.shape, lambda i: (0, 0)), rows(d),
                  pl.BlockSpec((1, d), lambda i: (0, 0))],
        out_specs=rows(d), out_shape=jax.ShapeDtypeStruct((n, d), F32),
        compiler_params=_cparams(1), name="out_projection",
    )(mf, md, w_out.astype(BF16), x2d, g_post.reshape(1, d))


def _t5_bucket(rel):
    half = NUM_BUCKETS // 2
    max_exact = half // 2
    ret = jnp.where(rel > 0, half, 0)
    n = jnp.abs(rel)
    nf = jnp.maximum(n, 1).astype(jnp.float32)
    large = max_exact + (jnp.log(nf / max_exact) / math.log(MAX_DISTANCE / max_exact)
                         * (half - max_exact)).astype(jnp.int32)
    large = jnp.minimum(large, half - 1)
    return ret + jnp.where(n < max_exact, n, large)


def _rel_bias_of(rel_bias, rel):
    return rel_bias[_t5_bucket(rel)].astype(F32).T


def _toeplitz(g, rows, cols):
    h = g.shape[0]
    period = rows + cols
    v = jnp.concatenate([g[:, rows - 1:], jnp.zeros((h, 1), g.dtype), g[:, :rows - 1]], axis=1)
    flat = jnp.tile(v, (1, rows))[:, :rows * (period - 1)]
    return flat.reshape(h, rows, period - 1)[:, :, :cols]


def _chunk_mask(bias, row_pos, col_pos, rows_are_keys):
    r = (row_pos // CHUNK)[:, None]
    c = (col_pos // CHUNK)[None, :]
    mask = (r <= c) if rows_are_keys else (c <= r)
    return jnp.where(mask[None], bias * LOG2E, NEG_INF)


def _prompt_layer(x, g_pre, w_in, b_f, lams, subln_g, w_out, g_post, rel_bias, layer_idx, *, t=256, tm=512,
                  tm_out=1024, fox_pairs=4, diff_heads=4):
    b, s, d = x.shape
    assert b == 1 and s % t == 0 and s // LANES >= 1
    x2d = x.reshape(s, d)
    (fk, fv, dk, dv, logft, gate, qt, k16, dk16, vt) = _in_projection(
        x2d, g_pre, w_in, b_f, transposed=True, tm=tm, tk=t)

    _, hi, mid, lo = _prefix_sums(logft.reshape(1, FOX_HEADS, s), seg=LANES, carry=True, scale=LOG2E)
    pieces = jnp.stack([hi, mid, lo], axis=0).reshape(PIECES, FOX_HEADS, s)
    kext = jnp.concatenate([
        (-pieces).transpose(2, 1, 0).reshape(s, ONES_AT), jnp.ones((s, PIECES), BF16),
        jnp.zeros((s, LANES - ONES_AT - PIECES), BF16)], axis=1)
    sel = (jnp.arange(ONES_AT)[None, :] // PIECES == jnp.arange(FOX_HEADS)[:, None]).astype(BF16)
    qext = jnp.concatenate([
        jnp.broadcast_to(sel[:, :, None], (FOX_HEADS, ONES_AT, s)), pieces.transpose(1, 0, 2),
        jnp.zeros((FOX_HEADS, QEXT_ROWS - ONES_AT - PIECES, s), BF16)], axis=1)

    mf = _fox_prompt_attention(qt, qext, k16, kext, vt, gate, t=t, pairs=fox_pairs)

    lam_init = 0.8 - 0.6 * math.exp(-0.3 * layer_idx)
    assert t >= MAX_DISTANCE
    pos = jnp.arange(t, dtype=jnp.int32)
    off = jnp.arange(-(t - 1), t, dtype=jnp.int32)
    far = _rel_bias_of(rel_bias, jnp.full((1,), -2 * t, jnp.int32))
    bias_diag = _chunk_mask(_toeplitz(_rel_bias_of(rel_bias, -off) - far, t, t), pos, pos, True)
    bias_sub = _chunk_mask(_toeplitz(_rel_bias_of(rel_bias, -off - t) - far, t, t), pos, pos + t, True)
    md = _diff_prompt_attention(qt, dk16, vt, bias_diag, bias_sub, gate, subln_g.reshape(1, -1), lams,
                                t=t, heads=diff_heads, lam_init=lam_init)

    y = _out_projection(mf, md, w_out, x2d, g_post, tm=tm_out)
    return y.reshape(b, s, d), (fk, fv, logft.T, dk, dv)


def _sample_layer(x, past, g_pre, w_in, b_f, lams, subln_g, w_out, g_post, rel_bias, layer_idx, *, tm=512,
                  fox_pairs=4, diff_chunk=4096):
    b, s, d = x.shape
    pk, pv, plogf, pdk, pdv = past
    p_len = pk.shape[1]
    assert s <= LANES and LANES % s == 0 and p_len % LANES == 0
    x2d = x.reshape(b * s, d)
    fk, fv, dk, dv, logft, gate, q16 = _in_projection(x2d, g_pre, w_in, b_f, transposed=False, tm=tm, tk=tm)

    bias_c = _prefix_sums(plogf.transpose(0, 2, 1), seg=LANES, carry=True, suffix=True, scale=LOG2E)[0]
    neg_cum_n = _prefix_sums(logft.reshape(1, FOX_HEADS, b * s), seg=s, carry=False, scale=-LOG2E)[0]
    bias_n = jnp.concatenate([neg_cum_n.reshape(FOX_HEADS, b, s).transpose(1, 0, 2),
                              jnp.full((b, FOX_HEADS, LANES - s), NEG_INF, F32)], axis=-1)
    mf = _fox_sample_attention(q16, pk.transpose(0, 2, 3, 1), pv.transpose(0, 2, 3, 1), fk, fv, bias_c, bias_n,
                               gate, s=s, pairs=fox_pairs)

    lam_init = 0.8 - 0.6 * math.exp(-0.3 * layer_idx)
    pc = min(diff_chunk, p_len)
    tail = MAX_DISTANCE
    assert tail % LANES == 0 and pc >= tail
    q_pos = p_len + jnp.arange(s, dtype=jnp.int32)
    far = _rel_bias_of(rel_bias, jnp.full((1,), -2 * p_len, jnp.int32))
    off = jnp.arange(-(s - 1), tail, dtype=jnp.int32)
    bd_c = _chunk_mask(_toeplitz(_rel_bias_of(rel_bias, off - tail) - far, s, tail), q_pos,
                       p_len - tail + jnp.arange(tail, dtype=jnp.int32), False)
    off = jnp.arange(-(s - 1), s, dtype=jnp.int32)
    bd_n = _chunk_mask(_toeplitz(_rel_bias_of(rel_bias, off) - far, s, s), q_pos, q_pos, False)
    bd_n = jnp.concatenate([bd_n, jnp.full((DIFF_HEADS, s, LANES - s), NEG_INF, F32)], axis=-1)
    md = _diff_sample_attention(q16, pdk.reshape(b, p_len * DIFF_HEADS, 2 * DIFF_HD),
                                pdv.reshape(b, p_len * DIFF_HEADS, 2 * DIFF_HD), dk, dv, bd_c, bd_n, gate,
                                subln_g.reshape(1, -1), lams, s=s, pc=pc, lam_init=lam_init)

    y = _out_projection(mf, md, w_out, x2d, g_post, tm=tm)
    return y.reshape(b, s, d), (fk, fv, logft.T, dk, dv)


def kernel(x_prompt, x_sample, cache_fox_k, cache_fox_v, cache_fox_logf, cache_diff_k, cache_diff_v,
           norm_pre_g, w_in, forget_bias, lambda_q1, lambda_k1, lambda_q2, lambda_k2, subln_g, w_out,
           norm_post_g, rel_bias):
    depth = w_in.shape[0]
    y_p, y_s = x_prompt, x_sample
    rows_p, rows_s = [], []
    for l in range(depth):
        lams = tuple(a[l].reshape(1, -1) for a in (lambda_q1, lambda_k1, lambda_q2, lambda_k2))
        params = (norm_pre_g[l], w_in[l], forget_bias[l], lams, subln_g[l], w_out[l], norm_post_g[l], rel_bias, l)
        y_p, new_p = _prompt_layer(y_p, *params)
        past = (cache_fox_k[l], cache_fox_v[l], cache_fox_logf[l], cache_diff_k[l], cache_diff_v[l])
        y_s, new_s = _sample_layer(y_s, past, *params)
        rows_p.append(new_p)
        rows_s.append(new_s)

    bp, sp = x_prompt.shape[:2]
    bs, ss = x_sample.shape[:2]

    def stack(rows, idx, shape):
        return jnp.stack([r[idx].reshape(shape) for r in rows], axis=0)

    outs = [y_p, y_s]
    for rows, (b, s) in ((rows_p, (bp, sp)), (rows_s, (bs, ss))):
        outs += [stack(rows, 0, (b, s, FOX_HEADS, FOX_HD)), stack(rows, 1, (b, s, FOX_HEADS, FOX_HD)),
                 stack(rows, 2, (b, s, FOX_HEADS)), stack(rows, 3, (b, s, DIFF_HEADS, 2 * DIFF_HD)),
                 stack(rows, 4, (b, s, DIFF_HEADS, 2 * DIFF_HD))]
    return tuple(outs)
```

```python
import functools
import math

import jax
import jax.numpy as jnp
import numpy as np
from jax import lax
from jax.experimental import pallas as pl
from jax.experimental.pallas import tpu as pltpu

F32 = jnp.float32
BF16 = jnp.bfloat16

FOX_HEADS = 8
FOX_HD = 64
DIFF_HEADS = 4
DIFF_HD = 64
FOX_W = FOX_HEADS * FOX_HD
DIFF_W = DIFF_HEADS * 2 * DIFF_HD
CHUNK = 64
NUM_BUCKETS = 32
MAX_DISTANCE = 128
EPS = 1e-6
NEG_INF = -1e30
LOG2E = 1.4426950408889634

LANES = 128
BF16_SUBLANES = 16
HEAD_PAIR_W = 2 * FOX_HD
VMEM_LIMIT_BYTES = 56 * 1024 * 1024

PIECES = 3
ONES_AT = PIECES * FOX_HEADS
QEXT_ROWS = 32


def _cparams(n_axes):
    return pltpu.CompilerParams(
        dimension_semantics=("arbitrary",) * n_axes,
        vmem_limit_bytes=VMEM_LIMIT_BYTES,
    )


def _split3(a):
    hi = a.astype(BF16)
    r1 = a - hi.astype(F32)
    mid = r1.astype(BF16)
    lo = (r1 - mid.astype(F32)).astype(BF16)
    return hi, mid, lo


def _nt_dot(a, b):
    return lax.dot_general(a, b, (((1,), (1,)), ((), ())), preferred_element_type=F32)


def _dot(a, b):
    return jnp.dot(a, b, preferred_element_type=F32)


def _inproj_body(x_ref, g_ref, wkv_ref, wg_ref, wff_ref, bf_ref, wq_ref,
                 fk_ref, fv_ref, dk_ref, dv_ref, logf_ref, gate_ref, q_ref, *rest,
                 transposed, tk, qscale):
    x = x_ref[...]
    ms = jnp.mean(x * x, axis=-1, keepdims=True)
    h = (x * lax.rsqrt(ms + EPS)) * g_ref[...]
    hb = h.astype(BF16)

    fk = _dot(hb, wkv_ref[:, 0 * FOX_W:1 * FOX_W])
    fk_ref[...] = fk
    fv = _dot(hb, wkv_ref[:, 1 * FOX_W:2 * FOX_W])
    fv_ref[...] = fv
    dk = _dot(hb, wkv_ref[:, 2 * FOX_W:3 * FOX_W])
    dv = _dot(hb, wkv_ref[:, 3 * FOX_W:4 * FOX_W])
    tm = dk.shape[0]
    for hd in range(DIFF_HEADS):
        cols = slice(hd * 2 * DIFF_HD, (hd + 1) * 2 * DIFF_HD)
        dk_ref[pl.ds(hd, tm, stride=DIFF_HEADS), :] = dk[:, cols]
        dv_ref[pl.ds(hd, tm, stride=DIFF_HEADS), :] = dv[:, cols]

    fft = _nt_dot(wff_ref[...], hb)
    logf_ref[...] = jax.nn.log_sigmoid(fft + bf_ref[...])[:FOX_HEADS]

    gates = _dot(hb, wg_ref[...])
    gate_ref[...] = (gates * jax.nn.sigmoid(gates)).astype(BF16)

    q = _dot(hb, wq_ref[...]) * qscale
    if transposed:
        k16_ref, dk16_ref, vt_ref = rest
        k16_ref[...] = fk.astype(BF16)
        dk16_ref[...] = dk.astype(BF16)
        q_ref[...] = q.T.astype(BF16)
        vt = jnp.concatenate([fv, dv], axis=1).T.astype(BF16)
        for c in range(vt.shape[1] // tk):
            vt_ref[c] = vt[:, c * tk:(c + 1) * tk]
    else:
        q_ref[...] = q.astype(BF16)


def _in_projection(x2d, g_pre, w_in, b_f, *, transposed, tm, tk):
    n, d = x2d.shape
    tm = min(tm, n)
    assert n % tm == 0 and (not transposed or tm % tk == 0)
    qscale = FOX_HD ** -0.5 * LOG2E
    o = np.cumsum([0, FOX_W, FOX_W, FOX_W, FOX_HEADS, FOX_W, DIFF_W, DIFF_W, DIFF_W, DIFF_W])
    w_fq, w_fk, w_fv, w_ff, w_fg, w_dq, w_dk, w_dv, w_dg = [w_in[:, o[i]:o[i + 1]] for i in range(9)]
    wkv = jnp.concatenate([w_fk, w_fv, w_dk, w_dv], axis=1).astype(BF16)
    wg = jnp.concatenate([w_fg, w_dg], axis=1).astype(BF16)
    wff = jnp.pad(w_ff.T, ((0, BF16_SUBLANES - FOX_HEADS), (0, 0))).astype(BF16)
    bfp = jnp.pad(b_f.reshape(FOX_HEADS, 1), ((0, BF16_SUBLANES - FOX_HEADS), (0, 0)))
    wq = jnp.concatenate([w_fq, w_dq], axis=1).astype(BF16)

    const = lambda shape: pl.BlockSpec(shape, lambda i: (0,) * len(shape))
    rows = lambda w: pl.BlockSpec((tm, w), lambda i: (i, 0))
    in_specs = [rows(d), const((1, d)), const(wkv.shape), const(wg.shape), const(wff.shape),
                const(bfp.shape), const(wq.shape)]
    by_head = jax.ShapeDtypeStruct((DIFF_HEADS * n, 2 * DIFF_HD), F32)
    by_head_spec = pl.BlockSpec((DIFF_HEADS * tm, 2 * DIFF_HD), lambda i: (i, 0))
    out_shape = [jax.ShapeDtypeStruct((n, FOX_W), F32)] * 2 + [by_head] * 2
    out_shape += [jax.ShapeDtypeStruct((FOX_HEADS, n), F32), jax.ShapeDtypeStruct((n, FOX_W + DIFF_W), BF16)]
    out_specs = [rows(FOX_W), rows(FOX_W), by_head_spec, by_head_spec,
                 pl.BlockSpec((FOX_HEADS, tm), lambda i: (0, i)), rows(FOX_W + DIFF_W)]
    if transposed:
        out_shape += [jax.ShapeDtypeStruct((FOX_W + DIFF_W, n), BF16),
                      jax.ShapeDtypeStruct((n, FOX_W), BF16), jax.ShapeDtypeStruct((n, DIFF_W), BF16),
                      jax.ShapeDtypeStruct((n // tk, FOX_W + DIFF_W, tk), BF16)]
        out_specs += [pl.BlockSpec((FOX_W + DIFF_W, tm), lambda i: (0, i)), rows(FOX_W), rows(DIFF_W),
                      pl.BlockSpec((tm // tk, FOX_W + DIFF_W, tk), lambda i: (i, 0, 0))]
    else:
        out_shape += [jax.ShapeDtypeStruct((n, FOX_W + DIFF_W), BF16)]
        out_specs += [rows(FOX_W + DIFF_W)]
    return pl.pallas_call(
        functools.partial(_inproj_body, transposed=transposed, tk=tk, qscale=qscale),
        grid=(n // tm,), in_specs=in_specs, out_specs=out_specs, out_shape=out_shape,
        compiler_params=_cparams(1), name="in_projection_t" if transposed else "in_projection_r",
    )(x2d, g_pre.reshape(1, d), wkv, wg, wff, bfp, wq)


PREFIX_ROWS_PER_STEP = 512


def _prefix_body(x_ref, val_ref, hi_ref, mid_ref, lo_ref, *, seg, carry, suffix, scale):
    nb, heads, length = x_ref.shape
    slabs = length // LANES
    rows = nb * slabs * heads
    x = jnp.concatenate([x_ref[g, :, c * LANES:(c + 1) * LANES] for g in range(nb) for c in range(slabs)], axis=0)
    a = lax.broadcasted_iota(jnp.int32, (LANES, LANES), 0)
    b = lax.broadcasted_iota(jnp.int32, (LANES, LANES), 1)
    sh = int(math.log2(seg))
    upper = ((a <= b) & ((a >> sh) == (b >> sh))).astype(BF16)
    pieces = _split3(x)
    w = sum(_dot(p, upper) for p in pieces)
    if carry:
        ones = jnp.ones((LANES, LANES), BF16)
        tot3 = _split3(sum(_dot(p, ones) for p in pieces))
        ri = lax.broadcasted_iota(jnp.int32, (rows, rows), 0)
        ci = lax.broadcasted_iota(jnp.int32, (rows, rows), 1)
        gsh = int(math.log2(slabs * heads))
        same_seq = ((ri >> gsh) == (ci >> gsh)) & ((ri & (heads - 1)) == (ci & (heads - 1)))
        if suffix:
            w = sum(_dot((same_seq & (ci > ri)).astype(BF16), p) + p.astype(F32) for p in tot3) - w
        else:
            w = w + sum(_dot((same_seq & (ci < ri)).astype(BF16), p) for p in tot3)
    val = w * scale
    hi, mid, lo = _split3(val)
    for g in range(nb):
        for c in range(slabs):
            r0 = (g * slabs + c) * heads
            lanes = slice(c * LANES, (c + 1) * LANES)
            val_ref[g, :, lanes] = val[r0:r0 + heads]
            hi_ref[g, :, lanes] = hi[r0:r0 + heads]
            mid_ref[g, :, lanes] = mid[r0:r0 + heads]
            lo_ref[g, :, lanes] = lo[r0:r0 + heads]


def _prefix_sums(x, *, seg, carry, suffix=False, scale=1.0):
    groups, heads, length = x.shape
    slabs = length // LANES
    assert length % LANES == 0 and seg & (seg - 1) == 0 and (seg == LANES if carry else seg <= LANES)
    assert heads == FOX_HEADS and (slabs * heads) & (slabs * heads - 1) == 0
    nb = max(1, min(groups, PREFIX_ROWS_PER_STEP // (slabs * heads)))
    assert groups % nb == 0
    spec = pl.BlockSpec((nb, heads, length), lambda i: (i, 0, 0))
    return pl.pallas_call(
        functools.partial(_prefix_body, seg=seg, carry=carry, suffix=suffix, scale=scale),
        grid=(groups // nb,), in_specs=[spec], out_specs=[spec] * 4,
        out_shape=[jax.ShapeDtypeStruct(x.shape, F32)] + [jax.ShapeDtypeStruct(x.shape, BF16)] * 3,
        compiler_params=_cparams(1), name="prefix_sums",
    )(x)


SUM_ROWS = 16
SCORES_AHEAD = 2
FAR_UNROLL = 8


def _with_sum_rows(vt):
    row = lax.broadcasted_iota(jnp.int32, (SUM_ROWS, vt.shape[1]), 0)
    return jnp.concatenate([vt, jnp.where(row == 0, 1.0, 0.0).astype(BF16)], axis=0)


def _store_scores(buf, h, s):
    buf[0][h] = s
    buf[1][h] = jnp.max(s, axis=0, keepdims=True)


def _online_update_t(s, tile_max, h, m_ref, acc_ref, vt_aug):
    m = m_ref[h]
    mn = jnp.maximum(m, tile_max)
    alpha = jnp.exp2(m - mn)
    p = jnp.exp2(s - mn)
    acc_ref[h] = alpha * acc_ref[h] + _dot(vt_aug, p.astype(BF16))
    m_ref[h] = mn


def _run_far_tiles(n, half_step, buf_a, buf_b):
    def unrolled(base, count):
        bufs = (buf_a, buf_b)
        for u in range(count):
            half_step(base + u, bufs[u % 2], bufs[(u + 1) % 2])

    def body(jj, c):
        unrolled(FAR_UNROLL * jj, FAR_UNROLL)
        return c

    lax.fori_loop(0, n // FAR_UNROLL, body, 0)
    for count in range(FAR_UNROLL - 2, 0, -2):
        @pl.when(n % FAR_UNROLL // 2 * 2 == count)
        def _():
            unrolled(n // FAR_UNROLL * FAR_UNROLL, count)


def _init_state(m_ref, l_ref, acc_ref):
    m_ref[...] = jnp.full(m_ref.shape, NEG_INF, F32)
    if l_ref is not None:
        l_ref[...] = jnp.zeros(l_ref.shape, F32)
    acc_ref[...] = jnp.zeros(acc_ref.shape, F32)


def _resident(shape, index_map):
    return pl.BlockSpec(shape, index_map, pipeline_mode=pl.Buffered(1))


def _fox_prompt_body(q_ref, qx_ref, qn_ref, qxn_ref, k_ref, kx_ref, vt_ref, gate_ref, o_ref, m_ref, acc_ref,
                     sa_ref, sb_ref, ma_ref, mb_ref, *, t, pairs):
    i = pl.program_id(1)
    heads = 2 * pairs
    buf_a, buf_b = (sa_ref, ma_ref), (sb_ref, mb_ref)
    _init_state(m_ref, None, acc_ref)
    z64 = jnp.zeros((FOX_HD, t), BF16)
    zpad = jnp.zeros((LANES - QEXT_ROWS, t), BF16)

    def augmented_queries(q_blk_ref, qx_blk_ref):
        out = []
        for p in range(pairs):
            q = q_blk_ref[p * HEAD_PAIR_W:(p + 1) * HEAD_PAIR_W, :]
            out.append(jnp.concatenate([q[:FOX_HD], z64, qx_blk_ref[2 * p], zpad], axis=0))
            out.append(jnp.concatenate([z64, q[FOX_HD:], qx_blk_ref[2 * p + 1], zpad], axis=0))
        return out

    qa = augmented_queries(q_ref, qx_ref)
    qa_next = augmented_queries(qn_ref, qxn_ref)

    def keys(j, p):
        off = pl.multiple_of(j * t, t)
        return jnp.concatenate([k_ref[pl.ds(off, t), p * HEAD_PAIR_W:(p + 1) * HEAD_PAIR_W],
                                kx_ref[pl.ds(off, t), :]], axis=1)

    def consume(j, cur, h, diagonal, between=None):
        s = cur[0][h]
        if between is not None:
            between()
        if diagonal:
            kk = lax.broadcasted_iota(jnp.int32, (t, t), 0)
            qq = lax.broadcasted_iota(jnp.int32, (t, t), 1)
            s = jnp.where(kk <= qq, s, NEG_INF)
            tile_max = jnp.max(s, axis=0, keepdims=True)
        else:
            tile_max = cur[1][h]
        _online_update_t(s, tile_max, h, m_ref, acc_ref,
                         _with_sum_rows(vt_ref[j, h * FOX_HD:(h + 1) * FOX_HD, :]))

    def half_step(j, cur, nxt):
        ka = {}

        def prefetch(h):
            if h // 2 not in ka:
                ka[h // 2] = keys(j + 1, h // 2)
            _store_scores(nxt, h, _dot(ka[h // 2], qa[h]))

        for h in range(min(SCORES_AHEAD, heads)):
            prefetch(h)
        for h in range(heads):
            consume(j, cur, h, False)
            if h + SCORES_AHEAD < heads:
                prefetch(h + SCORES_AHEAD)

    def first_tile_scores(h, queries):
        _store_scores(buf_a, h, _dot(keys(0, h // 2), queries[h]))

    def last_step(j, cur):
        for h in range(heads):
            consume(j, cur, h, True, between=functools.partial(first_tile_scores, h, qa_next))

    @pl.when(i == 0)
    def _():
        for h in range(heads):
            first_tile_scores(h, qa)

    _run_far_tiles(i, half_step, buf_a, buf_b)

    @pl.when(i % 2 == 0)
    def _():
        last_step(i, buf_a)

    @pl.when(i % 2 == 1)
    def _():
        half_step(i - 1, buf_a, buf_b)
        last_step(i, buf_b)

    for p in range(pairs):
        a0, a1 = acc_ref[2 * p], acc_ref[2 * p + 1]
        ot = jnp.concatenate([a0[:FOX_HD] / a0[FOX_HD:FOX_HD + 1], a1[:FOX_HD] / a1[FOX_HD:FOX_HD + 1]], axis=0)
        cols = slice(p * HEAD_PAIR_W, (p + 1) * HEAD_PAIR_W)
        o_ref[:, cols] = (ot.T * gate_ref[:, cols].astype(F32)).astype(BF16)


def _fox_prompt_attention(qt, qext, k16, kext, vt, gate, *, t, pairs):
    n = k16.shape[0]
    nq = n // t
    w = pairs * HEAD_PAIR_W
    heads = 2 * pairs
    assert (FOX_HEADS // 2) % pairs == 0
    return pl.pallas_call(
        functools.partial(_fox_prompt_body, t=t, pairs=pairs),
        grid=(FOX_HEADS // heads, nq),
        in_specs=[pl.BlockSpec((w, t), lambda g, i: (g, i)),
                  pl.BlockSpec((heads, QEXT_ROWS, t), lambda g, i: (g, 0, i)),
                  pl.BlockSpec((w, t), lambda g, i: (g, jnp.minimum(i + 1, nq - 1))),
                  pl.BlockSpec((heads, QEXT_ROWS, t), lambda g, i: (g, 0, jnp.minimum(i + 1, nq - 1))),
                  _resident((n, w), lambda g, i: (0, g)),
                  _resident((n, LANES), lambda g, i: (0, 0)),
                  _resident((nq, w, t), lambda g, i: (0, g, 0)),
                  pl.BlockSpec((t, w), lambda g, i: (i, g))],
        out_specs=pl.BlockSpec((t, w), lambda g, i: (i, g)),
        out_shape=jax.ShapeDtypeStruct((n, FOX_W), BF16),
        scratch_shapes=[pltpu.VMEM((heads, 1, t), F32), pltpu.VMEM((heads, FOX_HD + SUM_ROWS, t), F32),
                        pltpu.VMEM((heads, t, t), F32), pltpu.VMEM((heads, t, t), F32),
                        pltpu.VMEM((heads, 1, t), F32), pltpu.VMEM((heads, 1, t), F32)],
        compiler_params=_cparams(2), name="fox_prompt_attention",
    )(qt, qext, qt, qext, k16, kext, vt, gate)


def _lambda_value(lq1, lk1, lq2, lk2, lam_init):
    a = jnp.sum(lq1 * lk1, axis=-1, keepdims=True)
    b = jnp.sum(lq2 * lk2, axis=-1, keepdims=True)
    return jnp.exp(a) - jnp.exp(b) + lam_init


def _subln_gate(y, g, gate, lam_init):
    ms = jnp.mean(y * y, axis=-1, keepdims=True)
    yn = (y * lax.rsqrt(ms + EPS)) * g
    return ((yn * (1.0 - lam_init)) * gate.astype(F32)).astype(BF16)


def _diff_prompt_body(q_ref, qn_ref, k_ref, vt_ref, bd_ref, bs_ref, gate_ref, sg_ref, lq1_ref, lk1_ref, lq2_ref,
                      lk2_ref, o_ref, m_ref, acc_ref, sa_ref, sb_ref, ma_ref, mb_ref, *, t, heads, lam_init):
    i = pl.program_id(1)
    w = 2 * DIFF_HD
    buf_a, buf_b = (sa_ref, ma_ref), (sb_ref, mb_ref)
    _init_state(m_ref, None, acc_ref)
    z64 = jnp.zeros((DIFF_HD, t), BF16)

    def padded_queries(q_blk_ref):
        out = []
        for h in range(heads):
            q = q_blk_ref[h * w:(h + 1) * w, :]
            out.append(jnp.concatenate([q[:DIFF_HD], z64], axis=0))
            out.append(jnp.concatenate([z64, q[DIFF_HD:]], axis=0))
        return out

    qa = padded_queries(q_ref)
    qa_next = padded_queries(qn_ref)

    def keys(j, h):
        return k_ref[pl.ds(pl.multiple_of(j * t, t), t), h * w:(h + 1) * w]

    def consume(j, cur, c, bias_ref, between=None):
        h = c // 2
        s = cur[0][c]
        if between is not None:
            between()
        if bias_ref is not None:
            s = s + bias_ref[h]
            tile_max = jnp.max(s, axis=0, keepdims=True)
        else:
            tile_max = cur[1][c]
        _online_update_t(s, tile_max, c, m_ref, acc_ref, _with_sum_rows(vt_ref[j, h * w:(h + 1) * w, :]))

    def half_step(j, cur, nxt, bias_ref=None):
        kk = {}

        def prefetch(c):
            if c // 2 not in kk:
                kk[c // 2] = keys(j + 1, c // 2)
            _store_scores(nxt, c, _dot(kk[c // 2], qa[c]))

        for c in range(min(SCORES_AHEAD, 2 * heads)):
            prefetch(c)
        for c in range(2 * heads):
            consume(j, cur, c, bias_ref)
            if c + SCORES_AHEAD < 2 * heads:
                prefetch(c + SCORES_AHEAD)

    def first_tile_scores(c, queries):
        _store_scores(buf_a, c, _dot(keys(0, c // 2), queries[c]))

    def last_step(j, cur):
        for c in range(2 * heads):
            consume(j, cur, c, bd_ref, between=functools.partial(first_tile_scores, c, qa_next))

    @pl.when(i == 0)
    def _():
        for c in range(2 * heads):
            first_tile_scores(c, qa)

    _run_far_tiles(jnp.maximum(i - 1, 0), half_step, buf_a, buf_b)

    @pl.when(i == 0)
    def _():
        last_step(i, buf_a)

    @pl.when(i % 2 == 1)
    def _():
        half_step(i - 1, buf_a, buf_b, bs_ref)
        last_step(i, buf_b)

    @pl.when((i >= 2) & (i % 2 == 0))
    def _():
        half_step(i - 2, buf_a, buf_b)
        half_step(i - 1, buf_b, buf_a, bs_ref)
        last_step(i, buf_a)

    lam = _lambda_value(lq1_ref[...], lk1_ref[...], lq2_ref[...], lk2_ref[...], lam_init)
    for h in range(heads):
        a0, a1 = acc_ref[2 * h], acc_ref[2 * h + 1]
        ot = a0[:w] / a0[w:w + 1] - lam * (a1[:w] / a1[w:w + 1])
        cols = slice(h * w, (h + 1) * w)
        o_ref[:, cols] = _subln_gate(ot.T, sg_ref[...], gate_ref[:, cols], lam_init)


def _diff_prompt_attention(qt, dk16, vt, bias_diag, bias_sub, gate, subln_g, lams, *, t, heads, lam_init):
    n = dk16.shape[0]
    nq = n // t
    w = heads * 2 * DIFF_HD
    assert DIFF_HEADS % heads == 0 and FOX_W % w == 0
    fox_blocks = FOX_W // w
    vec = lambda width: pl.BlockSpec((1, width), lambda g, i: (0, 0))
    return pl.pallas_call(
        functools.partial(_diff_prompt_body, t=t, heads=heads, lam_init=lam_init),
        grid=(DIFF_HEADS // heads, nq),
        in_specs=[pl.BlockSpec((w, t), lambda g, i: (fox_blocks + g, i)),
                  pl.BlockSpec((w, t), lambda g, i: (fox_blocks + g, jnp.minimum(i + 1, nq - 1))),
                  _resident((n, w), lambda g, i: (0, g)),
                  _resident((nq, w, t), lambda g, i: (0, fox_blocks + g, 0)),
                  pl.BlockSpec((heads, t, t), lambda g, i: (g, 0, 0)),
                  pl.BlockSpec((heads, t, t), lambda g, i: (g, 0, 0)),
                  pl.BlockSpec((t, w), lambda g, i: (i, fox_blocks + g)),
                  vec(2 * DIFF_HD), vec(DIFF_HD), vec(DIFF_HD), vec(DIFF_HD), vec(DIFF_HD)],
        out_specs=pl.BlockSpec((t, w), lambda g, i: (i, g)),
        out_shape=jax.ShapeDtypeStruct((n, DIFF_W), BF16),
        scratch_shapes=[pltpu.VMEM((2 * heads, 1, t), F32),
                        pltpu.VMEM((2 * heads, 2 * DIFF_HD + SUM_ROWS, t), F32),
                        pltpu.VMEM((2 * heads, t, t), F32), pltpu.VMEM((2 * heads, t, t), F32),
                        pltpu.VMEM((2 * heads, 1, t), F32), pltpu.VMEM((2 * heads, 1, t), F32)],
        compiler_params=_cparams(2), name="diff_prompt_attention",
    )(qt, qt, dk16, vt, bias_diag, bias_sub, gate, subln_g, *lams)


def _block_diag_queries(qb):
    lane = lax.broadcasted_iota(jnp.int32, qb.shape, 1)
    zero = jnp.zeros_like(qb)
    return jnp.concatenate([jnp.where(lane < FOX_HD, qb, zero), jnp.where(lane >= FOX_HD, qb, zero)], axis=0)


def _fox_sample_body(q_ref, kc_ref, vc_ref, kn_ref, vn_ref, bc_ref, bn_ref, gate_ref, o_ref, *, s, pairs):
    p_len = kc_ref.shape[-1]
    zrows = jnp.zeros((LANES - s, HEAD_PAIR_W), BF16)
    qi = lax.broadcasted_iota(jnp.int32, (2 * s, LANES), 0)
    ki = lax.broadcasted_iota(jnp.int32, (2 * s, LANES), 1)
    causal = ki <= jnp.where(qi >= s, qi - s, qi)
    lane = lax.broadcasted_iota(jnp.int32, (s, HEAD_PAIR_W), 1)

    def rows_bias(b, width):
        return jnp.concatenate([jnp.broadcast_to(b[0:1], (s, width)), jnp.broadcast_to(b[1:2], (s, width))], axis=0)

    def pair_bias(ref, p):
        rows = ref[0, 2 * p:2 * p + 2, :]
        for g in range(1, FOX_HEADS // (2 * pairs)):
            first = 2 * (g * pairs + p)
            rows = jnp.where(pl.program_id(1) == g, ref[0, first:first + 2, :], rows)
        return rows

    scores = []
    for p in range(pairs):
        cols = slice(p * HEAD_PAIR_W, (p + 1) * HEAD_PAIR_W)
        qbd = _block_diag_queries(q_ref[:, cols])
        kt = kc_ref[0, 2 * p:2 * p + 2].reshape(HEAD_PAIR_W, p_len).astype(BF16)
        kn = jnp.concatenate([kn_ref[:, cols].astype(BF16), zrows], axis=0)
        sc = _dot(qbd, kt) + rows_bias(pair_bias(bc_ref, p), p_len)
        sn = _nt_dot(qbd, kn) + rows_bias(pair_bias(bn_ref, p), LANES)
        scores.append((sc, jnp.where(causal, sn, NEG_INF)))
    for p, (sc, sn) in enumerate(scores):
        cols = slice(p * HEAD_PAIR_W, (p + 1) * HEAD_PAIR_W)
        vt = vc_ref[0, 2 * p:2 * p + 2].reshape(HEAD_PAIR_W, p_len).astype(BF16)
        vn = jnp.concatenate([vn_ref[:, cols].astype(BF16), zrows], axis=0)
        m = jnp.maximum(jnp.max(sc, axis=1, keepdims=True), jnp.max(sn, axis=1, keepdims=True))
        pc = jnp.exp2(sc - m)
        pn = jnp.exp2(sn - m)
        l = jnp.sum(pc, axis=1, keepdims=True) + jnp.sum(pn, axis=1, keepdims=True)
        o = (_nt_dot(pc.astype(BF16), vt) + _dot(pn.astype(BF16), vn)) / l
        out = jnp.where(lane < FOX_HD, o[:s], o[s:])
        o_ref[:, cols] = (out * gate_ref[:, cols].astype(F32)).astype(BF16)


def _fox_sample_attention(q16, cache_kt, cache_vt, fk32, fv32, bias_c, bias_n, gate, *, s, pairs):
    b, _, _, p_len = cache_kt.shape
    groups = FOX_HEADS // (2 * pairs)
    w = pairs * HEAD_PAIR_W
    blk = lambda: pl.BlockSpec((s, w), lambda bi, g: (bi, g))
    cache = lambda: pl.BlockSpec((1, 2 * pairs, FOX_HD, p_len), lambda bi, g: (bi, g, 0, 0))
    return pl.pallas_call(
        functools.partial(_fox_sample_body, s=s, pairs=pairs),
        grid=(b, groups),
        in_specs=[blk(), cache(), cache(), blk(), blk(),
                  pl.BlockSpec((1, FOX_HEADS, p_len), lambda bi, g: (bi, 0, 0)),
                  pl.BlockSpec((1, FOX_HEADS, LANES), lambda bi, g: (bi, 0, 0)),
                  blk()],
        out_specs=blk(),
        out_shape=jax.ShapeDtypeStruct((b * s, FOX_W), BF16),
        compiler_params=_cparams(2), name="fox_sample_attention",
    )(q16, cache_kt, cache_vt, fk32, fv32, bias_c, bias_n, gate)


def _diff_sample_body(q_ref, kc_ref, vc_ref, kn_ref, vn_ref, bc_ref, bn_ref, gate_ref, sg_ref, lq1_ref,
                      lk1_ref, lq2_ref, lk2_ref, o_ref, m_ref, l_ref, acc_ref, *, s, pc, lam_init):
    c = pl.program_id(1)
    last = pl.num_programs(1) - 1
    w = 2 * DIFF_HD

    @pl.when(c == 0)
    def _():
        _init_state(m_ref, l_ref, acc_ref)

    qbd = [_block_diag_queries(q_ref[:, h * w:(h + 1) * w]) for h in range(DIFF_HEADS)]

    def update(h, sc, v16):
        m = m_ref[h]
        mn = jnp.maximum(m, jnp.max(sc, axis=1, keepdims=True))
        alpha = jnp.exp2(m - mn)
        p = jnp.exp2(sc - mn)
        l_ref[h] = alpha * l_ref[h] + jnp.sum(p, axis=1, keepdims=True)
        acc_ref[h] = alpha * acc_ref[h] + _dot(p.astype(BF16), v16)
        m_ref[h] = mn

    def chunk(with_bias):
        scores = [_nt_dot(qbd[h], kc_ref[0, pl.ds(h, pc, stride=DIFF_HEADS), :].astype(BF16))
                  for h in range(DIFF_HEADS)]
        for h, sc in enumerate(scores):
            if with_bias:
                tail = bc_ref.shape[-1]
                near = sc[:, pc - tail:] + jnp.concatenate([bc_ref[h], bc_ref[h]], axis=0)
                sc = jnp.concatenate([sc[:, :pc - tail], near], axis=1) if pc > tail else near
            update(h, sc, vc_ref[0, pl.ds(h, pc, stride=DIFF_HEADS), :].astype(BF16))

    @pl.when(c != last)
    def _():
        chunk(False)

    @pl.when(c == last)
    def _():
        chunk(True)
        lam = _lambda_value(lq1_ref[...], lk1_ref[...], lq2_ref[...], lk2_ref[...], lam_init)
        zrows = jnp.zeros((LANES - s, w), BF16)
        for h in range(DIFF_HEADS):
            cols = slice(h * w, (h + 1) * w)
            new_rows = pl.ds(h, s, stride=DIFF_HEADS)
            kn = jnp.concatenate([kn_ref[new_rows, :].astype(BF16), zrows], axis=0)
            vn = jnp.concatenate([vn_ref[new_rows, :].astype(BF16), zrows], axis=0)
            sn = _nt_dot(qbd[h], kn) + jnp.concatenate([bn_ref[h], bn_ref[h]], axis=0)
            update(h, sn, vn)
            o = acc_ref[h] / l_ref[h]
            o = o[:s] - lam * o[s:]
            o_ref[:, cols] = _subln_gate(o, sg_ref[...], gate_ref[:, cols], lam_init)


def _diff_sample_attention(q16, cache_k, cache_v, dk32, dv32, bias_c, bias_n, gate, subln_g, lams, *, s, pc,
                           lam_init):
    b, rows, w = cache_k.shape
    p_len = rows // DIFF_HEADS
    assert p_len % pc == 0 and bias_c.shape[:2] == (DIFF_HEADS, s) and bias_c.shape[2] <= pc
    cache = lambda: pl.BlockSpec((1, DIFF_HEADS * pc, w), lambda bi, c: (bi, c, 0))
    out = lambda: pl.BlockSpec((s, DIFF_W), lambda bi, c: (bi, 0))
    new = lambda: pl.BlockSpec((DIFF_HEADS * s, w), lambda bi, c: (bi, 0))
    half = lambda: pl.BlockSpec((s, DIFF_W), lambda bi, c: (bi, 1))
    const = lambda shape: pl.BlockSpec(shape, lambda bi, c: (0,) * len(shape))
    return pl.pallas_call(
        functools.partial(_diff_sample_body, s=s, pc=pc, lam_init=lam_init),
        grid=(b, p_len // pc),
        in_specs=[half(), cache(), cache(), new(), new(), const(bias_c.shape), const(bias_n.shape), half(),
                  const((1, w)), const((1, DIFF_HD)), const((1, DIFF_HD)), const((1, DIFF_HD)), const((1, DIFF_HD))],
        out_specs=out(),
        out_shape=jax.ShapeDtypeStruct((b * s, DIFF_W), BF16),
        scratch_shapes=[pltpu.VMEM((DIFF_HEADS, 2 * s, 1), F32), pltpu.VMEM((DIFF_HEADS, 2 * s, 1), F32),
                        pltpu.VMEM((DIFF_HEADS, 2 * s, w), F32)],
        compiler_params=_cparams(2), name="diff_sample_attention",
    )(q16, cache_k, cache_v, dk32, dv32, bias_c, bias_n, gate, subln_g, *lams)


def _outproj_body(mf_ref, md_ref, w_ref, x_ref, g_ref, y_ref):
    out = _dot(mf_ref[...], w_ref[:FOX_W, :]) + _dot(md_ref[...], w_ref[FOX_W:, :])
    ms = jnp.mean(out * out, axis=-1, keepdims=True)
    y_ref[...] = x_ref[...] + (out * lax.rsqrt(ms + EPS)) * g_ref[...]


def _out_projection(mf, md, w_out, x2d, g_post, *, tm):
    n, d = x2d.shape
    tm = min(tm, n)
    assert n % tm == 0
    rows = lambda w: pl.BlockSpec((tm, w), lambda i: (i, 0))
    return pl.pallas_call(
        _outproj_body, grid=(n // tm,),
        in_specs=[rows(FOX_W), rows(DIFF_W), pl.BlockSpec(w_out.shape, lambda i: (0, 0)), rows(d),
                  pl.BlockSpec((1, d), lambda i: (0, 0))],
        out_specs=rows(d), out_shape=jax.ShapeDtypeStruct((n, d), F32),
        compiler_params=_cparams(1), name="out_projection",
    )(mf, md, w_out.astype(BF16), x2d, g_post.reshape(1, d))


def _t5_bucket(rel):
    half = NUM_BUCKETS // 2
    max_exact = half // 2
    ret = jnp.where(rel > 0, half, 0)
    n = jnp.abs(rel)
    nf = jnp.maximum(n, 1).astype(jnp.float32)
    large = max_exact + (jnp.log(nf / max_exact) / math.log(MAX_DISTANCE / max_exact)
                         * (half - max_exact)).astype(jnp.int32)
    large = jnp.minimum(large, half - 1)
    return ret + jnp.where(n < max_exact, n, large)


def _rel_bias_of(rel_bias, rel):
    return rel_bias[_t5_bucket(rel)].astype(F32).T


def _toeplitz(g, rows, cols):
    h = g.shape[0]
    period = rows + cols
    v = jnp.concatenate([g[:, rows - 1:], jnp.zeros((h, 1), g.dtype), g[:, :rows - 1]], axis=1)
    flat = jnp.tile(v, (1, rows))[:, :rows * (period - 1)]
    return flat.reshape(h, rows, period - 1)[:, :, :cols]


def _chunk_mask(bias, row_pos, col_pos, rows_are_keys):
    r = (row_pos // CHUNK)[:, None]
    c = (col_pos // CHUNK)[None, :]
    mask = (r <= c) if rows_are_keys else (c <= r)
    return jnp.where(mask[None], bias * LOG2E, NEG_INF)


def _prompt_layer(x, g_pre, w_in, b_f, lams, subln_g, w_out, g_post, rel_bias, layer_idx, *, t=256, tm=512,
                  tm_out=2048, fox_pairs=4, diff_heads=4):
    b, s, d = x.shape
    assert b == 1 and s % t == 0 and s // LANES >= 1
    x2d = x.reshape(s, d)
    (fk, fv, dk, dv, logft, gate, qt, k16, dk16, vt) = _in_projection(
        x2d, g_pre, w_in, b_f, transposed=True, tm=tm, tk=t)

    _, hi, mid, lo = _prefix_sums(logft.reshape(1, FOX_HEADS, s), seg=LANES, carry=True, scale=LOG2E)
    pieces = jnp.stack([hi, mid, lo], axis=0).reshape(PIECES, FOX_HEADS, s)
    kext = jnp.concatenate([
        (-pieces).transpose(2, 1, 0).reshape(s, ONES_AT), jnp.ones((s, PIECES), BF16),
        jnp.zeros((s, LANES - ONES_AT - PIECES), BF16)], axis=1)
    sel = (jnp.arange(ONES_AT)[None, :] // PIECES == jnp.arange(FOX_HEADS)[:, None]).astype(BF16)
    qext = jnp.concatenate([
        jnp.broadcast_to(sel[:, :, None], (FOX_HEADS, ONES_AT, s)), pieces.transpose(1, 0, 2),
        jnp.zeros((FOX_HEADS, QEXT_ROWS - ONES_AT - PIECES, s), BF16)], axis=1)

    mf = _fox_prompt_attention(qt, qext, k16, kext, vt, gate, t=t, pairs=fox_pairs)

    lam_init = 0.8 - 0.6 * math.exp(-0.3 * layer_idx)
    assert t >= MAX_DISTANCE
    pos = jnp.arange(t, dtype=jnp.int32)
    off = jnp.arange(-(t - 1), t, dtype=jnp.int32)
    far = _rel_bias_of(rel_bias, jnp.full((1,), -2 * t, jnp.int32))
    bias_diag = _chunk_mask(_toeplitz(_rel_bias_of(rel_bias, -off) - far, t, t), pos, pos, True)
    bias_sub = _chunk_mask(_toeplitz(_rel_bias_of(rel_bias, -off - t) - far, t, t), pos, pos + t, True)
    md = _diff_prompt_attention(qt, dk16, vt, bias_diag, bias_sub, gate, subln_g.reshape(1, -1), lams,
                                t=t, heads=diff_heads, lam_init=lam_init)

    y = _out_projection(mf, md, w_out, x2d, g_post, tm=tm_out)
    return y.reshape(b, s, d), (fk, fv, logft.T, dk, dv)


def _sample_layer(x, past, g_pre, w_in, b_f, lams, subln_g, w_out, g_post, rel_bias, layer_idx, *, tm=512,
                  fox_pairs=4, diff_chunk=4096):
    b, s, d = x.shape
    pk, pv, plogf, pdk, pdv = past
    p_len = pk.shape[1]
    assert s <= LANES and LANES % s == 0 and p_len % LANES == 0
    x2d = x.reshape(b * s, d)
    fk, fv, dk, dv, logft, gate, q16 = _in_projection(x2d, g_pre, w_in, b_f, transposed=False, tm=tm, tk=tm)

    bias_c = _prefix_sums(plogf.transpose(0, 2, 1), seg=LANES, carry=True, suffix=True, scale=LOG2E)[0]
    neg_cum_n = _prefix_sums(logft.reshape(1, FOX_HEADS, b * s), seg=s, carry=False, scale=-LOG2E)[0]
    bias_n = jnp.concatenate([neg_cum_n.reshape(FOX_HEADS, b, s).transpose(1, 0, 2),
                              jnp.full((b, FOX_HEADS, LANES - s), NEG_INF, F32)], axis=-1)
    mf = _fox_sample_attention(q16, pk.transpose(0, 2, 3, 1), pv.transpose(0, 2, 3, 1), fk, fv, bias_c, bias_n,
                               gate, s=s, pairs=fox_pairs)

    lam_init = 0.8 - 0.6 * math.exp(-0.3 * layer_idx)
    pc = min(diff_chunk, p_len)
    tail = MAX_DISTANCE
    assert tail % LANES == 0 and pc >= tail
    q_pos = p_len + jnp.arange(s, dtype=jnp.int32)
    far = _rel_bias_of(rel_bias, jnp.full((1,), -2 * p_len, jnp.int32))
    off = jnp.arange(-(s - 1), tail, dtype=jnp.int32)
    bd_c = _chunk_mask(_toeplitz(_rel_bias_of(rel_bias, off - tail) - far, s, tail), q_pos,
                       p_len - tail + jnp.arange(tail, dtype=jnp.int32), False)
    off = jnp.arange(-(s - 1), s, dtype=jnp.int32)
    bd_n = _chunk_mask(_toeplitz(_rel_bias_of(rel_bias, off) - far, s, s), q_pos, q_pos, False)
    bd_n = jnp.concatenate([bd_n, jnp.full((DIFF_HEADS, s, LANES - s), NEG_INF, F32)], axis=-1)
    md = _diff_sample_attention(q16, pdk.reshape(b, p_len * DIFF_HEADS, 2 * DIFF_HD),
                                pdv.reshape(b, p_len * DIFF_HEADS, 2 * DIFF_HD), dk, dv, bd_c, bd_n, gate,
                                subln_g.reshape(1, -1), lams, s=s, pc=pc, lam_init=lam_init)

    y = _out_projection(mf, md, w_out, x2d, g_post, tm=tm)
    return y.reshape(b, s, d), (fk, fv, logft.T, dk, dv)


def kernel(x_prompt, x_sample, cache_fox_k, cache_fox_v, cache_fox_logf, cache_diff_k, cache_diff_v,
           norm_pre_g, w_in, forget_bias, lambda_q1, lambda_k1, lambda_q2, lambda_k2, subln_g, w_out,
           norm_post_g, rel_bias):
    depth = w_in.shape[0]
    y_p, y_s = x_prompt, x_sample
    rows_p, rows_s = [], []
    for l in range(depth):
        lams = tuple(a[l].reshape(1, -1) for a in (lambda_q1, lambda_k1, lambda_q2, lambda_k2))
        params = (norm_pre_g[l], w_in[l], forget_bias[l], lams, subln_g[l], w_out[l], norm_post_g[l], rel_bias, l)
        y_p, new_p = _prompt_layer(y_p, *params)
        past = (cache_fox_k[l], cache_fox_v[l], cache_fox_logf[l], cache_diff_k[l], cache_diff_v[l])
        y_s, new_s = _sample_layer(y_s, past, *params)
        rows_p.append(new_p)
        rows_s.append(new_s)

    bp, sp = x_prompt.shape[:2]
    bs, ss = x_sample.shape[:2]

    def stack(rows, idx, shape):
        return jnp.stack([r[idx].reshape(shape) for r in rows], axis=0)

    outs = [y_p, y_s]
    for rows, (b, s) in ((rows_p, (bp, sp)), (rows_s, (bs, ss))):
        outs += [stack(rows, 0, (b, s, FOX_HEADS, FOX_HD)), stack(rows, 1, (b, s, FOX_HEADS, FOX_HD)),
                 stack(rows, 2, (b, s, FOX_HEADS)), stack(rows, 3, (b, s, DIFF_HEADS, 2 * DIFF_HD)),
                 stack(rows, 4, (b, s, DIFF_HEADS, 2 * DIFF_HD))]
    return tuple(outs)
```

```python
import functools
import math

import jax
import jax.numpy as jnp
import numpy as np
from jax import lax
from jax.experimental import pallas as pl
from jax.experimental.pallas import tpu as pltpu

F32 = jnp.float32
BF16 = jnp.bfloat16

FOX_HEADS = 8
FOX_HD = 64
DIFF_HEADS = 4
DIFF_HD = 64
FOX_W = FOX_HEADS * FOX_HD
DIFF_W = DIFF_HEADS * 2 * DIFF_HD
CHUNK = 64
NUM_BUCKETS = 32
MAX_DISTANCE = 128
EPS = 1e-6
NEG_INF = -1e30
LOG2E = 1.4426950408889634

LANES = 128
BF16_SUBLANES = 16
HEAD_PAIR_W = 2 * FOX_HD
VMEM_LIMIT_BYTES = 56 * 1024 * 1024

PIECES = 3
ONES_AT = PIECES * FOX_HEADS
QEXT_ROWS = 32


def _cparams(n_axes):
    return pltpu.CompilerParams(
        dimension_semantics=("arbitrary",) * n_axes,
        vmem_limit_bytes=VMEM_LIMIT_BYTES,
    )


def _split3(a):
    hi = a.astype(BF16)
    r1 = a - hi.astype(F32)
    mid = r1.astype(BF16)
    lo = (r1 - mid.astype(F32)).astype(BF16)
    return hi, mid, lo


def _nt_dot(a, b):
    return lax.dot_general(a, b, (((1,), (1,)), ((), ())), preferred_element_type=F32)


def _dot(a, b):
    return jnp.dot(a, b, preferred_element_type=F32)


def _inproj_body(x_ref, g_ref, wkv_ref, wg_ref, wff_ref, bf_ref, wq_ref,
                 fk_ref, fv_ref, dk_ref, dv_ref, logf_ref, gate_ref, q_ref, *rest,
                 transposed, tk, qscale):
    x = x_ref[...]
    ms = jnp.mean(x * x, axis=-1, keepdims=True)
    h = (x * lax.rsqrt(ms + EPS)) * g_ref[...]
    hb = h.astype(BF16)

    fk = _dot(hb, wkv_ref[:, 0 * FOX_W:1 * FOX_W])
    fk_ref[...] = fk
    fv = _dot(hb, wkv_ref[:, 1 * FOX_W:2 * FOX_W])
    fv_ref[...] = fv
    dk = _dot(hb, wkv_ref[:, 2 * FOX_W:3 * FOX_W])
    dv = _dot(hb, wkv_ref[:, 3 * FOX_W:4 * FOX_W])
    tm = dk.shape[0]
    for hd in range(DIFF_HEADS):
        cols = slice(hd * 2 * DIFF_HD, (hd + 1) * 2 * DIFF_HD)
        dk_ref[pl.ds(hd, tm, stride=DIFF_HEADS), :] = dk[:, cols]
        dv_ref[pl.ds(hd, tm, stride=DIFF_HEADS), :] = dv[:, cols]

    fft = _nt_dot(wff_ref[...], hb)
    logf_ref[...] = jax.nn.log_sigmoid(fft + bf_ref[...])[:FOX_HEADS]

    gates = _dot(hb, wg_ref[...])
    gate_ref[...] = (gates * jax.nn.sigmoid(gates)).astype(BF16)

    q = _dot(hb, wq_ref[...]) * qscale
    if transposed:
        k16_ref, dk16_ref, vt_ref = rest
        k16_ref[...] = fk.astype(BF16)
        dk16_ref[...] = dk.astype(BF16)
        q_ref[...] = q.T.astype(BF16)
        vt = jnp.concatenate([fv, dv], axis=1).T.astype(BF16)
        for c in range(vt.shape[1] // tk):
            vt_ref[c] = vt[:, c * tk:(c + 1) * tk]
    else:
        q_ref[...] = q.astype(BF16)


def _in_projection(x2d, g_pre, w_in, b_f, *, transposed, tm, tk):
    n, d = x2d.shape
    tm = min(tm, n)
    assert n % tm == 0 and (not transposed or tm % tk == 0)
    qscale = FOX_HD ** -0.5 * LOG2E
    o = np.cumsum([0, FOX_W, FOX_W, FOX_W, FOX_HEADS, FOX_W, DIFF_W, DIFF_W, DIFF_W, DIFF_W])
    w_fq, w_fk, w_fv, w_ff, w_fg, w_dq, w_dk, w_dv, w_dg = [w_in[:, o[i]:o[i + 1]] for i in range(9)]
    wkv = jnp.concatenate([w_fk, w_fv, w_dk, w_dv], axis=1).astype(BF16)
    wg = jnp.concatenate([w_fg, w_dg], axis=1).astype(BF16)
    wff = jnp.pad(w_ff.T, ((0, BF16_SUBLANES - FOX_HEADS), (0, 0))).astype(BF16)
    bfp = jnp.pad(b_f.reshape(FOX_HEADS, 1), ((0, BF16_SUBLANES - FOX_HEADS), (0, 0)))
    wq = jnp.concatenate([w_fq, w_dq], axis=1).astype(BF16)

    const = lambda shape: pl.BlockSpec(shape, lambda i: (0,) * len(shape))
    rows = lambda w: pl.BlockSpec((tm, w), lambda i: (i, 0))
    in_specs = [rows(d), const((1, d)), const(wkv.shape), const(wg.shape), const(wff.shape),
                const(bfp.shape), const(wq.shape)]
    by_head = jax.ShapeDtypeStruct((DIFF_HEADS * n, 2 * DIFF_HD), F32)
    by_head_spec = pl.BlockSpec((DIFF_HEADS * tm, 2 * DIFF_HD), lambda i: (i, 0))
    out_shape = [jax.ShapeDtypeStruct((n, FOX_W), F32)] * 2 + [by_head] * 2
    out_shape += [jax.ShapeDtypeStruct((FOX_HEADS, n), F32), jax.ShapeDtypeStruct((n, FOX_W + DIFF_W), BF16)]
    out_specs = [rows(FOX_W), rows(FOX_W), by_head_spec, by_head_spec,
                 pl.BlockSpec((FOX_HEADS, tm), lambda i: (0, i)), rows(FOX_W + DIFF_W)]
    if transposed:
        out_shape += [jax.ShapeDtypeStruct((FOX_W + DIFF_W, n), BF16),
                      jax.ShapeDtypeStruct((n, FOX_W), BF16), jax.ShapeDtypeStruct((n, DIFF_W), BF16),
                      jax.ShapeDtypeStruct((n // tk, FOX_W + DIFF_W, tk), BF16)]
        out_specs += [pl.BlockSpec((FOX_W + DIFF_W, tm), lambda i: (0, i)), rows(FOX_W), rows(DIFF_W),
                      pl.BlockSpec((tm // tk, FOX_W + DIFF_W, tk), lambda i: (i, 0, 0))]
    else:
        out_shape += [jax.ShapeDtypeStruct((n, FOX_W + DIFF_W), BF16)]
        out_specs += [rows(FOX_W + DIFF_W)]
    return pl.pallas_call(
        functools.partial(_inproj_body, transposed=transposed, tk=tk, qscale=qscale),
        grid=(n // tm,), in_specs=in_specs, out_specs=out_specs, out_shape=out_shape,
        compiler_params=_cparams(1), name="in_projection_t" if transposed else "in_projection_r",
    )(x2d, g_pre.reshape(1, d), wkv, wg, wff, bfp, wq)


PREFIX_ROWS_PER_STEP = 512


def _prefix_body(x_ref, val_ref, hi_ref, mid_ref, lo_ref, *, seg, carry, suffix, scale):
    nb, heads, length = x_ref.shape
    slabs = length // LANES
    rows = nb * slabs * heads
    x = jnp.concatenate([x_ref[g, :, c * LANES:(c + 1) * LANES] for g in range(nb) for c in range(slabs)], axis=0)
    a = lax.broadcasted_iota(jnp.int32, (LANES, LANES), 0)
    b = lax.broadcasted_iota(jnp.int32, (LANES, LANES), 1)
    sh = int(math.log2(seg))
    upper = ((a <= b) & ((a >> sh) == (b >> sh))).astype(BF16)
    pieces = _split3(x)
    w = sum(_dot(p, upper) for p in pieces)
    if carry:
        ones = jnp.ones((LANES, LANES), BF16)
        tot3 = _split3(sum(_dot(p, ones) for p in pieces))
        ri = lax.broadcasted_iota(jnp.int32, (rows, rows), 0)
        ci = lax.broadcasted_iota(jnp.int32, (rows, rows), 1)
        gsh = int(math.log2(slabs * heads))
        same_seq = ((ri >> gsh) == (ci >> gsh)) & ((ri & (heads - 1)) == (ci & (heads - 1)))
        if suffix:
            w = sum(_dot((same_seq & (ci > ri)).astype(BF16), p) + p.astype(F32) for p in tot3) - w
        else:
            w = w + sum(_dot((same_seq & (ci < ri)).astype(BF16), p) for p in tot3)
    val = w * scale
    hi, mid, lo = _split3(val)
    for g in range(nb):
        for c in range(slabs):
            r0 = (g * slabs + c) * heads
            lanes = slice(c * LANES, (c + 1) * LANES)
            val_ref[g, :, lanes] = val[r0:r0 + heads]
            hi_ref[g, :, lanes] = hi[r0:r0 + heads]
            mid_ref[g, :, lanes] = mid[r0:r0 + heads]
            lo_ref[g, :, lanes] = lo[r0:r0 + heads]


def _prefix_sums(x, *, seg, carry, suffix=False, scale=1.0):
    groups, heads, length = x.shape
    slabs = length // LANES
    assert length % LANES == 0 and seg & (seg - 1) == 0 and (seg == LANES if carry else seg <= LANES)
    assert heads == FOX_HEADS and (slabs * heads) & (slabs * heads - 1) == 0
    nb = max(1, min(groups, PREFIX_ROWS_PER_STEP // (slabs * heads)))
    assert groups % nb == 0
    spec = pl.BlockSpec((nb, heads, length), lambda i: (i, 0, 0))
    return pl.pallas_call(
        functools.partial(_prefix_body, seg=seg, carry=carry, suffix=suffix, scale=scale),
        grid=(groups // nb,), in_specs=[spec], out_specs=[spec] * 4,
        out_shape=[jax.ShapeDtypeStruct(x.shape, F32)] + [jax.ShapeDtypeStruct(x.shape, BF16)] * 3,
        compiler_params=_cparams(1), name="prefix_sums",
    )(x)


SUM_ROWS = 16
SCORES_AHEAD = 3
FAR_UNROLL = 8


def _with_sum_rows(vt):
    row = lax.broadcasted_iota(jnp.int32, (SUM_ROWS, vt.shape[1]), 0)
    return jnp.concatenate([vt, jnp.where(row == 0, 1.0, 0.0).astype(BF16)], axis=0)


def _store_scores(buf, h, s):
    buf[0][h] = s
    buf[1][h] = jnp.max(s, axis=0, keepdims=True)


def _online_update_t(s, tile_max, h, m_ref, acc_ref, vt_aug):
    m = m_ref[h]
    mn = jnp.maximum(m, tile_max)
    alpha = jnp.exp2(m - mn)
    p = jnp.exp2(s - mn)
    acc_ref[h] = alpha * acc_ref[h] + _dot(vt_aug, p.astype(BF16))
    m_ref[h] = mn


def _run_far_tiles(n, half_step, buf_a, buf_b):
    def unrolled(base, count):
        bufs = (buf_a, buf_b)
        for u in range(count):
            half_step(base + u, bufs[u % 2], bufs[(u + 1) % 2])

    def body(jj, c):
        unrolled(FAR_UNROLL * jj, FAR_UNROLL)
        return c

    lax.fori_loop(0, n // FAR_UNROLL, body, 0)
    for count in range(FAR_UNROLL - 2, 0, -2):
        @pl.when(n % FAR_UNROLL // 2 * 2 == count)
        def _():
            unrolled(n // FAR_UNROLL * FAR_UNROLL, count)


def _init_state(m_ref, l_ref, acc_ref):
    m_ref[...] = jnp.full(m_ref.shape, NEG_INF, F32)
    if l_ref is not None:
        l_ref[...] = jnp.zeros(l_ref.shape, F32)
    acc_ref[...] = jnp.zeros(acc_ref.shape, F32)


def _resident(shape, index_map):
    return pl.BlockSpec(shape, index_map, pipeline_mode=pl.Buffered(1))


def _fox_prompt_body(q_ref, qx_ref, qn_ref, qxn_ref, k_ref, kx_ref, vt_ref, gate_ref, o_ref, m_ref, acc_ref,
                     sa_ref, sb_ref, ma_ref, mb_ref, *, t, pairs):
    i = pl.program_id(1)
    heads = 2 * pairs
    buf_a, buf_b = (sa_ref, ma_ref), (sb_ref, mb_ref)
    _init_state(m_ref, None, acc_ref)
    z64 = jnp.zeros((FOX_HD, t), BF16)
    zpad = jnp.zeros((LANES - QEXT_ROWS, t), BF16)

    def augmented_queries(q_blk_ref, qx_blk_ref):
        out = []
        for p in range(pairs):
            q = q_blk_ref[p * HEAD_PAIR_W:(p + 1) * HEAD_PAIR_W, :]
            out.append(jnp.concatenate([q[:FOX_HD], z64, qx_blk_ref[2 * p], zpad], axis=0))
            out.append(jnp.concatenate([z64, q[FOX_HD:], qx_blk_ref[2 * p + 1], zpad], axis=0))
        return out

    qa = augmented_queries(q_ref, qx_ref)
    qa_next = augmented_queries(qn_ref, qxn_ref)

    def keys(j, p):
        off = pl.multiple_of(j * t, t)
        return jnp.concatenate([k_ref[pl.ds(off, t), p * HEAD_PAIR_W:(p + 1) * HEAD_PAIR_W],
                                kx_ref[pl.ds(off, t), :]], axis=1)

    def consume(j, cur, h, diagonal, between=None):
        s = cur[0][h]
        if between is not None:
            between()
        if diagonal:
            kk = lax.broadcasted_iota(jnp.int32, (t, t), 0)
            qq = lax.broadcasted_iota(jnp.int32, (t, t), 1)
            s = jnp.where(kk <= qq, s, NEG_INF)
            tile_max = jnp.max(s, axis=0, keepdims=True)
        else:
            tile_max = cur[1][h]
        _online_update_t(s, tile_max, h, m_ref, acc_ref,
                         _with_sum_rows(vt_ref[j, h * FOX_HD:(h + 1) * FOX_HD, :]))

    def half_step(j, cur, nxt):
        ka = {}

        def prefetch(h):
            if h // 2 not in ka:
                ka[h // 2] = keys(j + 1, h // 2)
            _store_scores(nxt, h, _dot(ka[h // 2], qa[h]))

        for h in range(min(SCORES_AHEAD, heads)):
            prefetch(h)
        for h in range(heads):
            consume(j, cur, h, False)
            if h + SCORES_AHEAD < heads:
                prefetch(h + SCORES_AHEAD)

    def first_tile_scores(h, queries):
        _store_scores(buf_a, h, _dot(keys(0, h // 2), queries[h]))

    def last_step(j, cur):
        for h in range(heads):
            consume(j, cur, h, True, between=functools.partial(first_tile_scores, h, qa_next))

    @pl.when(i == 0)
    def _():
        for h in range(heads):
            first_tile_scores(h, qa)

    _run_far_tiles(i, half_step, buf_a, buf_b)

    @pl.when(i % 2 == 0)
    def _():
        last_step(i, buf_a)

    @pl.when(i % 2 == 1)
    def _():
        half_step(i - 1, buf_a, buf_b)
        last_step(i, buf_b)

    for p in range(pairs):
        a0, a1 = acc_ref[2 * p], acc_ref[2 * p + 1]
        ot = jnp.concatenate([a0[:FOX_HD] / a0[FOX_HD:FOX_HD + 1], a1[:FOX_HD] / a1[FOX_HD:FOX_HD + 1]], axis=0)
        cols = slice(p * HEAD_PAIR_W, (p + 1) * HEAD_PAIR_W)
        o_ref[:, cols] = (ot.T * gate_ref[:, cols].astype(F32)).astype(BF16)


def _fox_prompt_attention(qt, qext, k16, kext, vt, gate, *, t, pairs):
    n = k16.shape[0]
    nq = n // t
    w = pairs * HEAD_PAIR_W
    heads = 2 * pairs
    assert (FOX_HEADS // 2) % pairs == 0
    return pl.pallas_call(
        functools.partial(_fox_prompt_body, t=t, pairs=pairs),
        grid=(FOX_HEADS // heads, nq),
        in_specs=[pl.BlockSpec((w, t), lambda g, i: (g, i)),
                  pl.BlockSpec((heads, QEXT_ROWS, t), lambda g, i: (g, 0, i)),
                  pl.BlockSpec((w, t), lambda g, i: (g, jnp.minimum(i + 1, nq - 1))),
                  pl.BlockSpec((heads, QEXT_ROWS, t), lambda g, i: (g, 0, jnp.minimum(i + 1, nq - 1))),
                  _resident((n, w), lambda g, i: (0, g)),
                  _resident((n, LANES), lambda g, i: (0, 0)),
                  _resident((nq, w, t), lambda g, i: (0, g, 0)),
                  pl.BlockSpec((t, w), lambda g, i: (i, g))],
        out_specs=pl.BlockSpec((t, w), lambda g, i: (i, g)),
        out_shape=jax.ShapeDtypeStruct((n, FOX_W), BF16),
        scratch_shapes=[pltpu.VMEM((heads, 1, t), F32), pltpu.VMEM((heads, FOX_HD + SUM_ROWS, t), F32),
                        pltpu.VMEM((heads, t, t), F32), pltpu.VMEM((heads, t, t), F32),
                        pltpu.VMEM((heads, 1, t), F32), pltpu.VMEM((heads, 1, t), F32)],
        compiler_params=_cparams(2), name="fox_prompt_attention",
    )(qt, qext, qt, qext, k16, kext, vt, gate)


def _lambda_value(lq1, lk1, lq2, lk2, lam_init):
    a = jnp.sum(lq1 * lk1, axis=-1, keepdims=True)
    b = jnp.sum(lq2 * lk2, axis=-1, keepdims=True)
    return jnp.exp(a) - jnp.exp(b) + lam_init


def _subln_gate(y, g, gate, lam_init):
    ms = jnp.mean(y * y, axis=-1, keepdims=True)
    yn = (y * lax.rsqrt(ms + EPS)) * g
    return ((yn * (1.0 - lam_init)) * gate.astype(F32)).astype(BF16)


def _diff_prompt_body(q_ref, qn_ref, k_ref, vt_ref, bd_ref, bs_ref, gate_ref, sg_ref, lq1_ref, lk1_ref, lq2_ref,
                      lk2_ref, o_ref, m_ref, acc_ref, sa_ref, sb_ref, ma_ref, mb_ref, *, t, heads, lam_init):
    i = pl.program_id(1)
    w = 2 * DIFF_HD
    buf_a, buf_b = (sa_ref, ma_ref), (sb_ref, mb_ref)
    _init_state(m_ref, None, acc_ref)
    z64 = jnp.zeros((DIFF_HD, t), BF16)

    def padded_queries(q_blk_ref):
        out = []
        for h in range(heads):
            q = q_blk_ref[h * w:(h + 1) * w, :]
            out.append(jnp.concatenate([q[:DIFF_HD], z64], axis=0))
            out.append(jnp.concatenate([z64, q[DIFF_HD:]], axis=0))
        return out

    qa = padded_queries(q_ref)
    qa_next = padded_queries(qn_ref)

    def keys(j, h):
        return k_ref[pl.ds(pl.multiple_of(j * t, t), t), h * w:(h + 1) * w]

    def consume(j, cur, c, bias_ref, between=None):
        h = c // 2
        s = cur[0][c]
        if between is not None:
            between()
        if bias_ref is not None:
            s = s + bias_ref[h]
            tile_max = jnp.max(s, axis=0, keepdims=True)
        else:
            tile_max = cur[1][c]
        _online_update_t(s, tile_max, c, m_ref, acc_ref, _with_sum_rows(vt_ref[j, h * w:(h + 1) * w, :]))

    def half_step(j, cur, nxt, bias_ref=None):
        kk = {}

        def prefetch(c):
            if c // 2 not in kk:
                kk[c // 2] = keys(j + 1, c // 2)
            _store_scores(nxt, c, _dot(kk[c // 2], qa[c]))

        for c in range(min(SCORES_AHEAD, 2 * heads)):
            prefetch(c)
        for c in range(2 * heads):
            consume(j, cur, c, bias_ref)
            if c + SCORES_AHEAD < 2 * heads:
                prefetch(c + SCORES_AHEAD)

    def first_tile_scores(c, queries):
        _store_scores(buf_a, c, _dot(keys(0, c // 2), queries[c]))

    def last_step(j, cur):
        for c in range(2 * heads):
            consume(j, cur, c, bd_ref, between=functools.partial(first_tile_scores, c, qa_next))

    @pl.when(i == 0)
    def _():
        for c in range(2 * heads):
            first_tile_scores(c, qa)

    _run_far_tiles(jnp.maximum(i - 1, 0), half_step, buf_a, buf_b)

    @pl.when(i == 0)
    def _():
        last_step(i, buf_a)

    @pl.when(i % 2 == 1)
    def _():
        half_step(i - 1, buf_a, buf_b, bs_ref)
        last_step(i, buf_b)

    @pl.when((i >= 2) & (i % 2 == 0))
    def _():
        half_step(i - 2, buf_a, buf_b)
        half_step(i - 1, buf_b, buf_a, bs_ref)
        last_step(i, buf_a)

    lam = _lambda_value(lq1_ref[...], lk1_ref[...], lq2_ref[...], lk2_ref[...], lam_init)
    for h in range(heads):
        a0, a1 = acc_ref[2 * h], acc_ref[2 * h + 1]
        ot = a0[:w] / a0[w:w + 1] - lam * (a1[:w] / a1[w:w + 1])
        cols = slice(h * w, (h + 1) * w)
        o_ref[:, cols] = _subln_gate(ot.T, sg_ref[...], gate_ref[:, cols], lam_init)


def _diff_prompt_attention(qt, dk16, vt, bias_diag, bias_sub, gate, subln_g, lams, *, t, heads, lam_init):
    n = dk16.shape[0]
    nq = n // t
    w = heads * 2 * DIFF_HD
    assert DIFF_HEADS % heads == 0 and FOX_W % w == 0
    fox_blocks = FOX_W // w
    vec = lambda width: pl.BlockSpec((1, width), lambda g, i: (0, 0))
    return pl.pallas_call(
        functools.partial(_diff_prompt_body, t=t, heads=heads, lam_init=lam_init),
        grid=(DIFF_HEADS // heads, nq),
        in_specs=[pl.BlockSpec((w, t), lambda g, i: (fox_blocks + g, i)),
                  pl.BlockSpec((w, t), lambda g, i: (fox_blocks + g, jnp.minimum(i + 1, nq - 1))),
                  _resident((n, w), lambda g, i: (0, g)),
                  _resident((nq, w, t), lambda g, i: (0, fox_blocks + g, 0)),
                  pl.BlockSpec((heads, t, t), lambda g, i: (g, 0, 0)),
                  pl.BlockSpec((heads, t, t), lambda g, i: (g, 0, 0)),
                  pl.BlockSpec((t, w), lambda g, i: (i, fox_blocks + g)),
                  vec(2 * DIFF_HD), vec(DIFF_HD), vec(DIFF_HD), vec(DIFF_HD), vec(DIFF_HD)],
        out_specs=pl.BlockSpec((t, w), lambda g, i: (i, g)),
        out_shape=jax.ShapeDtypeStruct((n, DIFF_W), BF16),
        scratch_shapes=[pltpu.VMEM((2 * heads, 1, t), F32),
                        pltpu.VMEM((2 * heads, 2 * DIFF_HD + SUM_ROWS, t), F32),
                        pltpu.VMEM((2 * heads, t, t), F32), pltpu.VMEM((2 * heads, t, t), F32),
                        pltpu.VMEM((2 * heads, 1, t), F32), pltpu.VMEM((2 * heads, 1, t), F32)],
        compiler_params=_cparams(2), name="diff_prompt_attention",
    )(qt, qt, dk16, vt, bias_diag, bias_sub, gate, subln_g, *lams)


def _block_diag_queries(qb):
    lane = lax.broadcasted_iota(jnp.int32, qb.shape, 1)
    zero = jnp.zeros_like(qb)
    return jnp.concatenate([jnp.where(lane < FOX_HD, qb, zero), jnp.where(lane >= FOX_HD, qb, zero)], axis=0)


def _fox_sample_body(q_ref, kc_ref, vc_ref, kn_ref, vn_ref, bc_ref, bn_ref, gate_ref, o_ref, *, s, pairs):
    p_len = kc_ref.shape[-1]
    zrows = jnp.zeros((LANES - s, HEAD_PAIR_W), BF16)
    qi = lax.broadcasted_iota(jnp.int32, (2 * s, LANES), 0)
    ki = lax.broadcasted_iota(jnp.int32, (2 * s, LANES), 1)
    causal = ki <= jnp.where(qi >= s, qi - s, qi)
    lane = lax.broadcasted_iota(jnp.int32, (s, HEAD_PAIR_W), 1)

    def rows_bias(b, width):
        return jnp.concatenate([jnp.broadcast_to(b[0:1], (s, width)), jnp.broadcast_to(b[1:2], (s, width))], axis=0)

    def pair_bias(ref, p):
        rows = ref[0, 2 * p:2 * p + 2, :]
        for g in range(1, FOX_HEADS // (2 * pairs)):
            first = 2 * (g * pairs + p)
            rows = jnp.where(pl.program_id(1) == g, ref[0, first:first + 2, :], rows)
        return rows

    scores = []
    for p in range(pairs):
        cols = slice(p * HEAD_PAIR_W, (p + 1) * HEAD_PAIR_W)
        qbd = _block_diag_queries(q_ref[:, cols])
        kt = kc_ref[0, 2 * p:2 * p + 2].reshape(HEAD_PAIR_W, p_len).astype(BF16)
        kn = jnp.concatenate([kn_ref[:, cols].astype(BF16), zrows], axis=0)
        sc = _dot(qbd, kt) + rows_bias(pair_bias(bc_ref, p), p_len)
        sn = _nt_dot(qbd, kn) + rows_bias(pair_bias(bn_ref, p), LANES)
        scores.append((sc, jnp.where(causal, sn, NEG_INF)))
    for p, (sc, sn) in enumerate(scores):
        cols = slice(p * HEAD_PAIR_W, (p + 1) * HEAD_PAIR_W)
        vt = vc_ref[0, 2 * p:2 * p + 2].reshape(HEAD_PAIR_W, p_len).astype(BF16)
        vn = jnp.concatenate([vn_ref[:, cols].astype(BF16), zrows], axis=0)
        m = jnp.maximum(jnp.max(sc, axis=1, keepdims=True), jnp.max(sn, axis=1, keepdims=True))
        pc = jnp.exp2(sc - m)
        pn = jnp.exp2(sn - m)
        l = jnp.sum(pc, axis=1, keepdims=True) + jnp.sum(pn, axis=1, keepdims=True)
        o = (_nt_dot(pc.astype(BF16), vt) + _dot(pn.astype(BF16), vn)) / l
        out = jnp.where(lane < FOX_HD, o[:s], o[s:])
        o_ref[:, cols] = (out * gate_ref[:, cols].astype(F32)).astype(BF16)


def _fox_sample_attention(q16, cache_kt, cache_vt, fk32, fv32, bias_c, bias_n, gate, *, s, pairs):
    b, _, _, p_len = cache_kt.shape
    groups = FOX_HEADS // (2 * pairs)
    w = pairs * HEAD_PAIR_W
    blk = lambda: pl.BlockSpec((s, w), lambda bi, g: (bi, g))
    cache = lambda: pl.BlockSpec((1, 2 * pairs, FOX_HD, p_len), lambda bi, g: (bi, g, 0, 0))
    return pl.pallas_call(
        functools.partial(_fox_sample_body, s=s, pairs=pairs),
        grid=(b, groups),
        in_specs=[blk(), cache(), cache(), blk(), blk(),
                  pl.BlockSpec((1, FOX_HEADS, p_len), lambda bi, g: (bi, 0, 0)),
                  pl.BlockSpec((1, FOX_HEADS, LANES), lambda bi, g: (bi, 0, 0)),
                  blk()],
        out_specs=blk(),
        out_shape=jax.ShapeDtypeStruct((b * s, FOX_W), BF16),
        compiler_params=_cparams(2), name="fox_sample_attention",
    )(q16, cache_kt, cache_vt, fk32, fv32, bias_c, bias_n, gate)


def _diff_sample_body(q_ref, kc_ref, vc_ref, kn_ref, vn_ref, bc_ref, bn_ref, gate_ref, sg_ref, lq1_ref,
                      lk1_ref, lq2_ref, lk2_ref, o_ref, m_ref, l_ref, acc_ref, *, s, pc, lam_init):
    c = pl.program_id(1)
    last = pl.num_programs(1) - 1
    w = 2 * DIFF_HD

    @pl.when(c == 0)
    def _():
        _init_state(m_ref, l_ref, acc_ref)

    qbd = [_block_diag_queries(q_ref[:, h * w:(h + 1) * w]) for h in range(DIFF_HEADS)]

    def update(h, sc, v16):
        m = m_ref[h]
        mn = jnp.maximum(m, jnp.max(sc, axis=1, keepdims=True))
        alpha = jnp.exp2(m - mn)
        p = jnp.exp2(sc - mn)
        l_ref[h] = alpha * l_ref[h] + jnp.sum(p, axis=1, keepdims=True)
        acc_ref[h] = alpha * acc_ref[h] + _dot(p.astype(BF16), v16)
        m_ref[h] = mn

    def chunk(with_bias):
        scores = [_nt_dot(qbd[h], kc_ref[0, pl.ds(h, pc, stride=DIFF_HEADS), :].astype(BF16))
                  for h in range(DIFF_HEADS)]
        for h, sc in enumerate(scores):
            if with_bias:
                tail = bc_ref.shape[-1]
                near = sc[:, pc - tail:] + jnp.concatenate([bc_ref[h], bc_ref[h]], axis=0)
                sc = jnp.concatenate([sc[:, :pc - tail], near], axis=1) if pc > tail else near
            update(h, sc, vc_ref[0, pl.ds(h, pc, stride=DIFF_HEADS), :].astype(BF16))

    @pl.when(c != last)
    def _():
        chunk(False)

    @pl.when(c == last)
    def _():
        chunk(True)
        lam = _lambda_value(lq1_ref[...], lk1_ref[...], lq2_ref[...], lk2_ref[...], lam_init)
        zrows = jnp.zeros((LANES - s, w), BF16)
        for h in range(DIFF_HEADS):
            cols = slice(h * w, (h + 1) * w)
            new_rows = pl.ds(h, s, stride=DIFF_HEADS)
            kn = jnp.concatenate([kn_ref[new_rows, :].astype(BF16), zrows], axis=0)
            vn = jnp.concatenate([vn_ref[new_rows, :].astype(BF16), zrows], axis=0)
            sn = _nt_dot(qbd[h], kn) + jnp.concatenate([bn_ref[h], bn_ref[h]], axis=0)
            update(h, sn, vn)
            o = acc_ref[h] / l_ref[h]
            o = o[:s] - lam * o[s:]
            o_ref[:, cols] = _subln_gate(o, sg_ref[...], gate_ref[:, cols], lam_init)


def _diff_sample_attention(q16, cache_k, cache_v, dk32, dv32, bias_c, bias_n, gate, subln_g, lams, *, s, pc,
                           lam_init):
    b, rows, w = cache_k.shape
    p_len = rows // DIFF_HEADS
    assert p_len % pc == 0 and bias_c.shape[:2] == (DIFF_HEADS, s) and bias_c.shape[2] <= pc
    cache = lambda: pl.BlockSpec((1, DIFF_HEADS * pc, w), lambda bi, c: (bi, c, 0))
    out = lambda: pl.BlockSpec((s, DIFF_W), lambda bi, c: (bi, 0))
    new = lambda: pl.BlockSpec((DIFF_HEADS * s, w), lambda bi, c: (bi, 0))
    half = lambda: pl.BlockSpec((s, DIFF_W), lambda bi, c: (bi, 1))
    const = lambda shape: pl.BlockSpec(shape, lambda bi, c: (0,) * len(shape))
    return pl.pallas_call(
        functools.partial(_diff_sample_body, s=s, pc=pc, lam_init=lam_init),
        grid=(b, p_len // pc),
        in_specs=[half(), cache(), cache(), new(), new(), const(bias_c.shape), const(bias_n.shape), half(),
                  const((1, w)), const((1, DIFF_HD)), const((1, DIFF_HD)), const((1, DIFF_HD)), const((1, DIFF_HD))],
        out_specs=out(),
        out_shape=jax.ShapeDtypeStruct((b * s, DIFF_W), BF16),
        scratch_shapes=[pltpu.VMEM((DIFF_HEADS, 2 * s, 1), F32), pltpu.VMEM((DIFF_HEADS, 2 * s, 1), F32),
                        pltpu.VMEM((DIFF_HEADS, 2 * s, w), F32)],
        compiler_params=_cparams(2), name="diff_sample_attention",
    )(q16, cache_k, cache_v, dk32, dv32, bias_c, bias_n, gate, subln_g, *lams)


def _outproj_body(mf_ref, md_ref, w_ref, x_ref, g_ref, y_ref):
    out = _dot(mf_ref[...], w_ref[:FOX_W, :]) + _dot(md_ref[...], w_ref[FOX_W:, :])
    ms = jnp.mean(out * out, axis=-1, keepdims=True)
    y_ref[...] = x_ref[...] + (out * lax.rsqrt(ms + EPS)) * g_ref[...]


def _out_projection(mf, md, w_out, x2d, g_post, *, tm):
    n, d = x2d.shape
    tm = min(tm, n)
    assert n % tm == 0
    rows = lambda w: pl.BlockSpec((tm, w), lambda i: (i, 0))
    return pl.pallas_call(
        _outproj_body, grid=(n // tm,),
        in_specs=[rows(FOX_W), rows(DIFF_W), pl.BlockSpec(w_out.shape, lambda i: (0, 0)), rows(d),
                  pl.BlockSpec((1, d), lambda i: (0, 0))],
        out_specs=rows(d), out_shape=jax.ShapeDtypeStruct((n, d), F32),
        compiler_params=_cparams(1), name="out_projection",
    )(mf, md, w_out.astype(BF16), x2d, g_post.reshape(1, d))


def _t5_bucket(rel):
    half = NUM_BUCKETS // 2
    max_exact = half // 2
    ret = jnp.where(rel > 0, half, 0)
    n = jnp.abs(rel)
    nf = jnp.maximum(n, 1).astype(jnp.float32)
    large = max_exact + (jnp.log(nf / max_exact) / math.log(MAX_DISTANCE / max_exact)
                         * (half - max_exact)).astype(jnp.int32)
    large = jnp.minimum(large, half - 1)
    return ret + jnp.where(n < max_exact, n, large)


def _rel_bias_of(rel_bias, rel):
    return rel_bias[_t5_bucket(rel)].astype(F32).T


def _toeplitz(g, rows, cols):
    h = g.shape[0]
    period = rows + cols
    v = jnp.concatenate([g[:, rows - 1:], jnp.zeros((h, 1), g.dtype), g[:, :rows - 1]], axis=1)
    flat = jnp.tile(v, (1, rows))[:, :rows * (period - 1)]
    return flat.reshape(h, rows, period - 1)[:, :, :cols]


def _chunk_mask(bias, row_pos, col_pos, rows_are_keys):
    r = (row_pos // CHUNK)[:, None]
    c = (col_pos // CHUNK)[None, :]
    mask = (r <= c) if rows_are_keys else (c <= r)
    return jnp.where(mask[None], bias * LOG2E, NEG_INF)


def _prompt_layer(x, g_pre, w_in, b_f, lams, subln_g, w_out, g_post, rel_bias, layer_idx, *, t=256, tm=512,
                  tm_out=2048, fox_pairs=4, diff_heads=4):
    b, s, d = x.shape
    assert b == 1 and s % t == 0 and s // LANES >= 1
    x2d = x.reshape(s, d)
    (fk, fv, dk, dv, logft, gate, qt, k16, dk16, vt) = _in_projection(
        x2d, g_pre, w_in, b_f, transposed=True, tm=tm, tk=t)

    _, hi, mid, lo = _prefix_sums(logft.reshape(1, FOX_HEADS, s), seg=LANES, carry=True, scale=LOG2E)
    pieces = jnp.stack([hi, mid, lo], axis=0).reshape(PIECES, FOX_HEADS, s)
    kext = jnp.concatenate([
        (-pieces).transpose(2, 1, 0).reshape(s, ONES_AT), jnp.ones((s, PIECES), BF16),
        jnp.zeros((s, LANES - ONES_AT - PIECES), BF16)], axis=1)
    sel = (jnp.arange(ONES_AT)[None, :] // PIECES == jnp.arange(FOX_HEADS)[:, None]).astype(BF16)
    qext = jnp.concatenate([
        jnp.broadcast_to(sel[:, :, None], (FOX_HEADS, ONES_AT, s)), pieces.transpose(1, 0, 2),
        jnp.zeros((FOX_HEADS, QEXT_ROWS - ONES_AT - PIECES, s), BF16)], axis=1)

    mf = _fox_prompt_attention(qt, qext, k16, kext, vt, gate, t=t, pairs=fox_pairs)

    lam_init = 0.8 - 0.6 * math.exp(-0.3 * layer_idx)
    assert t >= MAX_DISTANCE
    pos = jnp.arange(t, dtype=jnp.int32)
    off = jnp.arange(-(t - 1), t, dtype=jnp.int32)
    far = _rel_bias_of(rel_bias, jnp.full((1,), -2 * t, jnp.int32))
    bias_diag = _chunk_mask(_toeplitz(_rel_bias_of(rel_bias, -off) - far, t, t), pos, pos, True)
    bias_sub = _chunk_mask(_toeplitz(_rel_bias_of(rel_bias, -off - t) - far, t, t), pos, pos + t, True)
    md = _diff_prompt_attention(qt, dk16, vt, bias_diag, bias_sub, gate, subln_g.reshape(1, -1), lams,
                                t=t, heads=diff_heads, lam_init=lam_init)

    y = _out_projection(mf, md, w_out, x2d, g_post, tm=tm_out)
    return y.reshape(b, s, d), (fk, fv, logft.T, dk, dv)


def _sample_layer(x, past, g_pre, w_in, b_f, lams, subln_g, w_out, g_post, rel_bias, layer_idx, *, tm=512,
                  fox_pairs=4, diff_chunk=4096):
    b, s, d = x.shape
    pk, pv, plogf, pdk, pdv = past
    p_len = pk.shape[1]
    assert s <= LANES and LANES % s == 0 and p_len % LANES == 0
    x2d = x.reshape(b * s, d)
    fk, fv, dk, dv, logft, gate, q16 = _in_projection(x2d, g_pre, w_in, b_f, transposed=False, tm=tm, tk=tm)

    bias_c = _prefix_sums(plogf.transpose(0, 2, 1), seg=LANES, carry=True, suffix=True, scale=LOG2E)[0]
    neg_cum_n = _prefix_sums(logft.reshape(1, FOX_HEADS, b * s), seg=s, carry=False, scale=-LOG2E)[0]
    bias_n = jnp.concatenate([neg_cum_n.reshape(FOX_HEADS, b, s).transpose(1, 0, 2),
                              jnp.full((b, FOX_HEADS, LANES - s), NEG_INF, F32)], axis=-1)
    mf = _fox_sample_attention(q16, pk.transpose(0, 2, 3, 1), pv.transpose(0, 2, 3, 1), fk, fv, bias_c, bias_n,
                               gate, s=s, pairs=fox_pairs)

    lam_init = 0.8 - 0.6 * math.exp(-0.3 * layer_idx)
    pc = min(diff_chunk, p_len)
    tail = MAX_DISTANCE
    assert tail % LANES == 0 and pc >= tail
    q_pos = p_len + jnp.arange(s, dtype=jnp.int32)
    far = _rel_bias_of(rel_bias, jnp.full((1,), -2 * p_len, jnp.int32))
    off = jnp.arange(-(s - 1), tail, dtype=jnp.int32)
    bd_c = _chunk_mask(_toeplitz(_rel_bias_of(rel_bias, off - tail) - far, s, tail), q_pos,
                       p_len - tail + jnp.arange(tail, dtype=jnp.int32), False)
    off = jnp.arange(-(s - 1), s, dtype=jnp.int32)
    bd_n = _chunk_mask(_toeplitz(_rel_bias_of(rel_bias, off) - far, s, s), q_pos, q_pos, False)
    bd_n = jnp.concatenate([bd_n, jnp.full((DIFF_HEADS, s, LANES - s), NEG_INF, F32)], axis=-1)
    md = _diff_sample_attention(q16, pdk.reshape(b, p_len * DIFF_HEADS, 2 * DIFF_HD),
                                pdv.reshape(b, p_len * DIFF_HEADS, 2 * DIFF_HD), dk, dv, bd_c, bd_n, gate,
                                subln_g.reshape(1, -1), lams, s=s, pc=pc, lam_init=lam_init)

    y = _out_projection(mf, md, w_out, x2d, g_post, tm=tm)
    return y.reshape(b, s, d), (fk, fv, logft.T, dk, dv)


def kernel(x_prompt, x_sample, cache_fox_k, cache_fox_v, cache_fox_logf, cache_diff_k, cache_diff_v,
           norm_pre_g, w_in, forget_bias, lambda_q1, lambda_k1, lambda_q2, lambda_k2, subln_g, w_out,
           norm_post_g, rel_bias):
    depth = w_in.shape[0]
    y_p, y_s = x_prompt, x_sample
    rows_p, rows_s = [], []
    for l in range(depth):
        lams = tuple(a[l].reshape(1, -1) for a in (lambda_q1, lambda_k1, lambda_q2, lambda_k2))
        params = (norm_pre_g[l], w_in[l], forget_bias[l], lams, subln_g[l], w_out[l], norm_post_g[l], rel_bias, l)
        y_p, new_p = _prompt_layer(y_p, *params)
        past = (cache_fox_k[l], cache_fox_v[l], cache_fox_logf[l], cache_diff_k[l], cache_diff_v[l])
        y_s, new_s = _sample_layer(y_s, past, *params)
        rows_p.append(new_p)
        rows_s.append(new_s)

    bp, sp = x_prompt.shape[:2]
    bs, ss = x_sample.shape[:2]

    def stack(rows, idx, shape):
        return jnp.stack([r[idx].reshape(shape) for r in rows], axis=0)

    outs = [y_p, y_s]
    for rows, (b, s) in ((rows_p, (bp, sp)), (rows_s, (bs, ss))):
        outs += [stack(rows, 0, (b, s, FOX_HEADS, FOX_HD)), stack(rows, 1, (b, s, FOX_HEADS, FOX_HD)),
                 stack(rows, 2, (b, s, FOX_HEADS)), stack(rows, 3, (b, s, DIFF_HEADS, 2 * DIFF_HD)),
                 stack(rows, 4, (b, s, DIFF_HEADS, 2 * DIFF_HD))]
    return tuple(outs)
```
